```python
import math
import jax
import jax.numpy as jnp
from jax import lax
import numpy as np

D_MODEL = 1024
BATCH = 32
SEQ = 256
DEPTH = 2
DEC_BATCH = 4
DEC_SEQ = 1024
PAST_LEN = 256

GRID_W = 64
N_ATTN = 4
DH_ATTN = 64
W_ATTN = N_ATTN * 2 * DH_ATTN
N_FOUR = 4
DG_FOUR = 128
W_FOUR = N_FOUR * DG_FOUR
N_MLSTM = 4
DH_MLSTM = 128
W_MLSTM = N_MLSTM * DH_MLSTM
N_GATE = 4 * N_MLSTM
P_IN = 3 * W_ATTN + W_FOUR + 4 * W_MLSTM + N_GATE
SPLIT_AT = (W_ATTN, 2 * W_ATTN, 3 * W_ATTN, 3 * W_ATTN + W_FOUR,
            3 * W_ATTN + W_FOUR + W_MLSTM, 3 * W_ATTN + W_FOUR + 2 * W_MLSTM,
            3 * W_ATTN + W_FOUR + 3 * W_MLSTM, 3 * W_ATTN + W_FOUR + 4 * W_MLSTM)
N_BRANCH = 3
D_FF = 2816
N_MOD = 9
CHUNK = 64
Q_BLOCK = 128
ROPE_BASE = 10000.0
ROPE_AXIS_PAIRS = DH_ATTN // 4
ATTN_SCALE = DH_ATTN ** -0.5
MLSTM_K_SCALE = DH_MLSTM ** -0.5
EPS = 1e-6

kernel_name = 'hybrid_diffattn_fnet_mlstm_macaron_step'


def rms_norm(x, g):
    xf = x.astype(jnp.float32)
    y = xf * lax.rsqrt(jnp.mean(xf * xf, axis=-1, keepdims=True) + EPS)
    return (y * g.astype(jnp.float32)).astype(x.dtype)


def modulate(x, shift, scale):
    return x * (1 + scale[:, None, :]) + shift[:, None, :]


def adaln(cvec, w, b):
    m = jax.nn.silu(cvec) @ w + b
    return jnp.split(m, N_MOD, axis=-1)


def swiglu(u, w_in, w_out):
    a, g = jnp.split(u @ w_in, 2, axis=-1)
    return (jax.nn.silu(a) * g) @ w_out


def grid_rope(n_tok):
    rows = n_tok // GRID_W
    row = jnp.repeat(jnp.arange(rows, dtype=jnp.float32), GRID_W)
    col = (jnp.arange(n_tok) % GRID_W).astype(jnp.float32)
    inv = ROPE_BASE ** (-jnp.arange(ROPE_AXIS_PAIRS, dtype=jnp.float32) / ROPE_AXIS_PAIRS)
    ang = jnp.concatenate([row[:, None] * inv, col[:, None] * inv], axis=-1)
    return jnp.cos(ang), jnp.sin(ang)


def apply_rope(x, cos, sin):
    B, H, T, _ = x.shape
    xm = x.reshape(B, H, T, 2, DH_ATTN)
    c = cos[:, None, :].astype(x.dtype)
    s = sin[:, None, :].astype(x.dtype)
    half = DH_ATTN // 2
    x1, x2 = xm[..., :half], xm[..., half:]
    out = jnp.concatenate([x1 * c - x2 * s, x1 * s + x2 * c], axis=-1)
    return out.reshape(B, H, T, 2 * DH_ATTN)


def diff_lambda(lam_p, lam_init):
    lp = lam_p.astype(jnp.float32)
    return jnp.exp(jnp.sum(lp[0] * lp[1])) - jnp.exp(jnp.sum(lp[2] * lp[3])) + lam_init


def diff_attention(q, k, v, lam, lam_init, g_sub):
    B, H, Tq, _ = q.shape
    nb = Tq // Q_BLOCK
    k1, k2 = k[..., :DH_ATTN], k[..., DH_ATTN:]

    def block(qb):
        s1 = jnp.einsum('bhqd,bhkd->bhqk', qb[..., :DH_ATTN], k1).astype(jnp.float32) * ATTN_SCALE
        s2 = jnp.einsum('bhqd,bhkd->bhqk', qb[..., DH_ATTN:], k2).astype(jnp.float32) * ATTN_SCALE
        p = jax.nn.softmax(s1, axis=-1) - lam * jax.nn.softmax(s2, axis=-1)
        return jnp.einsum('bhqk,bhkd->bhqd', p.astype(v.dtype), v)

    qs = jnp.moveaxis(q.reshape(B, H, nb, Q_BLOCK, 2 * DH_ATTN), 2, 0)
    o = lax.map(block, qs)
    o = jnp.moveaxis(o, 0, 2).reshape(B, H, Tq, 2 * DH_ATTN)
    return rms_norm(o, g_sub) * (1.0 - lam_init)


def fourier_mix(z):
    B, T, _ = z.shape
    zg = z.reshape(B, T, N_FOUR, DG_FOUR).astype(jnp.float32)
    f = jnp.fft.fft2(zg, axes=(1, 3), norm='ortho').real
    return f.reshape(B, T, W_FOUR).astype(z.dtype)


def mlstm_scan(q, k, v, ig, lf, C0, n0, m0):
    B, H, T, D = q.shape
    nc = T // CHUNK

    def chunks(a):
        return jnp.moveaxis(a.reshape((B, H, nc, CHUNK) + a.shape[3:]), 2, 0)

    lower = jnp.tril(jnp.ones((CHUNK, CHUNK), dtype=bool))

    def step(carry, xs):
        C, n, m = carry
        qc, kc, vc, ic, fc = xs
        b = jnp.cumsum(fc, axis=-1)
        logw = jnp.where(lower, b[..., :, None] - b[..., None, :] + ic[..., None, :], -jnp.inf)
        prev = b + m[..., None]
        m_t = jnp.maximum(prev, jnp.max(logw, axis=-1))
        w = jnp.exp(logw - m_t[..., None])
        sp = jnp.exp(prev - m_t)
        s = jnp.einsum('bhtd,bhsd->bhts', qc, kc) * w
        num = sp[..., None] * jnp.einsum('bhvk,bhtk->bhtv', C, qc) + jnp.einsum('bhts,bhsv->bhtv', s, vc)
        den = sp * jnp.einsum('bhk,bhtk->bht', n, qc) + jnp.sum(s, axis=-1)
        h = num / jnp.maximum(jnp.abs(den), jnp.exp(-m_t))[..., None]
        m_new = m_t[..., -1]
        wl = jnp.exp(b[..., -1:] - b + ic - m_new[..., None])
        decay = jnp.exp(b[..., -1] + m - m_new)
        C_new = decay[..., None, None] * C + jnp.einsum('bhs,bhsv,bhsk->bhvk', wl, vc, kc)
        n_new = decay[..., None] * n + jnp.einsum('bhs,bhsk->bhk', wl, kc)
        return (C_new, n_new, m_new), h

    (C1, n1, m1), hs = lax.scan(step, (C0, n0, m0),
                                (chunks(q), chunks(k), chunks(v), chunks(ig), chunks(lf)))
    return jnp.moveaxis(hs, 0, 2).reshape(B, H, T, D), (C1, n1, m1)


def mlstm_bidirectional(q, k, v, ig_f, lf_f, ig_b, lf_b, C0, n0, m0):
    h_f, (Cf, nf, mf) = mlstm_scan(q, k, v, ig_f, lf_f, C0[:, 0], n0[:, 0], m0[:, 0])

    def rev(a):
        return jnp.flip(a, axis=2)

    h_b, (Cb, nb, mb) = mlstm_scan(rev(q), rev(k), rev(v), rev(ig_b), rev(lf_b),
                                   C0[:, 1], n0[:, 1], m0[:, 1])
    return h_f + rev(h_b), (jnp.stack([Cf, Cb], axis=1), jnp.stack([nf, nb], axis=1),
                            jnp.stack([mf, mb], axis=1))


def token_mixer(u, lp, lam_init, ctx):
    B, T, _ = u.shape
    z = u @ lp['w_in']
    za_q, za_k, za_v, zf, zm_q, zm_k, zm_v, zm_o, zg = jnp.split(z, SPLIT_AT, axis=-1)

    def attn_heads(a):
        return a.reshape(B, T, N_ATTN, 2 * DH_ATTN).transpose(0, 2, 1, 3)

    q, k, v = attn_heads(za_q), attn_heads(za_k), attn_heads(za_v)
    if ctx is None:
        q_use, k_use, v_use = q, k, v
    else:
        k_ctx, v_ctx, C0, n0, m0 = ctx
        cos, sin = grid_rope(T)
        q_use = apply_rope(q, cos, sin)
        k_use = jnp.concatenate([k_ctx.astype(k.dtype), apply_rope(k, cos, sin)], axis=2)
        v_use = jnp.concatenate([v_ctx.astype(v.dtype), v], axis=2)
    lam = diff_lambda(lp['attn_lambda'], lam_init)
    a_out = diff_attention(q_use, k_use, v_use, lam, lam_init, lp['g_attn_sub'])
    a_out = a_out.transpose(0, 2, 1, 3).reshape(B, T, W_ATTN)

    f_out = fourier_mix(zf)

    def mlstm_heads(a):
        return a.reshape(B, T, N_MLSTM, DH_MLSTM).transpose(0, 2, 1, 3).astype(jnp.float32)

    mq, mk, mv = mlstm_heads(zm_q), mlstm_heads(zm_k) * MLSTM_K_SCALE, mlstm_heads(zm_v)
    gates = (zg + lp['b_mgate']).astype(jnp.float32).reshape(B, T, 4, N_MLSTM).transpose(2, 0, 3, 1)
    ig_f, lf_f = gates[0], jax.nn.log_sigmoid(gates[1])
    ig_b, lf_b = gates[2], jax.nn.log_sigmoid(gates[3])
    if ctx is None:
        C0 = jnp.zeros((B, 2, N_MLSTM, DH_MLSTM, DH_MLSTM), jnp.float32)
        n0 = jnp.zeros((B, 2, N_MLSTM, DH_MLSTM), jnp.float32)
        m0 = jnp.zeros((B, 2, N_MLSTM), jnp.float32)
    h, (C1, n1, m1) = mlstm_bidirectional(mq, mk, mv, ig_f, lf_f, ig_b, lf_b,
                                          C0.astype(jnp.float32), n0.astype(jnp.float32),
                                          m0.astype(jnp.float32))
    h = rms_norm(h, lp['g_mlstm']).transpose(0, 2, 1, 3).reshape(B, T, W_MLSTM).astype(u.dtype)
    m_out = h * jax.nn.sigmoid(zm_o)

    gb = jax.nn.sigmoid(u @ lp['w_branch_gate']).reshape(B, T, N_BRANCH, D_MODEL)
    merged = (gb[:, :, 0] * (a_out @ lp['w_br_attn'])
              + gb[:, :, 1] * (f_out @ lp['w_br_four'])
              + gb[:, :, 2] * (m_out @ lp['w_br_mlstm']))
    out = merged @ lp['w_out']
    if ctx is None:
        dt = u.dtype
        return out, (k, v, C1.astype(dt), n1.astype(dt), m1.astype(dt))
    return out, None


def trunk_layer(x, mods, lp, lam_init, ctx):
    sh1, sc1, gt1, sh2, sc2, gt2, sh3, sc3, gt3 = mods
    g = lp['g_norm']
    u = modulate(rms_norm(x, g[0]), sh1, sc1)
    x = x + 0.5 * gt1[:, None, :] * swiglu(u, lp['w_ffn1_in'], lp['w_ffn1_out'])
    u = modulate(rms_norm(x, g[1]), sh2, sc2)
    y, ctx_out = token_mixer(u, lp, lam_init, ctx)
    x = x + gt2[:, None, :] * y
    u = modulate(rms_norm(x, g[2]), sh3, sc3)
    x = x + 0.5 * gt3[:, None, :] * swiglu(u, lp['w_ffn2_in'], lp['w_ffn2_out'])
    return x, ctx_out


def setup_inputs(seed: int = 0) -> dict:
    key = jax.random.key(seed)
    ks = jax.random.split(key, 32)
    D = D_MODEL

    def nrm(k, shape, scale):
        return scale * jax.random.normal(k, shape, jnp.float32)

    is_forget = jnp.array([False, True, False, True])
    b_in_gate = nrm(ks[16], (DEPTH, 4, N_MLSTM), 0.1)
    b_fg_gate = 3.0 + 3.0 * jax.random.uniform(ks[17], (DEPTH, 4, N_MLSTM), jnp.float32)
    b_mgate = jnp.where(is_forget[None, :, None], b_fg_gate, b_in_gate).reshape(DEPTH, N_GATE)
    return {
        'x_prompt': nrm(ks[0], (BATCH, SEQ, D), 1.0),
        'x_sample': nrm(ks[1], (DEC_BATCH, DEC_SEQ, D), 1.0),
        'cache_k': nrm(ks[2], (DEC_BATCH, DEPTH, N_ATTN, PAST_LEN, 2 * DH_ATTN), 1.0),
        'cache_v': nrm(ks[3], (DEC_BATCH, DEPTH, N_ATTN, PAST_LEN, 2 * DH_ATTN), 1.0),
        'state_C': nrm(ks[4], (DEC_BATCH, DEPTH, 2, N_MLSTM, DH_MLSTM, DH_MLSTM), 0.3),
        'state_n': nrm(ks[5], (DEC_BATCH, DEPTH, 2, N_MLSTM, DH_MLSTM), 0.3),
        'state_m': nrm(ks[6], (DEC_BATCH, DEPTH, 2, N_MLSTM), 1.0),
        'c': nrm(ks[7], (DEC_BATCH, D), 1.0),
        'c_ctx': nrm(ks[8], (D,), 1.0),
        'w_ada': nrm(ks[9], (DEPTH, D, N_MOD * D), 0.5 * D ** -0.5),
        'b_ada': nrm(ks[10], (DEPTH, N_MOD * D), 0.02),
        'g_norm': 1.0 + nrm(ks[11], (DEPTH, 3, D), 0.02),
        'w_ffn1_in': nrm(ks[12], (DEPTH, D, 2 * D_FF), D ** -0.5),
        'w_ffn1_out': nrm(ks[13], (DEPTH, D_FF, D), D_FF ** -0.5),
        'w_ffn2_in': nrm(ks[14], (DEPTH, D, 2 * D_FF), D ** -0.5),
        'w_ffn2_out': nrm(ks[15], (DEPTH, D_FF, D), D_FF ** -0.5),
        'w_in': nrm(ks[18], (DEPTH, D, P_IN), D ** -0.5),
        'b_mgate': b_mgate,
        'attn_lambda': nrm(ks[19], (DEPTH, 4, DH_ATTN), 0.1),
        'g_attn_sub': 1.0 + nrm(ks[20], (DEPTH, 2 * DH_ATTN), 0.02),
        'g_mlstm': 1.0 + nrm(ks[21], (DEPTH, DH_MLSTM), 0.02),
        'w_branch_gate': nrm(ks[22], (DEPTH, D, N_BRANCH * D), D ** -0.5),
        'w_br_attn': nrm(ks[23], (DEPTH, W_ATTN, D), W_ATTN ** -0.5),
        'w_br_four': nrm(ks[24], (DEPTH, W_FOUR, D), W_FOUR ** -0.5),
        'w_br_mlstm': nrm(ks[25], (DEPTH, W_MLSTM, D), W_MLSTM ** -0.5),
        'w_out': nrm(ks[26], (DEPTH, D, D), D ** -0.5),
        'g_final': 1.0 + nrm(ks[27], (D,), 0.02),
    }


def reference(x_prompt, x_sample, cache_k, cache_v, state_C, state_n, state_m, c, c_ctx,
              w_ada, b_ada, g_norm, w_ffn1_in, w_ffn1_out, w_ffn2_in, w_ffn2_out, w_in, b_mgate,
              attn_lambda, g_attn_sub, g_mlstm, w_branch_gate, w_br_attn, w_br_four, w_br_mlstm,
              w_out, g_final):
    hp = x_prompt
    hs = x_sample
    ks_l, vs_l, Cs_l, ns_l, ms_l = [], [], [], [], []
    for l in range(DEPTH):
        lam_init = 0.8 - 0.6 * math.exp(-0.3 * l)
        lp = {
            'g_norm': g_norm[l], 'w_ffn1_in': w_ffn1_in[l], 'w_ffn1_out': w_ffn1_out[l],
            'w_ffn2_in': w_ffn2_in[l], 'w_ffn2_out': w_ffn2_out[l], 'w_in': w_in[l],
            'b_mgate': b_mgate[l], 'attn_lambda': attn_lambda[l], 'g_attn_sub': g_attn_sub[l],
            'g_mlstm': g_mlstm[l], 'w_branch_gate': w_branch_gate[l], 'w_br_attn': w_br_attn[l],
            'w_br_four': w_br_four[l], 'w_br_mlstm': w_br_mlstm[l], 'w_out': w_out[l],
        }
        mods_ctx = adaln(c_ctx[None, :], w_ada[l], b_ada[l])
        mods_lat = adaln(c, w_ada[l], b_ada[l])
        hp, (k_l, v_l, C_l, n_l, m_l) = trunk_layer(hp, mods_ctx, lp, lam_init, None)
        ctx_l = (cache_k[:, l], cache_v[:, l], state_C[:, l], state_n[:, l], state_m[:, l])
        hs, _ = trunk_layer(hs, mods_lat, lp, lam_init, ctx_l)
        ks_l.append(k_l)
        vs_l.append(v_l)
        Cs_l.append(C_l)
        ns_l.append(n_l)
        ms_l.append(m_l)
    y_prompt = rms_norm(hp, g_final)
    y_sample = rms_norm(hs, g_final)
    new_cache_k = jnp.stack(ks_l, axis=1)
    new_cache_v = jnp.stack(vs_l, axis=1)
    new_state_C = jnp.stack(Cs_l, axis=1)
    new_state_n = jnp.stack(ns_l, axis=1)
    new_state_m = jnp.stack(ms_l, axis=1)
    return (y_prompt, y_sample, new_cache_k, new_cache_v, new_state_C, new_state_n, new_state_m)
```

```python
import functools
import math

import numpy as np
import jax
import jax.numpy as jnp
from jax import lax
from jax.experimental import pallas as pl
from jax.experimental.pallas import tpu as pltpu

D_MODEL = 1024
DEPTH = 2
GRID_W = 64
N_ATTN = 4
DH_ATTN = 64
DV_ATTN = 2 * DH_ATTN
W_ATTN = N_ATTN * DV_ATTN
N_FOUR = 4
DG_FOUR = 128
W_FOUR = N_FOUR * DG_FOUR
N_MLSTM = 4
DH_MLSTM = 128
W_MLSTM = N_MLSTM * DH_MLSTM
N_GATE = 4 * N_MLSTM
P_MAIN = 3 * W_ATTN + W_FOUR + 4 * W_MLSTM
LANES = 128
P_PAD = P_MAIN + LANES
D_FF = 2816
N_MOD = 9
N_COND = 8
ROPE_BASE = 10000.0
ROPE_AXIS_PAIRS = DH_ATTN // 4
ATTN_SCALE = DH_ATTN ** -0.5
MLSTM_K_SCALE = DH_MLSTM ** -0.5
EPS = 1e-6
MLSTM_CHUNK = 128
VMEM_LIMIT = 56 * 1024 * 1024

F32 = jnp.float32
BF16 = jnp.bfloat16


def _cparams(n_grid):
    return pltpu.CompilerParams(dimension_semantics=("arbitrary",) * n_grid,
                                vmem_limit_bytes=VMEM_LIMIT)


def _bdot(a, b):
    return jnp.dot(a.astype(BF16), b.astype(BF16), preferred_element_type=F32)


def _bdot_nt(a, b):
    return lax.dot_general(a.astype(BF16), b.astype(BF16), (((1,), (1,)), ((), ())),
                           preferred_element_type=F32)


def _bdot_tn(a, b):
    return lax.dot_general(a.astype(BF16), b.astype(BF16), (((0,), (0,)), ((), ())),
                           preferred_element_type=F32)


def _sigmoid(x):
    return 1.0 / (1.0 + jnp.exp(-x))


def _rms(x, g):
    return x * lax.rsqrt(jnp.mean(x * x, axis=-1, keepdims=True) + EPS) * g


def _modulated_norm(x, g, mod, base):
    return _rms(x, g) * (1.0 + mod[base + 1:base + 2]) + mod[base:base + 1]


def _ada_kernel(c_ref, w_ref, b_ref, o_ref):
    c = c_ref[...]
    s = c * _sigmoid(c)
    o_ref[...] = _bdot(s, w_ref[...]) + b_ref[...]


def _ada_call(cond, w_ada, b_ada):
    tn = 1024
    n_out = N_MOD * D_MODEL
    return pl.pallas_call(
        _ada_kernel,
        grid=(DEPTH, n_out // tn),
        in_specs=[
            pl.BlockSpec((N_COND, D_MODEL), lambda l, j: (0, 0)),
            pl.BlockSpec((None, D_MODEL, tn), lambda l, j: (l, 0, j)),
            pl.BlockSpec((None, 1, tn), lambda l, j: (l, 0, j)),
        ],
        out_specs=pl.BlockSpec((None, N_COND, tn), lambda l, j: (l, 0, j)),
        out_shape=jax.ShapeDtypeStruct((DEPTH, N_COND, n_out), F32),
        compiler_params=_cparams(2),
        name="adaln",
    )(cond, w_ada, b_ada.reshape(DEPTH, 1, n_out))


FF_CHUNK = D_FF // 2


def _ffn_kernel(x_ref, mod_ref, g_ref, gf_ref, win_ref, wout_ref, o_ref, *, base, final):
    x = x_ref[...]
    mod = mod_ref[...]
    u = _modulated_norm(x, g_ref[...], mod, base).astype(BF16)
    y = None
    for c in range(D_FF // FF_CHUNK):
        lo = c * FF_CHUNK
        a = jnp.dot(u, win_ref[:, lo:lo + FF_CHUNK], preferred_element_type=F32)
        g = jnp.dot(u, win_ref[:, D_FF + lo:D_FF + lo + FF_CHUNK], preferred_element_type=F32)
        hh = (a * _sigmoid(a) * g).astype(BF16)
        part = jnp.dot(hh, wout_ref[lo:lo + FF_CHUNK, :], preferred_element_type=F32)
        y = part if y is None else y + part
    xn = x + 0.5 * mod[base + 2:base + 3] * y
    if final:
        xn = _rms(xn, gf_ref[...])
    o_ref[...] = xn


def _mod_spec(seq_tiles, mod_row0):
    if seq_tiles == 0:
        return pl.BlockSpec((None, N_MOD, D_MODEL), lambda i: (mod_row0, 0, 0))
    return pl.BlockSpec((None, N_MOD, D_MODEL), lambda i: (mod_row0 + i // seq_tiles, 0, 0))


def _ffn_call(x, mods, g, g_final, w_in, w_out, *, base, final, tm, seq_tiles, mod_row0):
    n = x.shape[0]
    return pl.pallas_call(
        functools.partial(_ffn_kernel, base=base, final=final),
        grid=(n // tm,),
        in_specs=[
            pl.BlockSpec((tm, D_MODEL), lambda i: (i, 0)),
            _mod_spec(seq_tiles, mod_row0),
            pl.BlockSpec((1, D_MODEL), lambda i: (0, 0)),
            pl.BlockSpec((1, D_MODEL), lambda i: (0, 0)),
            pl.BlockSpec((D_MODEL, 2 * D_FF), lambda i: (0, 0)),
            pl.BlockSpec((D_FF, D_MODEL), lambda i: (0, 0)),
        ],
        out_specs=pl.BlockSpec((tm, D_MODEL), lambda i: (i, 0)),
        out_shape=jax.ShapeDtypeStruct((n, D_MODEL), F32),
        compiler_params=_cparams(1),
        name="ffn",
    )(x, mods, g, g_final, w_in, w_out)


def _swap32(x):
    lane = lax.broadcasted_iota(jnp.int32, x.shape, 1)
    return jnp.where((lane & (DH_ATTN // 2)) == 0,
                     pltpu.roll(x, LANES - DH_ATTN // 2, 1), pltpu.roll(x, DH_ATTN // 2, 1))


def _proj_kernel(x_ref, mod_ref, g_ref, w_ref, bg_ref, cos_ref, sin_ref,
                 q_ref, k_ref, v_ref, zf_ref, mq_ref, mk_ref, mv_ref, mo_ref, gate_ref, *, rope):
    x = x_ref[...]
    u = _modulated_norm(x, g_ref[...], mod_ref[...], 3).astype(BF16)
    z = jnp.dot(u, w_ref[...], preferred_element_type=F32)
    if rope:
        cos_t = cos_ref[...]
        sin_t = sin_ref[...]
    for h in range(N_ATTN):
        zq = z[:, h * DV_ATTN:(h + 1) * DV_ATTN]
        zk = z[:, W_ATTN + h * DV_ATTN:W_ATTN + (h + 1) * DV_ATTN]
        if rope:
            zq = zq * cos_t + _swap32(zq) * sin_t
            zk = zk * cos_t + _swap32(zk) * sin_t
        q_ref[h] = zq
        k_ref[h] = zk
        v_ref[h] = z[:, 2 * W_ATTN + h * DV_ATTN:2 * W_ATTN + (h + 1) * DV_ATTN]
    off = 3 * W_ATTN
    zf_ref[...] = z[:, off:off + W_FOUR]
    off += W_FOUR
    mq_ref[...] = z[:, off:off + W_MLSTM]
    mk_ref[...] = z[:, off + W_MLSTM:off + 2 * W_MLSTM]
    mv_ref[...] = z[:, off + 2 * W_MLSTM:off + 3 * W_MLSTM]
    mo_ref[...] = z[:, off + 3 * W_MLSTM:off + 4 * W_MLSTM]
    gp = z[:, P_MAIN:P_PAD] + bg_ref[...]
    lane = lax.broadcasted_iota(jnp.int32, gp.shape, 1)
    is_forget = (lane < N_GATE) & ((lane & N_MLSTM) != 0)
    log_sig = jnp.minimum(gp, 0.0) - jnp.log1p(jnp.exp(-jnp.abs(gp)))
    gate_ref[...] = jnp.where(is_forget, log_sig, gp)


def _proj_call(x, mods, g, w, bg, cos_t, sin_t, *, batch, seq, tm, mod_row0, per_seq_mod, rope):
    n = x.shape[0]
    tps = seq // tm
    head_shape = jax.ShapeDtypeStruct((batch, N_ATTN, seq, DV_ATTN), F32)
    tok_shape = jax.ShapeDtypeStruct((n, W_ATTN), F32)
    head_spec = pl.BlockSpec((None, N_ATTN, tm, DV_ATTN), lambda i: (i // tps, 0, i % tps, 0))
    tok_spec = pl.BlockSpec((tm, W_ATTN), lambda i: (i, 0))
    rope_spec = pl.BlockSpec((tm, LANES), lambda i: (i % tps, 0))
    return pl.pallas_call(
        functools.partial(_proj_kernel, rope=rope),
        grid=(n // tm,),
        in_specs=[
            pl.BlockSpec((tm, D_MODEL), lambda i: (i, 0)),
            _mod_spec(tps if per_seq_mod else 0, mod_row0),
            pl.BlockSpec((1, D_MODEL), lambda i: (0, 0)),
            pl.BlockSpec((D_MODEL, P_PAD), lambda i: (0, 0)),
            pl.BlockSpec((1, LANES), lambda i: (0, 0)),
            rope_spec, rope_spec,
        ],
        out_specs=[head_spec, head_spec, head_spec, tok_spec, tok_spec, tok_spec, tok_spec, tok_spec,
                   pl.BlockSpec((tm, LANES), lambda i: (i, 0))],
        out_shape=[head_shape, head_shape, head_shape, tok_shape, tok_shape, tok_shape, tok_shape,
                   tok_shape, jax.ShapeDtypeStruct((n, LANES), F32)],
        compiler_params=_cparams(1),
        name="mixer_proj",
    )(x, mods, g, w, bg, cos_t, sin_t)


def _attn_kernel(*refs, lam_init, cached):
    if cached:
        q_ref, k_ref, v_ref, ck_ref, cv_ref, lam_ref, g_ref, o_ref = refs
    else:
        q_ref, k_ref, v_ref, lam_ref, g_ref, o_ref = refs
    lp = lam_ref[...]
    lam = (jnp.exp(jnp.sum(lp[0:1] * lp[1:2], axis=-1, keepdims=True))
           - jnp.exp(jnp.sum(lp[2:3] * lp[3:4], axis=-1, keepdims=True)) + lam_init)
    q = q_ref[...]
    lane = lax.broadcasted_iota(jnp.int32, q.shape, 1)
    q_maps = (jnp.where(lane < DH_ATTN, q, 0.0).astype(BF16),
              jnp.where(lane >= DH_ATTN, q, 0.0).astype(BF16))
    keys = [k_ref[...].astype(BF16)]
    vals = [v_ref[...].astype(BF16)]
    if cached:
        keys.insert(0, ck_ref[...].astype(BF16))
        vals.insert(0, cv_ref[...].astype(BF16))
    probs = []
    for qm in q_maps:
        s = [_bdot_nt(qm, kk) * ATTN_SCALE for kk in keys]
        m = functools.reduce(jnp.maximum, [jnp.max(si, axis=-1, keepdims=True) for si in s])
        e = [jnp.exp(si - m) for si in s]
        den = functools.reduce(jnp.add, [jnp.sum(ei, axis=-1, keepdims=True) for ei in e])
        probs.append([ei / den for ei in e])
    o = None
    for j, vv in enumerate(vals):
        p = probs[0][j] - lam * probs[1][j]
        part = jnp.dot(p.astype(BF16), vv, preferred_element_type=F32)
        o = part if o is None else o + part
    o_ref[...] = _rms(o, g_ref[...]) * (1.0 - lam_init)


def _attn_call(q, k, v, cache, lam_p, g_sub, *, layer, lam_init, tq):
    batch, _, seq, _ = q.shape
    nq = seq // tq
    q_spec = pl.BlockSpec((None, None, tq, DV_ATTN), lambda b, h, i: (b, h, i, 0))
    kv_spec = pl.BlockSpec((None, None, seq, DV_ATTN), lambda b, h, i: (b, h, 0, 0))
    in_specs = [q_spec, kv_spec, kv_spec]
    args = [q, k, v]
    if cache is not None:
        past = cache[0].shape[3]
        c_spec = pl.BlockSpec((None, None, None, past, DV_ATTN), lambda b, h, i: (b, layer, h, 0, 0))
        in_specs += [c_spec, c_spec]
        args += list(cache)
    in_specs += [pl.BlockSpec((4, DH_ATTN), lambda b, h, i: (0, 0)),
                 pl.BlockSpec((1, DV_ATTN), lambda b, h, i: (0, 0))]
    args += [lam_p, g_sub]
    return pl.pallas_call(
        functools.partial(_attn_kernel, lam_init=lam_init, cached=cache is not None),
        grid=(batch, N_ATTN, nq),
        in_specs=in_specs,
        out_specs=pl.BlockSpec((tq, DV_ATTN), lambda b, h, i: (b * nq + i, h)),
        out_shape=jax.ShapeDtypeStruct((batch * seq, W_ATTN), F32),
        compiler_params=_cparams(3),
        name="diff_attn",
    )(*args)


def _dft_tables(seq):
    def cs(n):
        j = np.arange(n)
        ang = 2.0 * np.pi * ((j[:, None] * j[None, :]) % n) / n
        return np.cos(ang) / math.sqrt(n), np.sin(ang) / math.sqrt(n)

    cd, sd = cs(DG_FOUR)
    ct, st = cs(seq)
    w_d = jnp.asarray(np.concatenate([cd, sd], axis=1), F32)
    return w_d.astype(BF16), jnp.asarray(ct, F32).astype(BF16), jnp.asarray(-st, F32).astype(BF16)


def _four_kernel(z_ref, wd_ref, ct_ref, st_ref, o_ref):
    y_cos, y_sin = [], []
    for gidx in range(N_FOUR):
        y = jnp.dot(z_ref[:, gidx * DG_FOUR:(gidx + 1) * DG_FOUR].astype(BF16), wd_ref[...],
                    preferred_element_type=F32)
        y_cos.append(y[:, :DG_FOUR].astype(BF16))
        y_sin.append(y[:, DG_FOUR:].astype(BF16))
    y_cos = jnp.concatenate(y_cos, axis=1)
    y_sin = jnp.concatenate(y_sin, axis=1)
    o_ref[...] = (jnp.dot(ct_ref[...], y_cos, preferred_element_type=F32)
                  + jnp.dot(st_ref[...], y_sin, preferred_element_type=F32))


def _four_call(zf, *, batch, seq):
    w_d, ct, st = _dft_tables(seq)
    return pl.pallas_call(
        _four_kernel,
        grid=(batch,),
        in_specs=[
            pl.BlockSpec((seq, W_FOUR), lambda b: (b, 0)),
            pl.BlockSpec((DG_FOUR, 2 * DG_FOUR), lambda b: (0, 0)),
            pl.BlockSpec((seq, seq), lambda b: (0, 0)),
            pl.BlockSpec((seq, seq), lambda b: (0, 0)),
        ],
        out_specs=pl.BlockSpec((seq, W_FOUR), lambda b: (b, 0)),
        out_shape=jax.ShapeDtypeStruct((batch * seq, W_FOUR), F32),
        compiler_params=_cparams(1),
        name="fourier_mix",
    )(zf, w_d, ct, st)


def _mlstm_kernel(*refs, seq, seeded, emit_state):
    mq_ref, mk_ref, mv_ref, mo_ref, gate_ref, g_ref = refs[:6]
    pos = 6
    if seeded:
        c0_ref, n0_ref, m0_ref = refs[pos:pos + 3]
        pos += 3
    o_ref = refs[pos]
    pos += 1
    if emit_state:
        c1_ref, n1_ref, m1_ref = refs[pos:pos + 3]
        pos += 3
    c_sc, n_sc, m_sc, hf_sc, hb_sc = refs[pos:]

    L = MLSTM_CHUNK
    nc = seq // L

    for d in range(2):
        for h in range(N_MLSTM):
            j = d * N_MLSTM + h
            if seeded:
                c_sc[j] = c0_ref[d, h]
                n_sc[j:j + 1, :] = n0_ref[d, h:h + 1, :]
                m_sc[j:j + 1, :] = jnp.broadcast_to(m0_ref[d:d + 1, h:h + 1], (1, LANES))
            else:
                c_sc[j] = jnp.zeros((DH_MLSTM, DH_MLSTM), F32)
                n_sc[j:j + 1, :] = jnp.zeros((1, LANES), F32)
                m_sc[j:j + 1, :] = jnp.zeros((1, LANES), F32)

    row = lax.broadcasted_iota(jnp.int32, (L, L), 0)
    col = lax.broadcasted_iota(jnp.int32, (L, L), 1)
    masks = (col <= row, col >= row)
    tri = tuple(mk.astype(F32) for mk in masks)

    def gate_terms(d, r0):
        g = gate_ref[pl.ds(r0, L), :]
        cum = jnp.dot(tri[d], g, preferred_element_type=F32, precision=lax.Precision.HIGHEST)
        return g, g.T, cum, cum.T

    def chain_step(d, h, r0, terms):
        j = d * N_MLSTM + h
        rows = pl.ds(r0, L)
        g, g_t, cum, cum_t = terms
        ci = 2 * d * N_MLSTM + h
        cf = ci + N_MLSTM
        icol, irow = g[:, ci:ci + 1], g_t[ci:ci + 1, :]
        bcol, brow = cum[:, cf:cf + 1], cum_t[cf:cf + 1, :]
        last = L - 1 if d == 0 else 0
        btot = bcol[last:last + 1, :]
        m_prev = m_sc[j:j + 1, 0:1]
        n_prev = n_sc[j:j + 1, :]
        c_prev = c_sc[j]

        hs = slice(h * DH_MLSTM, (h + 1) * DH_MLSTM)
        q = mq_ref[rows, hs]
        k = mk_ref[rows, hs] * MLSTM_K_SCALE
        v = mv_ref[rows, hs]
        qb, kb = q.astype(BF16), k.astype(BF16)

        logw = jnp.where(masks[d], bcol - brow + irow, -jnp.inf)
        prev = bcol + m_prev
        m_t = jnp.maximum(prev, jnp.max(logw, axis=-1, keepdims=True))
        w = jnp.exp(logw - m_t)
        sp = jnp.exp(prev - m_t)
        s = _bdot_nt(qb, kb) * w
        num = sp * _bdot_nt(qb, c_prev) + _bdot(s, v)
        den = sp * jnp.sum(q * n_prev, axis=-1, keepdims=True) + jnp.sum(s, axis=-1, keepdims=True)
        hval = num / jnp.maximum(jnp.abs(den), jnp.exp(-m_t))
        m_new = m_t[last:last + 1, :]
        wl = jnp.exp(btot - bcol + icol - m_new)
        decay = jnp.exp(btot + m_prev - m_new)
        c_sc[j] = decay * c_prev + _bdot_tn(v * wl, kb)
        n_sc[j:j + 1, :] = decay * n_prev + jnp.sum(k * wl, axis=0, keepdims=True)
        m_sc[j:j + 1, :] = jnp.broadcast_to(m_new, (1, LANES))
        if d == 0:
            hf_sc[rows, hs] = hval
        else:
            hb_sc[rows, hs] = hval

    def chunk_step(r_fwd, r_bwd):
        t_fwd = gate_terms(0, r_fwd)
        t_bwd = gate_terms(1, r_bwd)
        for h in range(N_MLSTM):
            chain_step(0, h, r_fwd, t_fwd)
            chain_step(1, h, r_bwd, t_bwd)

    if nc <= 2:
        for c in range(nc):
            chunk_step(c * L, (nc - 1 - c) * L)
    else:
        def body(c, carry):
            chunk_step(pl.multiple_of(c * L, L), pl.multiple_of((nc - 1 - c) * L, L))
            return carry
        lax.fori_loop(0, nc, body, 0)

    g_m = g_ref[...]
    for h in range(N_MLSTM):
        hs = slice(h * DH_MLSTM, (h + 1) * DH_MLSTM)
        hsum = hf_sc[:, hs] + hb_sc[:, hs]
        o_ref[:, hs] = _rms(hsum, g_m) * _sigmoid(mo_ref[:, hs])

    if emit_state:
        for d in range(2):
            for h in range(N_MLSTM):
                j = d * N_MLSTM + h
                c1_ref[d, h] = c_sc[j]
                n1_ref[d, h:h + 1, :] = n_sc[j:j + 1, :]
                m1_ref[d:d + 1, h:h + 1] = m_sc[j:j + 1, 0:1]


def _mlstm_call(mq, mk, mv, mo, gates, g_m, state, *, layer, batch, seq, emit_state):
    tok_spec = pl.BlockSpec((seq, W_MLSTM), lambda b: (b, 0))
    in_specs = [tok_spec, tok_spec, tok_spec, tok_spec,
                pl.BlockSpec((seq, LANES), lambda b: (b, 0)),
                pl.BlockSpec((1, DH_MLSTM), lambda b: (0, 0))]
    args = [mq, mk, mv, mo, gates, g_m]
    if state is not None:
        in_specs += [
            pl.BlockSpec((None, None, 2, N_MLSTM, DH_MLSTM, DH_MLSTM), lambda b: (b, layer, 0, 0, 0, 0)),
            pl.BlockSpec((None, None, 2, N_MLSTM, DH_MLSTM), lambda b: (b, layer, 0, 0, 0)),
            pl.BlockSpec((None, None, 2, N_MLSTM), lambda b: (b, layer, 0, 0)),
        ]
        args += list(state)
    out_specs = [tok_spec]
    out_shape = [jax.ShapeDtypeStruct((batch * seq, W_MLSTM), F32)]
    if emit_state:
        out_specs += [
            pl.BlockSpec((None, 2, N_MLSTM, DH_MLSTM, DH_MLSTM), lambda b: (b, 0, 0, 0, 0)),
            pl.BlockSpec((None, 2, N_MLSTM, DH_MLSTM), lambda b: (b, 0, 0, 0)),
            pl.BlockSpec((None, 2, N_MLSTM), lambda b: (b, 0, 0)),
        ]
        out_shape += [
            jax.ShapeDtypeStruct((batch, 2, N_MLSTM, DH_MLSTM, DH_MLSTM), F32),
            jax.ShapeDtypeStruct((batch, 2, N_MLSTM, DH_MLSTM), F32),
            jax.ShapeDtypeStruct((batch, 2, N_MLSTM), F32),
        ]
    n_chain = 2 * N_MLSTM
    return pl.pallas_call(
        functools.partial(_mlstm_kernel, seq=seq, seeded=state is not None, emit_state=emit_state),
        grid=(batch,),
        in_specs=in_specs,
        out_specs=out_specs,
        out_shape=out_shape,
        scratch_shapes=[
            pltpu.VMEM((n_chain, DH_MLSTM, DH_MLSTM), F32),
            pltpu.VMEM((n_chain, LANES), F32),
            pltpu.VMEM((n_chain, LANES), F32),
            pltpu.VMEM((seq, W_MLSTM), F32),
            pltpu.VMEM((seq, W_MLSTM), F32),
        ],
        compiler_params=_cparams(1),
        name="mlstm",
    )(*args)


def _merge_kernel(x_ref, a_ref, f_ref, m_ref, mod_ref, g_ref, wg_ref, wa_ref, wf_ref, wm_ref, wo_ref, o_ref):
    x = x_ref[...]
    mod = mod_ref[...]
    u = _modulated_norm(x, g_ref[...], mod, 3).astype(BF16)
    gb = _sigmoid(jnp.dot(u, wg_ref[...], preferred_element_type=F32))
    merged = (gb[:, :D_MODEL] * _bdot(a_ref[...], wa_ref[...])
              + gb[:, D_MODEL:2 * D_MODEL] * _bdot(f_ref[...], wf_ref[...])
              + gb[:, 2 * D_MODEL:] * _bdot(m_ref[...], wm_ref[...]))
    out = _bdot(merged, wo_ref[...])
    o_ref[...] = x + mod[5:6] * out


def _merge_call(x, a, f, m, mods, g, wg, wa, wf, wm, wo, *, tm, seq_tiles, mod_row0):
    n = x.shape[0]
    full = lambda shape: pl.BlockSpec(shape, lambda i: (0,) * len(shape))
    br_spec = pl.BlockSpec((tm, W_ATTN), lambda i: (i, 0))
    return pl.pallas_call(
        _merge_kernel,
        grid=(n // tm,),
        in_specs=[
            pl.BlockSpec((tm, D_MODEL), lambda i: (i, 0)),
            br_spec, br_spec, br_spec,
            _mod_spec(seq_tiles, mod_row0),
            full((1, D_MODEL)),
            full((D_MODEL, 3 * D_MODEL)),
            full((W_ATTN, D_MODEL)), full((W_FOUR, D_MODEL)), full((W_MLSTM, D_MODEL)),
            full((D_MODEL, D_MODEL)),
        ],
        out_specs=pl.BlockSpec((tm, D_MODEL), lambda i: (i, 0)),
        out_shape=jax.ShapeDtypeStruct((n, D_MODEL), F32),
        compiler_params=_cparams(1),
        name="merge",
    )(x, a, f, m, mods, g, wg, wa, wf, wm, wo)


def _rope_tables(n_tok):
    tok = np.arange(n_tok)
    inv = ROPE_BASE ** (-np.arange(ROPE_AXIS_PAIRS, dtype=np.float32) / ROPE_AXIS_PAIRS)
    ang = np.concatenate([(tok // GRID_W).astype(np.float32)[:, None] * inv,
                          (tok % GRID_W).astype(np.float32)[:, None] * inv], axis=-1).astype(np.float32)
    c, s = np.cos(ang), np.sin(ang)
    cos_t = np.concatenate([c, c, c, c], axis=-1)
    sin_t = np.concatenate([-s, s, -s, s], axis=-1)
    return jnp.asarray(cos_t, F32), jnp.asarray(sin_t, F32)


def _trunk(x, group, l, mods, wl, cache, state):
    batch, seq, mod_row0, per_seq_mod = group
    tm = 512
    seq_tiles = (seq // tm if seq >= tm else 0) if per_seq_mod else 0
    if per_seq_mod and seq < tm:
        raise ValueError("row tile spans several conditioned sequences")
    lam_init = 0.8 - 0.6 * math.exp(-0.3 * l)
    x = _ffn_call(x, mods, wl['g0'], wl['g_final'], wl['ffn1_in'], wl['ffn1_out'],
                  base=0, final=False, tm=tm, seq_tiles=seq_tiles, mod_row0=mod_row0)
    rope = cache is not None
    cos_t, sin_t = _rope_tables(seq)
    q, k, v, zf, mq, mk, mv, mo, gates = _proj_call(
        x, mods, wl['g1'], wl['w_in'], wl['b_gate'], cos_t, sin_t,
        batch=batch, seq=seq, tm=256, mod_row0=mod_row0, per_seq_mod=per_seq_mod, rope=rope)
    a = _attn_call(q, k, v, cache, wl['attn_lambda'], wl['g_attn_sub'], layer=l, lam_init=lam_init, tq=256)
    f = _four_call(zf, batch=batch, seq=seq)
    m_res = _mlstm_call(mq, mk, mv, mo, gates, wl['g_mlstm'], state, layer=l, batch=batch, seq=seq,
                        emit_state=state is None)
    x = _merge_call(x, a, f, m_res[0], mods, wl['g1'], wl['w_bg'], wl['w_br_a'], wl['w_br_f'],
                    wl['w_br_m'], wl['w_out'], tm=tm, seq_tiles=seq_tiles, mod_row0=mod_row0)
    x = _ffn_call(x, mods, wl['g2'], wl['g_final'], wl['ffn2_in'], wl['ffn2_out'],
                  base=6, final=(l == DEPTH - 1), tm=tm, seq_tiles=seq_tiles, mod_row0=mod_row0)
    return x, (k, v) + tuple(m_res[1:])


def kernel(x_prompt, x_sample, cache_k, cache_v, state_C, state_n, state_m, c, c_ctx, w_ada, b_ada, g_norm,
           w_ffn1_in, w_ffn1_out, w_ffn2_in, w_ffn2_out, w_in, b_mgate, attn_lambda, g_attn_sub, g_mlstm,
           w_branch_gate, w_br_attn, w_br_four, w_br_mlstm, w_out, g_final):
    batch, seq, _ = x_prompt.shape
    dec_batch, dec_seq, _ = x_sample.shape
    cond = jnp.zeros((N_COND, D_MODEL), F32).at[0].set(c_ctx).at[1:1 + dec_batch].set(c)
    mods_all = _ada_call(cond, w_ada, b_ada).reshape(DEPTH, N_COND, N_MOD, D_MODEL)

    hp = x_prompt.reshape(batch * seq, D_MODEL)
    hs = x_sample.reshape(dec_batch * dec_seq, D_MODEL)
    prompt_group = (batch, seq, 0, False)
    sample_group = (dec_batch, dec_seq, 1, True)
    ctx_out = []
    for l in range(DEPTH):
        w_in_pad = jnp.pad(w_in[l], ((0, 0), (0, P_PAD - w_in.shape[-1]))).astype(BF16)
        wl = {
            'g0': g_norm[l, 0:1], 'g1': g_norm[l, 1:2], 'g2': g_norm[l, 2:3], 'g_final': g_final[None, :],
            'ffn1_in': w_ffn1_in[l].astype(BF16), 'ffn1_out': w_ffn1_out[l].astype(BF16),
            'ffn2_in': w_ffn2_in[l].astype(BF16), 'ffn2_out': w_ffn2_out[l].astype(BF16),
            'w_in': w_in_pad,
            'b_gate': jnp.pad(b_mgate[l], (0, LANES - N_GATE))[None, :],
            'attn_lambda': attn_lambda[l], 'g_attn_sub': g_attn_sub[l][None, :],
            'g_mlstm': g_mlstm[l][None, :],
            'w_bg': w_branch_gate[l].astype(BF16), 'w_br_a': w_br_attn[l].astype(BF16),
            'w_br_f': w_br_four[l].astype(BF16), 'w_br_m': w_br_mlstm[l].astype(BF16),
            'w_out': w_out[l].astype(BF16),
        }
        hp, ctx_l = _trunk(hp, prompt_group, l, mods_all[l], wl, None, None)
        hs, _ = _trunk(hs, sample_group, l, mods_all[l], wl, (cache_k, cache_v), (state_C, state_n, state_m))
        ctx_out.append(ctx_l)
    outs = [jnp.stack([ctx_out[l][i] for l in range(DEPTH)], axis=1) for i in range(5)]
    return (hp.reshape(batch, seq, D_MODEL), hs.reshape(dec_batch, dec_seq, D_MODEL), *outs)
```

```python
import functools
import math

import numpy as np
import jax
import jax.numpy as jnp
from jax import lax
from jax.experimental import pallas as pl
from jax.experimental.pallas import tpu as pltpu

D_MODEL = 1024
DEPTH = 2
GRID_W = 64
N_ATTN = 4
DH_ATTN = 64
DV_ATTN = 2 * DH_ATTN
W_ATTN = N_ATTN * DV_ATTN
N_FOUR = 4
DG_FOUR = 128
W_FOUR = N_FOUR * DG_FOUR
N_MLSTM = 4
DH_MLSTM = 128
W_MLSTM = N_MLSTM * DH_MLSTM
N_GATE = 4 * N_MLSTM
LANES = 128
SUBLANES = 16
MV_LO = 3 * W_ATTN + W_FOUR + 2 * W_MLSTM
MV_HI = MV_LO + W_MLSTM
P_MAIN = MV_HI + W_MLSTM
P_ROW = P_MAIN - W_MLSTM + LANES
P_COL = W_MLSTM + LANES
D_FF = 2816
N_MOD = 9
N_COND = 8
ROPE_BASE = 10000.0
ROPE_AXIS_PAIRS = DH_ATTN // 4
ATTN_SCALE = DH_ATTN ** -0.5
MLSTM_K_SCALE = DH_MLSTM ** -0.5
EPS = 1e-6
MLSTM_CHUNK = 256
VMEM_LIMIT = 56 * 1024 * 1024

F32 = jnp.float32
BF16 = jnp.bfloat16
HIGHEST = lax.Precision.HIGHEST


def _cparams(n_grid):
    return pltpu.CompilerParams(dimension_semantics=("arbitrary",) * n_grid,
                                vmem_limit_bytes=VMEM_LIMIT)


def _bdot(a, b):
    return jnp.dot(a.astype(BF16), b.astype(BF16), preferred_element_type=F32)


def _bdot_nt(a, b, precision=None):
    if precision is None:
        a, b = a.astype(BF16), b.astype(BF16)
    return lax.dot_general(a, b, (((1,), (1,)), ((), ())), preferred_element_type=F32,
                           precision=precision)


def _sigmoid(x):
    return 1.0 / (1.0 + jnp.exp(-x))


def _log_sigmoid(x):
    return jnp.minimum(x, 0.0) - jnp.log1p(jnp.exp(-jnp.abs(x)))


def _rms(x, g):
    return x * lax.rsqrt(jnp.mean(x * x, axis=-1, keepdims=True) + EPS) * g


def _modulated_norm(x, g, mod, base):
    return _rms(x, g) * (1.0 + mod[base + 1:base + 2]) + mod[base:base + 1]


def _layer_spec(shape, layer):
    return pl.BlockSpec((None,) + shape, lambda *_: (layer,) + (0,) * len(shape))


def _mod_spec(seq_tiles, mod_row0):
    if seq_tiles == 0:
        return pl.BlockSpec((None, N_MOD, D_MODEL), lambda i: (mod_row0, 0, 0))
    return pl.BlockSpec((None, N_MOD, D_MODEL), lambda i: (mod_row0 + i // seq_tiles, 0, 0))


def _ada_kernel(c_ref, w_ref, b_ref, o_ref):
    c = c_ref[...]
    s = c * _sigmoid(c)
    o_ref[...] = _bdot(s, w_ref[...]) + b_ref[...]


def _ada_call(cond, w_ada, b_ada):
    tn = 1024
    n_out = N_MOD * D_MODEL
    return pl.pallas_call(
        _ada_kernel,
        grid=(DEPTH, n_out // tn),
        in_specs=[
            pl.BlockSpec((N_COND, D_MODEL), lambda l, j: (0, 0)),
            pl.BlockSpec((None, D_MODEL, tn), lambda l, j: (l, 0, j)),
            pl.BlockSpec((None, 1, tn), lambda l, j: (l, 0, j)),
        ],
        out_specs=pl.BlockSpec((None, N_COND, tn), lambda l, j: (l, 0, j)),
        out_shape=jax.ShapeDtypeStruct((DEPTH, N_COND, n_out), F32),
        compiler_params=_cparams(2),
        name="adaln",
    )(cond, w_ada, b_ada.reshape(DEPTH, 1, n_out))


FF_CHUNK = D_FF // 2


def _ffn_kernel(x_ref, mod_ref, g_ref, gf_ref, win_ref, wout_ref, o_ref, *, base, final):
    x = x_ref[...]
    mod = mod_ref[...]
    u = _modulated_norm(x, g_ref[...], mod, base).astype(BF16)
    y = None
    for c in range(D_FF // FF_CHUNK):
        lo = c * FF_CHUNK
        a = jnp.dot(u, win_ref[:, lo:lo + FF_CHUNK], preferred_element_type=F32)
        g = jnp.dot(u, win_ref[:, D_FF + lo:D_FF + lo + FF_CHUNK], preferred_element_type=F32)
        hh = (a * _sigmoid(a) * g).astype(BF16)
        part = jnp.dot(hh, wout_ref[lo:lo + FF_CHUNK, :], preferred_element_type=F32)
        y = part if y is None else y + part
    xn = x + 0.5 * mod[base + 2:base + 3] * y
    if final:
        xn = _rms(xn, gf_ref[...])
    o_ref[...] = xn


def _ffn_call(x, mods, g_norm, g_final, w_in, w_out, *, layer, sub, final, tm, seq_tiles, mod_row0):
    n = x.shape[0]
    return pl.pallas_call(
        functools.partial(_ffn_kernel, base=3 * sub, final=final),
        grid=(n // tm,),
        in_specs=[
            pl.BlockSpec((tm, D_MODEL), lambda i: (i, 0)),
            _mod_spec(seq_tiles, mod_row0),
            pl.BlockSpec((None, None, 1, D_MODEL), lambda i: (layer, sub, 0, 0)),
            pl.BlockSpec((1, D_MODEL), lambda i: (0, 0)),
            _layer_spec((D_MODEL, 2 * D_FF), layer),
            _layer_spec((D_FF, D_MODEL), layer),
        ],
        out_specs=pl.BlockSpec((tm, D_MODEL), lambda i: (i, 0)),
        out_shape=jax.ShapeDtypeStruct((n, D_MODEL), F32),
        compiler_params=_cparams(1),
        name="ffn",
    )(x, mods, g_norm, g_final, w_in, w_out)


def _swap32(x):
    lane = lax.broadcasted_iota(jnp.int32, x.shape, 1)
    return jnp.where((lane & (DH_ATTN // 2)) == 0,
                     pltpu.roll(x, LANES - DH_ATTN // 2, 1), pltpu.roll(x, DH_ATTN // 2, 1))


def _proj_kernel(*refs, rope, n_alias):
    (x_ref, mod_ref, g_ref, w_ref, wt_ref, bg_ref, bgt_ref, cos_ref, sin_ref) = refs[:9]
    (q_ref, k_ref, v_ref, zf_ref, mq_ref, mk_ref, mo_ref, gate_ref, vt_ref, gt_ref) = refs[9 + n_alias:]
    x = x_ref[...]
    u = _modulated_norm(x, g_ref[...], mod_ref[...], 3).astype(BF16)
    z = jnp.dot(u, w_ref[...], preferred_element_type=F32)
    zt = _bdot_nt(wt_ref[...], u)
    if rope:
        cos_t = cos_ref[...]
        sin_t = sin_ref[...]
    for h in range(N_ATTN):
        zq = z[:, h * DV_ATTN:(h + 1) * DV_ATTN]
        zk = z[:, W_ATTN + h * DV_ATTN:W_ATTN + (h + 1) * DV_ATTN]
        if rope:
            zq = zq * cos_t + _swap32(zq) * sin_t
            zk = zk * cos_t + _swap32(zk) * sin_t
        q_ref[h] = zq
        k_ref[h] = zk
        v_ref[h] = z[:, 2 * W_ATTN + h * DV_ATTN:2 * W_ATTN + (h + 1) * DV_ATTN]
    off = 3 * W_ATTN
    zf_ref[...] = z[:, off:off + W_FOUR]
    off += W_FOUR
    mq_ref[...] = z[:, off:off + W_MLSTM]
    mk_ref[...] = z[:, off + W_MLSTM:off + 2 * W_MLSTM]
    mo_ref[...] = z[:, off + 2 * W_MLSTM:off + 3 * W_MLSTM]
    gp = z[:, off + 3 * W_MLSTM:] + bg_ref[...]
    lane = lax.broadcasted_iota(jnp.int32, gp.shape, 1)
    gate_ref[...] = jnp.where((lane < N_GATE) & ((lane & N_MLSTM) != 0), _log_sigmoid(gp), gp)
    vt_ref[...] = zt[:W_MLSTM]
    gpt = zt[W_MLSTM:] + bgt_ref[...]
    sub = lax.broadcasted_iota(jnp.int32, gpt.shape, 0)
    gt_ref[...] = jnp.where((sub < N_GATE) & ((sub & N_MLSTM) != 0), _log_sigmoid(gpt), gpt)


def _proj_call(x, mods, g_norm, w_row, w_col, bg, bgt, cos_t, sin_t, kv_prev, *, layer, batch, seq, tm,
               mod_row0, per_seq_mod, rope, kv_stacked):
    n = x.shape[0]
    tps = seq // tm
    head_shape = jax.ShapeDtypeStruct((batch, N_ATTN, seq, DV_ATTN), F32)
    head_spec = pl.BlockSpec((None, N_ATTN, tm, DV_ATTN), lambda i: (i // tps, 0, i % tps, 0))
    if kv_stacked:
        kv_shape = jax.ShapeDtypeStruct((batch, DEPTH, N_ATTN, seq, DV_ATTN), F32)
        kv_spec = pl.BlockSpec((None, None, N_ATTN, tm, DV_ATTN), lambda i: (i // tps, layer, 0, i % tps, 0))
    else:
        kv_shape, kv_spec = head_shape, head_spec
    tok_shape = jax.ShapeDtypeStruct((n, W_ATTN), F32)
    tok_spec = pl.BlockSpec((tm, W_ATTN), lambda i: (i, 0))
    rope_spec = pl.BlockSpec((tm, LANES), lambda i: (i % tps, 0))
    alias_in = list(kv_prev) if kv_prev is not None else []
    n_in = 9
    return pl.pallas_call(
        functools.partial(_proj_kernel, rope=rope, n_alias=len(alias_in)),
        grid=(n // tm,),
        in_specs=[
            pl.BlockSpec((tm, D_MODEL), lambda i: (i, 0)),
            _mod_spec(tps if per_seq_mod else 0, mod_row0),
            pl.BlockSpec((None, None, 1, D_MODEL), lambda i: (layer, 1, 0, 0)),
            _layer_spec((D_MODEL, P_ROW), layer),
            _layer_spec((P_COL, D_MODEL), layer),
            _layer_spec((1, LANES), layer),
            _layer_spec((LANES, 1), layer),
            rope_spec, rope_spec,
        ] + [pl.BlockSpec(memory_space=pl.ANY)] * len(alias_in),
        out_specs=[head_spec, kv_spec, kv_spec, tok_spec, tok_spec, tok_spec, tok_spec,
                   pl.BlockSpec((tm, LANES), lambda i: (i, 0)),
                   pl.BlockSpec((W_MLSTM, tm), lambda i: (0, i)),
                   pl.BlockSpec((LANES, tm), lambda i: (0, i))],
        out_shape=[head_shape, kv_shape, kv_shape, tok_shape, tok_shape, tok_shape, tok_shape,
                   jax.ShapeDtypeStruct((n, LANES), F32),
                   jax.ShapeDtypeStruct((W_MLSTM, n), F32),
                   jax.ShapeDtypeStruct((LANES, n), F32)],
        input_output_aliases={n_in + j: 1 + j for j in range(len(alias_in))},
        compiler_params=_cparams(1),
        name="mixer_proj",
    )(x, mods, g_norm, w_row, w_col, bg, bgt, cos_t, sin_t, *alias_in)


def _attn_kernel(*refs, lam_init, cached):
    if cached:
        q_ref, k_ref, v_ref, ck_ref, cv_ref, lam_ref, g_ref, o_ref = refs
    else:
        q_ref, k_ref, v_ref, lam_ref, g_ref, o_ref = refs
    lp = lam_ref[...]
    lam = (jnp.exp(jnp.sum(lp[0:1] * lp[1:2], axis=-1, keepdims=True))
           - jnp.exp(jnp.sum(lp[2:3] * lp[3:4], axis=-1, keepdims=True)) + lam_init)
    q = q_ref[...]
    lane = lax.broadcasted_iota(jnp.int32, q.shape, 1)
    q_maps = (jnp.where(lane < DH_ATTN, q, 0.0).astype(BF16),
              jnp.where(lane >= DH_ATTN, q, 0.0).astype(BF16))
    keys = [k_ref[...].astype(BF16)]
    vals = [v_ref[...].astype(BF16)]
    if cached:
        keys.insert(0, ck_ref[...].astype(BF16))
        vals.insert(0, cv_ref[...].astype(BF16))
    probs = []
    for qm in q_maps:
        s = [_bdot_nt(qm, kk) * ATTN_SCALE for kk in keys]
        m = functools.reduce(jnp.maximum, [jnp.max(si, axis=-1, keepdims=True) for si in s])
        e = [jnp.exp(si - m) for si in s]
        den = functools.reduce(jnp.add, [jnp.sum(ei, axis=-1, keepdims=True) for ei in e])
        probs.append([ei / den for ei in e])
    o = None
    for j, vv in enumerate(vals):
        p = probs[0][j] - lam * probs[1][j]
        part = jnp.dot(p.astype(BF16), vv, preferred_element_type=F32)
        o = part if o is None else o + part
    o_ref[...] = _rms(o, g_ref[...]) * (1.0 - lam_init)


def _head_spec(arr, rows, layer):
    if arr.ndim == 5:
        return pl.BlockSpec((None, None, None, rows, DV_ATTN), lambda b, h, i: (b, layer, h, 0, 0))
    return pl.BlockSpec((None, None, rows, DV_ATTN), lambda b, h, i: (b, h, 0, 0))


def _attn_call(q, k, v, cache, lam_p, g_sub, *, layer, lam_init, tq):
    batch, _, seq, _ = q.shape
    nq = seq // tq
    in_specs = [pl.BlockSpec((None, None, tq, DV_ATTN), lambda b, h, i: (b, h, i, 0)),
                _head_spec(k, seq, layer), _head_spec(v, seq, layer)]
    args = [q, k, v]
    if cache is not None:
        past = cache[0].shape[3]
        in_specs += [_head_spec(cache[0], past, layer), _head_spec(cache[1], past, layer)]
        args += list(cache)
    in_specs += [pl.BlockSpec((None, 4, DH_ATTN), lambda b, h, i: (layer, 0, 0)),
                 pl.BlockSpec((None, 1, DV_ATTN), lambda b, h, i: (layer, 0, 0))]
    args += [lam_p, g_sub]
    return pl.pallas_call(
        functools.partial(_attn_kernel, lam_init=lam_init, cached=cache is not None),
        grid=(batch, N_ATTN, nq),
        in_specs=in_specs,
        out_specs=pl.BlockSpec((tq, DV_ATTN), lambda b, h, i: (b * nq + i, h)),
        out_shape=jax.ShapeDtypeStruct((batch * seq, W_ATTN), F32),
        compiler_params=_cparams(3),
        name="diff_attn",
    )(*args)


def _dft_tables(seq):
    def cs(n):
        j = np.arange(n)
        ang = 2.0 * np.pi * ((j[:, None] * j[None, :]) % n) / n
        return np.cos(ang) / math.sqrt(n), np.sin(ang) / math.sqrt(n)

    cd, sd = cs(DG_FOUR)
    ct, st = cs(seq)
    w_d = jnp.asarray(np.concatenate([cd, sd], axis=1), F32)
    return w_d.astype(BF16), jnp.asarray(ct, F32).astype(BF16), jnp.asarray(-st, F32).astype(BF16)


def _four_kernel(z_ref, wd_ref, ct_ref, st_ref, o_ref):
    y_cos, y_sin = [], []
    for gidx in range(N_FOUR):
        y = jnp.dot(z_ref[:, gidx * DG_FOUR:(gidx + 1) * DG_FOUR].astype(BF16), wd_ref[...],
                    preferred_element_type=F32)
        y_cos.append(y[:, :DG_FOUR].astype(BF16))
        y_sin.append(y[:, DG_FOUR:].astype(BF16))
    y_cos = jnp.concatenate(y_cos, axis=1)
    y_sin = jnp.concatenate(y_sin, axis=1)
    o_ref[...] = (jnp.dot(ct_ref[...], y_cos, preferred_element_type=F32)
                  + jnp.dot(st_ref[...], y_sin, preferred_element_type=F32))


def _four_call(zf, *, batch, seq):
    w_d, ct, st = _dft_tables(seq)
    return pl.pallas_call(
        _four_kernel,
        grid=(batch,),
        in_specs=[
            pl.BlockSpec((seq, W_FOUR), lambda b: (b, 0)),
            pl.BlockSpec((DG_FOUR, 2 * DG_FOUR), lambda b: (0, 0)),
            pl.BlockSpec((seq, seq), lambda b: (0, 0)),
            pl.BlockSpec((seq, seq), lambda b: (0, 0)),
        ],
        out_specs=pl.BlockSpec((seq, W_FOUR), lambda b: (b, 0)),
        out_shape=jax.ShapeDtypeStruct((batch * seq, W_FOUR), F32),
        compiler_params=_cparams(1),
        name="fourier_mix",
    )(zf, w_d, ct, st)


def _mlstm_kernel(*refs, seq, nb, seeded, emit_state, n_alias):
    mq_ref, mk_ref, vt_ref, mo_ref, gate_ref, gt_ref, g_ref = refs[:7]
    pos = 7
    if seeded:
        c0_ref, n0_ref, m0_ref = refs[pos:pos + 3]
        pos += 3
    pos += n_alias
    o_ref = refs[pos]
    if emit_state:
        c1_ref, n1_ref, m1_ref = refs[pos + 1:pos + 4]

    L = min(MLSTM_CHUNK, seq)
    nc = seq // L
    s_idx = lax.broadcasted_iota(jnp.int32, (L, L), 0)
    t_idx = lax.broadcasted_iota(jnp.int32, (L, L), 1)
    before = (s_idx <= t_idx, s_idx >= t_idx)
    tri = (jnp.where(before[1], 1.0, 0.0), jnp.where(before[0], 1.0, 0.0))
    chains = [(bi, d, h) for bi in range(nb) for d in range(2) for h in range(N_MLSTM)]

    state = {}
    for (bi, d, h) in chains:
        if seeded:
            state[bi, d, h] = (c0_ref[bi, d, h], n0_ref[bi, d, h:h + 1, :], m0_ref[bi, d:d + 1, h:h + 1])
        else:
            state[bi, d, h] = (None, None, jnp.zeros((1, 1), F32))

    h_t = {}
    for c in range(nc):
        terms = {}
        for bi in range(nb):
            for d in range(2):
                r0 = bi * seq + (c if d == 0 else nc - 1 - c) * L
                g = gate_ref[r0:r0 + L, :]
                g_t = gt_ref[:, r0:r0 + L]
                cum = jnp.dot(tri[d], g, preferred_element_type=F32, precision=HIGHEST)
                cum_t = _bdot_nt(g_t, tri[d], precision=HIGHEST)
                col = g - pltpu.roll(cum, LANES - N_MLSTM, 1)
                terms[bi, d] = (r0, col, g_t, cum_t)

        new_state = {}
        for (bi, d, h) in chains:
            r0, col, g_t, cum_t = terms[bi, d]
            c_prev, n_prev, m_prev = state[bi, d, h]
            has_state = c_prev is not None
            need_update = emit_state or c < nc - 1
            ci = 2 * d * N_MLSTM + h
            cf = ci + N_MLSTM
            last = L - 1 if d == 0 else 0
            hs = slice(h * DH_MLSTM, (h + 1) * DH_MLSTM)
            q = mq_ref[r0:r0 + L, hs].astype(BF16)
            k = (mk_ref[r0:r0 + L, hs] * MLSTM_K_SCALE).astype(BF16)
            v_t = vt_ref[hs, r0:r0 + L]
            i_row, b_row = g_t[ci:ci + 1, :], cum_t[cf:cf + 1, :]

            cb = jnp.where(before[d], col[:, ci:ci + 1], -jnp.inf)
            m_row = jnp.maximum(jnp.max(cb, axis=0, keepdims=True), m_prev)
            s_t = _bdot_nt(k, q) * jnp.exp(cb - m_row)
            num_t = jnp.dot(v_t.astype(BF16), s_t.astype(BF16), preferred_element_type=F32)
            den = jnp.sum(s_t, axis=0, keepdims=True)
            if has_state:
                sp = jnp.exp(m_prev - m_row)
                cn = jnp.concatenate([c_prev, jnp.broadcast_to(n_prev, (SUBLANES, DH_MLSTM))], axis=0)
                cq = _bdot_nt(cn, q)
                num_t = num_t + sp * cq[:DH_MLSTM]
                den = den + sp * cq[DH_MLSTM:DH_MLSTM + 1]
            m_t = b_row + m_row
            h_t[bi, d, h, c] = num_t / jnp.maximum(jnp.abs(den), jnp.exp(-m_t))
            if need_update:
                b_tot = b_row[:, last:last + 1]
                m_new = m_t[:, last:last + 1]
                wl = jnp.exp(b_tot + (i_row - b_row) - m_new)
                vw = jnp.concatenate([v_t * wl, jnp.broadcast_to(wl, (SUBLANES, L))], axis=0)
                upd = jnp.dot(vw.astype(BF16), k, preferred_element_type=F32)
                c_new, n_new = upd[:DH_MLSTM], upd[DH_MLSTM:DH_MLSTM + 1]
                if has_state:
                    decay = jnp.exp(b_tot + m_prev - m_new)
                    c_new = decay * c_prev + c_new
                    n_new = decay * n_prev + n_new
                new_state[bi, d, h] = (c_new, n_new, m_new)
        state = new_state

    g_m = g_ref[...]
    for bi in range(nb):
        for h in range(N_MLSTM):
            hs = slice(h * DH_MLSTM, (h + 1) * DH_MLSTM)
            fwd = [h_t[bi, 0, h, c] for c in range(nc)]
            bwd = [h_t[bi, 1, h, nc - 1 - c] for c in range(nc)]
            hsum = (fwd[0] if nc == 1 else jnp.concatenate(fwd, axis=1)) \
                + (bwd[0] if nc == 1 else jnp.concatenate(bwd, axis=1))
            y = hsum * lax.rsqrt(jnp.mean(hsum * hsum, axis=0, keepdims=True) + EPS)
            rows = slice(bi * seq, (bi + 1) * seq)
            o_ref[rows, hs] = y.T * g_m * _sigmoid(mo_ref[rows, hs])

    if emit_state:
        for (bi, d, h) in chains:
            c_fin, n_fin, m_fin = state[bi, d, h]
            c1_ref[bi, d, h] = c_fin
            n1_ref[bi, d, h:h + 1, :] = n_fin
            m1_ref[bi, d:d + 1, h:h + 1] = m_fin


def _mlstm_call(mq, mk, vt, mo, gates, gates_t, g_m, state, state_prev, *, layer, batch, seq, nb, emit_state):
    rows = nb * seq
    tok_spec = pl.BlockSpec((rows, W_MLSTM), lambda b: (b, 0))
    in_specs = [tok_spec, tok_spec,
                pl.BlockSpec((W_MLSTM, rows), lambda b: (0, b)),
                tok_spec,
                pl.BlockSpec((rows, LANES), lambda b: (b, 0)),
                pl.BlockSpec((LANES, rows), lambda b: (0, b)),
                pl.BlockSpec((None, 1, DH_MLSTM), lambda b: (layer, 0, 0))]
    args = [mq, mk, vt, mo, gates, gates_t, g_m]
    state_specs = [
        pl.BlockSpec((nb, None, 2, N_MLSTM, DH_MLSTM, DH_MLSTM), lambda b: (b, layer, 0, 0, 0, 0)),
        pl.BlockSpec((nb, None, 2, N_MLSTM, DH_MLSTM), lambda b: (b, layer, 0, 0, 0)),
        pl.BlockSpec((nb, None, 2, N_MLSTM), lambda b: (b, layer, 0, 0)),
    ]
    if state is not None:
        in_specs += state_specs
        args += list(state)
    alias_in = list(state_prev) if state_prev is not None else []
    n_in = len(args)
    in_specs += [pl.BlockSpec(memory_space=pl.ANY)] * len(alias_in)
    args += alias_in
    out_specs = [tok_spec]
    out_shape = [jax.ShapeDtypeStruct((batch * seq, W_MLSTM), F32)]
    if emit_state:
        out_specs += state_specs
        out_shape += [
            jax.ShapeDtypeStruct((batch, DEPTH, 2, N_MLSTM, DH_MLSTM, DH_MLSTM), F32),
            jax.ShapeDtypeStruct((batch, DEPTH, 2, N_MLSTM, DH_MLSTM), F32),
            jax.ShapeDtypeStruct((batch, DEPTH, 2, N_MLSTM), F32),
        ]
    return pl.pallas_call(
        functools.partial(_mlstm_kernel, seq=seq, nb=nb, seeded=state is not None, emit_state=emit_state,
                          n_alias=len(alias_in)),
        grid=(batch // nb,),
        in_specs=in_specs,
        out_specs=out_specs,
        out_shape=out_shape,
        input_output_aliases={n_in + j: 1 + j for j in range(len(alias_in))},
        compiler_params=_cparams(1),
        name="mlstm",
    )(*args)


def _merge_kernel(x_ref, a_ref, f_ref, m_ref, mod_ref, g_ref, wg_ref, wa_ref, wf_ref, wm_ref, wo_ref, o_ref):
    x = x_ref[...]
    mod = mod_ref[...]
    u = _modulated_norm(x, g_ref[...], mod, 3).astype(BF16)
    gb = _sigmoid(jnp.dot(u, wg_ref[...], preferred_element_type=F32))
    merged = (gb[:, :D_MODEL] * _bdot(a_ref[...], wa_ref[...])
              + gb[:, D_MODEL:2 * D_MODEL] * _bdot(f_ref[...], wf_ref[...])
              + gb[:, 2 * D_MODEL:] * _bdot(m_ref[...], wm_ref[...]))
    out = _bdot(merged, wo_ref[...])
    o_ref[...] = x + mod[5:6] * out


def _merge_call(x, a, f, m, mods, g_norm, wg, wa, wf, wm, wo, *, layer, tm, seq_tiles, mod_row0):
    n = x.shape[0]
    br_spec = pl.BlockSpec((tm, W_ATTN), lambda i: (i, 0))
    return pl.pallas_call(
        _merge_kernel,
        grid=(n // tm,),
        in_specs=[
            pl.BlockSpec((tm, D_MODEL), lambda i: (i, 0)),
            br_spec, br_spec, br_spec,
            _mod_spec(seq_tiles, mod_row0),
            pl.BlockSpec((None, None, 1, D_MODEL), lambda i: (layer, 1, 0, 0)),
            _layer_spec((D_MODEL, 3 * D_MODEL), layer),
            _layer_spec((W_ATTN, D_MODEL), layer), _layer_spec((W_FOUR, D_MODEL), layer),
            _layer_spec((W_MLSTM, D_MODEL), layer),
            _layer_spec((D_MODEL, D_MODEL), layer),
        ],
        out_specs=pl.BlockSpec((tm, D_MODEL), lambda i: (i, 0)),
        out_shape=jax.ShapeDtypeStruct((n, D_MODEL), F32),
        compiler_params=_cparams(1),
        name="merge",
    )(x, a, f, m, mods, g_norm, wg, wa, wf, wm, wo)


def _rope_tables(n_tok):
    tok = np.arange(n_tok)
    inv = ROPE_BASE ** (-np.arange(ROPE_AXIS_PAIRS, dtype=np.float32) / ROPE_AXIS_PAIRS)
    ang = np.concatenate([(tok // GRID_W).astype(np.float32)[:, None] * inv,
                          (tok % GRID_W).astype(np.float32)[:, None] * inv], axis=-1).astype(np.float32)
    c, s = np.cos(ang), np.sin(ang)
    cos_t = np.concatenate([c, c, c, c], axis=-1)
    sin_t = np.concatenate([-s, s, -s, s], axis=-1)
    return jnp.asarray(cos_t, F32), jnp.asarray(sin_t, F32)


def _trunk(x, group, l, mods, w, cache, state, ctx_prev):
    batch, seq, mod_row0, per_seq_mod, nb = group
    tm = 512
    if per_seq_mod and seq % tm:
        raise ValueError("row tile spans several conditioned sequences")
    seq_tiles = seq // tm if per_seq_mod else 0
    lam_init = 0.8 - 0.6 * math.exp(-0.3 * l)
    is_ctx = cache is None
    x = _ffn_call(x, mods, w['g_norm'], w['g_final'], w['ffn1_in'], w['ffn1_out'], layer=l, sub=0,
                  final=False, tm=tm, seq_tiles=seq_tiles, mod_row0=mod_row0)
    cos_t, sin_t = _rope_tables(seq)
    q, k, v, zf, mq, mk, mo, gates, vt, gates_t = _proj_call(
        x, mods, w['g_norm'], w['w_row'], w['w_col'], w['b_gate'], w['b_gate_t'], cos_t, sin_t,
        ctx_prev[:2] if ctx_prev is not None else None,
        layer=l, batch=batch, seq=seq, tm=256, mod_row0=mod_row0, per_seq_mod=per_seq_mod,
        rope=not is_ctx, kv_stacked=is_ctx)
    a = _attn_call(q, k, v, cache, w['attn_lambda'], w['g_attn_sub'], layer=l, lam_init=lam_init, tq=256)
    f = _four_call(zf, batch=batch, seq=seq)
    m_res = _mlstm_call(mq, mk, vt, mo, gates, gates_t, w['g_mlstm'], state,
                        ctx_prev[2:] if ctx_prev is not None else None,
                        layer=l, batch=batch, seq=seq, nb=nb, emit_state=is_ctx)
    x = _merge_call(x, a, f, m_res[0], mods, w['g_norm'], w['w_bg'], w['w_br_a'], w['w_br_f'], w['w_br_m'],
                    w['w_out'], layer=l, tm=tm, seq_tiles=seq_tiles, mod_row0=mod_row0)
    x = _ffn_call(x, mods, w['g_norm'], w['g_final'], w['ffn2_in'], w['ffn2_out'], layer=l, sub=2,
                  final=(l == DEPTH - 1), tm=tm, seq_tiles=seq_tiles, mod_row0=mod_row0)
    return x, ((k, v) + tuple(m_res[1:]) if is_ctx else None)


def kernel(x_prompt, x_sample, cache_k, cache_v, state_C, state_n, state_m, c, c_ctx, w_ada, b_ada, g_norm,
           w_ffn1_in, w_ffn1_out, w_ffn2_in, w_ffn2_out, w_in, b_mgate, attn_lambda, g_attn_sub, g_mlstm,
           w_branch_gate, w_br_attn, w_br_four, w_br_mlstm, w_out, g_final):
    batch, seq, _ = x_prompt.shape
    dec_batch, dec_seq, _ = x_sample.shape
    cond = jnp.zeros((N_COND, D_MODEL), F32).at[0].set(c_ctx).at[1:1 + dec_batch].set(c)
    mods_all = _ada_call(cond, w_ada, b_ada).reshape(DEPTH, N_COND, N_MOD, D_MODEL)

    gate_w = jnp.pad(w_in[:, :, P_MAIN:], ((0, 0), (0, 0), (0, LANES - N_GATE)))
    b_gate = jnp.pad(b_mgate, ((0, 0), (0, LANES - N_GATE)))
    w = {
        'g_norm': g_norm[:, :, None, :], 'g_final': g_final[None, :],
        'ffn1_in': w_ffn1_in.astype(BF16), 'ffn1_out': w_ffn1_out.astype(BF16),
        'ffn2_in': w_ffn2_in.astype(BF16), 'ffn2_out': w_ffn2_out.astype(BF16),
        'w_row': jnp.concatenate([w_in[:, :, :MV_LO], w_in[:, :, MV_HI:P_MAIN], gate_w], axis=-1).astype(BF16),
        'w_col': jnp.swapaxes(jnp.concatenate([w_in[:, :, MV_LO:MV_HI], gate_w], axis=-1), 1, 2).astype(BF16),
        'b_gate': b_gate[:, None, :], 'b_gate_t': b_gate[:, :, None],
        'attn_lambda': attn_lambda, 'g_attn_sub': g_attn_sub[:, None, :], 'g_mlstm': g_mlstm[:, None, :],
        'w_bg': w_branch_gate.astype(BF16), 'w_br_a': w_br_attn.astype(BF16),
        'w_br_f': w_br_four.astype(BF16), 'w_br_m': w_br_mlstm.astype(BF16), 'w_out': w_out.astype(BF16),
    }

    hp = x_prompt.reshape(batch * seq, D_MODEL)
    hs = x_sample.reshape(dec_batch * dec_seq, D_MODEL)
    prompt_group = (batch, seq, 0, False, 2)
    sample_group = (dec_batch, dec_seq, 1, True, 1)
    ctx = None
    for l in range(DEPTH):
        hp, ctx = _trunk(hp, prompt_group, l, mods_all[l], w, None, None, ctx)
        hs, _ = _trunk(hs, sample_group, l, mods_all[l], w, (cache_k, cache_v),
                       (state_C, state_n, state_m), None)
    return (hp.reshape(batch, seq, D_MODEL), hs.reshape(dec_batch, dec_seq, D_MODEL), *ctx)
```

```python
import functools
import math

import numpy as np
import jax
import jax.numpy as jnp
from jax import lax
from jax.experimental import pallas as pl
from jax.experimental.pallas import tpu as pltpu

D_MODEL = 1024
DEPTH = 2
GRID_W = 64
N_ATTN = 4
DH_ATTN = 64
DV_ATTN = 2 * DH_ATTN
W_ATTN = N_ATTN * DV_ATTN
N_FOUR = 4
DG_FOUR = 128
W_FOUR = N_FOUR * DG_FOUR
N_MLSTM = 4
DH_MLSTM = 128
W_MLSTM = N_MLSTM * DH_MLSTM
N_GATE = 4 * N_MLSTM
LANES = 128
SUBLANES = 16
MV_LO = 3 * W_ATTN + W_FOUR + 2 * W_MLSTM
MV_HI = MV_LO + W_MLSTM
P_MAIN = MV_HI + W_MLSTM
P_ROW = P_MAIN - W_MLSTM + LANES
P_COL = W_MLSTM + LANES
D_FF = 2816
N_MOD = 9
N_COND = 8
ROPE_BASE = 10000.0
ROPE_AXIS_PAIRS = DH_ATTN // 4
ATTN_SCALE = DH_ATTN ** -0.5
MLSTM_K_SCALE = DH_MLSTM ** -0.5
EPS = 1e-6
MLSTM_CHUNK = 256
VMEM_LIMIT = 56 * 1024 * 1024

F32 = jnp.float32
BF16 = jnp.bfloat16
HIGHEST = lax.Precision.HIGHEST


def _cparams(n_grid):
    return pltpu.CompilerParams(dimension_semantics=("arbitrary",) * n_grid,
                                vmem_limit_bytes=VMEM_LIMIT)


def _bdot(a, b):
    return jnp.dot(a.astype(BF16), b.astype(BF16), preferred_element_type=F32)


def _bdot_nt(a, b, precision=None):
    if precision is None:
        a, b = a.astype(BF16), b.astype(BF16)
    return lax.dot_general(a, b, (((1,), (1,)), ((), ())), preferred_element_type=F32,
                           precision=precision)


def _sigmoid(x):
    return 1.0 / (1.0 + jnp.exp(-x))


def _log_sigmoid(x):
    return jnp.minimum(x, 0.0) - jnp.log1p(jnp.exp(-jnp.abs(x)))


def _rms(x, g):
    return x * lax.rsqrt(jnp.mean(x * x, axis=-1, keepdims=True) + EPS) * g


def _modulated_norm(x, g, mod, base):
    return _rms(x, g) * (1.0 + mod[base + 1:base + 2]) + mod[base:base + 1]


def _layer_spec(shape, layer):
    return pl.BlockSpec((None,) + shape, lambda *_: (layer,) + (0,) * len(shape),
                        pipeline_mode=pl.Buffered(1))


def _mod_spec(seq_tiles, mod_row0):
    if seq_tiles == 0:
        return pl.BlockSpec((None, N_MOD, D_MODEL), lambda i: (mod_row0, 0, 0))
    return pl.BlockSpec((None, N_MOD, D_MODEL), lambda i: (mod_row0 + i // seq_tiles, 0, 0))


def _ada_kernel(c_ref, w_ref, b_ref, o_ref):
    c = c_ref[...]
    s = c * _sigmoid(c)
    o_ref[...] = _bdot(s, w_ref[...]) + b_ref[...]


def _ada_call(cond, w_ada, b_ada):
    tn = 1024
    n_out = N_MOD * D_MODEL
    return pl.pallas_call(
        _ada_kernel,
        grid=(DEPTH, n_out // tn),
        in_specs=[
            pl.BlockSpec((N_COND, D_MODEL), lambda l, j: (0, 0)),
            pl.BlockSpec((None, D_MODEL, tn), lambda l, j: (l, 0, j)),
            pl.BlockSpec((None, 1, tn), lambda l, j: (l, 0, j)),
        ],
        out_specs=pl.BlockSpec((None, N_COND, tn), lambda l, j: (l, 0, j)),
        out_shape=jax.ShapeDtypeStruct((DEPTH, N_COND, n_out), F32),
        compiler_params=_cparams(2),
        name="adaln",
    )(cond, w_ada, b_ada.reshape(DEPTH, 1, n_out))


FF_CHUNKS = (768, 768, 768, 512)


def _ffn_kernel(x_ref, mod_ref, g_ref, gf_ref, win_ref, wout_ref, o_ref, *, base, final):
    x = x_ref[...]
    mod = mod_ref[...]
    u = _modulated_norm(x, g_ref[...], mod, base).astype(BF16)
    y = None
    lo = 0
    for width in FF_CHUNKS:
        a = jnp.dot(u, win_ref[:, lo:lo + width], preferred_element_type=F32)
        g = jnp.dot(u, win_ref[:, D_FF + lo:D_FF + lo + width], preferred_element_type=F32)
        hh = (a * _sigmoid(a) * g).astype(BF16)
        part = jnp.dot(hh, wout_ref[lo:lo + width, :], preferred_element_type=F32)
        y = part if y is None else y + part
        lo += width
    xn = x + 0.5 * mod[base + 2:base + 3] * y
    if final:
        xn = _rms(xn, gf_ref[...])
    o_ref[...] = xn


def _ffn_call(x, mods, g_norm, g_final, w_in, w_out, *, layer, sub, final, tm, seq_tiles, mod_row0):
    n = x.shape[0]
    return pl.pallas_call(
        functools.partial(_ffn_kernel, base=3 * sub, final=final),
        grid=(n // tm,),
        in_specs=[
            pl.BlockSpec((tm, D_MODEL), lambda i: (i, 0)),
            _mod_spec(seq_tiles, mod_row0),
            pl.BlockSpec((None, None, 1, D_MODEL), lambda i: (layer, sub, 0, 0)),
            pl.BlockSpec((1, D_MODEL), lambda i: (0, 0)),
            _layer_spec((D_MODEL, 2 * D_FF), layer),
            _layer_spec((D_FF, D_MODEL), layer),
        ],
        out_specs=pl.BlockSpec((tm, D_MODEL), lambda i: (i, 0)),
        out_shape=jax.ShapeDtypeStruct((n, D_MODEL), F32),
        compiler_params=_cparams(1),
        name="ffn",
    )(x, mods, g_norm, g_final, w_in, w_out)


def _swap32(x):
    lane = lax.broadcasted_iota(jnp.int32, x.shape, 1)
    return jnp.where((lane & (DH_ATTN // 2)) == 0,
                     pltpu.roll(x, LANES - DH_ATTN // 2, 1), pltpu.roll(x, DH_ATTN // 2, 1))


def _proj_kernel(*refs, rope, n_alias):
    (x_ref, mod_ref, g_ref, w_ref, wt_ref, bg_ref, bgt_ref, cos_ref, sin_ref) = refs[:9]
    (q_ref, k_ref, v_ref, zf_ref, mq_ref, mk_ref, mo_ref, gate_ref, vt_ref, gt_ref) = refs[9 + n_alias:]
    x = x_ref[...]
    u = _modulated_norm(x, g_ref[...], mod_ref[...], 3).astype(BF16)
    z = jnp.dot(u, w_ref[...], preferred_element_type=F32)
    zt = _bdot_nt(wt_ref[...], u)
    if rope:
        cos_t = cos_ref[...]
        sin_t = sin_ref[...]
    for h in range(N_ATTN):
        zq = z[:, h * DV_ATTN:(h + 1) * DV_ATTN]
        zk = z[:, W_ATTN + h * DV_ATTN:W_ATTN + (h + 1) * DV_ATTN]
        if rope:
            zq = zq * cos_t + _swap32(zq) * sin_t
            zk = zk * cos_t + _swap32(zk) * sin_t
        q_ref[h] = zq.astype(q_ref.dtype)
        k_ref[h] = zk.astype(k_ref.dtype)
        v_ref[h] = z[:, 2 * W_ATTN + h * DV_ATTN:2 * W_ATTN + (h + 1) * DV_ATTN].astype(v_ref.dtype)
    off = 3 * W_ATTN
    zf_ref[...] = z[:, off:off + W_FOUR].astype(zf_ref.dtype)
    off += W_FOUR
    mq_ref[...] = z[:, off:off + W_MLSTM].astype(mq_ref.dtype)
    mk_ref[...] = (z[:, off + W_MLSTM:off + 2 * W_MLSTM] * MLSTM_K_SCALE).astype(mk_ref.dtype)
    mo_ref[...] = z[:, off + 2 * W_MLSTM:off + 3 * W_MLSTM]
    gp = z[:, off + 3 * W_MLSTM:] + bg_ref[...]
    lane = lax.broadcasted_iota(jnp.int32, gp.shape, 1)
    gate_ref[...] = jnp.where((lane < N_GATE) & ((lane & N_MLSTM) != 0), _log_sigmoid(gp), gp)
    vt_ref[...] = zt[:W_MLSTM]
    gpt = zt[W_MLSTM:] + bgt_ref[...]
    sub = lax.broadcasted_iota(jnp.int32, gpt.shape, 0)
    gt_ref[...] = jnp.where((sub < N_GATE) & ((sub & N_MLSTM) != 0), _log_sigmoid(gpt), gpt)


def _proj_call(x, mods, g_norm, w_row, w_col, bg, bgt, cos_t, sin_t, kv_prev, *, layer, batch, seq, tm,
               mod_row0, per_seq_mod, rope, kv_stacked):
    n = x.shape[0]
    tps = seq // tm
    head_shape = jax.ShapeDtypeStruct((batch, N_ATTN, seq, DV_ATTN), BF16)
    head_spec = pl.BlockSpec((None, N_ATTN, tm, DV_ATTN), lambda i: (i // tps, 0, i % tps, 0))
    if kv_stacked:
        kv_shape = jax.ShapeDtypeStruct((batch, DEPTH, N_ATTN, seq, DV_ATTN), F32)
        kv_spec = pl.BlockSpec((None, None, N_ATTN, tm, DV_ATTN), lambda i: (i // tps, layer, 0, i % tps, 0))
    else:
        kv_shape, kv_spec = head_shape, head_spec
    tok_shape = jax.ShapeDtypeStruct((n, W_ATTN), F32)
    tok_bf16 = jax.ShapeDtypeStruct((n, W_ATTN), BF16)
    tok_spec = pl.BlockSpec((tm, W_ATTN), lambda i: (i, 0))
    rope_spec = pl.BlockSpec((tm, LANES), lambda i: (i % tps, 0))
    alias_in = list(kv_prev) if kv_prev is not None else []
    n_in = 9
    return pl.pallas_call(
        functools.partial(_proj_kernel, rope=rope, n_alias=len(alias_in)),
        grid=(n // tm,),
        in_specs=[
            pl.BlockSpec((tm, D_MODEL), lambda i: (i, 0)),
            _mod_spec(tps if per_seq_mod else 0, mod_row0),
            pl.BlockSpec((None, None, 1, D_MODEL), lambda i: (layer, 1, 0, 0)),
            _layer_spec((D_MODEL, P_ROW), layer),
            _layer_spec((P_COL, D_MODEL), layer),
            _layer_spec((1, LANES), layer),
            _layer_spec((LANES, 1), layer),
            rope_spec, rope_spec,
        ] + [pl.BlockSpec(memory_space=pl.ANY)] * len(alias_in),
        out_specs=[head_spec, kv_spec, kv_spec, tok_spec, tok_spec, tok_spec, tok_spec,
                   pl.BlockSpec((tm, LANES), lambda i: (i, 0)),
                   pl.BlockSpec((W_MLSTM, tm), lambda i: (0, i)),
                   pl.BlockSpec((LANES, tm), lambda i: (0, i))],
        out_shape=[head_shape, kv_shape, kv_shape, tok_bf16, tok_bf16, tok_bf16, tok_shape,
                   jax.ShapeDtypeStruct((n, LANES), F32),
                   jax.ShapeDtypeStruct((W_MLSTM, n), F32),
                   jax.ShapeDtypeStruct((LANES, n), F32)],
        input_output_aliases={n_in + j: 1 + j for j in range(len(alias_in))},
        compiler_params=_cparams(1),
        name="mixer_proj",
    )(x, mods, g_norm, w_row, w_col, bg, bgt, cos_t, sin_t, *alias_in)


def _attn_kernel(*refs, lam_init, cached, nb, tq):
    if cached:
        q_ref, k_ref, v_ref, ck_ref, cv_ref, lam_ref, g_ref, o_ref = refs
    else:
        q_ref, k_ref, v_ref, lam_ref, g_ref, o_ref = refs
    lp = lam_ref[...]
    lam = (jnp.exp(jnp.sum(lp[0:1] * lp[1:2], axis=-1, keepdims=True))
           - jnp.exp(jnp.sum(lp[2:3] * lp[3:4], axis=-1, keepdims=True)) + lam_init)
    g_sub = g_ref[...]
    lane = lax.broadcasted_iota(jnp.int32, (tq, DV_ATTN), 1)
    for bi in range(nb):
        for h in range(N_ATTN):
            q = q_ref[bi, h].astype(F32) * ATTN_SCALE
            q_maps = (jnp.where(lane < DH_ATTN, q, 0.0).astype(BF16),
                      jnp.where(lane >= DH_ATTN, q, 0.0).astype(BF16))
            keys = [k_ref[bi, h].astype(BF16)]
            vals = [v_ref[bi, h].astype(BF16)]
            if cached:
                keys.insert(0, ck_ref[bi, h].astype(BF16))
                vals.insert(0, cv_ref[bi, h].astype(BF16))
            exps, dens = [], []
            for qm in q_maps:
                s = [_bdot_nt(qm, kk) for kk in keys]
                m = functools.reduce(jnp.maximum, [jnp.max(si, axis=-1, keepdims=True) for si in s])
                e = [jnp.exp(si - m) for si in s]
                exps.append(e)
                dens.append(functools.reduce(jnp.add, [jnp.sum(ei, axis=-1, keepdims=True) for ei in e]))
            r1 = 1.0 / dens[0]
            r2 = lam / dens[1]
            o = None
            for j, vv in enumerate(vals):
                p = exps[0][j] * r1 - exps[1][j] * r2
                part = jnp.dot(p.astype(BF16), vv, preferred_element_type=F32)
                o = part if o is None else o + part
            o_ref[bi * tq:(bi + 1) * tq, h * DV_ATTN:(h + 1) * DV_ATTN] = (
                _rms(o, g_sub) * (1.0 - lam_init)).astype(o_ref.dtype)


def _heads_spec(arr, nb, rows, layer):
    if arr.ndim == 5:
        return pl.BlockSpec((nb, None, N_ATTN, rows, DV_ATTN), lambda b, i: (b, layer, 0, 0, 0))
    return pl.BlockSpec((nb, N_ATTN, rows, DV_ATTN), lambda b, i: (b, 0, 0, 0))


def _attn_call(q, k, v, cache, lam_p, g_sub, *, layer, lam_init, tq, nb):
    batch, _, seq, _ = q.shape
    nq = seq // tq
    if nq > 1 and nb > 1:
        raise ValueError("output row blocks are contiguous only for nb == 1 or tq == seq")
    in_specs = [pl.BlockSpec((nb, N_ATTN, tq, DV_ATTN), lambda b, i: (b, 0, i, 0)),
                _heads_spec(k, nb, seq, layer), _heads_spec(v, nb, seq, layer)]
    args = [q, k, v]
    if cache is not None:
        past = cache[0].shape[3]
        in_specs += [_heads_spec(cache[0], nb, past, layer), _heads_spec(cache[1], nb, past, layer)]
        args += list(cache)
    in_specs += [pl.BlockSpec((None, 4, DH_ATTN), lambda b, i: (layer, 0, 0)),
                 pl.BlockSpec((None, 1, DV_ATTN), lambda b, i: (layer, 0, 0))]
    args += [lam_p, g_sub]
    return pl.pallas_call(
        functools.partial(_attn_kernel, lam_init=lam_init, cached=cache is not None, nb=nb, tq=tq),
        grid=(batch // nb, nq),
        in_specs=in_specs,
        out_specs=pl.BlockSpec((nb * tq, W_ATTN), lambda b, i: (b * nq + i, 0)),
        out_shape=jax.ShapeDtypeStruct((batch * seq, W_ATTN), BF16),
        compiler_params=_cparams(2),
        name="diff_attn",
    )(*args)


def _dft_tables(seq):
    def cs(n):
        j = np.arange(n)
        ang = 2.0 * np.pi * ((j[:, None] * j[None, :]) % n) / n
        return np.cos(ang) / math.sqrt(n), np.sin(ang) / math.sqrt(n)

    cd, sd = cs(DG_FOUR)
    ct, st = cs(seq)
    w_d = jnp.asarray(np.concatenate([cd, sd], axis=1), F32)
    return w_d.astype(BF16), jnp.asarray(ct, F32).astype(BF16), jnp.asarray(-st, F32).astype(BF16)


def _four_kernel(z_ref, wd_ref, ct_ref, st_ref, o_ref):
    y_cos, y_sin = [], []
    for gidx in range(N_FOUR):
        y = jnp.dot(z_ref[:, gidx * DG_FOUR:(gidx + 1) * DG_FOUR].astype(BF16), wd_ref[...],
                    preferred_element_type=F32)
        y_cos.append(y[:, :DG_FOUR].astype(BF16))
        y_sin.append(y[:, DG_FOUR:].astype(BF16))
    y_cos = jnp.concatenate(y_cos, axis=1)
    y_sin = jnp.concatenate(y_sin, axis=1)
    o_ref[...] = (jnp.dot(ct_ref[...], y_cos, preferred_element_type=F32)
                  + jnp.dot(st_ref[...], y_sin, preferred_element_type=F32)).astype(o_ref.dtype)


def _four_call(zf, *, batch, seq):
    w_d, ct, st = _dft_tables(seq)
    return pl.pallas_call(
        _four_kernel,
        grid=(batch,),
        in_specs=[
            pl.BlockSpec((seq, W_FOUR), lambda b: (b, 0)),
            pl.BlockSpec((DG_FOUR, 2 * DG_FOUR), lambda b: (0, 0)),
            pl.BlockSpec((seq, seq), lambda b: (0, 0)),
            pl.BlockSpec((seq, seq), lambda b: (0, 0)),
        ],
        out_specs=pl.BlockSpec((seq, W_FOUR), lambda b: (b, 0)),
        out_shape=jax.ShapeDtypeStruct((batch * seq, W_FOUR), BF16),
        compiler_params=_cparams(1),
        name="fourier_mix",
    )(zf, w_d, ct, st)


def _mlstm_kernel(*refs, seq, nb, seeded, emit_state, n_alias):
    mq_ref, mk_ref, vt_ref, mo_ref, gate_ref, gt_ref, g_ref = refs[:7]
    pos = 7
    if seeded:
        c0_ref, n0_ref, m0_ref = refs[pos:pos + 3]
        pos += 3
    pos += n_alias
    o_ref = refs[pos]
    if emit_state:
        c1_ref, n1_ref, m1_ref = refs[pos + 1:pos + 4]

    L = min(MLSTM_CHUNK, seq)
    nc = seq // L
    s_idx = lax.broadcasted_iota(jnp.int32, (L, L), 0)
    t_idx = lax.broadcasted_iota(jnp.int32, (L, L), 1)
    before = (s_idx <= t_idx, s_idx >= t_idx)
    tri = (jnp.where(before[1], 1.0, 0.0), jnp.where(before[0], 1.0, 0.0))
    chains = [(bi, d, h) for bi in range(nb) for d in range(2) for h in range(N_MLSTM)]

    state = {}
    for (bi, d, h) in chains:
        if seeded:
            state[bi, d, h] = (c0_ref[bi, d, h], n0_ref[bi, d, h:h + 1, :], m0_ref[bi, d:d + 1, h:h + 1])
        else:
            state[bi, d, h] = (None, None, jnp.zeros((1, 1), F32))

    h_t = {}
    for c in range(nc):
        terms = {}
        for bi in range(nb):
            for d in range(2):
                r0 = bi * seq + (c if d == 0 else nc - 1 - c) * L
                g = gate_ref[r0:r0 + L, :]
                g_t = gt_ref[:, r0:r0 + L]
                cum = jnp.dot(tri[d], g, preferred_element_type=F32, precision=HIGHEST)
                cum_t = _bdot_nt(g_t, tri[d], precision=HIGHEST)
                col = g - pltpu.roll(cum, LANES - N_MLSTM, 1)
                terms[bi, d] = (r0, col, g_t, cum_t)

        new_state = {}
        for (bi, d, h) in chains:
            r0, col, g_t, cum_t = terms[bi, d]
            c_prev, n_prev, m_prev = state[bi, d, h]
            has_state = c_prev is not None
            need_update = emit_state or c < nc - 1
            ci = 2 * d * N_MLSTM + h
            cf = ci + N_MLSTM
            last = L - 1 if d == 0 else 0
            hs = slice(h * DH_MLSTM, (h + 1) * DH_MLSTM)
            q = mq_ref[r0:r0 + L, hs]
            k = mk_ref[r0:r0 + L, hs]
            v_t = vt_ref[hs, r0:r0 + L]
            i_row, b_row = g_t[ci:ci + 1, :], cum_t[cf:cf + 1, :]

            cb = jnp.where(before[d], col[:, ci:ci + 1], -jnp.inf)
            m_row = jnp.maximum(jnp.max(cb, axis=0, keepdims=True), m_prev)
            s_t = _bdot_nt(k, q) * jnp.exp(cb - m_row)
            num_t = jnp.dot(v_t.astype(BF16), s_t.astype(BF16), preferred_element_type=F32)
            den = jnp.sum(s_t, axis=0, keepdims=True)
            if has_state:
                sp = jnp.exp(m_prev - m_row)
                cn = jnp.concatenate([c_prev, jnp.broadcast_to(n_prev, (SUBLANES, DH_MLSTM))], axis=0)
                cq = _bdot_nt(cn, q)
                num_t = num_t + sp * cq[:DH_MLSTM]
                den = den + sp * cq[DH_MLSTM:DH_MLSTM + 1]
            m_t = b_row + m_row
            h_t[bi, d, h, c] = num_t / jnp.maximum(jnp.abs(den), jnp.exp(-m_t))
            if need_update:
                b_tot = b_row[:, last:last + 1]
                m_new = m_t[:, last:last + 1]
                wl = jnp.exp(b_tot + (i_row - b_row) - m_new)
                vw = jnp.concatenate([v_t * wl, jnp.broadcast_to(wl, (SUBLANES, L))], axis=0)
                upd = jnp.dot(vw.astype(BF16), k, preferred_element_type=F32)
                c_new, n_new = upd[:DH_MLSTM], upd[DH_MLSTM:DH_MLSTM + 1]
                if has_state:
                    decay = jnp.exp(b_tot + m_prev - m_new)
                    c_new = decay * c_prev + c_new
                    n_new = decay * n_prev + n_new
                new_state[bi, d, h] = (c_new, n_new, m_new)
        state = new_state

    g_m = g_ref[...]
    for bi in range(nb):
        for h in range(N_MLSTM):
            hs = slice(h * DH_MLSTM, (h + 1) * DH_MLSTM)
            fwd = [h_t[bi, 0, h, c] for c in range(nc)]
            bwd = [h_t[bi, 1, h, nc - 1 - c] for c in range(nc)]
            hsum = (fwd[0] if nc == 1 else jnp.concatenate(fwd, axis=1)) \
                + (bwd[0] if nc == 1 else jnp.concatenate(bwd, axis=1))
            y = hsum * lax.rsqrt(jnp.mean(hsum * hsum, axis=0, keepdims=True) + EPS)
            rows = slice(bi * seq, (bi + 1) * seq)
            o_ref[rows, hs] = (y.T * g_m * _sigmoid(mo_ref[rows, hs])).astype(o_ref.dtype)

    if emit_state:
        for (bi, d, h) in chains:
            c_fin, n_fin, m_fin = state[bi, d, h]
            c1_ref[bi, d, h] = c_fin
            n1_ref[bi, d, h:h + 1, :] = n_fin
            m1_ref[bi, d:d + 1, h:h + 1] = m_fin


def _mlstm_call(mq, mk, vt, mo, gates, gates_t, g_m, state, state_prev, *, layer, batch, seq, nb, emit_state):
    rows = nb * seq
    tok_spec = pl.BlockSpec((rows, W_MLSTM), lambda b: (b, 0))
    in_specs = [tok_spec, tok_spec,
                pl.BlockSpec((W_MLSTM, rows), lambda b: (0, b)),
                tok_spec,
                pl.BlockSpec((rows, LANES), lambda b: (b, 0)),
                pl.BlockSpec((LANES, rows), lambda b: (0, b)),
                pl.BlockSpec((None, 1, DH_MLSTM), lambda b: (layer, 0, 0))]
    args = [mq, mk, vt, mo, gates, gates_t, g_m]
    state_specs = [
        pl.BlockSpec((nb, None, 2, N_MLSTM, DH_MLSTM, DH_MLSTM), lambda b: (b, layer, 0, 0, 0, 0)),
        pl.BlockSpec((nb, None, 2, N_MLSTM, DH_MLSTM), lambda b: (b, layer, 0, 0, 0)),
        pl.BlockSpec((nb, None, 2, N_MLSTM), lambda b: (b, layer, 0, 0)),
    ]
    if state is not None:
        in_specs += state_specs
        args += list(state)
    alias_in = list(state_prev) if state_prev is not None else []
    n_in = len(args)
    in_specs += [pl.BlockSpec(memory_space=pl.ANY)] * len(alias_in)
    args += alias_in
    out_specs = [tok_spec]
    out_shape = [jax.ShapeDtypeStruct((batch * seq, W_MLSTM), BF16)]
    if emit_state:
        out_specs += state_specs
        out_shape += [
            jax.ShapeDtypeStruct((batch, DEPTH, 2, N_MLSTM, DH_MLSTM, DH_MLSTM), F32),
            jax.ShapeDtypeStruct((batch, DEPTH, 2, N_MLSTM, DH_MLSTM), F32),
            jax.ShapeDtypeStruct((batch, DEPTH, 2, N_MLSTM), F32),
        ]
    return pl.pallas_call(
        functools.partial(_mlstm_kernel, seq=seq, nb=nb, seeded=state is not None, emit_state=emit_state,
                          n_alias=len(alias_in)),
        grid=(batch // nb,),
        in_specs=in_specs,
        out_specs=out_specs,
        out_shape=out_shape,
        input_output_aliases={n_in + j: 1 + j for j in range(len(alias_in))},
        compiler_params=_cparams(1),
        name="mlstm",
    )(*args)


def _merge_kernel(x_ref, a_ref, f_ref, m_ref, mod_ref, g_ref, wg_ref, wa_ref, wf_ref, wm_ref, wo_ref, o_ref):
    x = x_ref[...]
    mod = mod_ref[...]
    u = _modulated_norm(x, g_ref[...], mod, 3).astype(BF16)
    merged = None
    for j, (br_ref, w_ref) in enumerate(((a_ref, wa_ref), (f_ref, wf_ref), (m_ref, wm_ref))):
        gate = _sigmoid(jnp.dot(u, wg_ref[:, j * D_MODEL:(j + 1) * D_MODEL], preferred_element_type=F32))
        term = gate * _bdot(br_ref[...], w_ref[...])
        merged = term if merged is None else merged + term
    out = _bdot(merged, wo_ref[...])
    o_ref[...] = x + mod[5:6] * out


def _merge_call(x, a, f, m, mods, g_norm, wg, wa, wf, wm, wo, *, layer, tm, seq_tiles, mod_row0):
    n = x.shape[0]
    br_spec = pl.BlockSpec((tm, W_ATTN), lambda i: (i, 0))
    return pl.pallas_call(
        _merge_kernel,
        grid=(n // tm,),
        in_specs=[
            pl.BlockSpec((tm, D_MODEL), lambda i: (i, 0)),
            br_spec, br_spec, br_spec,
            _mod_spec(seq_tiles, mod_row0),
            pl.BlockSpec((None, None, 1, D_MODEL), lambda i: (layer, 1, 0, 0)),
            _layer_spec((D_MODEL, 3 * D_MODEL), layer),
            _layer_spec((W_ATTN, D_MODEL), layer), _layer_spec((W_FOUR, D_MODEL), layer),
            _layer_spec((W_MLSTM, D_MODEL), layer),
            _layer_spec((D_MODEL, D_MODEL), layer),
        ],
        out_specs=pl.BlockSpec((tm, D_MODEL), lambda i: (i, 0)),
        out_shape=jax.ShapeDtypeStruct((n, D_MODEL), F32),
        compiler_params=_cparams(1),
        name="merge",
    )(x, a, f, m, mods, g_norm, wg, wa, wf, wm, wo)


def _rope_tables(n_tok):
    tok = np.arange(n_tok)
    inv = ROPE_BASE ** (-np.arange(ROPE_AXIS_PAIRS, dtype=np.float32) / ROPE_AXIS_PAIRS)
    ang = np.concatenate([(tok // GRID_W).astype(np.float32)[:, None] * inv,
                          (tok % GRID_W).astype(np.float32)[:, None] * inv], axis=-1).astype(np.float32)
    c, s = np.cos(ang), np.sin(ang)
    cos_t = np.concatenate([c, c, c, c], axis=-1)
    sin_t = np.concatenate([-s, s, -s, s], axis=-1)
    return jnp.asarray(cos_t, F32), jnp.asarray(sin_t, F32)


def _trunk(x, group, l, mods, w, cache, state, ctx_prev):
    batch, seq, mod_row0, per_seq_mod, nb, attn_nb, attn_tq = group
    tm = 512
    tm_ffn = 1024
    if per_seq_mod and (seq % tm or seq % tm_ffn):
        raise ValueError("row tile spans several conditioned sequences")
    seq_tiles = seq // tm if per_seq_mod else 0
    ffn_tiles = seq // tm_ffn if per_seq_mod else 0
    lam_init = 0.8 - 0.6 * math.exp(-0.3 * l)
    is_ctx = cache is None
    x = _ffn_call(x, mods, w['g_norm'], w['g_final'], w['ffn1_in'], w['ffn1_out'], layer=l, sub=0,
                  final=False, tm=tm_ffn, seq_tiles=ffn_tiles, mod_row0=mod_row0)
    cos_t, sin_t = _rope_tables(seq)
    q, k, v, zf, mq, mk, mo, gates, vt, gates_t = _proj_call(
        x, mods, w['g_norm'], w['w_row'], w['w_col'], w['b_gate'], w['b_gate_t'], cos_t, sin_t,
        ctx_prev[:2] if ctx_prev is not None else None,
        layer=l, batch=batch, seq=seq, tm=256, mod_row0=mod_row0, per_seq_mod=per_seq_mod,
        rope=not is_ctx, kv_stacked=is_ctx)
    a = _attn_call(q, k, v, cache, w['attn_lambda'], w['g_attn_sub'], layer=l, lam_init=lam_init,
                   tq=attn_tq, nb=attn_nb)
    f = _four_call(zf, batch=batch, seq=seq)
    m_res = _mlstm_call(mq, mk, vt, mo, gates, gates_t, w['g_mlstm'], state,
                        ctx_prev[2:] if ctx_prev is not None else None,
                        layer=l, batch=batch, seq=seq, nb=nb, emit_state=is_ctx)
    x = _merge_call(x, a, f, m_res[0], mods, w['g_norm'], w['w_bg'], w['w_br_a'], w['w_br_f'], w['w_br_m'],
                    w['w_out'], layer=l, tm=tm_ffn, seq_tiles=ffn_tiles, mod_row0=mod_row0)
    x = _ffn_call(x, mods, w['g_norm'], w['g_final'], w['ffn2_in'], w['ffn2_out'], layer=l, sub=2,
                  final=(l == DEPTH - 1), tm=tm_ffn, seq_tiles=ffn_tiles, mod_row0=mod_row0)
    return x, ((k, v) + tuple(m_res[1:]) if is_ctx else None)


def kernel(x_prompt, x_sample, cache_k, cache_v, state_C, state_n, state_m, c, c_ctx, w_ada, b_ada, g_norm,
           w_ffn1_in, w_ffn1_out, w_ffn2_in, w_ffn2_out, w_in, b_mgate, attn_lambda, g_attn_sub, g_mlstm,
           w_branch_gate, w_br_attn, w_br_four, w_br_mlstm, w_out, g_final):
    batch, seq, _ = x_prompt.shape
    dec_batch, dec_seq, _ = x_sample.shape
    cond = jnp.zeros((N_COND, D_MODEL), F32).at[0].set(c_ctx).at[1:1 + dec_batch].set(c)
    mods_all = _ada_call(cond, w_ada, b_ada).reshape(DEPTH, N_COND, N_MOD, D_MODEL)

    gate_w = jnp.pad(w_in[:, :, P_MAIN:], ((0, 0), (0, 0), (0, LANES - N_GATE)))
    b_gate = jnp.pad(b_mgate, ((0, 0), (0, LANES - N_GATE)))
    w = {
        'g_norm': g_norm[:, :, None, :], 'g_final': g_final[None, :],
        'ffn1_in': w_ffn1_in.astype(BF16), 'ffn1_out': w_ffn1_out.astype(BF16),
        'ffn2_in': w_ffn2_in.astype(BF16), 'ffn2_out': w_ffn2_out.astype(BF16),
        'w_row': jnp.concatenate([w_in[:, :, :MV_LO], w_in[:, :, MV_HI:P_MAIN], gate_w], axis=-1).astype(BF16),
        'w_col': jnp.swapaxes(jnp.concatenate([w_in[:, :, MV_LO:MV_HI], gate_w], axis=-1), 1, 2).astype(BF16),
        'b_gate': b_gate[:, None, :], 'b_gate_t': b_gate[:, :, None],
        'attn_lambda': attn_lambda, 'g_attn_sub': g_attn_sub[:, None, :], 'g_mlstm': g_mlstm[:, None, :],
        'w_bg': w_branch_gate.astype(BF16), 'w_br_a': w_br_attn.astype(BF16),
        'w_br_f': w_br_four.astype(BF16), 'w_br_m': w_br_mlstm.astype(BF16), 'w_out': w_out.astype(BF16),
    }

    hp = x_prompt.reshape(batch * seq, D_MODEL)
    hs = x_sample.reshape(dec_batch * dec_seq, D_MODEL)
    prompt_group = (batch, seq, 0, False, 2, 4, seq)
    sample_group = (dec_batch, dec_seq, 1, True, 1, 1, dec_seq // 2)
    ctx = None
    for l in range(DEPTH):
        hp, ctx = _trunk(hp, prompt_group, l, mods_all[l], w, None, None, ctx)
        hs, _ = _trunk(hs, sample_group, l, mods_all[l], w, (cache_k, cache_v),
                       (state_C, state_n, state_m), None)
    return (hp.reshape(batch, seq, D_MODEL), hs.reshape(dec_batch, dec_seq, D_MODEL), *ctx)
```

```python
import functools
import math

import numpy as np
import jax
import jax.numpy as jnp
from jax import lax
from jax.experimental import pallas as pl
from jax.experimental.pallas import tpu as pltpu

D_MODEL = 1024
DEPTH = 2
GRID_W = 64
N_ATTN = 4
DH_ATTN = 64
DV_ATTN = 2 * DH_ATTN
W_ATTN = N_ATTN * DV_ATTN
N_FOUR = 4
DG_FOUR = 128
W_FOUR = N_FOUR * DG_FOUR
N_MLSTM = 4
DH_MLSTM = 128
W_MLSTM = N_MLSTM * DH_MLSTM
N_GATE = 4 * N_MLSTM
LANES = 128
SUBLANES = 16
MV_LO = 3 * W_ATTN + W_FOUR + 2 * W_MLSTM
MV_HI = MV_LO + W_MLSTM
P_MAIN = MV_HI + W_MLSTM
P_ROW = P_MAIN - W_MLSTM + LANES
P_COL = W_ATTN + W_MLSTM + LANES
D_FF = 2816
N_MOD = 9
N_COND = 8
ROPE_BASE = 10000.0
ROPE_AXIS_PAIRS = DH_ATTN // 4
ATTN_SCALE = DH_ATTN ** -0.5
MLSTM_K_SCALE = DH_MLSTM ** -0.5
EPS = 1e-6
MLSTM_CHUNK = 256
VMEM_LIMIT = 56 * 1024 * 1024

F32 = jnp.float32
BF16 = jnp.bfloat16
HIGHEST = lax.Precision.HIGHEST


def _cparams(n_grid):
    return pltpu.CompilerParams(dimension_semantics=("arbitrary",) * n_grid,
                                vmem_limit_bytes=VMEM_LIMIT)


def _bdot(a, b):
    return jnp.dot(a.astype(BF16), b.astype(BF16), preferred_element_type=F32)


def _bdot_nt(a, b, precision=None):
    if precision is None:
        a, b = a.astype(BF16), b.astype(BF16)
    return lax.dot_general(a, b, (((1,), (1,)), ((), ())), preferred_element_type=F32,
                           precision=precision)


def _sigmoid(x):
    return 1.0 / (1.0 + jnp.exp(-x))


def _log_sigmoid(x):
    return jnp.minimum(x, 0.0) - jnp.log1p(jnp.exp(-jnp.abs(x)))


def _rms(x, g):
    return x * lax.rsqrt(jnp.mean(x * x, axis=-1, keepdims=True) + EPS) * g


def _modulated_norm(x, g, mod, base):
    return _rms(x, g) * (1.0 + mod[base + 1:base + 2]) + mod[base:base + 1]


def _layer_spec(shape, layer):
    return pl.BlockSpec((None,) + shape, lambda *_: (layer,) + (0,) * len(shape),
                        pipeline_mode=pl.Buffered(1))


def _mod_spec(seq_tiles, mod_row0):
    if seq_tiles == 0:
        return pl.BlockSpec((None, N_MOD, D_MODEL), lambda i: (mod_row0, 0, 0))
    return pl.BlockSpec((None, N_MOD, D_MODEL), lambda i: (mod_row0 + i // seq_tiles, 0, 0))


def _ada_kernel(c_ref, w_ref, b_ref, o_ref):
    c = c_ref[...]
    s = c * _sigmoid(c)
    o_ref[...] = _bdot(s, w_ref[...]) + b_ref[...]


def _ada_call(cond, w_ada, b_ada):
    tn = 1024
    n_out = N_MOD * D_MODEL
    return pl.pallas_call(
        _ada_kernel,
        grid=(DEPTH, n_out // tn),
        in_specs=[
            pl.BlockSpec((N_COND, D_MODEL), lambda l, j: (0, 0)),
            pl.BlockSpec((None, D_MODEL, tn), lambda l, j: (l, 0, j)),
            pl.BlockSpec((None, 1, tn), lambda l, j: (l, 0, j)),
        ],
        out_specs=pl.BlockSpec((None, N_COND, tn), lambda l, j: (l, 0, j)),
        out_shape=jax.ShapeDtypeStruct((DEPTH, N_COND, n_out), F32),
        compiler_params=_cparams(2),
        name="adaln",
    )(cond, w_ada, b_ada.reshape(DEPTH, 1, n_out))


FF_CHUNKS = (768, 768, 768, 512)


def _ffn_kernel(x_ref, mod_ref, g_ref, gf_ref, win_ref, wout_ref, o_ref, *, base, final):
    x = x_ref[...]
    mod = mod_ref[...]
    u = _modulated_norm(x, g_ref[...], mod, base).astype(BF16)
    y = None
    lo = 0
    for width in FF_CHUNKS:
        a = jnp.dot(u, win_ref[:, lo:lo + width], preferred_element_type=F32)
        g = jnp.dot(u, win_ref[:, D_FF + lo:D_FF + lo + width], preferred_element_type=F32)
        hh = (a * _sigmoid(a) * g).astype(BF16)
        part = jnp.dot(hh, wout_ref[lo:lo + width, :], preferred_element_type=F32)
        y = part if y is None else y + part
        lo += width
    xn = x + 0.5 * mod[base + 2:base + 3] * y
    if final:
        xn = _rms(xn, gf_ref[...])
    o_ref[...] = xn


def _ffn_call(x, mods, g_norm, g_final, w_in, w_out, *, layer, sub, final, tm, seq_tiles, mod_row0):
    n = x.shape[0]
    return pl.pallas_call(
        functools.partial(_ffn_kernel, base=3 * sub, final=final),
        grid=(n // tm,),
        in_specs=[
            pl.BlockSpec((tm, D_MODEL), lambda i: (i, 0)),
            _mod_spec(seq_tiles, mod_row0),
            pl.BlockSpec((None, None, 1, D_MODEL), lambda i: (layer, sub, 0, 0)),
            pl.BlockSpec((1, D_MODEL), lambda i: (0, 0)),
            _layer_spec((D_MODEL, 2 * D_FF), layer),
            _layer_spec((D_FF, D_MODEL), layer),
        ],
        out_specs=pl.BlockSpec((tm, D_MODEL), lambda i: (i, 0)),
        out_shape=jax.ShapeDtypeStruct((n, D_MODEL), F32),
        compiler_params=_cparams(1),
        name="ffn",
    )(x, mods, g_norm, g_final, w_in, w_out)


def _swap32(x):
    lane = lax.broadcasted_iota(jnp.int32, x.shape, 1)
    return jnp.where((lane & (DH_ATTN // 2)) == 0,
                     pltpu.roll(x, LANES - DH_ATTN // 2, 1), pltpu.roll(x, DH_ATTN // 2, 1))


def _proj_kernel(*refs, rope, n_alias, n_seq, rows):
    (x_ref, mod_ref, g_ref, w_ref, wt_ref, bg_ref, bgt_ref, cos_ref, sin_ref) = refs[:9]
    (q_ref, k_ref, v_ref, zf_ref, mq_ref, mk_ref, mo_ref, gate_ref, vat_ref, vt_ref, gt_ref) = refs[9 + n_alias:]
    x = x_ref[...]
    u = _modulated_norm(x, g_ref[...], mod_ref[...], 3).astype(BF16)
    za = jnp.dot(u, w_ref[:, :3 * W_ATTN], preferred_element_type=F32)
    zb = jnp.dot(u, w_ref[:, 3 * W_ATTN:], preferred_element_type=F32)
    zt = _bdot_nt(wt_ref[...], u)
    if rope:
        cos_t = cos_ref[...]
        sin_t = sin_ref[...]
    for h in range(N_ATTN):
        zq = za[:, h * DV_ATTN:(h + 1) * DV_ATTN]
        zk = za[:, W_ATTN + h * DV_ATTN:W_ATTN + (h + 1) * DV_ATTN]
        zv = za[:, 2 * W_ATTN + h * DV_ATTN:2 * W_ATTN + (h + 1) * DV_ATTN]
        if rope:
            zq = zq * cos_t + _swap32(zq) * sin_t
            zk = zk * cos_t + _swap32(zk) * sin_t
        for s in range(n_seq):
            rs = slice(s * rows, (s + 1) * rows)
            q_ref[s, h] = zq[rs].astype(q_ref.dtype)
            k_ref[s, h] = zk[rs].astype(k_ref.dtype)
            v_ref[s, h] = zv[rs].astype(v_ref.dtype)
    zf_ref[...] = zb[:, :W_FOUR].astype(zf_ref.dtype)
    off = W_FOUR
    mq_ref[...] = zb[:, off:off + W_MLSTM].astype(mq_ref.dtype)
    mk_ref[...] = (zb[:, off + W_MLSTM:off + 2 * W_MLSTM] * MLSTM_K_SCALE).astype(mk_ref.dtype)
    mo_ref[...] = zb[:, off + 2 * W_MLSTM:off + 3 * W_MLSTM]
    gp = zb[:, off + 3 * W_MLSTM:] + bg_ref[...]
    lane = lax.broadcasted_iota(jnp.int32, gp.shape, 1)
    gate_ref[...] = jnp.where((lane < N_GATE) & ((lane & N_MLSTM) != 0), _log_sigmoid(gp), gp)
    vat_ref[...] = zt[:W_ATTN].astype(vat_ref.dtype)
    vt_ref[...] = zt[W_ATTN:W_ATTN + W_MLSTM]
    gpt = zt[W_ATTN + W_MLSTM:] + bgt_ref[...]
    sub = lax.broadcasted_iota(jnp.int32, gpt.shape, 0)
    gt_ref[...] = jnp.where((sub < N_GATE) & ((sub & N_MLSTM) != 0), _log_sigmoid(gpt), gpt)


def _proj_call(x, mods, g_norm, w_row, w_col, bg, bgt, cos_t, sin_t, kv_prev, *, layer, batch, seq, tm,
               mod_row0, per_seq_mod, rope, kv_stacked):
    n = x.shape[0]
    n_seq, tps = max(tm // seq, 1), max(seq // tm, 1)
    rows = tm // n_seq
    if rope and n_seq > 1:
        raise ValueError("rotary tables are indexed per sequence piece")

    def head_idx(i):
        return (i // tps, 0, i % tps, 0)

    head_shape = jax.ShapeDtypeStruct((batch, N_ATTN, seq, DV_ATTN), BF16)
    head_spec = pl.BlockSpec((n_seq, N_ATTN, rows, DV_ATTN), head_idx)
    if kv_stacked:
        kv_shape = jax.ShapeDtypeStruct((batch, DEPTH, N_ATTN, seq, DV_ATTN), F32)
        kv_spec = pl.BlockSpec((n_seq, None, N_ATTN, rows, DV_ATTN),
                               lambda i: (i // tps, layer, 0, i % tps, 0))
    else:
        kv_shape, kv_spec = head_shape, head_spec
    tok_shape = jax.ShapeDtypeStruct((n, W_ATTN), F32)
    tok_bf16 = jax.ShapeDtypeStruct((n, W_ATTN), BF16)
    tok_spec = pl.BlockSpec((tm, W_ATTN), lambda i: (i, 0))
    rope_spec = pl.BlockSpec((rows, LANES), lambda i: (i % tps, 0))
    alias_in = list(kv_prev) if kv_prev is not None else []
    n_in = 9
    return pl.pallas_call(
        functools.partial(_proj_kernel, rope=rope, n_alias=len(alias_in), n_seq=n_seq, rows=rows),
        grid=(n // tm,),
        in_specs=[
            pl.BlockSpec((tm, D_MODEL), lambda i: (i, 0)),
            _mod_spec(tps if per_seq_mod else 0, mod_row0),
            pl.BlockSpec((None, None, 1, D_MODEL), lambda i: (layer, 1, 0, 0)),
            _layer_spec((D_MODEL, P_ROW), layer),
            _layer_spec((P_COL, D_MODEL), layer),
            _layer_spec((1, LANES), layer),
            _layer_spec((LANES, 1), layer),
            rope_spec, rope_spec,
        ] + [pl.BlockSpec(memory_space=pl.ANY)] * len(alias_in),
        out_specs=[head_spec, kv_spec, kv_spec, tok_spec, tok_spec, tok_spec, tok_spec,
                   pl.BlockSpec((tm, LANES), lambda i: (i, 0)),
                   pl.BlockSpec((W_ATTN, tm), lambda i: (0, i)),
                   pl.BlockSpec((W_MLSTM, tm), lambda i: (0, i)),
                   pl.BlockSpec((LANES, tm), lambda i: (0, i))],
        out_shape=[head_shape, kv_shape, kv_shape, tok_bf16, tok_bf16, tok_bf16, tok_shape,
                   jax.ShapeDtypeStruct((n, LANES), F32),
                   jax.ShapeDtypeStruct((W_ATTN, n), BF16),
                   jax.ShapeDtypeStruct((W_MLSTM, n), F32),
                   jax.ShapeDtypeStruct((LANES, n), F32)],
        input_output_aliases={n_in + j: 1 + j for j in range(len(alias_in))},
        compiler_params=_cparams(1),
        name="mixer_proj",
    )(x, mods, g_norm, w_row, w_col, bg, bgt, cos_t, sin_t, *alias_in)


def _attn_kernel(*refs, lam_init, cached, nb, tq, seq):
    if cached:
        q_ref, k_ref, vt_ref, ck_ref, cv_ref, lam_ref, g_ref, o_ref = refs
    else:
        q_ref, k_ref, vt_ref, lam_ref, g_ref, o_ref = refs
    lp = lam_ref[...]
    lam = (jnp.exp(jnp.sum(lp[0:1] * lp[1:2], axis=-1, keepdims=True))
           - jnp.exp(jnp.sum(lp[2:3] * lp[3:4], axis=-1, keepdims=True)) + lam_init)
    g_sub = g_ref[...]
    lane = lax.broadcasted_iota(jnp.int32, (tq, DV_ATTN), 1)
    for bi in range(nb):
        for h in range(N_ATTN):
            hs = slice(h * DV_ATTN, (h + 1) * DV_ATTN)
            q = q_ref[bi, h].astype(F32) * ATTN_SCALE
            q_maps = (jnp.where(lane < DH_ATTN, q, 0.0).astype(BF16),
                      jnp.where(lane >= DH_ATTN, q, 0.0).astype(BF16))
            keys = [k_ref[bi, h].astype(BF16)]
            vals_t = [vt_ref[hs, bi * seq:(bi + 1) * seq]]
            if cached:
                keys.insert(0, ck_ref[bi, h].astype(BF16))
                vals_t.insert(0, cv_ref[bi, h].T.astype(BF16))
            exps, dens = [], []
            for qm in q_maps:
                s = [_bdot_nt(kk, qm) for kk in keys]
                m = functools.reduce(jnp.maximum, [jnp.max(si, axis=0, keepdims=True) for si in s])
                e = [jnp.exp(si - m) for si in s]
                exps.append(e)
                dens.append(functools.reduce(jnp.add, [jnp.sum(ei, axis=0, keepdims=True) for ei in e]))
            r1 = 1.0 / dens[0]
            r2 = lam / dens[1]
            o_t = None
            for j, vv in enumerate(vals_t):
                p = exps[0][j] * r1 - exps[1][j] * r2
                part = jnp.dot(vv, p.astype(BF16), preferred_element_type=F32)
                o_t = part if o_t is None else o_t + part
            y_t = o_t * lax.rsqrt(jnp.mean(o_t * o_t, axis=0, keepdims=True) + EPS)
            o_ref[bi * tq:(bi + 1) * tq, hs] = (y_t.T * g_sub * (1.0 - lam_init)).astype(o_ref.dtype)


def _heads_spec(arr, nb, rows, layer):
    if arr.ndim == 5:
        return pl.BlockSpec((nb, None, N_ATTN, rows, DV_ATTN), lambda b, i: (b, layer, 0, 0, 0))
    return pl.BlockSpec((nb, N_ATTN, rows, DV_ATTN), lambda b, i: (b, 0, 0, 0))


def _attn_call(q, k, v_t, cache, lam_p, g_sub, *, layer, lam_init, tq, nb):
    batch, _, seq, _ = q.shape
    nq = seq // tq
    if nq > 1 and nb > 1:
        raise ValueError("output row blocks are contiguous only for nb == 1 or tq == seq")
    in_specs = [pl.BlockSpec((nb, N_ATTN, tq, DV_ATTN), lambda b, i: (b, 0, i, 0)),
                _heads_spec(k, nb, seq, layer),
                pl.BlockSpec((W_ATTN, nb * seq), lambda b, i: (0, b))]
    args = [q, k, v_t]
    if cache is not None:
        past = cache[0].shape[3]
        in_specs += [_heads_spec(cache[0], nb, past, layer), _heads_spec(cache[1], nb, past, layer)]
        args += list(cache)
    in_specs += [pl.BlockSpec((None, 4, DH_ATTN), lambda b, i: (layer, 0, 0)),
                 pl.BlockSpec((None, 1, DV_ATTN), lambda b, i: (layer, 0, 0))]
    args += [lam_p, g_sub]
    return pl.pallas_call(
        functools.partial(_attn_kernel, lam_init=lam_init, cached=cache is not None, nb=nb, tq=tq, seq=seq),
        grid=(batch // nb, nq),
        in_specs=in_specs,
        out_specs=pl.BlockSpec((nb * tq, W_ATTN), lambda b, i: (b * nq + i, 0)),
        out_shape=jax.ShapeDtypeStruct((batch * seq, W_ATTN), BF16),
        compiler_params=_cparams(2),
        name="diff_attn",
    )(*args)


def _dft_tables(seq):
    def cs(n):
        j = np.arange(n)
        ang = 2.0 * np.pi * ((j[:, None] * j[None, :]) % n) / n
        return np.cos(ang) / math.sqrt(n), np.sin(ang) / math.sqrt(n)

    cd, sd = cs(DG_FOUR)
    ct, st = cs(seq)
    w_d = jnp.asarray(np.concatenate([cd, sd], axis=1), F32)
    return w_d.astype(BF16), jnp.asarray(ct, F32).astype(BF16), jnp.asarray(-st, F32).astype(BF16)


def _four_kernel(z_ref, wd_ref, ct_ref, st_ref, o_ref):
    y_cos, y_sin = [], []
    for gidx in range(N_FOUR):
        y = jnp.dot(z_ref[:, gidx * DG_FOUR:(gidx + 1) * DG_FOUR].astype(BF16), wd_ref[...],
                    preferred_element_type=F32)
        y_cos.append(y[:, :DG_FOUR].astype(BF16))
        y_sin.append(y[:, DG_FOUR:].astype(BF16))
    y_cos = jnp.concatenate(y_cos, axis=1)
    y_sin = jnp.concatenate(y_sin, axis=1)
    o_ref[...] = (jnp.dot(ct_ref[...], y_cos, preferred_element_type=F32)
                  + jnp.dot(st_ref[...], y_sin, preferred_element_type=F32)).astype(o_ref.dtype)


def _four_call(zf, *, batch, seq):
    w_d, ct, st = _dft_tables(seq)
    return pl.pallas_call(
        _four_kernel,
        grid=(batch,),
        in_specs=[
            pl.BlockSpec((seq, W_FOUR), lambda b: (b, 0)),
            pl.BlockSpec((DG_FOUR, 2 * DG_FOUR), lambda b: (0, 0)),
            pl.BlockSpec((seq, seq), lambda b: (0, 0)),
            pl.BlockSpec((seq, seq), lambda b: (0, 0)),
        ],
        out_specs=pl.BlockSpec((seq, W_FOUR), lambda b: (b, 0)),
        out_shape=jax.ShapeDtypeStruct((batch * seq, W_FOUR), BF16),
        compiler_params=_cparams(1),
        name="fourier_mix",
    )(zf, w_d, ct, st)


def _mlstm_kernel(*refs, seq, nb, seeded, emit_state, n_alias):
    mq_ref, mk_ref, vt_ref, mo_ref, gate_ref, gt_ref, g_ref = refs[:7]
    pos = 7
    if seeded:
        c0_ref, n0_ref, m0_ref = refs[pos:pos + 3]
        pos += 3
    pos += n_alias
    o_ref = refs[pos]
    if emit_state:
        c1_ref, n1_ref, m1_ref = refs[pos + 1:pos + 4]

    L = min(MLSTM_CHUNK, seq)
    nc = seq // L
    s_idx = lax.broadcasted_iota(jnp.int32, (L, L), 0)
    t_idx = lax.broadcasted_iota(jnp.int32, (L, L), 1)
    before = (s_idx <= t_idx, s_idx >= t_idx)
    tri = (jnp.where(before[1], 1.0, 0.0), jnp.where(before[0], 1.0, 0.0))
    chains = [(bi, d, h) for bi in range(nb) for d in range(2) for h in range(N_MLSTM)]

    state = {}
    for (bi, d, h) in chains:
        if seeded:
            state[bi, d, h] = (c0_ref[bi, d, h], n0_ref[bi, d, h:h + 1, :], m0_ref[bi, d:d + 1, h:h + 1])
        else:
            state[bi, d, h] = (None, None, jnp.zeros((1, 1), F32))

    h_t = {}
    for c in range(nc):
        terms = {}
        for bi in range(nb):
            for d in range(2):
                r0 = bi * seq + (c if d == 0 else nc - 1 - c) * L
                g = gate_ref[r0:r0 + L, :]
                g_t = gt_ref[:, r0:r0 + L]
                cum = jnp.dot(tri[d], g, preferred_element_type=F32, precision=HIGHEST)
                cum_t = _bdot_nt(g_t, tri[d], precision=HIGHEST)
                col = g - pltpu.roll(cum, LANES - N_MLSTM, 1)
                terms[bi, d] = (r0, col, g_t, cum_t)

        new_state = {}
        for (bi, d, h) in chains:
            r0, col, g_t, cum_t = terms[bi, d]
            c_prev, n_prev, m_prev = state[bi, d, h]
            has_state = c_prev is not None
            need_update = emit_state or c < nc - 1
            ci = 2 * d * N_MLSTM + h
            cf = ci + N_MLSTM
            last = L - 1 if d == 0 else 0
            hs = slice(h * DH_MLSTM, (h + 1) * DH_MLSTM)
            q = mq_ref[r0:r0 + L, hs]
            k = mk_ref[r0:r0 + L, hs]
            v_t = vt_ref[hs, r0:r0 + L]
            i_row, b_row = g_t[ci:ci + 1, :], cum_t[cf:cf + 1, :]

            cb = jnp.where(before[d], col[:, ci:ci + 1], -jnp.inf)
            m_row = jnp.maximum(jnp.max(cb, axis=0, keepdims=True), m_prev)
            s_t = _bdot_nt(k, q) * jnp.exp(cb - m_row)
            num_t = jnp.dot(v_t.astype(BF16), s_t.astype(BF16), preferred_element_type=F32)
            den = jnp.sum(s_t, axis=0, keepdims=True)
            if has_state:
                sp = jnp.exp(m_prev - m_row)
                cn = jnp.concatenate([c_prev, jnp.broadcast_to(n_prev, (SUBLANES, DH_MLSTM))], axis=0)
                cq = _bdot_nt(cn, q)
                num_t = num_t + sp * cq[:DH_MLSTM]
                den = den + sp * cq[DH_MLSTM:DH_MLSTM + 1]
            m_t = b_row + m_row
            h_t[bi, d, h, c] = num_t / jnp.maximum(jnp.abs(den), jnp.exp(-m_t))
            if need_update:
                b_tot = b_row[:, last:last + 1]
                m_new = m_t[:, last:last + 1]
                wl = jnp.exp(b_tot + (i_row - b_row) - m_new)
                vw = jnp.concatenate([v_t * wl, jnp.broadcast_to(wl, (SUBLANES, L))], axis=0)
                upd = jnp.dot(vw.astype(BF16), k, preferred_element_type=F32)
                c_new, n_new = upd[:DH_MLSTM], upd[DH_MLSTM:DH_MLSTM + 1]
                if has_state:
                    decay = jnp.exp(b_tot + m_prev - m_new)
                    c_new = decay * c_prev + c_new
                    n_new = decay * n_prev + n_new
                new_state[bi, d, h] = (c_new, n_new, m_new)
        state = new_state

    g_m = g_ref[...]
    for bi in range(nb):
        for h in range(N_MLSTM):
            hs = slice(h * DH_MLSTM, (h + 1) * DH_MLSTM)
            fwd = [h_t[bi, 0, h, c] for c in range(nc)]
            bwd = [h_t[bi, 1, h, nc - 1 - c] for c in range(nc)]
            hsum = (fwd[0] if nc == 1 else jnp.concatenate(fwd, axis=1)) \
                + (bwd[0] if nc == 1 else jnp.concatenate(bwd, axis=1))
            y = hsum * lax.rsqrt(jnp.mean(hsum * hsum, axis=0, keepdims=True) + EPS)
            rows = slice(bi * seq, (bi + 1) * seq)
            o_ref[rows, hs] = (y.T * g_m * _sigmoid(mo_ref[rows, hs])).astype(o_ref.dtype)

    if emit_state:
        for (bi, d, h) in chains:
            c_fin, n_fin, m_fin = state[bi, d, h]
            c1_ref[bi, d, h] = c_fin
            n1_ref[bi, d, h:h + 1, :] = n_fin
            m1_ref[bi, d:d + 1, h:h + 1] = m_fin


def _mlstm_call(mq, mk, vt, mo, gates, gates_t, g_m, state, state_prev, *, layer, batch, seq, nb, emit_state):
    rows = nb * seq
    tok_spec = pl.BlockSpec((rows, W_MLSTM), lambda b: (b, 0))
    in_specs = [tok_spec, tok_spec,
                pl.BlockSpec((W_MLSTM, rows), lambda b: (0, b)),
                tok_spec,
                pl.BlockSpec((rows, LANES), lambda b: (b, 0)),
                pl.BlockSpec((LANES, rows), lambda b: (0, b)),
                pl.BlockSpec((None, 1, DH_MLSTM), lambda b: (layer, 0, 0))]
    args = [mq, mk, vt, mo, gates, gates_t, g_m]
    state_specs = [
        pl.BlockSpec((nb, None, 2, N_MLSTM, DH_MLSTM, DH_MLSTM), lambda b: (b, layer, 0, 0, 0, 0)),
        pl.BlockSpec((nb, None, 2, N_MLSTM, DH_MLSTM), lambda b: (b, layer, 0, 0, 0)),
        pl.BlockSpec((nb, None, 2, N_MLSTM), lambda b: (b, layer, 0, 0)),
    ]
    if state is not None:
        in_specs += state_specs
        args += list(state)
    alias_in = list(state_prev) if state_prev is not None else []
    n_in = len(args)
    in_specs += [pl.BlockSpec(memory_space=pl.ANY)] * len(alias_in)
    args += alias_in
    out_specs = [tok_spec]
    out_shape = [jax.ShapeDtypeStruct((batch * seq, W_MLSTM), BF16)]
    if emit_state:
        out_specs += state_specs
        out_shape += [
            jax.ShapeDtypeStruct((batch, DEPTH, 2, N_MLSTM, DH_MLSTM, DH_MLSTM), F32),
            jax.ShapeDtypeStruct((batch, DEPTH, 2, N_MLSTM, DH_MLSTM), F32),
            jax.ShapeDtypeStruct((batch, DEPTH, 2, N_MLSTM), F32),
        ]
    return pl.pallas_call(
        functools.partial(_mlstm_kernel, seq=seq, nb=nb, seeded=state is not None, emit_state=emit_state,
                          n_alias=len(alias_in)),
        grid=(batch // nb,),
        in_specs=in_specs,
        out_specs=out_specs,
        out_shape=out_shape,
        input_output_aliases={n_in + j: 1 + j for j in range(len(alias_in))},
        compiler_params=_cparams(1),
        name="mlstm",
    )(*args)


def _merge_kernel(x_ref, a_ref, f_ref, m_ref, mod_ref, g_ref, wg_ref, wa_ref, wf_ref, wm_ref, wo_ref, o_ref):
    x = x_ref[...]
    mod = mod_ref[...]
    u = _modulated_norm(x, g_ref[...], mod, 3).astype(BF16)
    merged = None
    for j, (br_ref, w_ref) in enumerate(((a_ref, wa_ref), (f_ref, wf_ref), (m_ref, wm_ref))):
        gate = _sigmoid(jnp.dot(u, wg_ref[:, j * D_MODEL:(j + 1) * D_MODEL], preferred_element_type=F32))
        term = gate * _bdot(br_ref[...], w_ref[...])
        merged = term if merged is None else merged + term
    out = _bdot(merged, wo_ref[...])
    o_ref[...] = x + mod[5:6] * out


def _merge_call(x, a, f, m, mods, g_norm, wg, wa, wf, wm, wo, *, layer, tm, seq_tiles, mod_row0):
    n = x.shape[0]
    br_spec = pl.BlockSpec((tm, W_ATTN), lambda i: (i, 0))
    return pl.pallas_call(
        _merge_kernel,
        grid=(n // tm,),
        in_specs=[
            pl.BlockSpec((tm, D_MODEL), lambda i: (i, 0)),
            br_spec, br_spec, br_spec,
            _mod_spec(seq_tiles, mod_row0),
            pl.BlockSpec((None, None, 1, D_MODEL), lambda i: (layer, 1, 0, 0)),
            _layer_spec((D_MODEL, 3 * D_MODEL), layer),
            _layer_spec((W_ATTN, D_MODEL), layer), _layer_spec((W_FOUR, D_MODEL), layer),
            _layer_spec((W_MLSTM, D_MODEL), layer),
            _layer_spec((D_MODEL, D_MODEL), layer),
        ],
        out_specs=pl.BlockSpec((tm, D_MODEL), lambda i: (i, 0)),
        out_shape=jax.ShapeDtypeStruct((n, D_MODEL), F32),
        compiler_params=_cparams(1),
        name="merge",
    )(x, a, f, m, mods, g_norm, wg, wa, wf, wm, wo)


def _rope_tables(n_tok):
    tok = np.arange(n_tok)
    inv = ROPE_BASE ** (-np.arange(ROPE_AXIS_PAIRS, dtype=np.float32) / ROPE_AXIS_PAIRS)
    ang = np.concatenate([(tok // GRID_W).astype(np.float32)[:, None] * inv,
                          (tok % GRID_W).astype(np.float32)[:, None] * inv], axis=-1).astype(np.float32)
    c, s = np.cos(ang), np.sin(ang)
    cos_t = np.concatenate([c, c, c, c], axis=-1)
    sin_t = np.concatenate([-s, s, -s, s], axis=-1)
    return jnp.asarray(cos_t, F32), jnp.asarray(sin_t, F32)


def _trunk(x, group, l, mods, w, cache, state, ctx_prev):
    batch, seq, mod_row0, per_seq_mod, nb, attn_nb, attn_tq = group
    tm = 512
    tm_ffn = 1024
    if per_seq_mod and (seq % tm or seq % tm_ffn):
        raise ValueError("row tile spans several conditioned sequences")
    seq_tiles = seq // tm if per_seq_mod else 0
    ffn_tiles = seq // tm_ffn if per_seq_mod else 0
    lam_init = 0.8 - 0.6 * math.exp(-0.3 * l)
    is_ctx = cache is None
    x = _ffn_call(x, mods, w['g_norm'], w['g_final'], w['ffn1_in'], w['ffn1_out'], layer=l, sub=0,
                  final=False, tm=tm_ffn, seq_tiles=ffn_tiles, mod_row0=mod_row0)
    cos_t, sin_t = _rope_tables(seq)
    q, k, v, zf, mq, mk, mo, gates, vat, vt, gates_t = _proj_call(
        x, mods, w['g_norm'], w['w_row'], w['w_col'], w['b_gate'], w['b_gate_t'], cos_t, sin_t,
        ctx_prev[:2] if ctx_prev is not None else None,
        layer=l, batch=batch, seq=seq, tm=tm, mod_row0=mod_row0, per_seq_mod=per_seq_mod,
        rope=not is_ctx, kv_stacked=is_ctx)
    a = _attn_call(q, k, vat, cache, w['attn_lambda'], w['g_attn_sub'], layer=l, lam_init=lam_init,
                   tq=attn_tq, nb=attn_nb)
    f = _four_call(zf, batch=batch, seq=seq)
    m_res = _mlstm_call(mq, mk, vt, mo, gates, gates_t, w['g_mlstm'], state,
                        ctx_prev[2:] if ctx_prev is not None else None,
                        layer=l, batch=batch, seq=seq, nb=nb, emit_state=is_ctx)
    x = _merge_call(x, a, f, m_res[0], mods, w['g_norm'], w['w_bg'], w['w_br_a'], w['w_br_f'], w['w_br_m'],
                    w['w_out'], layer=l, tm=tm_ffn, seq_tiles=ffn_tiles, mod_row0=mod_row0)
    x = _ffn_call(x, mods, w['g_norm'], w['g_final'], w['ffn2_in'], w['ffn2_out'], layer=l, sub=2,
                  final=(l == DEPTH - 1), tm=tm_ffn, seq_tiles=ffn_tiles, mod_row0=mod_row0)
    return x, ((k, v) + tuple(m_res[1:]) if is_ctx else None)


def kernel(x_prompt, x_sample, cache_k, cache_v, state_C, state_n, state_m, c, c_ctx, w_ada, b_ada, g_norm,
           w_ffn1_in, w_ffn1_out, w_ffn2_in, w_ffn2_out, w_in, b_mgate, attn_lambda, g_attn_sub, g_mlstm,
           w_branch_gate, w_br_attn, w_br_four, w_br_mlstm, w_out, g_final):
    batch, seq, _ = x_prompt.shape
    dec_batch, dec_seq, _ = x_sample.shape
    cond = jnp.zeros((N_COND, D_MODEL), F32).at[0].set(c_ctx).at[1:1 + dec_batch].set(c)
    mods_all = _ada_call(cond, w_ada, b_ada).reshape(DEPTH, N_COND, N_MOD, D_MODEL)

    gate_w = jnp.pad(w_in[:, :, P_MAIN:], ((0, 0), (0, 0), (0, LANES - N_GATE)))
    b_gate = jnp.pad(b_mgate, ((0, 0), (0, LANES - N_GATE)))
    w = {
        'g_norm': g_norm[:, :, None, :], 'g_final': g_final[None, :],
        'ffn1_in': w_ffn1_in.astype(BF16), 'ffn1_out': w_ffn1_out.astype(BF16),
        'ffn2_in': w_ffn2_in.astype(BF16), 'ffn2_out': w_ffn2_out.astype(BF16),
        'w_row': jnp.concatenate([w_in[:, :, :MV_LO], w_in[:, :, MV_HI:P_MAIN], gate_w], axis=-1).astype(BF16),
        'w_col': jnp.swapaxes(jnp.concatenate(
            [w_in[:, :, 2 * W_ATTN:3 * W_ATTN], w_in[:, :, MV_LO:MV_HI], gate_w], axis=-1), 1, 2).astype(BF16),
        'b_gate': b_gate[:, None, :], 'b_gate_t': b_gate[:, :, None],
        'attn_lambda': attn_lambda, 'g_attn_sub': g_attn_sub[:, None, :], 'g_mlstm': g_mlstm[:, None, :],
        'w_bg': w_branch_gate.astype(BF16), 'w_br_a': w_br_attn.astype(BF16),
        'w_br_f': w_br_four.astype(BF16), 'w_br_m': w_br_mlstm.astype(BF16), 'w_out': w_out.astype(BF16),
    }

    hp = x_prompt.reshape(batch * seq, D_MODEL)
    hs = x_sample.reshape(dec_batch * dec_seq, D_MODEL)
    prompt_group = (batch, seq, 0, False, 2, 4, seq)
    sample_group = (dec_batch, dec_seq, 1, True, 1, 1, dec_seq // 2)
    ctx = None
    for l in range(DEPTH):
        hp, ctx = _trunk(hp, prompt_group, l, mods_all[l], w, None, None, ctx)
        hs, _ = _trunk(hs, sample_group, l, mods_all[l], w, (cache_k, cache_v),
                       (state_C, state_n, state_m), None)
    return (hp.reshape(batch, seq, D_MODEL), hs.reshape(dec_batch, dec_seq, D_MODEL), *ctx)
```

```python
import functools
import math

import numpy as np
import jax
import jax.numpy as jnp
from jax import lax
from jax.experimental import pallas as pl
from jax.experimental.pallas import tpu as pltpu

D_MODEL = 1024
DEPTH = 2
GRID_W = 64
N_ATTN = 4
DH_ATTN = 64
DV_ATTN = 2 * DH_ATTN
W_ATTN = N_ATTN * DV_ATTN
N_FOUR = 4
DG_FOUR = 128
W_FOUR = N_FOUR * DG_FOUR
N_MLSTM = 4
DH_MLSTM = 128
W_MLSTM = N_MLSTM * DH_MLSTM
N_GATE = 4 * N_MLSTM
LANES = 128
SUBLANES = 16
MV_LO = 3 * W_ATTN + W_FOUR + 2 * W_MLSTM
MV_HI = MV_LO + W_MLSTM
P_MAIN = MV_HI + W_MLSTM
P_ROW = P_MAIN - W_MLSTM
P_COL = W_ATTN + W_MLSTM + LANES
D_FF = 2816
N_MOD = 9
N_COND = 8
ROPE_BASE = 10000.0
ROPE_AXIS_PAIRS = DH_ATTN // 4
ATTN_SCALE = DH_ATTN ** -0.5
LOG2E = 1.4426950408889634
MLSTM_K_SCALE = DH_MLSTM ** -0.5
EPS = 1e-6
MLSTM_CHUNK = 256
VMEM_LIMIT = 56 * 1024 * 1024

F32 = jnp.float32
BF16 = jnp.bfloat16
HIGHEST = lax.Precision.HIGHEST


def _cparams(n_grid):
    return pltpu.CompilerParams(dimension_semantics=("arbitrary",) * n_grid,
                                vmem_limit_bytes=VMEM_LIMIT)


def _bdot(a, b):
    return jnp.dot(a.astype(BF16), b.astype(BF16), preferred_element_type=F32)


def _bdot_nt(a, b, precision=None):
    if precision is None:
        a, b = a.astype(BF16), b.astype(BF16)
    return lax.dot_general(a, b, (((1,), (1,)), ((), ())), preferred_element_type=F32,
                           precision=precision)


def _sigmoid(x):
    return 1.0 / (1.0 + jnp.exp(-x))


def _log_sigmoid(x):
    return jnp.minimum(x, 0.0) - jnp.log1p(jnp.exp(-jnp.abs(x)))


def _rms(x, g):
    return x * lax.rsqrt(jnp.mean(x * x, axis=-1, keepdims=True) + EPS) * g


def _modulated_norm(x, g, mod, base):
    return _rms(x, g) * (1.0 + mod[base + 1:base + 2]) + mod[base:base + 1]


def _layer_spec(shape, layer):
    return pl.BlockSpec((None,) + shape, lambda *_: (layer,) + (0,) * len(shape),
                        pipeline_mode=pl.Buffered(1))


def _mod_spec(seq_tiles, mod_row0):
    if seq_tiles == 0:
        return pl.BlockSpec((None, N_MOD, D_MODEL), lambda i: (mod_row0, 0, 0))
    return pl.BlockSpec((None, N_MOD, D_MODEL), lambda i: (mod_row0 + i // seq_tiles, 0, 0))


def _ada_kernel(c_ref, w_ref, b_ref, o_ref):
    c = c_ref[...]
    s = c * _sigmoid(c)
    o_ref[...] = _bdot(s, w_ref[...]) + b_ref[...]


def _ada_call(cond, w_ada, b_ada):
    tn = 1024
    n_out = N_MOD * D_MODEL
    return pl.pallas_call(
        _ada_kernel,
        grid=(DEPTH, n_out // tn),
        in_specs=[
            pl.BlockSpec((N_COND, D_MODEL), lambda l, j: (0, 0)),
            pl.BlockSpec((None, D_MODEL, tn), lambda l, j: (l, 0, j)),
            pl.BlockSpec((None, 1, tn), lambda l, j: (l, 0, j)),
        ],
        out_specs=pl.BlockSpec((None, N_COND, tn), lambda l, j: (l, 0, j)),
        out_shape=jax.ShapeDtypeStruct((DEPTH, N_COND, n_out), F32),
        compiler_params=_cparams(2),
        name="adaln",
    )(cond, w_ada, b_ada.reshape(DEPTH, 1, n_out))


FF_CHUNKS = (768, 768, 768, 512)


def _ffn_kernel(x_ref, mod_ref, g_ref, gf_ref, win_ref, wout_ref, o_ref, *, base, final):
    x = x_ref[...]
    mod = mod_ref[...]
    u = _modulated_norm(x, g_ref[...], mod, base).astype(BF16)
    y = None
    lo = 0
    for width in FF_CHUNKS:
        a = jnp.dot(u, win_ref[:, lo:lo + width], preferred_element_type=F32)
        g = jnp.dot(u, win_ref[:, D_FF + lo:D_FF + lo + width], preferred_element_type=F32)
        hh = (a * _sigmoid(a) * g).astype(BF16)
        part = jnp.dot(hh, wout_ref[lo:lo + width, :], preferred_element_type=F32)
        y = part if y is None else y + part
        lo += width
    xn = x + 0.5 * mod[base + 2:base + 3] * y
    if final:
        xn = _rms(xn, gf_ref[...])
    o_ref[...] = xn


def _ffn_call(x, mods, g_norm, g_final, w_in, w_out, *, layer, sub, final, tm, seq_tiles, mod_row0):
    n = x.shape[0]
    return pl.pallas_call(
        functools.partial(_ffn_kernel, base=3 * sub, final=final),
        grid=(n // tm,),
        in_specs=[
            pl.BlockSpec((tm, D_MODEL), lambda i: (i, 0)),
            _mod_spec(seq_tiles, mod_row0),
            pl.BlockSpec((None, None, 1, D_MODEL), lambda i: (layer, sub, 0, 0)),
            pl.BlockSpec((1, D_MODEL), lambda i: (0, 0)),
            _layer_spec((D_MODEL, 2 * D_FF), layer),
            _layer_spec((D_FF, D_MODEL), layer),
        ],
        out_specs=pl.BlockSpec((tm, D_MODEL), lambda i: (i, 0)),
        out_shape=jax.ShapeDtypeStruct((n, D_MODEL), F32),
        compiler_params=_cparams(1),
        name="ffn",
    )(x, mods, g_norm, g_final, w_in, w_out)


def _swap32(x):
    lane = lax.broadcasted_iota(jnp.int32, x.shape, 1)
    return jnp.where((lane & (DH_ATTN // 2)) == 0,
                     pltpu.roll(x, LANES - DH_ATTN // 2, 1), pltpu.roll(x, DH_ATTN // 2, 1))


def _proj_kernel(*refs, rope, n_alias, n_seq, rows):
    (x_ref, mod_ref, g_ref, w_ref, wt_ref, bgt_ref, cos_ref, sin_ref) = refs[:8]
    (q_ref, k_ref, v_ref, zf_ref, mq_ref, mk_ref, mo_ref, vat_ref, vt_ref, gt_ref) = refs[8 + n_alias:]
    x = x_ref[...]
    u = _modulated_norm(x, g_ref[...], mod_ref[...], 3).astype(BF16)
    za = jnp.dot(u, w_ref[:, :3 * W_ATTN], preferred_element_type=F32)
    zb = jnp.dot(u, w_ref[:, 3 * W_ATTN:], preferred_element_type=F32)
    zt = _bdot_nt(wt_ref[...], u)
    if rope:
        cos_t = cos_ref[...]
        sin_t = sin_ref[...]
    for h in range(N_ATTN):
        zq = za[:, h * DV_ATTN:(h + 1) * DV_ATTN]
        zk = za[:, W_ATTN + h * DV_ATTN:W_ATTN + (h + 1) * DV_ATTN]
        zv = za[:, 2 * W_ATTN + h * DV_ATTN:2 * W_ATTN + (h + 1) * DV_ATTN]
        if rope:
            zq = zq * cos_t + _swap32(zq) * sin_t
            zk = zk * cos_t + _swap32(zk) * sin_t
        for s in range(n_seq):
            rs = slice(s * rows, (s + 1) * rows)
            q_ref[s, h] = zq[rs].astype(q_ref.dtype)
            k_ref[s, h] = zk[rs].astype(k_ref.dtype)
            v_ref[s, h] = zv[rs].astype(v_ref.dtype)
    zf_ref[...] = zb[:, :W_FOUR].astype(zf_ref.dtype)
    off = W_FOUR
    mq_ref[...] = zb[:, off:off + W_MLSTM].astype(mq_ref.dtype)
    mk_ref[...] = (zb[:, off + W_MLSTM:off + 2 * W_MLSTM] * MLSTM_K_SCALE).astype(mk_ref.dtype)
    mo_ref[...] = zb[:, off + 2 * W_MLSTM:off + 3 * W_MLSTM]
    vat_ref[...] = zt[:W_ATTN].astype(vat_ref.dtype)
    vt_ref[...] = zt[W_ATTN:W_ATTN + W_MLSTM]
    gpt = zt[W_ATTN + W_MLSTM:] + bgt_ref[...]
    sub = lax.broadcasted_iota(jnp.int32, gpt.shape, 0)
    gt_ref[...] = jnp.where((sub < N_GATE) & ((sub & N_MLSTM) != 0), _log_sigmoid(gpt), gpt)


def _proj_call(x, mods, g_norm, w_row, w_col, bgt, cos_t, sin_t, kv_prev, *, layer, batch, seq, tm,
               mod_row0, per_seq_mod, rope, kv_stacked):
    n = x.shape[0]
    n_seq, tps = max(tm // seq, 1), max(seq // tm, 1)
    rows = tm // n_seq
    if rope and n_seq > 1:
        raise ValueError("rotary tables are indexed per sequence piece")

    def head_idx(i):
        return (i // tps, 0, i % tps, 0)

    head_shape = jax.ShapeDtypeStruct((batch, N_ATTN, seq, DV_ATTN), BF16)
    head_spec = pl.BlockSpec((n_seq, N_ATTN, rows, DV_ATTN), head_idx)
    if kv_stacked:
        kv_shape = jax.ShapeDtypeStruct((batch, DEPTH, N_ATTN, seq, DV_ATTN), F32)
        kv_spec = pl.BlockSpec((n_seq, None, N_ATTN, rows, DV_ATTN),
                               lambda i: (i // tps, layer, 0, i % tps, 0))
    else:
        kv_shape, kv_spec = head_shape, head_spec
    tok_shape = jax.ShapeDtypeStruct((n, W_ATTN), F32)
    tok_bf16 = jax.ShapeDtypeStruct((n, W_ATTN), BF16)
    tok_spec = pl.BlockSpec((tm, W_ATTN), lambda i: (i, 0))
    rope_spec = pl.BlockSpec((rows, LANES), lambda i: (i % tps, 0))
    alias_in = list(kv_prev) if kv_prev is not None else []
    n_in = 8
    return pl.pallas_call(
        functools.partial(_proj_kernel, rope=rope, n_alias=len(alias_in), n_seq=n_seq, rows=rows),
        grid=(n // tm,),
        in_specs=[
            pl.BlockSpec((tm, D_MODEL), lambda i: (i, 0)),
            _mod_spec(tps if per_seq_mod else 0, mod_row0),
            pl.BlockSpec((None, None, 1, D_MODEL), lambda i: (layer, 1, 0, 0)),
            _layer_spec((D_MODEL, P_ROW), layer),
            _layer_spec((P_COL, D_MODEL), layer),
            _layer_spec((LANES, 1), layer),
            rope_spec, rope_spec,
        ] + [pl.BlockSpec(memory_space=pl.ANY)] * len(alias_in),
        out_specs=[head_spec, kv_spec, kv_spec, tok_spec, tok_spec, tok_spec, tok_spec,
                   pl.BlockSpec((W_ATTN, tm), lambda i: (0, i)),
                   pl.BlockSpec((W_MLSTM, tm), lambda i: (0, i)),
                   pl.BlockSpec((LANES, tm), lambda i: (0, i))],
        out_shape=[head_shape, kv_shape, kv_shape, tok_bf16, tok_bf16, tok_bf16, tok_shape,
                   jax.ShapeDtypeStruct((W_ATTN, n), BF16),
                   jax.ShapeDtypeStruct((W_MLSTM, n), F32),
                   jax.ShapeDtypeStruct((LANES, n), F32)],
        input_output_aliases={n_in + j: 1 + j for j in range(len(alias_in))},
        compiler_params=_cparams(1),
        name="mixer_proj",
    )(x, mods, g_norm, w_row, w_col, bgt, cos_t, sin_t, *alias_in)


def _attn_kernel(*refs, lam_init, cached, nb, tq, seq):
    if cached:
        q_ref, k_ref, vt_ref, ck_ref, cv_ref, lam_ref, g_ref, o_ref = refs
    else:
        q_ref, k_ref, vt_ref, lam_ref, g_ref, o_ref = refs
    lp = lam_ref[...]
    lam = (jnp.exp(jnp.sum(lp[0:1] * lp[1:2], axis=-1, keepdims=True))
           - jnp.exp(jnp.sum(lp[2:3] * lp[3:4], axis=-1, keepdims=True)) + lam_init)
    g_sub = g_ref[...]
    lane = lax.broadcasted_iota(jnp.int32, (tq, DV_ATTN), 1)
    results = []
    for bi in range(nb):
        for h in range(N_ATTN):
            hs = slice(h * DV_ATTN, (h + 1) * DV_ATTN)
            q = q_ref[bi, h].astype(F32) * (ATTN_SCALE * LOG2E)
            q_maps = (jnp.where(lane < DH_ATTN, q, 0.0).astype(BF16),
                      jnp.where(lane >= DH_ATTN, q, 0.0).astype(BF16))
            keys = [k_ref[bi, h].astype(BF16)]
            vals_t = [vt_ref[hs, bi * seq:(bi + 1) * seq]]
            if cached:
                keys.insert(0, ck_ref[bi, h].astype(BF16))
                vals_t.insert(0, cv_ref[bi, h].T.astype(BF16))
            exps, dens = [], []
            for qm in q_maps:
                s = [_bdot_nt(kk, qm) for kk in keys]
                m = functools.reduce(jnp.maximum, [jnp.max(si, axis=0, keepdims=True) for si in s])
                e = [jnp.exp2(si - m) for si in s]
                exps.append(e)
                dens.append(functools.reduce(jnp.add, [jnp.sum(ei, axis=0, keepdims=True) for ei in e]))
            r1 = 1.0 / dens[0]
            r2 = lam / dens[1]
            o_t = None
            for j, vv in enumerate(vals_t):
                p = exps[0][j] * r1 - exps[1][j] * r2
                part = jnp.dot(vv, p.astype(BF16), preferred_element_type=F32)
                o_t = part if o_t is None else o_t + part
            y_t = o_t * lax.rsqrt(jnp.mean(o_t * o_t, axis=0, keepdims=True) + EPS)
            results.append((bi, hs, (y_t.T * g_sub * (1.0 - lam_init)).astype(o_ref.dtype)))
    for bi, hs, y in results:
        o_ref[bi * tq:(bi + 1) * tq, hs] = y


def _heads_spec(arr, nb, rows, layer):
    if arr.ndim == 5:
        return pl.BlockSpec((nb, None, N_ATTN, rows, DV_ATTN), lambda b, i: (b, layer, 0, 0, 0))
    return pl.BlockSpec((nb, N_ATTN, rows, DV_ATTN), lambda b, i: (b, 0, 0, 0))


def _attn_call(q, k, v_t, cache, lam_p, g_sub, *, layer, lam_init, tq, nb):
    batch, _, seq, _ = q.shape
    nq = seq // tq
    if nq > 1 and nb > 1:
        raise ValueError("output row blocks are contiguous only for nb == 1 or tq == seq")
    in_specs = [pl.BlockSpec((nb, N_ATTN, tq, DV_ATTN), lambda b, i: (b, 0, i, 0)),
                _heads_spec(k, nb, seq, layer),
                pl.BlockSpec((W_ATTN, nb * seq), lambda b, i: (0, b))]
    args = [q, k, v_t]
    if cache is not None:
        past = cache[0].shape[3]
        in_specs += [_heads_spec(cache[0], nb, past, layer), _heads_spec(cache[1], nb, past, layer)]
        args += list(cache)
    in_specs += [pl.BlockSpec((None, 4, DH_ATTN), lambda b, i: (layer, 0, 0)),
                 pl.BlockSpec((None, 1, DV_ATTN), lambda b, i: (layer, 0, 0))]
    args += [lam_p, g_sub]
    return pl.pallas_call(
        functools.partial(_attn_kernel, lam_init=lam_init, cached=cache is not None, nb=nb, tq=tq, seq=seq),
        grid=(batch // nb, nq),
        in_specs=in_specs,
        out_specs=pl.BlockSpec((nb * tq, W_ATTN), lambda b, i: (b * nq + i, 0)),
        out_shape=jax.ShapeDtypeStruct((batch * seq, W_ATTN), BF16),
        compiler_params=_cparams(2),
        name="diff_attn",
    )(*args)


def _dft_tables(seq):
    def cs(n):
        j = np.arange(n)
        ang = 2.0 * np.pi * ((j[:, None] * j[None, :]) % n) / n
        return np.cos(ang) / math.sqrt(n), np.sin(ang) / math.sqrt(n)

    cd, sd = cs(DG_FOUR)
    ct, st = cs(seq)
    w_d = jnp.asarray(np.concatenate([cd, sd], axis=1), F32)
    return w_d.astype(BF16), jnp.asarray(ct, F32).astype(BF16), jnp.asarray(-st, F32).astype(BF16)


def _four_kernel(z_ref, wd_ref, ct_ref, st_ref, o_ref, *, nb, seq):
    y_cos, y_sin = [], []
    for gidx in range(N_FOUR):
        y = jnp.dot(z_ref[:, gidx * DG_FOUR:(gidx + 1) * DG_FOUR].astype(BF16), wd_ref[...],
                    preferred_element_type=F32)
        y_cos.append(y[:, :DG_FOUR].astype(BF16))
        y_sin.append(y[:, DG_FOUR:].astype(BF16))
    y_cos = jnp.concatenate(y_cos, axis=1)
    y_sin = jnp.concatenate(y_sin, axis=1)
    for s in range(nb):
        rows = slice(s * seq, (s + 1) * seq)
        o_ref[rows, :] = (jnp.dot(ct_ref[...], y_cos[rows], preferred_element_type=F32)
                          + jnp.dot(st_ref[...], y_sin[rows], preferred_element_type=F32)).astype(o_ref.dtype)


def _four_call(zf, *, batch, seq, nb):
    w_d, ct, st = _dft_tables(seq)
    return pl.pallas_call(
        functools.partial(_four_kernel, nb=nb, seq=seq),
        grid=(batch // nb,),
        in_specs=[
            pl.BlockSpec((nb * seq, W_FOUR), lambda b: (b, 0)),
            pl.BlockSpec((DG_FOUR, 2 * DG_FOUR), lambda b: (0, 0)),
            pl.BlockSpec((seq, seq), lambda b: (0, 0)),
            pl.BlockSpec((seq, seq), lambda b: (0, 0)),
        ],
        out_specs=pl.BlockSpec((nb * seq, W_FOUR), lambda b: (b, 0)),
        out_shape=jax.ShapeDtypeStruct((batch * seq, W_FOUR), BF16),
        compiler_params=_cparams(1),
        name="fourier_mix",
    )(zf, w_d, ct, st)


def _mlstm_kernel(*refs, seq, nb, seeded, emit_state, n_alias):
    mq_ref, mk_ref, vt_ref, mo_ref, gt_ref, g_ref = refs[:6]
    pos = 6
    if seeded:
        c0_ref, n0_ref, m0_ref = refs[pos:pos + 3]
        pos += 3
    pos += n_alias
    o_ref = refs[pos]
    if emit_state:
        c1_ref, n1_ref, m1_ref = refs[pos + 1:pos + 4]

    L = min(MLSTM_CHUNK, seq)
    nc = seq // L
    s_idx = lax.broadcasted_iota(jnp.int32, (L, L), 0)
    t_idx = lax.broadcasted_iota(jnp.int32, (L, L), 1)
    before = (s_idx <= t_idx, s_idx >= t_idx)
    tri = (jnp.where(before[1], 1.0, 0.0).astype(BF16),
           jnp.where(before[0], 1.0, 0.0).astype(BF16))
    chains = [(bi, d, h) for bi in range(nb) for d in range(2) for h in range(N_MLSTM)]

    state = {}
    for (bi, d, h) in chains:
        if seeded:
            state[bi, d, h] = (c0_ref[bi, d, h], n0_ref[bi, d, h:h + 1, :], m0_ref[bi, d:d + 1, h:h + 1])
        else:
            state[bi, d, h] = (None, None, jnp.zeros((1, 1), F32))

    h_t = {}
    for c in range(nc):
        terms = {}
        for bi in range(nb):
            for d in range(2):
                r0 = bi * seq + (c if d == 0 else nc - 1 - c) * L
                g_t = gt_ref[0:N_GATE, r0:r0 + L]
                hi = g_t.astype(BF16)
                rem = g_t - hi.astype(F32)
                mid = rem.astype(BF16)
                lo = (rem - mid.astype(F32)).astype(BF16)
                pieces = _bdot_nt(jnp.concatenate([hi, mid, lo], axis=0), tri[d])
                cum_t = pieces[0:N_GATE] + pieces[N_GATE:2 * N_GATE] + pieces[2 * N_GATE:]
                c_t = g_t - pltpu.roll(cum_t, N_GATE - N_MLSTM, 0)
                col = jnp.concatenate([c_t, jnp.zeros((LANES - N_GATE, L), F32)], axis=0).T
                terms[bi, d] = (r0, col, g_t, cum_t)

        new_state = {}
        for (bi, d, h) in chains:
            r0, col, g_t, cum_t = terms[bi, d]
            c_prev, n_prev, m_prev = state[bi, d, h]
            has_state = c_prev is not None
            need_update = emit_state or c < nc - 1
            ci = 2 * d * N_MLSTM + h
            cf = ci + N_MLSTM
            last = L - 1 if d == 0 else 0
            hs = slice(h * DH_MLSTM, (h + 1) * DH_MLSTM)
            q = mq_ref[r0:r0 + L, hs]
            k = mk_ref[r0:r0 + L, hs]
            v_t = vt_ref[hs, r0:r0 + L]
            i_row, b_row = g_t[ci:ci + 1, :], cum_t[cf:cf + 1, :]

            cb = jnp.where(before[d], col[:, ci:ci + 1], -jnp.inf)
            m_row = jnp.maximum(jnp.max(cb, axis=0, keepdims=True), m_prev)
            s_t = _bdot_nt(k, q) * jnp.exp(cb - m_row)
            num_t = jnp.dot(v_t.astype(BF16), s_t.astype(BF16), preferred_element_type=F32)
            den = jnp.sum(s_t, axis=0, keepdims=True)
            if has_state:
                sp = jnp.exp(m_prev - m_row)
                cn = jnp.concatenate([c_prev, jnp.broadcast_to(n_prev, (SUBLANES, DH_MLSTM))], axis=0)
                cq = _bdot_nt(cn, q)
                num_t = num_t + sp * cq[:DH_MLSTM]
                den = den + sp * cq[DH_MLSTM:DH_MLSTM + 1]
            m_t = b_row + m_row
            h_t[bi, d, h, c] = num_t / jnp.maximum(jnp.abs(den), jnp.exp(-m_t))
            if need_update:
                b_tot = b_row[:, last:last + 1]
                m_new = m_t[:, last:last + 1]
                wl = jnp.exp(b_tot + (i_row - b_row) - m_new)
                vw = jnp.concatenate([v_t * wl, jnp.broadcast_to(wl, (SUBLANES, L))], axis=0)
                upd = jnp.dot(vw.astype(BF16), k, preferred_element_type=F32)
                c_new, n_new = upd[:DH_MLSTM], upd[DH_MLSTM:DH_MLSTM + 1]
                if has_state:
                    decay = jnp.exp(b_tot + m_prev - m_new)
                    c_new = decay * c_prev + c_new
                    n_new = decay * n_prev + n_new
                new_state[bi, d, h] = (c_new, n_new, m_new)
        state = new_state

    g_m = g_ref[...]
    for bi in range(nb):
        for h in range(N_MLSTM):
            hs = slice(h * DH_MLSTM, (h + 1) * DH_MLSTM)
            fwd = [h_t[bi, 0, h, c] for c in range(nc)]
            bwd = [h_t[bi, 1, h, nc - 1 - c] for c in range(nc)]
            hsum = (fwd[0] if nc == 1 else jnp.concatenate(fwd, axis=1)) \
                + (bwd[0] if nc == 1 else jnp.concatenate(bwd, axis=1))
            y = hsum * lax.rsqrt(jnp.mean(hsum * hsum, axis=0, keepdims=True) + EPS)
            rows = slice(bi * seq, (bi + 1) * seq)
            o_ref[rows, hs] = (y.T * g_m * _sigmoid(mo_ref[rows, hs])).astype(o_ref.dtype)

    if emit_state:
        for (bi, d, h) in chains:
            c_fin, n_fin, m_fin = state[bi, d, h]
            c1_ref[bi, d, h] = c_fin
            n1_ref[bi, d, h:h + 1, :] = n_fin
            m1_ref[bi, d:d + 1, h:h + 1] = m_fin


def _mlstm_call(mq, mk, vt, mo, gates_t, g_m, state, state_prev, *, layer, batch, seq, nb, emit_state):
    rows = nb * seq
    tok_spec = pl.BlockSpec((rows, W_MLSTM), lambda b: (b, 0))
    in_specs = [tok_spec, tok_spec,
                pl.BlockSpec((W_MLSTM, rows), lambda b: (0, b)),
                tok_spec,
                pl.BlockSpec((LANES, rows), lambda b: (0, b)),
                pl.BlockSpec((None, 1, DH_MLSTM), lambda b: (layer, 0, 0))]
    args = [mq, mk, vt, mo, gates_t, g_m]
    state_specs = [
        pl.BlockSpec((nb, None, 2, N_MLSTM, DH_MLSTM, DH_MLSTM), lambda b: (b, layer, 0, 0, 0, 0)),
        pl.BlockSpec((nb, None, 2, N_MLSTM, DH_MLSTM), lambda b: (b, layer, 0, 0, 0)),
        pl.BlockSpec((nb, None, 2, N_MLSTM), lambda b: (b, layer, 0, 0)),
    ]
    if state is not None:
        in_specs += state_specs
        args += list(state)
    alias_in = list(state_prev) if state_prev is not None else []
    n_in = len(args)
    in_specs += [pl.BlockSpec(memory_space=pl.ANY)] * len(alias_in)
    args += alias_in
    out_specs = [tok_spec]
    out_shape = [jax.ShapeDtypeStruct((batch * seq, W_MLSTM), BF16)]
    if emit_state:
        out_specs += state_specs
        out_shape += [
            jax.ShapeDtypeStruct((batch, DEPTH, 2, N_MLSTM, DH_MLSTM, DH_MLSTM), F32),
            jax.ShapeDtypeStruct((batch, DEPTH, 2, N_MLSTM, DH_MLSTM), F32),
            jax.ShapeDtypeStruct((batch, DEPTH, 2, N_MLSTM), F32),
        ]
    return pl.pallas_call(
        functools.partial(_mlstm_kernel, seq=seq, nb=nb, seeded=state is not None, emit_state=emit_state,
                          n_alias=len(alias_in)),
        grid=(batch // nb,),
        in_specs=in_specs,
        out_specs=out_specs,
        out_shape=out_shape,
        input_output_aliases={n_in + j: 1 + j for j in range(len(alias_in))},
        compiler_params=_cparams(1),
        name="mlstm",
    )(*args)


def _merge_kernel(x_ref, a_ref, f_ref, m_ref, mod_ref, g_ref, wg_ref, wa_ref, wf_ref, wm_ref, wo_ref, o_ref):
    x = x_ref[...]
    mod = mod_ref[...]
    u = _modulated_norm(x, g_ref[...], mod, 3).astype(BF16)
    merged = None
    for j, (br_ref, w_ref) in enumerate(((a_ref, wa_ref), (f_ref, wf_ref), (m_ref, wm_ref))):
        gate = _sigmoid(jnp.dot(u, wg_ref[:, j * D_MODEL:(j + 1) * D_MODEL], preferred_element_type=F32))
        term = gate * _bdot(br_ref[...], w_ref[...])
        merged = term if merged is None else merged + term
    out = _bdot(merged, wo_ref[...])
    o_ref[...] = x + mod[5:6] * out


def _merge_call(x, a, f, m, mods, g_norm, wg, wa, wf, wm, wo, *, layer, tm, seq_tiles, mod_row0):
    n = x.shape[0]
    br_spec = pl.BlockSpec((tm, W_ATTN), lambda i: (i, 0))
    return pl.pallas_call(
        _merge_kernel,
        grid=(n // tm,),
        in_specs=[
            pl.BlockSpec((tm, D_MODEL), lambda i: (i, 0)),
            br_spec, br_spec, br_spec,
            _mod_spec(seq_tiles, mod_row0),
            pl.BlockSpec((None, None, 1, D_MODEL), lambda i: (layer, 1, 0, 0)),
            _layer_spec((D_MODEL, 3 * D_MODEL), layer),
            _layer_spec((W_ATTN, D_MODEL), layer), _layer_spec((W_FOUR, D_MODEL), layer),
            _layer_spec((W_MLSTM, D_MODEL), layer),
            _layer_spec((D_MODEL, D_MODEL), layer),
        ],
        out_specs=pl.BlockSpec((tm, D_MODEL), lambda i: (i, 0)),
        out_shape=jax.ShapeDtypeStruct((n, D_MODEL), F32),
        compiler_params=_cparams(1),
        name="merge",
    )(x, a, f, m, mods, g_norm, wg, wa, wf, wm, wo)


def _rope_tables(n_tok):
    tok = np.arange(n_tok)
    inv = ROPE_BASE ** (-np.arange(ROPE_AXIS_PAIRS, dtype=np.float32) / ROPE_AXIS_PAIRS)
    ang = np.concatenate([(tok // GRID_W).astype(np.float32)[:, None] * inv,
                          (tok % GRID_W).astype(np.float32)[:, None] * inv], axis=-1).astype(np.float32)
    c, s = np.cos(ang), np.sin(ang)
    cos_t = np.concatenate([c, c, c, c], axis=-1)
    sin_t = np.concatenate([-s, s, -s, s], axis=-1)
    return jnp.asarray(cos_t, F32), jnp.asarray(sin_t, F32)


def _trunk(x, group, l, mods, w, cache, state, ctx_prev):
    batch, seq, mod_row0, per_seq_mod, nb, attn_nb, attn_tq, four_nb = group
    tm = 512
    tm_ffn = 1024
    if per_seq_mod and (seq % tm or seq % tm_ffn):
        raise ValueError("row tile spans several conditioned sequences")
    seq_tiles = seq // tm if per_seq_mod else 0
    ffn_tiles = seq // tm_ffn if per_seq_mod else 0
    lam_init = 0.8 - 0.6 * math.exp(-0.3 * l)
    is_ctx = cache is None
    x = _ffn_call(x, mods, w['g_norm'], w['g_final'], w['ffn1_in'], w['ffn1_out'], layer=l, sub=0,
                  final=False, tm=tm_ffn, seq_tiles=ffn_tiles, mod_row0=mod_row0)
    cos_t, sin_t = _rope_tables(seq)
    q, k, v, zf, mq, mk, mo, vat, vt, gates_t = _proj_call(
        x, mods, w['g_norm'], w['w_row'], w['w_col'], w['b_gate_t'], cos_t, sin_t,
        ctx_prev[:2] if ctx_prev is not None else None,
        layer=l, batch=batch, seq=seq, tm=tm, mod_row0=mod_row0, per_seq_mod=per_seq_mod,
        rope=not is_ctx, kv_stacked=is_ctx)
    a = _attn_call(q, k, vat, cache, w['attn_lambda'], w['g_attn_sub'], layer=l, lam_init=lam_init,
                   tq=attn_tq, nb=attn_nb)
    f = _four_call(zf, batch=batch, seq=seq, nb=four_nb)
    m_res = _mlstm_call(mq, mk, vt, mo, gates_t, w['g_mlstm'], state,
                        ctx_prev[2:] if ctx_prev is not None else None,
                        layer=l, batch=batch, seq=seq, nb=nb, emit_state=is_ctx)
    x = _merge_call(x, a, f, m_res[0], mods, w['g_norm'], w['w_bg'], w['w_br_a'], w['w_br_f'], w['w_br_m'],
                    w['w_out'], layer=l, tm=tm_ffn, seq_tiles=ffn_tiles, mod_row0=mod_row0)
    x = _ffn_call(x, mods, w['g_norm'], w['g_final'], w['ffn2_in'], w['ffn2_out'], layer=l, sub=2,
                  final=(l == DEPTH - 1), tm=tm_ffn, seq_tiles=ffn_tiles, mod_row0=mod_row0)
    return x, ((k, v) + tuple(m_res[1:]) if is_ctx else None)


def kernel(x_prompt, x_sample, cache_k, cache_v, state_C, state_n, state_m, c, c_ctx, w_ada, b_ada, g_norm,
           w_ffn1_in, w_ffn1_out, w_ffn2_in, w_ffn2_out, w_in, b_mgate, attn_lambda, g_attn_sub, g_mlstm,
           w_branch_gate, w_br_attn, w_br_four, w_br_mlstm, w_out, g_final):
    batch, seq, _ = x_prompt.shape
    dec_batch, dec_seq, _ = x_sample.shape
    cond = jnp.zeros((N_COND, D_MODEL), F32).at[0].set(c_ctx).at[1:1 + dec_batch].set(c)
    mods_all = _ada_call(cond, w_ada, b_ada).reshape(DEPTH, N_COND, N_MOD, D_MODEL)

    gate_w = jnp.pad(w_in[:, :, P_MAIN:], ((0, 0), (0, 0), (0, LANES - N_GATE)))
    b_gate = jnp.pad(b_mgate, ((0, 0), (0, LANES - N_GATE)))
    w = {
        'g_norm': g_norm[:, :, None, :], 'g_final': g_final[None, :],
        'ffn1_in': w_ffn1_in.astype(BF16), 'ffn1_out': w_ffn1_out.astype(BF16),
        'ffn2_in': w_ffn2_in.astype(BF16), 'ffn2_out': w_ffn2_out.astype(BF16),
        'w_row': jnp.concatenate([w_in[:, :, :MV_LO], w_in[:, :, MV_HI:P_MAIN]], axis=-1).astype(BF16),
        'w_col': jnp.swapaxes(jnp.concatenate(
            [w_in[:, :, 2 * W_ATTN:3 * W_ATTN], w_in[:, :, MV_LO:MV_HI], gate_w], axis=-1), 1, 2).astype(BF16),
        'b_gate_t': b_gate[:, :, None],
        'attn_lambda': attn_lambda, 'g_attn_sub': g_attn_sub[:, None, :], 'g_mlstm': g_mlstm[:, None, :],
        'w_bg': w_branch_gate.astype(BF16), 'w_br_a': w_br_attn.astype(BF16),
        'w_br_f': w_br_four.astype(BF16), 'w_br_m': w_br_mlstm.astype(BF16), 'w_out': w_out.astype(BF16),
    }

    hp = x_prompt.reshape(batch * seq, D_MODEL)
    hs = x_sample.reshape(dec_batch * dec_seq, D_MODEL)
    prompt_group = (batch, seq, 0, False, 2, 4, seq, 4)
    sample_group = (dec_batch, dec_seq, 1, True, 1, 1, dec_seq // 2, 1)
    ctx = None
    for l in range(DEPTH):
        hp, ctx = _trunk(hp, prompt_group, l, mods_all[l], w, None, None, ctx)
        hs, _ = _trunk(hs, sample_group, l, mods_all[l], w, (cache_k, cache_v),
                       (state_C, state_n, state_m), None)
    return (hp.reshape(batch, seq, D_MODEL), hs.reshape(dec_batch, dec_seq, D_MODEL), *ctx)
```

```python
import functools
import math

import numpy as np
import jax
import jax.numpy as jnp
from jax import lax
from jax.experimental import pallas as pl
from jax.experimental.pallas import tpu as pltpu

D_MODEL = 1024
DEPTH = 2
GRID_W = 64
N_ATTN = 4
DH_ATTN = 64
DV_ATTN = 2 * DH_ATTN
W_ATTN = N_ATTN * DV_ATTN
N_FOUR = 4
DG_FOUR = 128
W_FOUR = N_FOUR * DG_FOUR
N_MLSTM = 4
DH_MLSTM = 128
W_MLSTM = N_MLSTM * DH_MLSTM
N_GATE = 4 * N_MLSTM
LANES = 128
SUBLANES = 16
MV_LO = 3 * W_ATTN + W_FOUR + 2 * W_MLSTM
MV_HI = MV_LO + W_MLSTM
P_MAIN = MV_HI + W_MLSTM
P_ROW = P_MAIN - W_MLSTM
P_COL = W_ATTN + W_MLSTM + LANES
D_FF = 2816
N_MOD = 9
N_COND = 8
ROPE_BASE = 10000.0
ROPE_AXIS_PAIRS = DH_ATTN // 4
ATTN_SCALE = DH_ATTN ** -0.5
LOG2E = 1.4426950408889634
MLSTM_K_SCALE = DH_MLSTM ** -0.5
EPS = 1e-6
MLSTM_CHUNK = 256
VMEM_LIMIT = 56 * 1024 * 1024
TM_WIDE = 1024
TM_PROJ = 512

F32 = jnp.float32
BF16 = jnp.bfloat16


def _cparams(n_grid):
    return pltpu.CompilerParams(dimension_semantics=("arbitrary",) * n_grid,
                                vmem_limit_bytes=VMEM_LIMIT)


def _bdot(a, b):
    return jnp.dot(a.astype(BF16), b.astype(BF16), preferred_element_type=F32)


def _bdot_nt(a, b):
    return lax.dot_general(a.astype(BF16), b.astype(BF16), (((1,), (1,)), ((), ())),
                           preferred_element_type=F32)


def _sigmoid(x):
    return 1.0 / (1.0 + jnp.exp(-x))


def _log_sigmoid(x):
    return jnp.minimum(x, 0.0) - jnp.log1p(jnp.exp(-jnp.abs(x)))


def _rms(x, g):
    return x * lax.rsqrt(jnp.mean(x * x, axis=-1, keepdims=True) + EPS) * g


def _modulated_norm(x, g, mod, base):
    return _rms(x, g) * (1.0 + mod[base + 1:base + 2]) + mod[base:base + 1]


def _layer_spec(shape, layer):
    return pl.BlockSpec((None,) + shape, lambda *_: (layer,) + (0,) * len(shape),
                        pipeline_mode=pl.Buffered(1))


class _Stream:
    def __init__(self, batch, seq, dec_batch, dec_seq):
        self.batch, self.seq, self.dec_batch, self.dec_seq = batch, seq, dec_batch, dec_seq
        self.n_ctx = batch * seq
        self.n_all = self.n_ctx + dec_batch * dec_seq

    def ctx_tiles(self, tm):
        if self.n_ctx % tm or self.dec_seq % tm:
            raise ValueError("row tile must divide the context rows and one latent sequence")
        return self.n_ctx // tm

    def mod_spec(self, tm):
        n_pt, tps = self.ctx_tiles(tm), self.dec_seq // tm
        return pl.BlockSpec((None, N_MOD, D_MODEL),
                            lambda i: (jnp.where(i < n_pt, 0, 1 + (i - n_pt) // tps), 0, 0))


def _ada_kernel(c_ref, w_ref, b_ref, o_ref):
    c = c_ref[...]
    s = c * _sigmoid(c)
    o_ref[...] = _bdot(s, w_ref[...]) + b_ref[...]


def _ada_call(cond, w_ada, b_ada):
    tn = 1024
    n_out = N_MOD * D_MODEL
    return pl.pallas_call(
        _ada_kernel,
        grid=(DEPTH, n_out // tn),
        in_specs=[
            pl.BlockSpec((N_COND, D_MODEL), lambda l, j: (0, 0)),
            pl.BlockSpec((None, D_MODEL, tn), lambda l, j: (l, 0, j)),
            pl.BlockSpec((None, 1, tn), lambda l, j: (l, 0, j)),
        ],
        out_specs=pl.BlockSpec((None, N_COND, tn), lambda l, j: (l, 0, j)),
        out_shape=jax.ShapeDtypeStruct((DEPTH, N_COND, n_out), F32),
        compiler_params=_cparams(2),
        name="adaln",
    )(cond, w_ada, b_ada.reshape(DEPTH, 1, n_out))


FF_CHUNKS = (768, 768, 768, 512)


def _ffn_kernel(*refs, base, n_x, final, n_pt):
    x_refs = refs[:n_x]
    mod_ref, g_ref, gf_ref, win_ref, wout_ref = refs[n_x:n_x + 5]
    o_refs = refs[n_x + 5:]
    i = pl.program_id(0)
    if n_x == 2:
        x = jnp.where(i < n_pt, x_refs[0][...], x_refs[1][...])
    else:
        x = x_refs[0][...]
    mod = mod_ref[...]
    u = _modulated_norm(x, g_ref[...], mod, base).astype(BF16)
    y = None
    lo = 0
    for width in FF_CHUNKS:
        a = jnp.dot(u, win_ref[:, lo:lo + width], preferred_element_type=F32)
        g = jnp.dot(u, win_ref[:, D_FF + lo:D_FF + lo + width], preferred_element_type=F32)
        hh = (a * _sigmoid(a) * g).astype(BF16)
        part = jnp.dot(hh, wout_ref[lo:lo + width, :], preferred_element_type=F32)
        y = part if y is None else y + part
        lo += width
    xn = x + 0.5 * mod[base + 2:base + 3] * y
    if not final:
        o_refs[0][...] = xn
    else:
        xn = _rms(xn, gf_ref[...])

        @pl.when(i < n_pt)
        def _():
            o_refs[0][...] = xn

        @pl.when(i >= n_pt)
        def _():
            o_refs[1][...] = xn


def _ffn_call(xs, mods, g_norm, g_final, w_in, w_out, st, *, layer, sub, final):
    tm = TM_WIDE
    n_pt = st.ctx_tiles(tm)
    n_tiles = st.n_all // tm
    ctx_spec = pl.BlockSpec((tm, D_MODEL), lambda i: (jnp.minimum(i, n_pt - 1), 0))
    lat_spec = pl.BlockSpec((tm, D_MODEL), lambda i: (jnp.maximum(i - n_pt, 0), 0))
    all_spec = pl.BlockSpec((tm, D_MODEL), lambda i: (i, 0))
    x_specs = [ctx_spec, lat_spec] if len(xs) == 2 else [all_spec]
    if final:
        out_specs = [ctx_spec, lat_spec]
        out_shape = [jax.ShapeDtypeStruct((st.n_ctx, D_MODEL), F32),
                     jax.ShapeDtypeStruct((st.n_all - st.n_ctx, D_MODEL), F32)]
    else:
        out_specs = [all_spec]
        out_shape = [jax.ShapeDtypeStruct((st.n_all, D_MODEL), F32)]
    return pl.pallas_call(
        functools.partial(_ffn_kernel, base=3 * sub, n_x=len(xs), final=final, n_pt=n_pt),
        grid=(n_tiles,),
        in_specs=x_specs + [
            st.mod_spec(tm),
            pl.BlockSpec((None, None, 1, D_MODEL), lambda i: (layer, sub, 0, 0)),
            pl.BlockSpec((1, D_MODEL), lambda i: (0, 0)),
            _layer_spec((D_MODEL, 2 * D_FF), layer),
            _layer_spec((D_FF, D_MODEL), layer),
        ],
        out_specs=out_specs,
        out_shape=out_shape,
        compiler_params=_cparams(1),
        name="ffn",
    )(*xs, mods, g_norm, g_final, w_in, w_out)


def _swap32(x):
    lane = lax.broadcasted_iota(jnp.int32, x.shape, 1)
    return jnp.where((lane & (DH_ATTN // 2)) == 0,
                     pltpu.roll(x, LANES - DH_ATTN // 2, 1), pltpu.roll(x, DH_ATTN // 2, 1))


def _proj_kernel(*refs, n_alias, n_pt, seqs_per_tile, seq):
    (x_ref, mod_ref, g_ref, w_ref, wt_ref, bgt_ref, cos_ref, sin_ref) = refs[:8]
    (q_ref, k_ref, ck_ref, cv_ref, zf_ref, mq_ref, mk_ref, mo_ref, vat_ref, vt_ref, gt_ref) = refs[8 + n_alias:]
    i = pl.program_id(0)
    is_latent = i >= n_pt
    x = x_ref[...]
    u = _modulated_norm(x, g_ref[...], mod_ref[...], 3).astype(BF16)
    za = jnp.dot(u, w_ref[:, :3 * W_ATTN], preferred_element_type=F32)
    zb = jnp.dot(u, w_ref[:, 3 * W_ATTN:], preferred_element_type=F32)
    zt = _bdot_nt(wt_ref[...], u)
    cos_t = cos_ref[...]
    sin_t = sin_ref[...]
    plain_k = []
    for h in range(N_ATTN):
        zq = za[:, h * DV_ATTN:(h + 1) * DV_ATTN]
        zk = za[:, W_ATTN + h * DV_ATTN:W_ATTN + (h + 1) * DV_ATTN]
        plain_k.append(zk)
        q_ref[h] = jnp.where(is_latent, zq * cos_t + _swap32(zq) * sin_t, zq).astype(q_ref.dtype)
        k_ref[h] = jnp.where(is_latent, zk * cos_t + _swap32(zk) * sin_t, zk).astype(k_ref.dtype)

    @pl.when(i < n_pt)
    def _():
        for h in range(N_ATTN):
            zv = za[:, 2 * W_ATTN + h * DV_ATTN:2 * W_ATTN + (h + 1) * DV_ATTN]
            for s in range(seqs_per_tile):
                rs = slice(s * seq, (s + 1) * seq)
                ck_ref[s, h] = plain_k[h][rs]
                cv_ref[s, h] = zv[rs]

    zf_ref[...] = zb[:, :W_FOUR].astype(zf_ref.dtype)
    off = W_FOUR
    mq_ref[...] = zb[:, off:off + W_MLSTM].astype(mq_ref.dtype)
    mk_ref[...] = (zb[:, off + W_MLSTM:off + 2 * W_MLSTM] * MLSTM_K_SCALE).astype(mk_ref.dtype)
    mo_ref[...] = zb[:, off + 2 * W_MLSTM:off + 3 * W_MLSTM]
    vat_ref[...] = zt[:W_ATTN].astype(vat_ref.dtype)
    vt_ref[...] = zt[W_ATTN:W_ATTN + W_MLSTM]
    gpt = zt[W_ATTN + W_MLSTM:] + bgt_ref[...]
    sub = lax.broadcasted_iota(jnp.int32, gpt.shape, 0)
    gt_ref[...] = jnp.where((sub < N_GATE) & ((sub & N_MLSTM) != 0), _log_sigmoid(gpt), gpt)


def _proj_call(x, mods, g_norm, w_row, w_col, bgt, cos_t, sin_t, kv_prev, st, *, layer):
    tm = TM_PROJ
    n_pt = st.ctx_tiles(tm)
    if tm % st.seq:
        raise ValueError("a context row tile must hold whole sequences")
    seqs_per_tile, tps = tm // st.seq, st.dec_seq // tm
    n = st.n_all
    head_shape = jax.ShapeDtypeStruct((N_ATTN, n, DV_ATTN), BF16)
    head_spec = pl.BlockSpec((N_ATTN, tm, DV_ATTN), lambda i: (0, i, 0))
    kv_shape = jax.ShapeDtypeStruct((st.batch, DEPTH, N_ATTN, st.seq, DV_ATTN), F32)
    kv_spec = pl.BlockSpec((seqs_per_tile, None, N_ATTN, st.seq, DV_ATTN),
                           lambda i: (jnp.minimum(i, n_pt - 1), layer, 0, 0, 0))
    tok_shape = jax.ShapeDtypeStruct((n, W_ATTN), F32)
    tok_bf16 = jax.ShapeDtypeStruct((n, W_ATTN), BF16)
    tok_spec = pl.BlockSpec((tm, W_ATTN), lambda i: (i, 0))
    rope_spec = pl.BlockSpec((tm, LANES), lambda i: (jnp.maximum(i - n_pt, 0) % tps, 0))
    alias_in = list(kv_prev) if kv_prev is not None else []
    n_in = 8
    return pl.pallas_call(
        functools.partial(_proj_kernel, n_alias=len(alias_in), n_pt=n_pt, seqs_per_tile=seqs_per_tile,
                          seq=st.seq),
        grid=(n // tm,),
        in_specs=[
            pl.BlockSpec((tm, D_MODEL), lambda i: (i, 0)),
            st.mod_spec(tm),
            pl.BlockSpec((None, None, 1, D_MODEL), lambda i: (layer, 1, 0, 0)),
            _layer_spec((D_MODEL, P_ROW), layer),
            _layer_spec((P_COL, D_MODEL), layer),
            _layer_spec((LANES, 1), layer),
            rope_spec, rope_spec,
        ] + [pl.BlockSpec(memory_space=pl.ANY)] * len(alias_in),
        out_specs=[head_spec, head_spec, kv_spec, kv_spec, tok_spec, tok_spec, tok_spec, tok_spec,
                   pl.BlockSpec((W_ATTN, tm), lambda i: (0, i)),
                   pl.BlockSpec((W_MLSTM, tm), lambda i: (0, i)),
                   pl.BlockSpec((LANES, tm), lambda i: (0, i))],
        out_shape=[head_shape, head_shape, kv_shape, kv_shape, tok_bf16, tok_bf16, tok_bf16, tok_shape,
                   jax.ShapeDtypeStruct((W_ATTN, n), BF16),
                   jax.ShapeDtypeStruct((W_MLSTM, n), F32),
                   jax.ShapeDtypeStruct((LANES, n), F32)],
        input_output_aliases={n_in + j: 2 + j for j in range(len(alias_in))},
        compiler_params=_cparams(1),
        name="mixer_proj",
    )(x, mods, g_norm, w_row, w_col, bgt, cos_t, sin_t, *alias_in)


def _attn_kernel(*refs, lam_init, cached, nb, tq, seq):
    q_ref, k_ref, vt_ref = refs[:3]
    pos = 3
    if cached:
        ck_ref, cv_ref = refs[pos:pos + 2]
        pos += 2
    lam_ref, g_ref = refs[pos:pos + 2]
    o_ref = refs[-1]
    lp = lam_ref[...]
    lam = (jnp.exp(jnp.sum(lp[0:1] * lp[1:2], axis=-1, keepdims=True))
           - jnp.exp(jnp.sum(lp[2:3] * lp[3:4], axis=-1, keepdims=True)) + lam_init)
    g_sub = g_ref[...]
    lane = lax.broadcasted_iota(jnp.int32, (tq, DV_ATTN), 1)
    results = []
    for bi in range(nb):
        for h in range(N_ATTN):
            hs = slice(h * DV_ATTN, (h + 1) * DV_ATTN)
            ks = slice(bi * seq, (bi + 1) * seq)
            q = q_ref[h, bi * tq:(bi + 1) * tq].astype(F32) * (ATTN_SCALE * LOG2E)
            q_maps = (jnp.where(lane < DH_ATTN, q, 0.0).astype(BF16),
                      jnp.where(lane >= DH_ATTN, q, 0.0).astype(BF16))
            keys = [k_ref[h, ks]]
            vals_t = [vt_ref[hs, ks]]
            if cached:
                keys.insert(0, ck_ref[h].astype(BF16))
                vals_t.insert(0, cv_ref[h].T.astype(BF16))
            exps, dens = [], []
            for qm in q_maps:
                s = [_bdot_nt(kk, qm) for kk in keys]
                m = functools.reduce(jnp.maximum, [jnp.max(si, axis=0, keepdims=True) for si in s])
                e = [jnp.exp2(si - m) for si in s]
                exps.append(e)
                dens.append(functools.reduce(jnp.add, [jnp.sum(ei, axis=0, keepdims=True) for ei in e]))
            r1 = 1.0 / dens[0]
            r2 = lam / dens[1]
            o_t = None
            for j, vv in enumerate(vals_t):
                p = exps[0][j] * r1 - exps[1][j] * r2
                part = jnp.dot(vv, p.astype(BF16), preferred_element_type=F32)
                o_t = part if o_t is None else o_t + part
            y_t = o_t * lax.rsqrt(jnp.mean(o_t * o_t, axis=0, keepdims=True) + EPS)
            results.append((bi, hs, (y_t.T * g_sub * (1.0 - lam_init)).astype(o_ref.dtype)))
    for bi, hs, y in results:
        o_ref[bi * tq:(bi + 1) * tq, hs] = y


def _attn_call(q, k, v_t, cache, lam_p, g_sub, out_prev, st, *, layer, lam_init, latent):
    n = st.n_all
    if latent:
        batch, seq, nb, tq = st.dec_batch, st.dec_seq, 1, st.dec_seq // 2
        row0 = st.n_ctx
    else:
        batch, seq, nb, tq = st.batch, st.seq, 4, st.seq
        row0 = 0
    nq = seq // tq
    if nq > 1 and nb > 1:
        raise ValueError("query blocks of several sequences are not contiguous rows")
    q0, s0 = row0 // (nb * tq), row0 // (nb * seq)
    in_specs = [pl.BlockSpec((N_ATTN, nb * tq, DV_ATTN), lambda b, i: (0, q0 + b * nq + i, 0)),
                pl.BlockSpec((N_ATTN, nb * seq, DV_ATTN), lambda b, i: (0, s0 + b, 0)),
                pl.BlockSpec((W_ATTN, nb * seq), lambda b, i: (0, s0 + b))]
    args = [q, k, v_t]
    if cache is not None:
        past = cache[0].shape[3]
        c_spec = pl.BlockSpec((None, None, N_ATTN, past, DV_ATTN), lambda b, i: (b, layer, 0, 0, 0))
        in_specs += [c_spec, c_spec]
        args += list(cache)
    in_specs += [pl.BlockSpec((None, 4, DH_ATTN), lambda b, i: (layer, 0, 0)),
                 pl.BlockSpec((None, 1, DV_ATTN), lambda b, i: (layer, 0, 0))]
    args += [lam_p, g_sub]
    aliases = {}
    if out_prev is not None:
        aliases = {len(args): 0}
        in_specs.append(pl.BlockSpec(memory_space=pl.ANY))
        args.append(out_prev)
    return pl.pallas_call(
        functools.partial(_attn_kernel, lam_init=lam_init, cached=cache is not None, nb=nb, tq=tq, seq=seq),
        grid=(batch // nb, nq),
        in_specs=in_specs,
        out_specs=pl.BlockSpec((nb * tq, W_ATTN), lambda b, i: (q0 + b * nq + i, 0)),
        out_shape=jax.ShapeDtypeStruct((n, W_ATTN), BF16),
        input_output_aliases=aliases,
        compiler_params=_cparams(2),
        name="diff_attn",
    )(*args)


def _dft_tables(seq):
    def cs(n):
        j = np.arange(n)
        ang = 2.0 * np.pi * ((j[:, None] * j[None, :]) % n) / n
        return np.cos(ang) / math.sqrt(n), np.sin(ang) / math.sqrt(n)

    cd, sd = cs(DG_FOUR)
    ct, st = cs(seq)
    w_d = jnp.asarray(np.concatenate([cd, sd], axis=1), F32)
    return w_d.astype(BF16), jnp.asarray(ct, F32).astype(BF16), jnp.asarray(-st, F32).astype(BF16)


def _four_kernel(*refs, nb, seq):
    z_ref, wd_ref, ct_ref, st_ref = refs[:4]
    o_ref = refs[-1]
    y_cos, y_sin = [], []
    for gidx in range(N_FOUR):
        y = jnp.dot(z_ref[:, gidx * DG_FOUR:(gidx + 1) * DG_FOUR], wd_ref[...], preferred_element_type=F32)
        y_cos.append(y[:, :DG_FOUR].astype(BF16))
        y_sin.append(y[:, DG_FOUR:].astype(BF16))
    y_cos = jnp.concatenate(y_cos, axis=1)
    y_sin = jnp.concatenate(y_sin, axis=1)
    for s in range(nb):
        rows = slice(s * seq, (s + 1) * seq)
        o_ref[rows, :] = (jnp.dot(ct_ref[...], y_cos[rows], preferred_element_type=F32)
                          + jnp.dot(st_ref[...], y_sin[rows], preferred_element_type=F32)).astype(o_ref.dtype)


def _four_call(zf, out_prev, st, *, latent):
    if latent:
        batch, seq, nb, row0 = st.dec_batch, st.dec_seq, 1, st.n_ctx
    else:
        batch, seq, nb, row0 = st.batch, st.seq, 4, 0
    b0 = row0 // (nb * seq)
    w_d, ct, s_t = _dft_tables(seq)
    row_spec = pl.BlockSpec((nb * seq, W_FOUR), lambda b: (b0 + b, 0))
    in_specs = [row_spec,
                pl.BlockSpec((DG_FOUR, 2 * DG_FOUR), lambda b: (0, 0)),
                pl.BlockSpec((seq, seq), lambda b: (0, 0)),
                pl.BlockSpec((seq, seq), lambda b: (0, 0))]
    args = [zf, w_d, ct, s_t]
    aliases = {}
    if out_prev is not None:
        aliases = {len(args): 0}
        in_specs.append(pl.BlockSpec(memory_space=pl.ANY))
        args.append(out_prev)
    return pl.pallas_call(
        functools.partial(_four_kernel, nb=nb, seq=seq),
        grid=(batch // nb,),
        in_specs=in_specs,
        out_specs=row_spec,
        out_shape=jax.ShapeDtypeStruct((st.n_all, W_FOUR), BF16),
        input_output_aliases=aliases,
        compiler_params=_cparams(1),
        name="fourier_mix",
    )(*args)


def _mlstm_kernel(*refs, seq, nb, seeded, emit_state, n_alias):
    mq_ref, mk_ref, vt_ref, mo_ref, gt_ref, g_ref = refs[:6]
    pos = 6
    if seeded:
        c0_ref, n0_ref, m0_ref = refs[pos:pos + 3]
        pos += 3
    pos += n_alias
    o_ref = refs[pos]
    if emit_state:
        c1_ref, n1_ref, m1_ref = refs[pos + 1:pos + 4]

    L = min(MLSTM_CHUNK, seq)
    nc = seq // L
    s_idx = lax.broadcasted_iota(jnp.int32, (L, L), 0)
    t_idx = lax.broadcasted_iota(jnp.int32, (L, L), 1)
    before = (s_idx <= t_idx, s_idx >= t_idx)
    tri = (jnp.where(before[1], 1.0, 0.0).astype(BF16),
           jnp.where(before[0], 1.0, 0.0).astype(BF16))
    chains = [(bi, d, h) for bi in range(nb) for d in range(2) for h in range(N_MLSTM)]

    state = {}
    for (bi, d, h) in chains:
        if seeded:
            state[bi, d, h] = (c0_ref[bi, d, h], n0_ref[bi, d, h:h + 1, :], m0_ref[bi, d:d + 1, h:h + 1])
        else:
            state[bi, d, h] = (None, None, jnp.zeros((1, 1), F32))

    h_t = {}
    for c in range(nc):
        terms = {}
        for bi in range(nb):
            for d in range(2):
                r0 = bi * seq + (c if d == 0 else nc - 1 - c) * L
                g_t = gt_ref[0:N_GATE, r0:r0 + L]
                hi = g_t.astype(BF16)
                rem = g_t - hi.astype(F32)
                mid = rem.astype(BF16)
                lo = (rem - mid.astype(F32)).astype(BF16)
                pieces = _bdot_nt(jnp.concatenate([hi, mid, lo], axis=0), tri[d])
                cum_t = pieces[0:N_GATE] + pieces[N_GATE:2 * N_GATE] + pieces[2 * N_GATE:]
                c_t = g_t - pltpu.roll(cum_t, N_GATE - N_MLSTM, 0)
                col = jnp.concatenate([c_t, jnp.zeros((LANES - N_GATE, L), F32)], axis=0).T
                terms[bi, d] = (r0, col, g_t, cum_t)

        new_state = {}
        for (bi, d, h) in chains:
            r0, col, g_t, cum_t = terms[bi, d]
            c_prev, n_prev, m_prev = state[bi, d, h]
            has_state = c_prev is not None
            need_update = emit_state or c < nc - 1
            ci = 2 * d * N_MLSTM + h
            cf = ci + N_MLSTM
            last = L - 1 if d == 0 else 0
            hs = slice(h * DH_MLSTM, (h + 1) * DH_MLSTM)
            q = mq_ref[r0:r0 + L, hs]
            k = mk_ref[r0:r0 + L, hs]
            v_t = vt_ref[hs, r0:r0 + L]
            i_row, b_row = g_t[ci:ci + 1, :], cum_t[cf:cf + 1, :]

            cb = jnp.where(before[d], col[:, ci:ci + 1], -jnp.inf)
            m_row = jnp.maximum(jnp.max(cb, axis=0, keepdims=True), m_prev)
            s_t = _bdot_nt(k, q) * jnp.exp(cb - m_row)
            num_t = jnp.dot(v_t.astype(BF16), s_t.astype(BF16), preferred_element_type=F32)
            den = jnp.sum(s_t, axis=0, keepdims=True)
            if has_state:
                sp = jnp.exp(m_prev - m_row)
                cn = jnp.concatenate([c_prev, jnp.broadcast_to(n_prev, (SUBLANES, DH_MLSTM))], axis=0)
                cq = _bdot_nt(cn, q)
                num_t = num_t + sp * cq[:DH_MLSTM]
                den = den + sp * cq[DH_MLSTM:DH_MLSTM + 1]
            m_t = b_row + m_row
            h_t[bi, d, h, c] = num_t / jnp.maximum(jnp.abs(den), jnp.exp(-m_t))
            if need_update:
                b_tot = b_row[:, last:last + 1]
                m_new = m_t[:, last:last + 1]
                wl = jnp.exp(b_tot + (i_row - b_row) - m_new)
                vw = jnp.concatenate([v_t * wl, jnp.broadcast_to(wl, (SUBLANES, L))], axis=0)
                upd = jnp.dot(vw.astype(BF16), k, preferred_element_type=F32)
                c_new, n_new = upd[:DH_MLSTM], upd[DH_MLSTM:DH_MLSTM + 1]
                if has_state:
                    decay = jnp.exp(b_tot + m_prev - m_new)
                    c_new = decay * c_prev + c_new
                    n_new = decay * n_prev + n_new
                new_state[bi, d, h] = (c_new, n_new, m_new)
        state = new_state

    g_m = g_ref[...]
    for bi in range(nb):
        for h in range(N_MLSTM):
            hs = slice(h * DH_MLSTM, (h + 1) * DH_MLSTM)
            fwd = [h_t[bi, 0, h, c] for c in range(nc)]
            bwd = [h_t[bi, 1, h, nc - 1 - c] for c in range(nc)]
            hsum = (fwd[0] if nc == 1 else jnp.concatenate(fwd, axis=1)) \
                + (bwd[0] if nc == 1 else jnp.concatenate(bwd, axis=1))
            y = hsum * lax.rsqrt(jnp.mean(hsum * hsum, axis=0, keepdims=True) + EPS)
            rows = slice(bi * seq, (bi + 1) * seq)
            o_ref[rows, hs] = (y.T * g_m * _sigmoid(mo_ref[rows, hs])).astype(o_ref.dtype)

    if emit_state:
        for (bi, d, h) in chains:
            c_fin, n_fin, m_fin = state[bi, d, h]
            c1_ref[bi, d, h] = c_fin
            n1_ref[bi, d, h:h + 1, :] = n_fin
            m1_ref[bi, d:d + 1, h:h + 1] = m_fin


def _mlstm_call(mq, mk, vt, mo, gates_t, g_m, state, alias_prev, st, *, layer, latent):
    if latent:
        batch, seq, nb, row0 = st.dec_batch, st.dec_seq, 1, st.n_ctx
    else:
        batch, seq, nb, row0 = st.batch, st.seq, 2, 0
    emit_state = not latent
    rows = nb * seq
    b0 = row0 // rows
    tok_spec = pl.BlockSpec((rows, W_MLSTM), lambda b: (b0 + b, 0))
    in_specs = [tok_spec, tok_spec,
                pl.BlockSpec((W_MLSTM, rows), lambda b: (0, b0 + b)),
                tok_spec,
                pl.BlockSpec((LANES, rows), lambda b: (0, b0 + b)),
                pl.BlockSpec((None, 1, DH_MLSTM), lambda b: (layer, 0, 0))]
    args = [mq, mk, vt, mo, gates_t, g_m]
    state_specs = [
        pl.BlockSpec((nb, None, 2, N_MLSTM, DH_MLSTM, DH_MLSTM), lambda b: (b, layer, 0, 0, 0, 0)),
        pl.BlockSpec((nb, None, 2, N_MLSTM, DH_MLSTM), lambda b: (b, layer, 0, 0, 0)),
        pl.BlockSpec((nb, None, 2, N_MLSTM), lambda b: (b, layer, 0, 0)),
    ]
    if state is not None:
        in_specs += state_specs
        args += list(state)
    alias_in = list(alias_prev) if alias_prev is not None else []
    n_in = len(args)
    in_specs += [pl.BlockSpec(memory_space=pl.ANY)] * len(alias_in)
    args += alias_in
    out_specs = [tok_spec]
    out_shape = [jax.ShapeDtypeStruct((st.n_all, W_MLSTM), BF16)]
    if emit_state:
        out_specs += state_specs
        out_shape += [
            jax.ShapeDtypeStruct((batch, DEPTH, 2, N_MLSTM, DH_MLSTM, DH_MLSTM), F32),
            jax.ShapeDtypeStruct((batch, DEPTH, 2, N_MLSTM, DH_MLSTM), F32),
            jax.ShapeDtypeStruct((batch, DEPTH, 2, N_MLSTM), F32),
        ]
    first_out = 1 if emit_state else 0
    return pl.pallas_call(
        functools.partial(_mlstm_kernel, seq=seq, nb=nb, seeded=state is not None, emit_state=emit_state,
                          n_alias=len(alias_in)),
        grid=(batch // nb,),
        in_specs=in_specs,
        out_specs=out_specs,
        out_shape=out_shape,
        input_output_aliases={n_in + j: first_out + j for j in range(len(alias_in))},
        compiler_params=_cparams(1),
        name="mlstm",
    )(*args)


def _merge_kernel(x_ref, a_ref, f_ref, m_ref, mod_ref, g_ref, wg_ref, wa_ref, wf_ref, wm_ref, wo_ref, o_ref):
    x = x_ref[...]
    mod = mod_ref[...]
    u = _modulated_norm(x, g_ref[...], mod, 3).astype(BF16)
    merged = None
    for j, (br_ref, w_ref) in enumerate(((a_ref, wa_ref), (f_ref, wf_ref), (m_ref, wm_ref))):
        gate = _sigmoid(jnp.dot(u, wg_ref[:, j * D_MODEL:(j + 1) * D_MODEL], preferred_element_type=F32))
        term = gate * _bdot(br_ref[...], w_ref[...])
        merged = term if merged is None else merged + term
    out = _bdot(merged, wo_ref[...])
    o_ref[...] = x + mod[5:6] * out


def _merge_call(x, a, f, m, mods, g_norm, wg, wa, wf, wm, wo, st, *, layer):
    tm = TM_WIDE
    n = st.n_all
    br_spec = pl.BlockSpec((tm, W_ATTN), lambda i: (i, 0))
    return pl.pallas_call(
        _merge_kernel,
        grid=(n // tm,),
        in_specs=[
            pl.BlockSpec((tm, D_MODEL), lambda i: (i, 0)),
            br_spec, br_spec, br_spec,
            st.mod_spec(tm),
            pl.BlockSpec((None, None, 1, D_MODEL), lambda i: (layer, 1, 0, 0)),
            _layer_spec((D_MODEL, 3 * D_MODEL), layer),
            _layer_spec((W_ATTN, D_MODEL), layer), _layer_spec((W_FOUR, D_MODEL), layer),
            _layer_spec((W_MLSTM, D_MODEL), layer),
            _layer_spec((D_MODEL, D_MODEL), layer),
        ],
        out_specs=pl.BlockSpec((tm, D_MODEL), lambda i: (i, 0)),
        out_shape=jax.ShapeDtypeStruct((n, D_MODEL), F32),
        compiler_params=_cparams(1),
        name="merge",
    )(x, a, f, m, mods, g_norm, wg, wa, wf, wm, wo)


def _rope_tables(n_tok):
    tok = np.arange(n_tok)
    inv = ROPE_BASE ** (-np.arange(ROPE_AXIS_PAIRS, dtype=np.float32) / ROPE_AXIS_PAIRS)
    ang = np.concatenate([(tok // GRID_W).astype(np.float32)[:, None] * inv,
                          (tok % GRID_W).astype(np.float32)[:, None] * inv], axis=-1).astype(np.float32)
    c, s = np.cos(ang), np.sin(ang)
    cos_t = np.concatenate([c, c, c, c], axis=-1)
    sin_t = np.concatenate([-s, s, -s, s], axis=-1)
    return jnp.asarray(cos_t, F32), jnp.asarray(sin_t, F32)


def kernel(x_prompt, x_sample, cache_k, cache_v, state_C, state_n, state_m, c, c_ctx, w_ada, b_ada, g_norm,
           w_ffn1_in, w_ffn1_out, w_ffn2_in, w_ffn2_out, w_in, b_mgate, attn_lambda, g_attn_sub, g_mlstm,
           w_branch_gate, w_br_attn, w_br_four, w_br_mlstm, w_out, g_final):
    batch, seq, _ = x_prompt.shape
    dec_batch, dec_seq, _ = x_sample.shape
    st = _Stream(batch, seq, dec_batch, dec_seq)
    cond = jnp.zeros((N_COND, D_MODEL), F32).at[0].set(c_ctx).at[1:1 + dec_batch].set(c)
    mods_all = _ada_call(cond, w_ada, b_ada).reshape(DEPTH, N_COND, N_MOD, D_MODEL)

    gate_w = jnp.pad(w_in[:, :, P_MAIN:], ((0, 0), (0, 0), (0, LANES - N_GATE)))
    b_gate_t = jnp.pad(b_mgate, ((0, 0), (0, LANES - N_GATE)))[:, :, None]
    g_norm4 = g_norm[:, :, None, :]
    g_fin = g_final[None, :]
    ffn_w = ((w_ffn1_in.astype(BF16), w_ffn1_out.astype(BF16)), (w_ffn2_in.astype(BF16), w_ffn2_out.astype(BF16)))
    w_row = jnp.concatenate([w_in[:, :, :MV_LO], w_in[:, :, MV_HI:P_MAIN]], axis=-1).astype(BF16)
    w_col = jnp.swapaxes(jnp.concatenate(
        [w_in[:, :, 2 * W_ATTN:3 * W_ATTN], w_in[:, :, MV_LO:MV_HI], gate_w], axis=-1), 1, 2).astype(BF16)
    w_bg, w_ba, w_bf, w_bm, w_o = (t.astype(BF16) for t in
                                   (w_branch_gate, w_br_attn, w_br_four, w_br_mlstm, w_out))
    g_sub3 = g_attn_sub[:, None, :]
    g_m3 = g_mlstm[:, None, :]
    cos_t, sin_t = _rope_tables(dec_seq)
    lat_state = (state_C, state_n, state_m)

    xs = (x_prompt.reshape(batch * seq, D_MODEL), x_sample.reshape(dec_batch * dec_seq, D_MODEL))
    kv, states = None, None
    for l in range(DEPTH):
        mods = mods_all[l]
        lam_init = 0.8 - 0.6 * math.exp(-0.3 * l)
        (x,) = _ffn_call(xs, mods, g_norm4, g_fin, *ffn_w[0], st, layer=l, sub=0, final=False)
        q, k, ck, cv, zf, mq, mk, mo, vat, vt, gates_t = _proj_call(
            x, mods, g_norm4, w_row, w_col, b_gate_t, cos_t, sin_t, kv, st, layer=l)
        kv = (ck, cv)
        a = _attn_call(q, k, vat, None, attn_lambda, g_sub3, None, st, layer=l, lam_init=lam_init,
                       latent=False)
        a = _attn_call(q, k, vat, (cache_k, cache_v), attn_lambda, g_sub3, a, st, layer=l,
                       lam_init=lam_init, latent=True)
        f = _four_call(zf, None, st, latent=False)
        f = _four_call(zf, f, st, latent=True)
        m, *states = _mlstm_call(mq, mk, vt, mo, gates_t, g_m3, None, states, st, layer=l, latent=False)
        (m,) = _mlstm_call(mq, mk, vt, mo, gates_t, g_m3, lat_state, (m,), st, layer=l, latent=True)
        x = _merge_call(x, a, f, m, mods, g_norm4, w_bg, w_ba, w_bf, w_bm, w_o, st, layer=l)
        xs = tuple(_ffn_call((x,), mods, g_norm4, g_fin, *ffn_w[1], st, layer=l, sub=2,
                             final=(l == DEPTH - 1)))
    y_prompt, y_sample = xs
    return (y_prompt.reshape(batch, seq, D_MODEL), y_sample.reshape(dec_batch, dec_seq, D_MODEL),
            *kv, *states)
```

```python
import functools
import math

import numpy as np
import jax
import jax.numpy as jnp
from jax import lax
from jax.experimental import pallas as pl
from jax.experimental.pallas import tpu as pltpu

D_MODEL = 1024
DEPTH = 2
GRID_W = 64
N_ATTN = 4
DH_ATTN = 64
DV_ATTN = 2 * DH_ATTN
W_ATTN = N_ATTN * DV_ATTN
N_FOUR = 4
DG_FOUR = 128
W_FOUR = N_FOUR * DG_FOUR
N_MLSTM = 4
DH_MLSTM = 128
W_MLSTM = N_MLSTM * DH_MLSTM
N_GATE = 4 * N_MLSTM
LANES = 128
SUBLANES = 16
MV_LO = 3 * W_ATTN + W_FOUR + 2 * W_MLSTM
MV_HI = MV_LO + W_MLSTM
P_MAIN = MV_HI + W_MLSTM
P_ROW = P_MAIN - W_MLSTM
P_COL = W_ATTN + W_MLSTM + LANES
D_FF = 2816
N_MOD = 9
N_COND = 8
ROPE_BASE = 10000.0
ROPE_AXIS_PAIRS = DH_ATTN // 4
ATTN_SCALE = DH_ATTN ** -0.5
LOG2E = 1.4426950408889634
MLSTM_K_SCALE = DH_MLSTM ** -0.5
EPS = 1e-6
MLSTM_CHUNK = 256
VMEM_LIMIT = 56 * 1024 * 1024
TM_WIDE = 1024
TM_PROJ = 512

F32 = jnp.float32
BF16 = jnp.bfloat16


def _cparams(n_grid):
    return pltpu.CompilerParams(dimension_semantics=("arbitrary",) * n_grid,
                                vmem_limit_bytes=VMEM_LIMIT)


def _bdot(a, b):
    return jnp.dot(a.astype(BF16), b.astype(BF16), preferred_element_type=F32)


def _bdot_nt(a, b):
    return lax.dot_general(a.astype(BF16), b.astype(BF16), (((1,), (1,)), ((), ())),
                           preferred_element_type=F32)


def _sigmoid(x):
    return 1.0 / (1.0 + jnp.exp(-x))


def _log_sigmoid(x):
    return jnp.minimum(x, 0.0) - jnp.log1p(jnp.exp(-jnp.abs(x)))


def _rms(x, g):
    return x * lax.rsqrt(jnp.mean(x * x, axis=-1, keepdims=True) + EPS) * g


def _modulated_norm(x, g, mod, base):
    return _rms(x, g) * (1.0 + mod[base + 1:base + 2]) + mod[base:base + 1]


def _layer_spec(shape, layer):
    return pl.BlockSpec((None,) + shape, lambda *_: (layer,) + (0,) * len(shape),
                        pipeline_mode=pl.Buffered(1))


class _Stream:
    def __init__(self, batch, seq, dec_batch, dec_seq):
        self.batch, self.seq, self.dec_batch, self.dec_seq = batch, seq, dec_batch, dec_seq
        self.n_ctx = batch * seq
        self.n_all = self.n_ctx + dec_batch * dec_seq

    def ctx_tiles(self, tm):
        if self.n_ctx % tm or self.dec_seq % tm:
            raise ValueError("row tile must divide the context rows and one latent sequence")
        return self.n_ctx // tm

    def mod_spec(self, tm):
        n_pt, tps = self.ctx_tiles(tm), self.dec_seq // tm
        return pl.BlockSpec((None, N_MOD, D_MODEL),
                            lambda i: (jnp.where(i < n_pt, 0, 1 + (i - n_pt) // tps), 0, 0))


def _ada_kernel(c_ref, w_ref, b_ref, o_ref):
    c = c_ref[...]
    s = c * _sigmoid(c)
    o_ref[...] = _bdot(s, w_ref[...]) + b_ref[...]


def _ada_call(cond, w_ada, b_ada):
    tn = 2304
    n_out = N_MOD * D_MODEL
    return pl.pallas_call(
        _ada_kernel,
        grid=(DEPTH, n_out // tn),
        in_specs=[
            pl.BlockSpec((N_COND, D_MODEL), lambda l, j: (0, 0)),
            pl.BlockSpec((None, D_MODEL, tn), lambda l, j: (l, 0, j)),
            pl.BlockSpec((None, 1, tn), lambda l, j: (l, 0, j)),
        ],
        out_specs=pl.BlockSpec((None, N_COND, tn), lambda l, j: (l, 0, j)),
        out_shape=jax.ShapeDtypeStruct((DEPTH, N_COND, n_out), F32),
        compiler_params=_cparams(2),
        name="adaln",
    )(cond, w_ada, b_ada.reshape(DEPTH, 1, n_out))


FF_CHUNKS = (768, 768, 768, 512)


def _ffn_kernel(*refs, base, n_x, final, n_pt):
    x_refs = refs[:n_x]
    mod_ref, g_ref, gf_ref, win_ref, wout_ref = refs[n_x:n_x + 5]
    o_refs = refs[n_x + 5:]
    i = pl.program_id(0)
    if n_x == 2:
        x = jnp.where(i < n_pt, x_refs[0][...], x_refs[1][...])
    else:
        x = x_refs[0][...]
    mod = mod_ref[...]
    u = _modulated_norm(x, g_ref[...], mod, base).astype(BF16)
    y = None
    lo = 0
    for width in FF_CHUNKS:
        a = jnp.dot(u, win_ref[:, lo:lo + width], preferred_element_type=F32)
        g = jnp.dot(u, win_ref[:, D_FF + lo:D_FF + lo + width], preferred_element_type=F32)
        hh = (a * _sigmoid(a) * g).astype(BF16)
        part = jnp.dot(hh, wout_ref[lo:lo + width, :], preferred_element_type=F32)
        y = part if y is None else y + part
        lo += width
    xn = x + 0.5 * mod[base + 2:base + 3] * y
    if not final:
        o_refs[0][...] = xn
    else:
        xn = _rms(xn, gf_ref[...])

        @pl.when(i < n_pt)
        def _():
            o_refs[0][...] = xn

        @pl.when(i >= n_pt)
        def _():
            o_refs[1][...] = xn


def _ffn_call(xs, mods, g_norm, g_final, w_in, w_out, st, *, layer, sub, final):
    tm = TM_WIDE
    n_pt = st.ctx_tiles(tm)
    n_tiles = st.n_all // tm
    ctx_spec = pl.BlockSpec((tm, D_MODEL), lambda i: (jnp.minimum(i, n_pt - 1), 0))
    lat_spec = pl.BlockSpec((tm, D_MODEL), lambda i: (jnp.maximum(i - n_pt, 0), 0))
    all_spec = pl.BlockSpec((tm, D_MODEL), lambda i: (i, 0))
    x_specs = [ctx_spec, lat_spec] if len(xs) == 2 else [all_spec]
    if final:
        out_specs = [ctx_spec, lat_spec]
        out_shape = [jax.ShapeDtypeStruct((st.n_ctx, D_MODEL), F32),
                     jax.ShapeDtypeStruct((st.n_all - st.n_ctx, D_MODEL), F32)]
    else:
        out_specs = [all_spec]
        out_shape = [jax.ShapeDtypeStruct((st.n_all, D_MODEL), F32)]
    return pl.pallas_call(
        functools.partial(_ffn_kernel, base=3 * sub, n_x=len(xs), final=final, n_pt=n_pt),
        grid=(n_tiles,),
        in_specs=x_specs + [
            st.mod_spec(tm),
            pl.BlockSpec((None, None, 1, D_MODEL), lambda i: (layer, sub, 0, 0)),
            pl.BlockSpec((1, D_MODEL), lambda i: (0, 0)),
            _layer_spec((D_MODEL, 2 * D_FF), layer),
            _layer_spec((D_FF, D_MODEL), layer),
        ],
        out_specs=out_specs,
        out_shape=out_shape,
        compiler_params=_cparams(1),
        name="ffn",
    )(*xs, mods, g_norm, g_final, w_in, w_out)


def _swap32(x):
    lane = lax.broadcasted_iota(jnp.int32, x.shape, 1)
    return jnp.where((lane & (DH_ATTN // 2)) == 0,
                     pltpu.roll(x, LANES - DH_ATTN // 2, 1), pltpu.roll(x, DH_ATTN // 2, 1))


def _proj_kernel(*refs, n_alias, n_pt, seqs_per_tile, seq):
    (x_ref, mod_ref, g_ref, w_ref, wgt_ref, bgt_ref, cos_ref, sin_ref) = refs[:8]
    (q_ref, k_ref, ck_ref, cv_ref, zf_ref, mq_ref, mk_ref, mo_ref, vat_ref, vt_ref, gt_ref,
     wt_sc) = refs[8 + n_alias:]
    i = pl.program_id(0)
    is_latent = i >= n_pt

    @pl.when(i == 0)
    def _():
        wt_sc[0:W_ATTN, :] = w_ref[:, 2 * W_ATTN:3 * W_ATTN].T
        wt_sc[W_ATTN:, :] = w_ref[:, MV_LO:MV_HI].T

    x = x_ref[...]
    u = _modulated_norm(x, g_ref[...], mod_ref[...], 3).astype(BF16)
    za = jnp.dot(u, w_ref[:, :3 * W_ATTN], preferred_element_type=F32)
    zb = jnp.dot(u, w_ref[:, 3 * W_ATTN:MV_LO], preferred_element_type=F32)
    zo = jnp.dot(u, w_ref[:, MV_HI:P_MAIN], preferred_element_type=F32)
    zt = _bdot_nt(wt_sc[...], u)
    zg = _bdot_nt(wgt_ref[...], u)
    cos_t = cos_ref[...]
    sin_t = sin_ref[...]
    plain_k = []
    for h in range(N_ATTN):
        zq = za[:, h * DV_ATTN:(h + 1) * DV_ATTN]
        zk = za[:, W_ATTN + h * DV_ATTN:W_ATTN + (h + 1) * DV_ATTN]
        plain_k.append(zk)
        q_ref[h] = jnp.where(is_latent, zq * cos_t + _swap32(zq) * sin_t, zq).astype(q_ref.dtype)
        k_ref[h] = jnp.where(is_latent, zk * cos_t + _swap32(zk) * sin_t, zk).astype(k_ref.dtype)

    @pl.when(i < n_pt)
    def _():
        for h in range(N_ATTN):
            zv = za[:, 2 * W_ATTN + h * DV_ATTN:2 * W_ATTN + (h + 1) * DV_ATTN]
            for s in range(seqs_per_tile):
                rs = slice(s * seq, (s + 1) * seq)
                ck_ref[s, h] = plain_k[h][rs]
                cv_ref[s, h] = zv[rs]

    zf_ref[...] = zb[:, :W_FOUR].astype(zf_ref.dtype)
    off = W_FOUR
    mq_ref[...] = zb[:, off:off + W_MLSTM].astype(mq_ref.dtype)
    mk_ref[...] = (zb[:, off + W_MLSTM:off + 2 * W_MLSTM] * MLSTM_K_SCALE).astype(mk_ref.dtype)
    mo_ref[...] = zo
    vat_ref[...] = zt[:W_ATTN].astype(vat_ref.dtype)
    vt_ref[...] = zt[W_ATTN:]
    gpt = zg + bgt_ref[...]
    sub = lax.broadcasted_iota(jnp.int32, gpt.shape, 0)
    gt_ref[...] = jnp.where((sub < N_GATE) & ((sub & N_MLSTM) != 0), _log_sigmoid(gpt), gpt)


def _proj_call(x, mods, g_norm, w_all, w_gate_t, bgt, cos_t, sin_t, kv_prev, st, *, layer):
    tm = TM_PROJ
    n_pt = st.ctx_tiles(tm)
    if tm % st.seq:
        raise ValueError("a context row tile must hold whole sequences")
    seqs_per_tile, tps = tm // st.seq, st.dec_seq // tm
    n = st.n_all
    head_shape = jax.ShapeDtypeStruct((N_ATTN, n, DV_ATTN), BF16)
    head_spec = pl.BlockSpec((N_ATTN, tm, DV_ATTN), lambda i: (0, i, 0))
    kv_shape = jax.ShapeDtypeStruct((st.batch, DEPTH, N_ATTN, st.seq, DV_ATTN), F32)
    kv_spec = pl.BlockSpec((seqs_per_tile, None, N_ATTN, st.seq, DV_ATTN),
                           lambda i: (jnp.minimum(i, n_pt - 1), layer, 0, 0, 0))
    tok_shape = jax.ShapeDtypeStruct((n, W_ATTN), F32)
    tok_bf16 = jax.ShapeDtypeStruct((n, W_ATTN), BF16)
    tok_spec = pl.BlockSpec((tm, W_ATTN), lambda i: (i, 0))
    rope_spec = pl.BlockSpec((tm, LANES), lambda i: (jnp.maximum(i - n_pt, 0) % tps, 0))
    alias_in = list(kv_prev) if kv_prev is not None else []
    n_in = 8
    return pl.pallas_call(
        functools.partial(_proj_kernel, n_alias=len(alias_in), n_pt=n_pt, seqs_per_tile=seqs_per_tile,
                          seq=st.seq),
        grid=(n // tm,),
        in_specs=[
            pl.BlockSpec((tm, D_MODEL), lambda i: (i, 0)),
            st.mod_spec(tm),
            pl.BlockSpec((None, None, 1, D_MODEL), lambda i: (layer, 1, 0, 0)),
            _layer_spec((D_MODEL, w_all.shape[-1]), layer),
            _layer_spec((LANES, D_MODEL), layer),
            _layer_spec((LANES, 1), layer),
            rope_spec, rope_spec,
        ] + [pl.BlockSpec(memory_space=pl.ANY)] * len(alias_in),
        out_specs=[head_spec, head_spec, kv_spec, kv_spec, tok_spec, tok_spec, tok_spec, tok_spec,
                   pl.BlockSpec((W_ATTN, tm), lambda i: (0, i)),
                   pl.BlockSpec((W_MLSTM, tm), lambda i: (0, i)),
                   pl.BlockSpec((LANES, tm), lambda i: (0, i))],
        out_shape=[head_shape, head_shape, kv_shape, kv_shape, tok_bf16, tok_bf16, tok_bf16, tok_shape,
                   jax.ShapeDtypeStruct((W_ATTN, n), BF16),
                   jax.ShapeDtypeStruct((W_MLSTM, n), F32),
                   jax.ShapeDtypeStruct((LANES, n), F32)],
        input_output_aliases={n_in + j: 2 + j for j in range(len(alias_in))},
        scratch_shapes=[pltpu.VMEM((W_ATTN + W_MLSTM, D_MODEL), BF16)],
        compiler_params=_cparams(1),
        name="mixer_proj",
    )(x, mods, g_norm, w_all, w_gate_t, bgt, cos_t, sin_t, *alias_in)


def _attn_kernel(*refs, lam_init, cached, nb, tq, seq):
    q_ref, k_ref, vt_ref = refs[:3]
    pos = 3
    if cached:
        ck_ref, cv_ref = refs[pos:pos + 2]
        pos += 2
    lam_ref, g_ref = refs[pos:pos + 2]
    o_ref = refs[-1]
    lp = lam_ref[...]
    lam = (jnp.exp(jnp.sum(lp[0:1] * lp[1:2], axis=-1, keepdims=True))
           - jnp.exp(jnp.sum(lp[2:3] * lp[3:4], axis=-1, keepdims=True)) + lam_init)
    g_sub = g_ref[...]
    lane = lax.broadcasted_iota(jnp.int32, (tq, DV_ATTN), 1)
    results = []
    for bi in range(nb):
        for h in range(N_ATTN):
            hs = slice(h * DV_ATTN, (h + 1) * DV_ATTN)
            ks = slice(bi * seq, (bi + 1) * seq)
            q = q_ref[h, bi * tq:(bi + 1) * tq].astype(F32) * (ATTN_SCALE * LOG2E)
            q_maps = (jnp.where(lane < DH_ATTN, q, 0.0).astype(BF16),
                      jnp.where(lane >= DH_ATTN, q, 0.0).astype(BF16))
            keys = [k_ref[h, ks]]
            vals_t = [vt_ref[hs, ks]]
            if cached:
                keys.insert(0, ck_ref[h].astype(BF16))
                vals_t.insert(0, cv_ref[h].T.astype(BF16))
            exps, dens = [], []
            for qm in q_maps:
                s = [_bdot_nt(kk, qm) for kk in keys]
                m = functools.reduce(jnp.maximum, [jnp.max(si, axis=0, keepdims=True) for si in s])
                e = [jnp.exp2(si - m) for si in s]
                exps.append(e)
                dens.append(functools.reduce(jnp.add, [jnp.sum(ei, axis=0, keepdims=True) for ei in e]))
            ratio = lam * dens[0] / dens[1]
            o_t = None
            for j, vv in enumerate(vals_t):
                p = exps[0][j] - exps[1][j] * ratio
                part = jnp.dot(vv, p.astype(BF16), preferred_element_type=F32)
                o_t = part if o_t is None else o_t + part
            o_t = o_t * (1.0 / dens[0])
            y_t = o_t * lax.rsqrt(jnp.mean(o_t * o_t, axis=0, keepdims=True) + EPS)
            results.append((bi, hs, (y_t.T * g_sub * (1.0 - lam_init)).astype(o_ref.dtype)))
    for bi, hs, y in results:
        o_ref[bi * tq:(bi + 1) * tq, hs] = y


def _attn_call(q, k, v_t, cache, lam_p, g_sub, out_prev, st, *, layer, lam_init, latent):
    n = st.n_all
    if latent:
        batch, seq, nb, tq = st.dec_batch, st.dec_seq, 1, st.dec_seq // 2
        row0 = st.n_ctx
    else:
        batch, seq, nb, tq = st.batch, st.seq, 4, st.seq
        row0 = 0
    nq = seq // tq
    if nq > 1 and nb > 1:
        raise ValueError("query blocks of several sequences are not contiguous rows")
    q0, s0 = row0 // (nb * tq), row0 // (nb * seq)
    in_specs = [pl.BlockSpec((N_ATTN, nb * tq, DV_ATTN), lambda b, i: (0, q0 + b * nq + i, 0)),
                pl.BlockSpec((N_ATTN, nb * seq, DV_ATTN), lambda b, i: (0, s0 + b, 0)),
                pl.BlockSpec((W_ATTN, nb * seq), lambda b, i: (0, s0 + b))]
    args = [q, k, v_t]
    if cache is not None:
        past = cache[0].shape[3]
        c_spec = pl.BlockSpec((None, None, N_ATTN, past, DV_ATTN), lambda b, i: (b, layer, 0, 0, 0))
        in_specs += [c_spec, c_spec]
        args += list(cache)
    in_specs += [pl.BlockSpec((None, 4, DH_ATTN), lambda b, i: (layer, 0, 0)),
                 pl.BlockSpec((None, 1, DV_ATTN), lambda b, i: (layer, 0, 0))]
    args += [lam_p, g_sub]
    aliases = {}
    if out_prev is not None:
        aliases = {len(args): 0}
        in_specs.append(pl.BlockSpec(memory_space=pl.ANY))
        args.append(out_prev)
    return pl.pallas_call(
        functools.partial(_attn_kernel, lam_init=lam_init, cached=cache is not None, nb=nb, tq=tq, seq=seq),
        grid=(batch // nb, nq),
        in_specs=in_specs,
        out_specs=pl.BlockSpec((nb * tq, W_ATTN), lambda b, i: (q0 + b * nq + i, 0)),
        out_shape=jax.ShapeDtypeStruct((n, W_ATTN), BF16),
        input_output_aliases=aliases,
        compiler_params=_cparams(2),
        name="diff_attn",
    )(*args)


def _dft_tables(seq):
    def cs(n):
        j = np.arange(n)
        ang = 2.0 * np.pi * ((j[:, None] * j[None, :]) % n) / n
        return np.cos(ang) / math.sqrt(n), np.sin(ang) / math.sqrt(n)

    cd, sd = cs(DG_FOUR)
    ct, st = cs(seq)
    w_d = jnp.asarray(np.concatenate([cd, sd], axis=1), F32)
    return w_d.astype(BF16), jnp.asarray(ct, F32).astype(BF16), jnp.asarray(-st, F32).astype(BF16)


def _four_kernel(*refs, nb, seq):
    z_ref, wd_ref, ct_ref, st_ref = refs[:4]
    o_ref = refs[-1]
    y_cos, y_sin = [], []
    for gidx in range(N_FOUR):
        y = jnp.dot(z_ref[:, gidx * DG_FOUR:(gidx + 1) * DG_FOUR], wd_ref[...], preferred_element_type=F32)
        y_cos.append(y[:, :DG_FOUR].astype(BF16))
        y_sin.append(y[:, DG_FOUR:].astype(BF16))
    y_cos = jnp.concatenate(y_cos, axis=1)
    y_sin = jnp.concatenate(y_sin, axis=1)
    for s in range(nb):
        rows = slice(s * seq, (s + 1) * seq)
        o_ref[rows, :] = (jnp.dot(ct_ref[...], y_cos[rows], preferred_element_type=F32)
                          + jnp.dot(st_ref[...], y_sin[rows], preferred_element_type=F32)).astype(o_ref.dtype)


def _four_call(zf, out_prev, st, *, latent):
    if latent:
        batch, seq, nb, row0 = st.dec_batch, st.dec_seq, 1, st.n_ctx
    else:
        batch, seq, nb, row0 = st.batch, st.seq, 4, 0
    b0 = row0 // (nb * seq)
    w_d, ct, s_t = _dft_tables(seq)
    row_spec = pl.BlockSpec((nb * seq, W_FOUR), lambda b: (b0 + b, 0))
    in_specs = [row_spec,
                pl.BlockSpec((DG_FOUR, 2 * DG_FOUR), lambda b: (0, 0)),
                pl.BlockSpec((seq, seq), lambda b: (0, 0)),
                pl.BlockSpec((seq, seq), lambda b: (0, 0))]
    args = [zf, w_d, ct, s_t]
    aliases = {}
    if out_prev is not None:
        aliases = {len(args): 0}
        in_specs.append(pl.BlockSpec(memory_space=pl.ANY))
        args.append(out_prev)
    return pl.pallas_call(
        functools.partial(_four_kernel, nb=nb, seq=seq),
        grid=(batch // nb,),
        in_specs=in_specs,
        out_specs=row_spec,
        out_shape=jax.ShapeDtypeStruct((st.n_all, W_FOUR), BF16),
        input_output_aliases=aliases,
        compiler_params=_cparams(1),
        name="fourier_mix",
    )(*args)


def _mlstm_kernel(*refs, seq, nb, seeded, emit_state, n_alias):
    mq_ref, mk_ref, vt_ref, mo_ref, gt_ref, g_ref = refs[:6]
    pos = 6
    if seeded:
        c0_ref, n0_ref, m0_ref = refs[pos:pos + 3]
        pos += 3
    pos += n_alias
    o_ref = refs[pos]
    if emit_state:
        c1_ref, n1_ref, m1_ref = refs[pos + 1:pos + 4]

    L = min(MLSTM_CHUNK, seq)
    nc = seq // L
    s_idx = lax.broadcasted_iota(jnp.int32, (L, L), 0)
    t_idx = lax.broadcasted_iota(jnp.int32, (L, L), 1)
    before = (s_idx <= t_idx, s_idx >= t_idx)
    tri = (jnp.where(before[1], 1.0, 0.0).astype(BF16),
           jnp.where(before[0], 1.0, 0.0).astype(BF16))
    chains = [(bi, d, h) for bi in range(nb) for d in range(2) for h in range(N_MLSTM)]

    state = {}
    for (bi, d, h) in chains:
        if seeded:
            state[bi, d, h] = (c0_ref[bi, d, h], n0_ref[bi, d, h:h + 1, :], m0_ref[bi, d:d + 1, h:h + 1])
        else:
            state[bi, d, h] = (None, None, jnp.zeros((1, 1), F32))

    h_t = {}
    for c in range(nc):
        terms = {}
        for bi in range(nb):
            for d in range(2):
                r0 = bi * seq + (c if d == 0 else nc - 1 - c) * L
                g_t = gt_ref[0:N_GATE, r0:r0 + L]
                hi = g_t.astype(BF16)
                rem = g_t - hi.astype(F32)
                mid = rem.astype(BF16)
                lo = (rem - mid.astype(F32)).astype(BF16)
                pieces = _bdot_nt(jnp.concatenate([hi, mid, lo], axis=0), tri[d])
                cum_t = pieces[0:N_GATE] + pieces[N_GATE:2 * N_GATE] + pieces[2 * N_GATE:]
                c_t = g_t - pltpu.roll(cum_t, N_GATE - N_MLSTM, 0)
                col = jnp.concatenate([c_t, jnp.zeros((LANES - N_GATE, L), F32)], axis=0).T
                terms[bi, d] = (r0, col, g_t, cum_t)

        new_state = {}
        for (bi, d, h) in chains:
            r0, col, g_t, cum_t = terms[bi, d]
            c_prev, n_prev, m_prev = state[bi, d, h]
            has_state = c_prev is not None
            need_update = emit_state or c < nc - 1
            ci = 2 * d * N_MLSTM + h
            cf = ci + N_MLSTM
            last = L - 1 if d == 0 else 0
            hs = slice(h * DH_MLSTM, (h + 1) * DH_MLSTM)
            q = mq_ref[r0:r0 + L, hs]
            k = mk_ref[r0:r0 + L, hs]
            v_t = vt_ref[hs, r0:r0 + L]
            i_row, b_row = g_t[ci:ci + 1, :], cum_t[cf:cf + 1, :]

            cb = jnp.where(before[d], col[:, ci:ci + 1], -jnp.inf)
            m_row = jnp.maximum(jnp.max(cb, axis=0, keepdims=True), m_prev)
            s_t = _bdot_nt(k, q) * jnp.exp(cb - m_row)
            num_t = jnp.dot(v_t.astype(BF16), s_t.astype(BF16), preferred_element_type=F32)
            den = jnp.sum(s_t, axis=0, keepdims=True)
            if has_state:
                sp = jnp.exp(m_prev - m_row)
                cn = jnp.concatenate([c_prev, jnp.broadcast_to(n_prev, (SUBLANES, DH_MLSTM))], axis=0)
                cq = _bdot_nt(cn, q)
                num_t = num_t + sp * cq[:DH_MLSTM]
                den = den + sp * cq[DH_MLSTM:DH_MLSTM + 1]
            m_t = b_row + m_row
            h_t[bi, d, h, c] = num_t / jnp.maximum(jnp.abs(den), jnp.exp(-m_t))
            if need_update:
                b_tot = b_row[:, last:last + 1]
                m_new = m_t[:, last:last + 1]
                wl = jnp.exp(b_tot + (i_row - b_row) - m_new)
                vw = jnp.concatenate([v_t * wl, jnp.broadcast_to(wl, (SUBLANES, L))], axis=0)
                upd = jnp.dot(vw.astype(BF16), k, preferred_element_type=F32)
                c_new, n_new = upd[:DH_MLSTM], upd[DH_MLSTM:DH_MLSTM + 1]
                if has_state:
                    decay = jnp.exp(b_tot + m_prev - m_new)
                    c_new = decay * c_prev + c_new
                    n_new = decay * n_prev + n_new
                new_state[bi, d, h] = (c_new, n_new, m_new)
        state = new_state

    g_m = g_ref[...]
    for bi in range(nb):
        for h in range(N_MLSTM):
            hs = slice(h * DH_MLSTM, (h + 1) * DH_MLSTM)
            fwd = [h_t[bi, 0, h, c] for c in range(nc)]
            bwd = [h_t[bi, 1, h, nc - 1 - c] for c in range(nc)]
            hsum = (fwd[0] if nc == 1 else jnp.concatenate(fwd, axis=1)) \
                + (bwd[0] if nc == 1 else jnp.concatenate(bwd, axis=1))
            y = hsum * lax.rsqrt(jnp.mean(hsum * hsum, axis=0, keepdims=True) + EPS)
            rows = slice(bi * seq, (bi + 1) * seq)
            o_ref[rows, hs] = (y.T * g_m * _sigmoid(mo_ref[rows, hs])).astype(o_ref.dtype)

    if emit_state:
        for (bi, d, h) in chains:
            c_fin, n_fin, m_fin = state[bi, d, h]
            c1_ref[bi, d, h] = c_fin
            n1_ref[bi, d, h:h + 1, :] = n_fin
            m1_ref[bi, d:d + 1, h:h + 1] = m_fin


def _mlstm_call(mq, mk, vt, mo, gates_t, g_m, state, alias_prev, st, *, layer, latent):
    if latent:
        batch, seq, nb, row0 = st.dec_batch, st.dec_seq, 1, st.n_ctx
    else:
        batch, seq, nb, row0 = st.batch, st.seq, 4, 0
    emit_state = not latent
    rows = nb * seq
    b0 = row0 // rows
    tok_spec = pl.BlockSpec((rows, W_MLSTM), lambda b: (b0 + b, 0))
    in_specs = [tok_spec, tok_spec,
                pl.BlockSpec((W_MLSTM, rows), lambda b: (0, b0 + b)),
                tok_spec,
                pl.BlockSpec((LANES, rows), lambda b: (0, b0 + b)),
                pl.BlockSpec((None, 1, DH_MLSTM), lambda b: (layer, 0, 0))]
    args = [mq, mk, vt, mo, gates_t, g_m]
    state_specs = [
        pl.BlockSpec((nb, None, 2, N_MLSTM, DH_MLSTM, DH_MLSTM), lambda b: (b, layer, 0, 0, 0, 0)),
        pl.BlockSpec((nb, None, 2, N_MLSTM, DH_MLSTM), lambda b: (b, layer, 0, 0, 0)),
        pl.BlockSpec((nb, None, 2, N_MLSTM), lambda b: (b, layer, 0, 0)),
    ]
    if state is not None:
        in_specs += state_specs
        args += list(state)
    alias_in = list(alias_prev) if alias_prev is not None else []
    n_in = len(args)
    in_specs += [pl.BlockSpec(memory_space=pl.ANY)] * len(alias_in)
    args += alias_in
    out_specs = [tok_spec]
    out_shape = [jax.ShapeDtypeStruct((st.n_all, W_MLSTM), BF16)]
    if emit_state:
        out_specs += state_specs
        out_shape += [
            jax.ShapeDtypeStruct((batch, DEPTH, 2, N_MLSTM, DH_MLSTM, DH_MLSTM), F32),
            jax.ShapeDtypeStruct((batch, DEPTH, 2, N_MLSTM, DH_MLSTM), F32),
            jax.ShapeDtypeStruct((batch, DEPTH, 2, N_MLSTM), F32),
        ]
    first_out = 1 if emit_state else 0
    return pl.pallas_call(
        functools.partial(_mlstm_kernel, seq=seq, nb=nb, seeded=state is not None, emit_state=emit_state,
                          n_alias=len(alias_in)),
        grid=(batch // nb,),
        in_specs=in_specs,
        out_specs=out_specs,
        out_shape=out_shape,
        input_output_aliases={n_in + j: first_out + j for j in range(len(alias_in))},
        compiler_params=_cparams(1),
        name="mlstm",
    )(*args)


def _merge_kernel(x_ref, a_ref, f_ref, m_ref, mod_ref, g_ref, wg_ref, wa_ref, wf_ref, wm_ref, wo_ref, o_ref):
    x = x_ref[...]
    mod = mod_ref[...]
    u = _modulated_norm(x, g_ref[...], mod, 3).astype(BF16)
    merged = None
    for j, (br_ref, w_ref) in enumerate(((a_ref, wa_ref), (f_ref, wf_ref), (m_ref, wm_ref))):
        gate = _sigmoid(jnp.dot(u, wg_ref[:, j * D_MODEL:(j + 1) * D_MODEL], preferred_element_type=F32))
        term = gate * _bdot(br_ref[...], w_ref[...])
        merged = term if merged is None else merged + term
    out = _bdot(merged, wo_ref[...])
    o_ref[...] = x + mod[5:6] * out


def _merge_call(x, a, f, m, mods, g_norm, wg, wa, wf, wm, wo, st, *, layer):
    tm = TM_WIDE
    n = st.n_all
    br_spec = pl.BlockSpec((tm, W_ATTN), lambda i: (i, 0))
    return pl.pallas_call(
        _merge_kernel,
        grid=(n // tm,),
        in_specs=[
            pl.BlockSpec((tm, D_MODEL), lambda i: (i, 0)),
            br_spec, br_spec, br_spec,
            st.mod_spec(tm),
            pl.BlockSpec((None, None, 1, D_MODEL), lambda i: (layer, 1, 0, 0)),
            _layer_spec((D_MODEL, 3 * D_MODEL), layer),
            _layer_spec((W_ATTN, D_MODEL), layer), _layer_spec((W_FOUR, D_MODEL), layer),
            _layer_spec((W_MLSTM, D_MODEL), layer),
            _layer_spec((D_MODEL, D_MODEL), layer),
        ],
        out_specs=pl.BlockSpec((tm, D_MODEL), lambda i: (i, 0)),
        out_shape=jax.ShapeDtypeStruct((n, D_MODEL), F32),
        compiler_params=_cparams(1),
        name="merge",
    )(x, a, f, m, mods, g_norm, wg, wa, wf, wm, wo)


def _rope_tables(n_tok):
    tok = np.arange(n_tok)
    inv = ROPE_BASE ** (-np.arange(ROPE_AXIS_PAIRS, dtype=np.float32) / ROPE_AXIS_PAIRS)
    ang = np.concatenate([(tok // GRID_W).astype(np.float32)[:, None] * inv,
                          (tok % GRID_W).astype(np.float32)[:, None] * inv], axis=-1).astype(np.float32)
    c, s = np.cos(ang), np.sin(ang)
    cos_t = np.concatenate([c, c, c, c], axis=-1)
    sin_t = np.concatenate([-s, s, -s, s], axis=-1)
    return jnp.asarray(cos_t, F32), jnp.asarray(sin_t, F32)


def kernel(x_prompt, x_sample, cache_k, cache_v, state_C, state_n, state_m, c, c_ctx, w_ada, b_ada, g_norm,
           w_ffn1_in, w_ffn1_out, w_ffn2_in, w_ffn2_out, w_in, b_mgate, attn_lambda, g_attn_sub, g_mlstm,
           w_branch_gate, w_br_attn, w_br_four, w_br_mlstm, w_out, g_final):
    batch, seq, _ = x_prompt.shape
    dec_batch, dec_seq, _ = x_sample.shape
    st = _Stream(batch, seq, dec_batch, dec_seq)
    cond = jnp.zeros((N_COND, D_MODEL), F32).at[0].set(c_ctx).at[1:1 + dec_batch].set(c)
    mods_all = _ada_call(cond, w_ada, b_ada).reshape(DEPTH, N_COND, N_MOD, D_MODEL)

    w_gate_t = jnp.pad(jnp.swapaxes(w_in[:, :, P_MAIN:], 1, 2), ((0, 0), (0, LANES - N_GATE), (0, 0))).astype(BF16)
    b_gate_t = jnp.pad(b_mgate, ((0, 0), (0, LANES - N_GATE)))[:, :, None]
    g_norm4 = g_norm[:, :, None, :]
    g_fin = g_final[None, :]
    ffn_w = ((w_ffn1_in.astype(BF16), w_ffn1_out.astype(BF16)), (w_ffn2_in.astype(BF16), w_ffn2_out.astype(BF16)))
    w_all = w_in.astype(BF16)
    w_bg, w_ba, w_bf, w_bm, w_o = (t.astype(BF16) for t in
                                   (w_branch_gate, w_br_attn, w_br_four, w_br_mlstm, w_out))
    g_sub3 = g_attn_sub[:, None, :]
    g_m3 = g_mlstm[:, None, :]
    cos_t, sin_t = _rope_tables(dec_seq)
    lat_state = (state_C, state_n, state_m)

    xs = (x_prompt.reshape(batch * seq, D_MODEL), x_sample.reshape(dec_batch * dec_seq, D_MODEL))
    kv, states = None, None
    for l in range(DEPTH):
        mods = mods_all[l]
        lam_init = 0.8 - 0.6 * math.exp(-0.3 * l)
        (x,) = _ffn_call(xs, mods, g_norm4, g_fin, *ffn_w[0], st, layer=l, sub=0, final=False)
        q, k, ck, cv, zf, mq, mk, mo, vat, vt, gates_t = _proj_call(
            x, mods, g_norm4, w_all, w_gate_t, b_gate_t, cos_t, sin_t, kv, st, layer=l)
        kv = (ck, cv)
        a = _attn_call(q, k, vat, None, attn_lambda, g_sub3, None, st, layer=l, lam_init=lam_init,
                       latent=False)
        a = _attn_call(q, k, vat, (cache_k, cache_v), attn_lambda, g_sub3, a, st, layer=l,
                       lam_init=lam_init, latent=True)
        f = _four_call(zf, None, st, latent=False)
        f = _four_call(zf, f, st, latent=True)
        m, *states = _mlstm_call(mq, mk, vt, mo, gates_t, g_m3, None, states, st, layer=l, latent=False)
        (m,) = _mlstm_call(mq, mk, vt, mo, gates_t, g_m3, lat_state, (m,), st, layer=l, latent=True)
        x = _merge_call(x, a, f, m, mods, g_norm4, w_bg, w_ba, w_bf, w_bm, w_o, st, layer=l)
        xs = tuple(_ffn_call((x,), mods, g_norm4, g_fin, *ffn_w[1], st, layer=l, sub=2,
                             final=(l == DEPTH - 1)))
    y_prompt, y_sample = xs
    return (y_prompt.reshape(batch, seq, D_MODEL), y_sample.reshape(dec_batch, dec_seq, D_MODEL),
            *kv, *states)
```

```python
import functools
import math

import numpy as np
import jax
import jax.numpy as jnp
from jax import lax
from jax.experimental import pallas as pl
from jax.experimental.pallas import tpu as pltpu

D_MODEL = 1024
DEPTH = 2
GRID_W = 64
N_ATTN = 4
DH_ATTN = 64
DV_ATTN = 2 * DH_ATTN
W_ATTN = N_ATTN * DV_ATTN
N_FOUR = 4
DG_FOUR = 128
W_FOUR = N_FOUR * DG_FOUR
N_MLSTM = 4
DH_MLSTM = 128
W_MLSTM = N_MLSTM * DH_MLSTM
N_GATE = 4 * N_MLSTM
LANES = 128
SUBLANES = 16
MV_LO = 3 * W_ATTN + W_FOUR + 2 * W_MLSTM
MV_HI = MV_LO + W_MLSTM
P_MAIN = MV_HI + W_MLSTM
P_ROW = P_MAIN - W_MLSTM
P_COL = W_ATTN + W_MLSTM + LANES
D_FF = 2816
N_MOD = 9
N_COND = 8
ROPE_BASE = 10000.0
ROPE_AXIS_PAIRS = DH_ATTN // 4
ATTN_SCALE = DH_ATTN ** -0.5
LOG2E = 1.4426950408889634
MLSTM_K_SCALE = DH_MLSTM ** -0.5
EPS = 1e-6
MLSTM_CHUNK = 256
VMEM_LIMIT = 56 * 1024 * 1024
TM_WIDE = 1024
TM_PROJ = 512

F32 = jnp.float32
BF16 = jnp.bfloat16


def _cparams(n_grid):
    return pltpu.CompilerParams(dimension_semantics=("arbitrary",) * n_grid,
                                vmem_limit_bytes=VMEM_LIMIT)


def _bdot(a, b):
    return jnp.dot(a.astype(BF16), b.astype(BF16), preferred_element_type=F32)


def _bdot_nt(a, b):
    return lax.dot_general(a.astype(BF16), b.astype(BF16), (((1,), (1,)), ((), ())),
                           preferred_element_type=F32)


def _sigmoid(x):
    return 1.0 / (1.0 + jnp.exp(-x))


def _log_sigmoid(x):
    return jnp.minimum(x, 0.0) - jnp.log1p(jnp.exp(-jnp.abs(x)))


def _rms(x, g):
    return x * lax.rsqrt(jnp.mean(x * x, axis=-1, keepdims=True) + EPS) * g


def _modulated_norm(x, g, mod, base):
    return _rms(x, g) * (1.0 + mod[base + 1:base + 2]) + mod[base:base + 1]


def _layer_spec(shape, layer):
    return pl.BlockSpec((None,) + shape, lambda *_: (layer,) + (0,) * len(shape),
                        pipeline_mode=pl.Buffered(1))


class _Stream:
    def __init__(self, batch, seq, dec_batch, dec_seq):
        self.batch, self.seq, self.dec_batch, self.dec_seq = batch, seq, dec_batch, dec_seq
        self.n_ctx = batch * seq
        self.n_all = self.n_ctx + dec_batch * dec_seq

    def ctx_tiles(self, tm):
        if self.n_ctx % tm or self.dec_seq % tm:
            raise ValueError("row tile must divide the context rows and one latent sequence")
        return self.n_ctx // tm

    def mod_spec(self, tm):
        n_pt, tps = self.ctx_tiles(tm), self.dec_seq // tm
        return pl.BlockSpec((None, N_MOD, D_MODEL),
                            lambda i: (jnp.where(i < n_pt, 0, 1 + (i - n_pt) // tps), 0, 0))


def _ada_kernel(c_ref, w_ref, b_ref, o_ref):
    c = c_ref[...]
    s = c * _sigmoid(c)
    o_ref[...] = _bdot(s, w_ref[...]) + b_ref[...]


def _ada_call(cond, w_ada, b_ada):
    tn = 2304
    n_out = N_MOD * D_MODEL
    return pl.pallas_call(
        _ada_kernel,
        grid=(DEPTH, n_out // tn),
        in_specs=[
            pl.BlockSpec((N_COND, D_MODEL), lambda l, j: (0, 0)),
            pl.BlockSpec((None, D_MODEL, tn), lambda l, j: (l, 0, j)),
            pl.BlockSpec((None, 1, tn), lambda l, j: (l, 0, j)),
        ],
        out_specs=pl.BlockSpec((None, N_COND, tn), lambda l, j: (l, 0, j)),
        out_shape=jax.ShapeDtypeStruct((DEPTH, N_COND, n_out), F32),
        compiler_params=_cparams(2),
        name="adaln",
    )(cond, w_ada, b_ada.reshape(DEPTH, 1, n_out))


FF_CHUNKS = (768, 768, 768, 512)


def _ffn_kernel(*refs, base, n_x, final, n_pt):
    x_refs = refs[:n_x]
    mod_ref, g_ref, gf_ref, win_ref, wout_ref = refs[n_x:n_x + 5]
    o_refs = refs[n_x + 5:]
    i = pl.program_id(0)
    if n_x == 2:
        x = jnp.where(i < n_pt, x_refs[0][...], x_refs[1][...])
    else:
        x = x_refs[0][...]
    mod = mod_ref[...]
    u = _modulated_norm(x, g_ref[...], mod, base).astype(BF16)
    y = None
    lo = 0
    for width in FF_CHUNKS:
        a = jnp.dot(u, win_ref[:, lo:lo + width], preferred_element_type=F32)
        g = jnp.dot(u, win_ref[:, D_FF + lo:D_FF + lo + width], preferred_element_type=F32)
        hh = (a * _sigmoid(a) * g).astype(BF16)
        part = jnp.dot(hh, wout_ref[lo:lo + width, :], preferred_element_type=F32)
        y = part if y is None else y + part
        lo += width
    xn = x + 0.5 * mod[base + 2:base + 3] * y
    if not final:
        o_refs[0][...] = xn
    else:
        xn = _rms(xn, gf_ref[...])

        @pl.when(i < n_pt)
        def _():
            o_refs[0][...] = xn

        @pl.when(i >= n_pt)
        def _():
            o_refs[1][...] = xn


def _ffn_call(xs, mods, g_norm, g_final, w_in, w_out, st, *, layer, sub, final):
    tm = TM_WIDE
    n_pt = st.ctx_tiles(tm)
    n_tiles = st.n_all // tm
    ctx_spec = pl.BlockSpec((tm, D_MODEL), lambda i: (jnp.minimum(i, n_pt - 1), 0))
    lat_spec = pl.BlockSpec((tm, D_MODEL), lambda i: (jnp.maximum(i - n_pt, 0), 0))
    all_spec = pl.BlockSpec((tm, D_MODEL), lambda i: (i, 0))
    x_specs = [ctx_spec, lat_spec] if len(xs) == 2 else [all_spec]
    if final:
        out_specs = [ctx_spec, lat_spec]
        out_shape = [jax.ShapeDtypeStruct((st.n_ctx, D_MODEL), F32),
                     jax.ShapeDtypeStruct((st.n_all - st.n_ctx, D_MODEL), F32)]
    else:
        out_specs = [all_spec]
        out_shape = [jax.ShapeDtypeStruct((st.n_all, D_MODEL), F32)]
    return pl.pallas_call(
        functools.partial(_ffn_kernel, base=3 * sub, n_x=len(xs), final=final, n_pt=n_pt),
        grid=(n_tiles,),
        in_specs=x_specs + [
            st.mod_spec(tm),
            pl.BlockSpec((None, None, 1, D_MODEL), lambda i: (layer, sub, 0, 0)),
            pl.BlockSpec((1, D_MODEL), lambda i: (0, 0)),
            _layer_spec((D_MODEL, 2 * D_FF), layer),
            _layer_spec((D_FF, D_MODEL), layer),
        ],
        out_specs=out_specs,
        out_shape=out_shape,
        compiler_params=_cparams(1),
        name="ffn",
    )(*xs, mods, g_norm, g_final, w_in, w_out)


def _swap32(x):
    lane = lax.broadcasted_iota(jnp.int32, x.shape, 1)
    return jnp.where((lane & (DH_ATTN // 2)) == 0,
                     pltpu.roll(x, LANES - DH_ATTN // 2, 1), pltpu.roll(x, DH_ATTN // 2, 1))


def _proj_kernel(*refs, n_alias, n_pt, seqs_per_tile, seq):
    (x_ref, mod_ref, g_ref, w_ref, wgt_ref, bgt_ref, cos_ref, sin_ref) = refs[:8]
    (q_ref, k_ref, ck_ref, cv_ref, zf_ref, mq_ref, mk_ref, mo_ref, vat_ref, vt_ref, gt_ref,
     wt_sc) = refs[8 + n_alias:]
    i = pl.program_id(0)
    is_latent = i >= n_pt

    @pl.when(i == 0)
    def _():
        wt_sc[0:W_ATTN, :] = w_ref[:, 2 * W_ATTN:3 * W_ATTN].T
        wt_sc[W_ATTN:, :] = w_ref[:, MV_LO:MV_HI].T

    x = x_ref[...]
    u = _modulated_norm(x, g_ref[...], mod_ref[...], 3).astype(BF16)
    za = jnp.dot(u, w_ref[:, :3 * W_ATTN], preferred_element_type=F32)
    zb = jnp.dot(u, w_ref[:, 3 * W_ATTN:MV_LO], preferred_element_type=F32)
    zo = jnp.dot(u, w_ref[:, MV_HI:P_MAIN], preferred_element_type=F32)
    zt = _bdot_nt(wt_sc[...], u)
    zg = _bdot_nt(wgt_ref[...], u)
    cos_t = cos_ref[...]
    sin_t = sin_ref[...]
    plain_k = []
    for h in range(N_ATTN):
        zq = za[:, h * DV_ATTN:(h + 1) * DV_ATTN]
        zk = za[:, W_ATTN + h * DV_ATTN:W_ATTN + (h + 1) * DV_ATTN]
        plain_k.append(zk)
        q_ref[h] = jnp.where(is_latent, zq * cos_t + _swap32(zq) * sin_t, zq).astype(q_ref.dtype)
        k_ref[h] = jnp.where(is_latent, zk * cos_t + _swap32(zk) * sin_t, zk).astype(k_ref.dtype)

    @pl.when(i < n_pt)
    def _():
        for h in range(N_ATTN):
            zv = za[:, 2 * W_ATTN + h * DV_ATTN:2 * W_ATTN + (h + 1) * DV_ATTN]
            for s in range(seqs_per_tile):
                rs = slice(s * seq, (s + 1) * seq)
                ck_ref[s, h] = plain_k[h][rs]
                cv_ref[s, h] = zv[rs]

    zf_ref[...] = zb[:, :W_FOUR].astype(zf_ref.dtype)
    off = W_FOUR
    mq_ref[...] = zb[:, off:off + W_MLSTM].astype(mq_ref.dtype)
    mk_ref[...] = (zb[:, off + W_MLSTM:off + 2 * W_MLSTM] * MLSTM_K_SCALE).astype(mk_ref.dtype)
    mo_ref[...] = zo
    vat_ref[...] = zt[:W_ATTN].astype(vat_ref.dtype)
    vt_ref[...] = zt[W_ATTN:]
    gpt = zg + bgt_ref[...]
    sub = lax.broadcasted_iota(jnp.int32, gpt.shape, 0)
    gt_ref[...] = jnp.where((sub < N_GATE) & ((sub & N_MLSTM) != 0), _log_sigmoid(gpt), gpt)


def _proj_call(x, mods, g_norm, w_all, w_gate_t, bgt, cos_t, sin_t, kv_prev, st, *, layer):
    tm = TM_PROJ
    n_pt = st.ctx_tiles(tm)
    if tm % st.seq:
        raise ValueError("a context row tile must hold whole sequences")
    seqs_per_tile, tps = tm // st.seq, st.dec_seq // tm
    n = st.n_all
    head_shape = jax.ShapeDtypeStruct((N_ATTN, n, DV_ATTN), BF16)
    head_spec = pl.BlockSpec((N_ATTN, tm, DV_ATTN), lambda i: (0, i, 0))
    kv_shape = jax.ShapeDtypeStruct((st.batch, DEPTH, N_ATTN, st.seq, DV_ATTN), F32)
    kv_spec = pl.BlockSpec((seqs_per_tile, None, N_ATTN, st.seq, DV_ATTN),
                           lambda i: (jnp.minimum(i, n_pt - 1), layer, 0, 0, 0))
    tok_shape = jax.ShapeDtypeStruct((n, W_ATTN), F32)
    tok_bf16 = jax.ShapeDtypeStruct((n, W_ATTN), BF16)
    tok_spec = pl.BlockSpec((tm, W_ATTN), lambda i: (i, 0))
    rope_spec = pl.BlockSpec((tm, LANES), lambda i: (jnp.maximum(i - n_pt, 0) % tps, 0))
    alias_in = list(kv_prev) if kv_prev is not None else []
    n_in = 8
    return pl.pallas_call(
        functools.partial(_proj_kernel, n_alias=len(alias_in), n_pt=n_pt, seqs_per_tile=seqs_per_tile,
                          seq=st.seq),
        grid=(n // tm,),
        in_specs=[
            pl.BlockSpec((tm, D_MODEL), lambda i: (i, 0)),
            st.mod_spec(tm),
            pl.BlockSpec((None, None, 1, D_MODEL), lambda i: (layer, 1, 0, 0)),
            _layer_spec((D_MODEL, w_all.shape[-1]), layer),
            _layer_spec((LANES, D_MODEL), layer),
            _layer_spec((LANES, 1), layer),
            rope_spec, rope_spec,
        ] + [pl.BlockSpec(memory_space=pl.ANY)] * len(alias_in),
        out_specs=[head_spec, head_spec, kv_spec, kv_spec, tok_spec, tok_spec, tok_spec, tok_spec,
                   pl.BlockSpec((W_ATTN, tm), lambda i: (0, i)),
                   pl.BlockSpec((W_MLSTM, tm), lambda i: (0, i)),
                   pl.BlockSpec((LANES, tm), lambda i: (0, i))],
        out_shape=[head_shape, head_shape, kv_shape, kv_shape, tok_bf16, tok_bf16, tok_bf16, tok_shape,
                   jax.ShapeDtypeStruct((W_ATTN, n), BF16),
                   jax.ShapeDtypeStruct((W_MLSTM, n), F32),
                   jax.ShapeDtypeStruct((LANES, n), F32)],
        input_output_aliases={n_in + j: 2 + j for j in range(len(alias_in))},
        scratch_shapes=[pltpu.VMEM((W_ATTN + W_MLSTM, D_MODEL), BF16)],
        compiler_params=_cparams(1),
        name="mixer_proj",
    )(x, mods, g_norm, w_all, w_gate_t, bgt, cos_t, sin_t, *alias_in)


def _attn_kernel(*refs, lam_init, cached, nb, tq, seq):
    q_ref, k_ref, vt_ref = refs[:3]
    pos = 3
    if cached:
        ck_ref, cv_ref = refs[pos:pos + 2]
        pos += 2
    lam_ref, g_ref = refs[pos:pos + 2]
    o_ref = refs[-1]
    lp = lam_ref[...]
    lam = (jnp.exp(jnp.sum(lp[0:1] * lp[1:2], axis=-1, keepdims=True))
           - jnp.exp(jnp.sum(lp[2:3] * lp[3:4], axis=-1, keepdims=True)) + lam_init)
    g_sub = g_ref[...]
    lane = lax.broadcasted_iota(jnp.int32, (tq, DV_ATTN), 1)
    heads = range(N_ATTN)
    results = []
    for bi in range(nb):
        ks = slice(bi * seq, (bi + 1) * seq)
        keys = [[k_ref[h, ks] for h in heads]]
        vals_t = [[vt_ref[h * DV_ATTN:(h + 1) * DV_ATTN, ks] for h in heads]]
        if cached:
            keys.insert(0, [ck_ref[h].astype(BF16) for h in heads])
            vals_t.insert(0, [cv_ref[h].T.astype(BF16) for h in heads])
        qh = [q_ref[h, bi * tq:(bi + 1) * tq].astype(F32) * (ATTN_SCALE * LOG2E) for h in heads]
        exps, dens = [], []
        for first_map in (True, False):
            qm = [jnp.where((lane < DH_ATTN) == first_map, q, 0.0).astype(BF16) for q in qh]
            s = [jnp.concatenate([_bdot_nt(kg[h], qm[h]) for h in heads], axis=1) for kg in keys]
            m = functools.reduce(jnp.maximum, [jnp.max(si, axis=0, keepdims=True) for si in s])
            e = [jnp.exp2(si - m) for si in s]
            exps.append(e)
            dens.append(functools.reduce(jnp.add, [jnp.sum(ei, axis=0, keepdims=True) for ei in e]))
        ratio = lam * dens[0] / dens[1]
        inv = 1.0 / dens[0]
        probs = [(e1 - e2 * ratio).astype(BF16) for e1, e2 in zip(*exps)]
        for h in heads:
            cols = slice(h * tq, (h + 1) * tq)
            o_t = None
            for pj, vg in zip(probs, vals_t):
                part = jnp.dot(vg[h], pj[:, cols], preferred_element_type=F32)
                o_t = part if o_t is None else o_t + part
            o_t = o_t * inv[:, cols]
            y_t = o_t * lax.rsqrt(jnp.mean(o_t * o_t, axis=0, keepdims=True) + EPS)
            results.append((bi, h, (y_t.T * g_sub * (1.0 - lam_init)).astype(o_ref.dtype)))
    for bi, h, y in results:
        o_ref[bi * tq:(bi + 1) * tq, h * DV_ATTN:(h + 1) * DV_ATTN] = y


def _attn_call(q, k, v_t, cache, lam_p, g_sub, out_prev, st, *, layer, lam_init, latent):
    n = st.n_all
    if latent:
        batch, seq, nb, tq = st.dec_batch, st.dec_seq, 1, st.dec_seq // 2
        row0 = st.n_ctx
    else:
        batch, seq, nb, tq = st.batch, st.seq, 4, st.seq
        row0 = 0
    nq = seq // tq
    if nq > 1 and nb > 1:
        raise ValueError("query blocks of several sequences are not contiguous rows")
    q0, s0 = row0 // (nb * tq), row0 // (nb * seq)
    in_specs = [pl.BlockSpec((N_ATTN, nb * tq, DV_ATTN), lambda b, i: (0, q0 + b * nq + i, 0)),
                pl.BlockSpec((N_ATTN, nb * seq, DV_ATTN), lambda b, i: (0, s0 + b, 0)),
                pl.BlockSpec((W_ATTN, nb * seq), lambda b, i: (0, s0 + b))]
    args = [q, k, v_t]
    if cache is not None:
        past = cache[0].shape[3]
        c_spec = pl.BlockSpec((None, None, N_ATTN, past, DV_ATTN), lambda b, i: (b, layer, 0, 0, 0))
        in_specs += [c_spec, c_spec]
        args += list(cache)
    in_specs += [pl.BlockSpec((None, 4, DH_ATTN), lambda b, i: (layer, 0, 0)),
                 pl.BlockSpec((None, 1, DV_ATTN), lambda b, i: (layer, 0, 0))]
    args += [lam_p, g_sub]
    aliases = {}
    if out_prev is not None:
        aliases = {len(args): 0}
        in_specs.append(pl.BlockSpec(memory_space=pl.ANY))
        args.append(out_prev)
    return pl.pallas_call(
        functools.partial(_attn_kernel, lam_init=lam_init, cached=cache is not None, nb=nb, tq=tq, seq=seq),
        grid=(batch // nb, nq),
        in_specs=in_specs,
        out_specs=pl.BlockSpec((nb * tq, W_ATTN), lambda b, i: (q0 + b * nq + i, 0)),
        out_shape=jax.ShapeDtypeStruct((n, W_ATTN), BF16),
        input_output_aliases=aliases,
        compiler_params=_cparams(2),
        name="diff_attn",
    )(*args)


def _dft_tables(seq):
    def cs(n):
        j = np.arange(n)
        ang = 2.0 * np.pi * ((j[:, None] * j[None, :]) % n) / n
        return np.cos(ang) / math.sqrt(n), np.sin(ang) / math.sqrt(n)

    cd, sd = cs(DG_FOUR)
    ct, st = cs(seq)
    w_d = jnp.asarray(np.concatenate([cd, sd], axis=1), F32)
    return w_d.astype(BF16), jnp.asarray(ct, F32).astype(BF16), jnp.asarray(-st, F32).astype(BF16)


def _four_kernel(*refs, nb, seq):
    z_ref, wd_ref, ct_ref, st_ref = refs[:4]
    o_ref = refs[-1]
    y_cos, y_sin = [], []
    for gidx in range(N_FOUR):
        y = jnp.dot(z_ref[:, gidx * DG_FOUR:(gidx + 1) * DG_FOUR], wd_ref[...], preferred_element_type=F32)
        y_cos.append(y[:, :DG_FOUR].astype(BF16))
        y_sin.append(y[:, DG_FOUR:].astype(BF16))
    y_cos = jnp.concatenate(y_cos, axis=1)
    y_sin = jnp.concatenate(y_sin, axis=1)
    for s in range(nb):
        rows = slice(s * seq, (s + 1) * seq)
        o_ref[rows, :] = (jnp.dot(ct_ref[...], y_cos[rows], preferred_element_type=F32)
                          + jnp.dot(st_ref[...], y_sin[rows], preferred_element_type=F32)).astype(o_ref.dtype)


def _four_call(zf, out_prev, st, *, latent):
    if latent:
        batch, seq, nb, row0 = st.dec_batch, st.dec_seq, 1, st.n_ctx
    else:
        batch, seq, nb, row0 = st.batch, st.seq, 4, 0
    b0 = row0 // (nb * seq)
    w_d, ct, s_t = _dft_tables(seq)
    row_spec = pl.BlockSpec((nb * seq, W_FOUR), lambda b: (b0 + b, 0))
    in_specs = [row_spec,
                pl.BlockSpec((DG_FOUR, 2 * DG_FOUR), lambda b: (0, 0)),
                pl.BlockSpec((seq, seq), lambda b: (0, 0)),
                pl.BlockSpec((seq, seq), lambda b: (0, 0))]
    args = [zf, w_d, ct, s_t]
    aliases = {}
    if out_prev is not None:
        aliases = {len(args): 0}
        in_specs.append(pl.BlockSpec(memory_space=pl.ANY))
        args.append(out_prev)
    return pl.pallas_call(
        functools.partial(_four_kernel, nb=nb, seq=seq),
        grid=(batch // nb,),
        in_specs=in_specs,
        out_specs=row_spec,
        out_shape=jax.ShapeDtypeStruct((st.n_all, W_FOUR), BF16),
        input_output_aliases=aliases,
        compiler_params=_cparams(1),
        name="fourier_mix",
    )(*args)


def _mlstm_kernel(*refs, seq, nb, seeded, emit_state, n_alias):
    mq_ref, mk_ref, vt_ref, mo_ref, gt_ref, g_ref = refs[:6]
    pos = 6
    if seeded:
        c0_ref, n0_ref, m0_ref = refs[pos:pos + 3]
        pos += 3
    pos += n_alias
    o_ref = refs[pos]
    if emit_state:
        c1_ref, n1_ref, m1_ref = refs[pos + 1:pos + 4]

    L = min(MLSTM_CHUNK, seq)
    nc = seq // L
    s_idx = lax.broadcasted_iota(jnp.int32, (L, L), 0)
    t_idx = lax.broadcasted_iota(jnp.int32, (L, L), 1)
    before = (s_idx <= t_idx, s_idx >= t_idx)
    tri = (jnp.where(before[1], 1.0, 0.0).astype(BF16),
           jnp.where(before[0], 1.0, 0.0).astype(BF16))
    chains = [(bi, d, h) for bi in range(nb) for d in range(2) for h in range(N_MLSTM)]

    state = {}
    for (bi, d, h) in chains:
        if seeded:
            state[bi, d, h] = (c0_ref[bi, d, h], n0_ref[bi, d, h:h + 1, :], m0_ref[bi, d:d + 1, h:h + 1])
        else:
            state[bi, d, h] = (None, None, jnp.zeros((1, 1), F32))

    h_t = {}
    for c in range(nc):
        terms = {}
        for bi in range(nb):
            for d in range(2):
                r0 = bi * seq + (c if d == 0 else nc - 1 - c) * L
                g_t = gt_ref[0:N_GATE, r0:r0 + L]
                hi = g_t.astype(BF16)
                rem = g_t - hi.astype(F32)
                mid = rem.astype(BF16)
                lo = (rem - mid.astype(F32)).astype(BF16)
                pieces = _bdot_nt(jnp.concatenate([hi, mid, lo], axis=0), tri[d])
                cum_t = pieces[0:N_GATE] + pieces[N_GATE:2 * N_GATE] + pieces[2 * N_GATE:]
                c_t = g_t - pltpu.roll(cum_t, N_GATE - N_MLSTM, 0)
                col = jnp.concatenate([c_t, jnp.zeros((LANES - N_GATE, L), F32)], axis=0).T
                terms[bi, d] = (r0, col, g_t, cum_t)

        new_state = {}
        for (bi, d, h) in chains:
            r0, col, g_t, cum_t = terms[bi, d]
            c_prev, n_prev, m_prev = state[bi, d, h]
            has_state = c_prev is not None
            need_update = emit_state or c < nc - 1
            ci = 2 * d * N_MLSTM + h
            cf = ci + N_MLSTM
            last = L - 1 if d == 0 else 0
            hs = slice(h * DH_MLSTM, (h + 1) * DH_MLSTM)
            q = mq_ref[r0:r0 + L, hs]
            k = mk_ref[r0:r0 + L, hs]
            v_t = vt_ref[hs, r0:r0 + L]
            i_row, b_row = g_t[ci:ci + 1, :], cum_t[cf:cf + 1, :]

            cb = jnp.where(before[d], col[:, ci:ci + 1], -jnp.inf)
            m_row = jnp.maximum(jnp.max(cb, axis=0, keepdims=True), m_prev)
            s_t = _bdot_nt(k, q) * jnp.exp(cb - m_row)
            num_t = jnp.dot(v_t.astype(BF16), s_t.astype(BF16), preferred_element_type=F32)
            den = jnp.sum(s_t, axis=0, keepdims=True)
            if has_state:
                sp = jnp.exp(m_prev - m_row)
                cn = jnp.concatenate([c_prev, jnp.broadcast_to(n_prev, (SUBLANES, DH_MLSTM))], axis=0)
                cq = _bdot_nt(cn, q)
                num_t = num_t + sp * cq[:DH_MLSTM]
                den = den + sp * cq[DH_MLSTM:DH_MLSTM + 1]
            m_t = b_row + m_row
            h_t[bi, d, h, c] = num_t / jnp.maximum(jnp.abs(den), jnp.exp(-m_t))
            if need_update:
                b_tot = b_row[:, last:last + 1]
                m_new = m_t[:, last:last + 1]
                wl = jnp.exp(b_tot + (i_row - b_row) - m_new)
                vw = jnp.concatenate([v_t * wl, jnp.broadcast_to(wl, (SUBLANES, L))], axis=0)
                upd = jnp.dot(vw.astype(BF16), k, preferred_element_type=F32)
                c_new, n_new = upd[:DH_MLSTM], upd[DH_MLSTM:DH_MLSTM + 1]
                if has_state:
                    decay = jnp.exp(b_tot + m_prev - m_new)
                    c_new = decay * c_prev + c_new
                    n_new = decay * n_prev + n_new
                new_state[bi, d, h] = (c_new, n_new, m_new)
        state = new_state

    g_m = g_ref[...]
    for bi in range(nb):
        for h in range(N_MLSTM):
            hs = slice(h * DH_MLSTM, (h + 1) * DH_MLSTM)
            fwd = [h_t[bi, 0, h, c] for c in range(nc)]
            bwd = [h_t[bi, 1, h, nc - 1 - c] for c in range(nc)]
            hsum = (fwd[0] if nc == 1 else jnp.concatenate(fwd, axis=1)) \
                + (bwd[0] if nc == 1 else jnp.concatenate(bwd, axis=1))
            y = hsum * lax.rsqrt(jnp.mean(hsum * hsum, axis=0, keepdims=True) + EPS)
            rows = slice(bi * seq, (bi + 1) * seq)
            o_ref[rows, hs] = (y.T * g_m * _sigmoid(mo_ref[rows, hs])).astype(o_ref.dtype)

    if emit_state:
        for (bi, d, h) in chains:
            c_fin, n_fin, m_fin = state[bi, d, h]
            c1_ref[bi, d, h] = c_fin
            n1_ref[bi, d, h:h + 1, :] = n_fin
            m1_ref[bi, d:d + 1, h:h + 1] = m_fin


def _mlstm_call(mq, mk, vt, mo, gates_t, g_m, state, alias_prev, st, *, layer, latent):
    if latent:
        batch, seq, nb, row0 = st.dec_batch, st.dec_seq, 1, st.n_ctx
    else:
        batch, seq, nb, row0 = st.batch, st.seq, 4, 0
    emit_state = not latent
    rows = nb * seq
    b0 = row0 // rows
    tok_spec = pl.BlockSpec((rows, W_MLSTM), lambda b: (b0 + b, 0))
    in_specs = [tok_spec, tok_spec,
                pl.BlockSpec((W_MLSTM, rows), lambda b: (0, b0 + b)),
                tok_spec,
                pl.BlockSpec((LANES, rows), lambda b: (0, b0 + b)),
                pl.BlockSpec((None, 1, DH_MLSTM), lambda b: (layer, 0, 0))]
    args = [mq, mk, vt, mo, gates_t, g_m]
    state_specs = [
        pl.BlockSpec((nb, None, 2, N_MLSTM, DH_MLSTM, DH_MLSTM), lambda b: (b, layer, 0, 0, 0, 0)),
        pl.BlockSpec((nb, None, 2, N_MLSTM, DH_MLSTM), lambda b: (b, layer, 0, 0, 0)),
        pl.BlockSpec((nb, None, 2, N_MLSTM), lambda b: (b, layer, 0, 0)),
    ]
    if state is not None:
        in_specs += state_specs
        args += list(state)
    alias_in = list(alias_prev) if alias_prev is not None else []
    n_in = len(args)
    in_specs += [pl.BlockSpec(memory_space=pl.ANY)] * len(alias_in)
    args += alias_in
    out_specs = [tok_spec]
    out_shape = [jax.ShapeDtypeStruct((st.n_all, W_MLSTM), BF16)]
    if emit_state:
        out_specs += state_specs
        out_shape += [
            jax.ShapeDtypeStruct((batch, DEPTH, 2, N_MLSTM, DH_MLSTM, DH_MLSTM), F32),
            jax.ShapeDtypeStruct((batch, DEPTH, 2, N_MLSTM, DH_MLSTM), F32),
            jax.ShapeDtypeStruct((batch, DEPTH, 2, N_MLSTM), F32),
        ]
    first_out = 1 if emit_state else 0
    return pl.pallas_call(
        functools.partial(_mlstm_kernel, seq=seq, nb=nb, seeded=state is not None, emit_state=emit_state,
                          n_alias=len(alias_in)),
        grid=(batch // nb,),
        in_specs=in_specs,
        out_specs=out_specs,
        out_shape=out_shape,
        input_output_aliases={n_in + j: first_out + j for j in range(len(alias_in))},
        compiler_params=_cparams(1),
        name="mlstm",
    )(*args)


def _merge_kernel(x_ref, a_ref, f_ref, m_ref, mod_ref, g_ref, wg_ref, wa_ref, wf_ref, wm_ref, wo_ref, o_ref):
    x = x_ref[...]
    mod = mod_ref[...]
    u = _modulated_norm(x, g_ref[...], mod, 3).astype(BF16)
    merged = None
    for j, (br_ref, w_ref) in enumerate(((a_ref, wa_ref), (f_ref, wf_ref), (m_ref, wm_ref))):
        gate = _sigmoid(jnp.dot(u, wg_ref[:, j * D_MODEL:(j + 1) * D_MODEL], preferred_element_type=F32))
        term = gate * _bdot(br_ref[...], w_ref[...])
        merged = term if merged is None else merged + term
    out = _bdot(merged, wo_ref[...])
    o_ref[...] = x + mod[5:6] * out


def _merge_call(x, a, f, m, mods, g_norm, wg, wa, wf, wm, wo, st, *, layer):
    tm = TM_WIDE
    n = st.n_all
    br_spec = pl.BlockSpec((tm, W_ATTN), lambda i: (i, 0))
    return pl.pallas_call(
        _merge_kernel,
        grid=(n // tm,),
        in_specs=[
            pl.BlockSpec((tm, D_MODEL), lambda i: (i, 0)),
            br_spec, br_spec, br_spec,
            st.mod_spec(tm),
            pl.BlockSpec((None, None, 1, D_MODEL), lambda i: (layer, 1, 0, 0)),
            _layer_spec((D_MODEL, 3 * D_MODEL), layer),
            _layer_spec((W_ATTN, D_MODEL), layer), _layer_spec((W_FOUR, D_MODEL), layer),
            _layer_spec((W_MLSTM, D_MODEL), layer),
            _layer_spec((D_MODEL, D_MODEL), layer),
        ],
        out_specs=pl.BlockSpec((tm, D_MODEL), lambda i: (i, 0)),
        out_shape=jax.ShapeDtypeStruct((n, D_MODEL), F32),
        compiler_params=_cparams(1),
        name="merge",
    )(x, a, f, m, mods, g_norm, wg, wa, wf, wm, wo)


def _rope_tables(n_tok):
    tok = np.arange(n_tok)
    inv = ROPE_BASE ** (-np.arange(ROPE_AXIS_PAIRS, dtype=np.float32) / ROPE_AXIS_PAIRS)
    ang = np.concatenate([(tok // GRID_W).astype(np.float32)[:, None] * inv,
                          (tok % GRID_W).astype(np.float32)[:, None] * inv], axis=-1).astype(np.float32)
    c, s = np.cos(ang), np.sin(ang)
    cos_t = np.concatenate([c, c, c, c], axis=-1)
    sin_t = np.concatenate([-s, s, -s, s], axis=-1)
    return jnp.asarray(cos_t, F32), jnp.asarray(sin_t, F32)


def kernel(x_prompt, x_sample, cache_k, cache_v, state_C, state_n, state_m, c, c_ctx, w_ada, b_ada, g_norm,
           w_ffn1_in, w_ffn1_out, w_ffn2_in, w_ffn2_out, w_in, b_mgate, attn_lambda, g_attn_sub, g_mlstm,
           w_branch_gate, w_br_attn, w_br_four, w_br_mlstm, w_out, g_final):
    batch, seq, _ = x_prompt.shape
    dec_batch, dec_seq, _ = x_sample.shape
    st = _Stream(batch, seq, dec_batch, dec_seq)
    cond = jnp.zeros((N_COND, D_MODEL), F32).at[0].set(c_ctx).at[1:1 + dec_batch].set(c)
    mods_all = _ada_call(cond, w_ada, b_ada).reshape(DEPTH, N_COND, N_MOD, D_MODEL)

    w_gate_t = jnp.pad(jnp.swapaxes(w_in[:, :, P_MAIN:], 1, 2), ((0, 0), (0, LANES - N_GATE), (0, 0))).astype(BF16)
    b_gate_t = jnp.pad(b_mgate, ((0, 0), (0, LANES - N_GATE)))[:, :, None]
    g_norm4 = g_norm[:, :, None, :]
    g_fin = g_final[None, :]
    ffn_w = ((w_ffn1_in.astype(BF16), w_ffn1_out.astype(BF16)), (w_ffn2_in.astype(BF16), w_ffn2_out.astype(BF16)))
    w_all = w_in[:, :, :P_MAIN].astype(BF16)
    w_bg, w_ba, w_bf, w_bm, w_o = (t.astype(BF16) for t in
                                   (w_branch_gate, w_br_attn, w_br_four, w_br_mlstm, w_out))
    g_sub3 = g_attn_sub[:, None, :]
    g_m3 = g_mlstm[:, None, :]
    cos_t, sin_t = _rope_tables(dec_seq)
    lat_state = (state_C, state_n, state_m)

    xs = (x_prompt.reshape(batch * seq, D_MODEL), x_sample.reshape(dec_batch * dec_seq, D_MODEL))
    kv, states = None, None
    for l in range(DEPTH):
        mods = mods_all[l]
        lam_init = 0.8 - 0.6 * math.exp(-0.3 * l)
        (x,) = _ffn_call(xs, mods, g_norm4, g_fin, *ffn_w[0], st, layer=l, sub=0, final=False)
        q, k, ck, cv, zf, mq, mk, mo, vat, vt, gates_t = _proj_call(
            x, mods, g_norm4, w_all, w_gate_t, b_gate_t, cos_t, sin_t, kv, st, layer=l)
        kv = (ck, cv)
        a = _attn_call(q, k, vat, None, attn_lambda, g_sub3, None, st, layer=l, lam_init=lam_init,
                       latent=False)
        a = _attn_call(q, k, vat, (cache_k, cache_v), attn_lambda, g_sub3, a, st, layer=l,
                       lam_init=lam_init, latent=True)
        f = _four_call(zf, None, st, latent=False)
        f = _four_call(zf, f, st, latent=True)
        m, *states = _mlstm_call(mq, mk, vt, mo, gates_t, g_m3, None, states, st, layer=l, latent=False)
        (m,) = _mlstm_call(mq, mk, vt, mo, gates_t, g_m3, lat_state, (m,), st, layer=l, latent=True)
        x = _merge_call(x, a, f, m, mods, g_norm4, w_bg, w_ba, w_bf, w_bm, w_o, st, layer=l)
        xs = tuple(_ffn_call((x,), mods, g_norm4, g_fin, *ffn_w[1], st, layer=l, sub=2,
                             final=(l == DEPTH - 1)))
    y_prompt, y_sample = xs
    return (y_prompt.reshape(batch, seq, D_MODEL), y_sample.reshape(dec_batch, dec_seq, D_MODEL),
            *kv, *states)
```

```python
import functools
import math

import numpy as np
import jax
import jax.numpy as jnp
from jax import lax
from jax.experimental import pallas as pl
from jax.experimental.pallas import tpu as pltpu

D_MODEL = 1024
DEPTH = 2
GRID_W = 64
N_ATTN = 4
DH_ATTN = 64
DV_ATTN = 2 * DH_ATTN
W_ATTN = N_ATTN * DV_ATTN
N_FOUR = 4
DG_FOUR = 128
W_FOUR = N_FOUR * DG_FOUR
N_MLSTM = 4
DH_MLSTM = 128
W_MLSTM = N_MLSTM * DH_MLSTM
N_GATE = 4 * N_MLSTM
LANES = 128
SUBLANES = 16
MV_LO = 3 * W_ATTN + W_FOUR + 2 * W_MLSTM
MV_HI = MV_LO + W_MLSTM
P_MAIN = MV_HI + W_MLSTM
P_ROW = P_MAIN - W_MLSTM
P_COL = W_ATTN + W_MLSTM + LANES
D_FF = 2816
N_MOD = 9
N_COND = 8
ROPE_BASE = 10000.0
ROPE_AXIS_PAIRS = DH_ATTN // 4
ATTN_SCALE = DH_ATTN ** -0.5
LOG2E = 1.4426950408889634
LN2 = 0.6931471805599453
MLSTM_K_SCALE = DH_MLSTM ** -0.5
EPS = 1e-6
MLSTM_CHUNK = 256
VMEM_LIMIT = 56 * 1024 * 1024
TM_WIDE = 1024
TM_PROJ = 512

F32 = jnp.float32
BF16 = jnp.bfloat16


def _cparams(n_grid):
    return pltpu.CompilerParams(dimension_semantics=("arbitrary",) * n_grid,
                                vmem_limit_bytes=VMEM_LIMIT)


def _bdot(a, b):
    return jnp.dot(a.astype(BF16), b.astype(BF16), preferred_element_type=F32)


def _bdot_nt(a, b):
    return lax.dot_general(a.astype(BF16), b.astype(BF16), (((1,), (1,)), ((), ())),
                           preferred_element_type=F32)


def _sigmoid(x):
    return 1.0 / (1.0 + jnp.exp(-x))


def _log_sigmoid(x):
    return jnp.minimum(x, 0.0) - jnp.log1p(jnp.exp(-jnp.abs(x)))


def _rms(x, g):
    return x * lax.rsqrt(jnp.mean(x * x, axis=-1, keepdims=True) + EPS) * g


def _modulated_norm(x, g, mod, base):
    return _rms(x, g) * (1.0 + mod[base + 1:base + 2]) + mod[base:base + 1]


def _layer_spec(shape, layer):
    return pl.BlockSpec((None,) + shape, lambda *_: (layer,) + (0,) * len(shape),
                        pipeline_mode=pl.Buffered(1))


class _Stream:
    def __init__(self, batch, seq, dec_batch, dec_seq):
        self.batch, self.seq, self.dec_batch, self.dec_seq = batch, seq, dec_batch, dec_seq
        self.n_ctx = batch * seq
        self.n_all = self.n_ctx + dec_batch * dec_seq

    def ctx_tiles(self, tm):
        if self.n_ctx % tm or self.dec_seq % tm:
            raise ValueError("row tile must divide the context rows and one latent sequence")
        return self.n_ctx // tm

    def mod_spec(self, tm):
        n_pt, tps = self.ctx_tiles(tm), self.dec_seq // tm
        return pl.BlockSpec((None, N_MOD, D_MODEL),
                            lambda i: (jnp.where(i < n_pt, 0, 1 + (i - n_pt) // tps), 0, 0))


def _ada_kernel(c_ref, w_ref, b_ref, o_ref):
    c = c_ref[...]
    s = c * _sigmoid(c)
    o_ref[...] = _bdot(s, w_ref[...]) + b_ref[...]


def _ada_call(cond, w_ada, b_ada):
    tn = 2304
    n_out = N_MOD * D_MODEL
    return pl.pallas_call(
        _ada_kernel,
        grid=(DEPTH, n_out // tn),
        in_specs=[
            pl.BlockSpec((N_COND, D_MODEL), lambda l, j: (0, 0)),
            pl.BlockSpec((None, D_MODEL, tn), lambda l, j: (l, 0, j)),
            pl.BlockSpec((None, 1, tn), lambda l, j: (l, 0, j)),
        ],
        out_specs=pl.BlockSpec((None, N_COND, tn), lambda l, j: (l, 0, j)),
        out_shape=jax.ShapeDtypeStruct((DEPTH, N_COND, n_out), F32),
        compiler_params=_cparams(2),
        name="adaln",
    )(cond, w_ada, b_ada.reshape(DEPTH, 1, n_out))


FF_CHUNKS = (768, 768, 768, 512)


def _ffn_kernel(*refs, base, n_x, final, n_pt):
    x_refs = refs[:n_x]
    mod_ref, g_ref, gf_ref, win_ref, wout_ref = refs[n_x:n_x + 5]
    o_refs = refs[n_x + 5:]
    i = pl.program_id(0)
    if n_x == 2:
        x = jnp.where(i < n_pt, x_refs[0][...], x_refs[1][...])
    else:
        x = x_refs[0][...]
    mod = mod_ref[...]
    u = _modulated_norm(x, g_ref[...], mod, base).astype(BF16)
    y = None
    lo = 0
    for width in FF_CHUNKS:
        a = jnp.dot(u, win_ref[:, lo:lo + width], preferred_element_type=F32)
        g = jnp.dot(u, win_ref[:, D_FF + lo:D_FF + lo + width], preferred_element_type=F32)
        hh = (a * _sigmoid(a) * g).astype(BF16)
        part = jnp.dot(hh, wout_ref[lo:lo + width, :], preferred_element_type=F32)
        y = part if y is None else y + part
        lo += width
    xn = x + 0.5 * mod[base + 2:base + 3] * y
    if not final:
        o_refs[0][...] = xn
    else:
        xn = _rms(xn, gf_ref[...])

        @pl.when(i < n_pt)
        def _():
            o_refs[0][...] = xn

        @pl.when(i >= n_pt)
        def _():
            o_refs[1][...] = xn


def _ffn_call(xs, mods, g_norm, g_final, w_in, w_out, st, *, layer, sub, final):
    tm = TM_WIDE
    n_pt = st.ctx_tiles(tm)
    n_tiles = st.n_all // tm
    ctx_spec = pl.BlockSpec((tm, D_MODEL), lambda i: (jnp.minimum(i, n_pt - 1), 0))
    lat_spec = pl.BlockSpec((tm, D_MODEL), lambda i: (jnp.maximum(i - n_pt, 0), 0))
    all_spec = pl.BlockSpec((tm, D_MODEL), lambda i: (i, 0))
    x_specs = [ctx_spec, lat_spec] if len(xs) == 2 else [all_spec]
    if final:
        out_specs = [ctx_spec, lat_spec]
        out_shape = [jax.ShapeDtypeStruct((st.n_ctx, D_MODEL), F32),
                     jax.ShapeDtypeStruct((st.n_all - st.n_ctx, D_MODEL), F32)]
    else:
        out_specs = [all_spec]
        out_shape = [jax.ShapeDtypeStruct((st.n_all, D_MODEL), F32)]
    return pl.pallas_call(
        functools.partial(_ffn_kernel, base=3 * sub, n_x=len(xs), final=final, n_pt=n_pt),
        grid=(n_tiles,),
        in_specs=x_specs + [
            st.mod_spec(tm),
            pl.BlockSpec((None, None, 1, D_MODEL), lambda i: (layer, sub, 0, 0)),
            pl.BlockSpec((1, D_MODEL), lambda i: (0, 0)),
            _layer_spec((D_MODEL, 2 * D_FF), layer),
            _layer_spec((D_FF, D_MODEL), layer),
        ],
        out_specs=out_specs,
        out_shape=out_shape,
        compiler_params=_cparams(1),
        name="ffn",
    )(*xs, mods, g_norm, g_final, w_in, w_out)


def _swap32(x):
    lane = lax.broadcasted_iota(jnp.int32, x.shape, 1)
    return jnp.where((lane & (DH_ATTN // 2)) == 0,
                     pltpu.roll(x, LANES - DH_ATTN // 2, 1), pltpu.roll(x, DH_ATTN // 2, 1))


def _proj_kernel(*refs, n_alias, n_pt, seqs_per_tile, seq):
    (x_ref, mod_ref, g_ref, w_ref, wg_ref, bgt_ref, cos_ref, sin_ref) = refs[:8]
    (q_ref, k_ref, ck_ref, cv_ref, zf_ref, mq_ref, mk_ref, mo_ref, vat_ref, vt_ref, gt_ref,
     wt_sc) = refs[8 + n_alias:]
    i = pl.program_id(0)
    is_latent = i >= n_pt

    @pl.when(i == 0)
    def _():
        wt_sc[0:W_ATTN, :] = w_ref[:, 2 * W_ATTN:3 * W_ATTN].T
        wt_sc[W_ATTN:W_ATTN + W_MLSTM, :] = w_ref[:, MV_LO:MV_HI].T
        wt_sc[W_ATTN + W_MLSTM:, :] = wg_ref[...].T.astype(BF16)

    x = x_ref[...]
    u = _modulated_norm(x, g_ref[...], mod_ref[...], 3).astype(BF16)
    za = jnp.dot(u, w_ref[:, :3 * W_ATTN], preferred_element_type=F32)
    zb = jnp.dot(u, w_ref[:, 3 * W_ATTN:MV_LO], preferred_element_type=F32)
    zo = jnp.dot(u, w_ref[:, MV_HI:P_MAIN], preferred_element_type=F32)
    zt = _bdot_nt(wt_sc[...], u)
    cos_t = cos_ref[...]
    sin_t = sin_ref[...]
    plain_k = []
    for h in range(N_ATTN):
        zq = za[:, h * DV_ATTN:(h + 1) * DV_ATTN]
        zk = za[:, W_ATTN + h * DV_ATTN:W_ATTN + (h + 1) * DV_ATTN]
        plain_k.append(zk)
        q_ref[h] = jnp.where(is_latent, zq * cos_t + _swap32(zq) * sin_t, zq).astype(q_ref.dtype)
        k_ref[h] = jnp.where(is_latent, zk * cos_t + _swap32(zk) * sin_t, zk).astype(k_ref.dtype)

    @pl.when(i < n_pt)
    def _():
        for h in range(N_ATTN):
            zv = za[:, 2 * W_ATTN + h * DV_ATTN:2 * W_ATTN + (h + 1) * DV_ATTN]
            for s in range(seqs_per_tile):
                rs = slice(s * seq, (s + 1) * seq)
                ck_ref[s, h] = plain_k[h][rs]
                cv_ref[s, h] = zv[rs]

    zf_ref[...] = zb[:, :W_FOUR].astype(zf_ref.dtype)
    off = W_FOUR
    mq_ref[...] = zb[:, off:off + W_MLSTM].astype(mq_ref.dtype)
    mk_ref[...] = (zb[:, off + W_MLSTM:off + 2 * W_MLSTM] * MLSTM_K_SCALE).astype(mk_ref.dtype)
    mo_ref[...] = zo
    vat_ref[...] = zt[:W_ATTN].astype(vat_ref.dtype)
    vt_ref[...] = zt[W_ATTN:W_ATTN + W_MLSTM]
    gpt = zt[W_ATTN + W_MLSTM:] + bgt_ref[...]
    sub = lax.broadcasted_iota(jnp.int32, gpt.shape, 0)
    gt_ref[...] = jnp.where((sub < N_GATE) & ((sub & N_MLSTM) != 0), _log_sigmoid(gpt), gpt)


def _proj_call(x, mods, g_norm, w_all, w_gate, bgt, cos_t, sin_t, kv_prev, st, *, layer):
    tm = TM_PROJ
    n_pt = st.ctx_tiles(tm)
    if tm % st.seq:
        raise ValueError("a context row tile must hold whole sequences")
    seqs_per_tile, tps = tm // st.seq, st.dec_seq // tm
    n = st.n_all
    head_shape = jax.ShapeDtypeStruct((N_ATTN, n, DV_ATTN), BF16)
    head_spec = pl.BlockSpec((N_ATTN, tm, DV_ATTN), lambda i: (0, i, 0))
    kv_shape = jax.ShapeDtypeStruct((st.batch, DEPTH, N_ATTN, st.seq, DV_ATTN), F32)
    kv_spec = pl.BlockSpec((seqs_per_tile, None, N_ATTN, st.seq, DV_ATTN),
                           lambda i: (jnp.minimum(i, n_pt - 1), layer, 0, 0, 0))
    tok_shape = jax.ShapeDtypeStruct((n, W_ATTN), F32)
    tok_bf16 = jax.ShapeDtypeStruct((n, W_ATTN), BF16)
    tok_spec = pl.BlockSpec((tm, W_ATTN), lambda i: (i, 0))
    rope_spec = pl.BlockSpec((tm, LANES), lambda i: (jnp.maximum(i - n_pt, 0) % tps, 0))
    alias_in = list(kv_prev) if kv_prev is not None else []
    n_in = 8
    return pl.pallas_call(
        functools.partial(_proj_kernel, n_alias=len(alias_in), n_pt=n_pt, seqs_per_tile=seqs_per_tile,
                          seq=st.seq),
        grid=(n // tm,),
        in_specs=[
            pl.BlockSpec((tm, D_MODEL), lambda i: (i, 0)),
            st.mod_spec(tm),
            pl.BlockSpec((None, None, 1, D_MODEL), lambda i: (layer, 1, 0, 0)),
            _layer_spec((D_MODEL, w_all.shape[-1]), layer),
            _layer_spec((D_MODEL, LANES), layer),
            _layer_spec((LANES, 1), layer),
            rope_spec, rope_spec,
        ] + [pl.BlockSpec(memory_space=pl.ANY)] * len(alias_in),
        out_specs=[head_spec, head_spec, kv_spec, kv_spec, tok_spec, tok_spec, tok_spec, tok_spec,
                   pl.BlockSpec((W_ATTN, tm), lambda i: (0, i)),
                   pl.BlockSpec((W_MLSTM, tm), lambda i: (0, i)),
                   pl.BlockSpec((LANES, tm), lambda i: (0, i))],
        out_shape=[head_shape, head_shape, kv_shape, kv_shape, tok_bf16, tok_bf16, tok_bf16, tok_shape,
                   jax.ShapeDtypeStruct((W_ATTN, n), BF16),
                   jax.ShapeDtypeStruct((W_MLSTM, n), F32),
                   jax.ShapeDtypeStruct((LANES, n), F32)],
        input_output_aliases={n_in + j: 2 + j for j in range(len(alias_in))},
        scratch_shapes=[pltpu.VMEM((P_COL, D_MODEL), BF16)],
        compiler_params=_cparams(1),
        name="mixer_proj",
    )(x, mods, g_norm, w_all, w_gate, bgt, cos_t, sin_t, *alias_in)


def _attn_kernel(*refs, lam_init, cached, nb, tq, seq):
    q_ref, k_ref, vt_ref = refs[:3]
    pos = 3
    if cached:
        ck_ref, cv_ref = refs[pos:pos + 2]
        pos += 2
    lam_ref, g_ref = refs[pos:pos + 2]
    o_ref = refs[-1]
    lp = lam_ref[...]
    lam = (jnp.exp(jnp.sum(lp[0:1] * lp[1:2], axis=-1, keepdims=True))
           - jnp.exp(jnp.sum(lp[2:3] * lp[3:4], axis=-1, keepdims=True)) + lam_init)
    g_sub = g_ref[...]
    lane = lax.broadcasted_iota(jnp.int32, (tq, DV_ATTN), 1)
    heads = range(N_ATTN)
    results = []
    for bi in range(nb):
        ks = slice(bi * seq, (bi + 1) * seq)
        keys = [[k_ref[h, ks] for h in heads]]
        vals_t = [[vt_ref[h * DV_ATTN:(h + 1) * DV_ATTN, ks] for h in heads]]
        if cached:
            keys.insert(0, [ck_ref[h].astype(BF16) for h in heads])
            vals_t.insert(0, [cv_ref[h].T.astype(BF16) for h in heads])
        qh = [q_ref[h, bi * tq:(bi + 1) * tq].astype(F32) * (ATTN_SCALE * LOG2E) for h in heads]
        exps, dens = [], []
        for first_map in (True, False):
            qm = [jnp.where((lane < DH_ATTN) == first_map, q, 0.0).astype(BF16) for q in qh]
            s = [jnp.concatenate([_bdot_nt(kg[h], qm[h]) for h in heads], axis=1) for kg in keys]
            m = functools.reduce(jnp.maximum, [jnp.max(si, axis=0, keepdims=True) for si in s])
            e = [jnp.exp2(si - m) for si in s]
            exps.append(e)
            dens.append(functools.reduce(jnp.add, [jnp.sum(ei, axis=0, keepdims=True) for ei in e]))
        ratio = lam * dens[0] / dens[1]
        inv = 1.0 / dens[0]
        probs = [(e1 - e2 * ratio).astype(BF16) for e1, e2 in zip(*exps)]
        for h in heads:
            cols = slice(h * tq, (h + 1) * tq)
            o_t = None
            for pj, vg in zip(probs, vals_t):
                part = jnp.dot(vg[h], pj[:, cols], preferred_element_type=F32)
                o_t = part if o_t is None else o_t + part
            o_t = o_t * inv[:, cols]
            y_t = o_t * lax.rsqrt(jnp.mean(o_t * o_t, axis=0, keepdims=True) + EPS)
            results.append((bi, h, (y_t.T * g_sub * (1.0 - lam_init)).astype(o_ref.dtype)))
    for bi, h, y in results:
        o_ref[bi * tq:(bi + 1) * tq, h * DV_ATTN:(h + 1) * DV_ATTN] = y


def _attn_call(q, k, v_t, cache, lam_p, g_sub, out_prev, st, *, layer, lam_init, latent):
    n = st.n_all
    if latent:
        batch, seq, nb, tq = st.dec_batch, st.dec_seq, 1, st.dec_seq // 2
        row0 = st.n_ctx
    else:
        batch, seq, nb, tq = st.batch, st.seq, 4, st.seq
        row0 = 0
    nq = seq // tq
    if nq > 1 and nb > 1:
        raise ValueError("query blocks of several sequences are not contiguous rows")
    q0, s0 = row0 // (nb * tq), row0 // (nb * seq)
    in_specs = [pl.BlockSpec((N_ATTN, nb * tq, DV_ATTN), lambda b, i: (0, q0 + b * nq + i, 0)),
                pl.BlockSpec((N_ATTN, nb * seq, DV_ATTN), lambda b, i: (0, s0 + b, 0)),
                pl.BlockSpec((W_ATTN, nb * seq), lambda b, i: (0, s0 + b))]
    args = [q, k, v_t]
    if cache is not None:
        past = cache[0].shape[3]
        c_spec = pl.BlockSpec((None, None, N_ATTN, past, DV_ATTN), lambda b, i: (b, layer, 0, 0, 0))
        in_specs += [c_spec, c_spec]
        args += list(cache)
    in_specs += [pl.BlockSpec((None, 4, DH_ATTN), lambda b, i: (layer, 0, 0)),
                 pl.BlockSpec((None, 1, DV_ATTN), lambda b, i: (layer, 0, 0))]
    args += [lam_p, g_sub]
    aliases = {}
    if out_prev is not None:
        aliases = {len(args): 0}
        in_specs.append(pl.BlockSpec(memory_space=pl.ANY))
        args.append(out_prev)
    return pl.pallas_call(
        functools.partial(_attn_kernel, lam_init=lam_init, cached=cache is not None, nb=nb, tq=tq, seq=seq),
        grid=(batch // nb, nq),
        in_specs=in_specs,
        out_specs=pl.BlockSpec((nb * tq, W_ATTN), lambda b, i: (q0 + b * nq + i, 0)),
        out_shape=jax.ShapeDtypeStruct((n, W_ATTN), BF16),
        input_output_aliases=aliases,
        compiler_params=_cparams(2),
        name="diff_attn",
    )(*args)


def _dft_tables(seq):
    def cs(n):
        j = np.arange(n)
        ang = 2.0 * np.pi * ((j[:, None] * j[None, :]) % n) / n
        return np.cos(ang) / math.sqrt(n), np.sin(ang) / math.sqrt(n)

    cd, sd = cs(DG_FOUR)
    ct, st = cs(seq)
    w_d = jnp.asarray(np.concatenate([cd, sd], axis=1), F32)
    return w_d.astype(BF16), jnp.asarray(ct, F32).astype(BF16), jnp.asarray(-st, F32).astype(BF16)


def _four_kernel(*refs, nb, seq):
    z_ref, wd_ref, ct_ref, st_ref = refs[:4]
    o_ref = refs[-1]
    y_cos, y_sin = [], []
    for gidx in range(N_FOUR):
        y = jnp.dot(z_ref[:, gidx * DG_FOUR:(gidx + 1) * DG_FOUR], wd_ref[...], preferred_element_type=F32)
        y_cos.append(y[:, :DG_FOUR].astype(BF16))
        y_sin.append(y[:, DG_FOUR:].astype(BF16))
    y_cos = jnp.concatenate(y_cos, axis=1)
    y_sin = jnp.concatenate(y_sin, axis=1)
    for s in range(nb):
        rows = slice(s * seq, (s + 1) * seq)
        o_ref[rows, :] = (jnp.dot(ct_ref[...], y_cos[rows], preferred_element_type=F32)
                          + jnp.dot(st_ref[...], y_sin[rows], preferred_element_type=F32)).astype(o_ref.dtype)


def _four_call(zf, out_prev, st, *, latent):
    if latent:
        batch, seq, nb, row0 = st.dec_batch, st.dec_seq, 1, st.n_ctx
    else:
        batch, seq, nb, row0 = st.batch, st.seq, 4, 0
    b0 = row0 // (nb * seq)
    w_d, ct, s_t = _dft_tables(seq)
    row_spec = pl.BlockSpec((nb * seq, W_FOUR), lambda b: (b0 + b, 0))
    in_specs = [row_spec,
                pl.BlockSpec((DG_FOUR, 2 * DG_FOUR), lambda b: (0, 0)),
                pl.BlockSpec((seq, seq), lambda b: (0, 0)),
                pl.BlockSpec((seq, seq), lambda b: (0, 0))]
    args = [zf, w_d, ct, s_t]
    aliases = {}
    if out_prev is not None:
        aliases = {len(args): 0}
        in_specs.append(pl.BlockSpec(memory_space=pl.ANY))
        args.append(out_prev)
    return pl.pallas_call(
        functools.partial(_four_kernel, nb=nb, seq=seq),
        grid=(batch // nb,),
        in_specs=in_specs,
        out_specs=row_spec,
        out_shape=jax.ShapeDtypeStruct((st.n_all, W_FOUR), BF16),
        input_output_aliases=aliases,
        compiler_params=_cparams(1),
        name="fourier_mix",
    )(*args)


def _mlstm_kernel(*refs, seq, nb, seeded, emit_state, n_alias):
    mq_ref, mk_ref, vt_ref, mo_ref, gt_ref, g_ref = refs[:6]
    pos = 6
    if seeded:
        c0_ref, n0_ref, m0_ref = refs[pos:pos + 3]
        pos += 3
    pos += n_alias
    o_ref = refs[pos]
    if emit_state:
        c1_ref, n1_ref, m1_ref = refs[pos + 1:pos + 4]

    L = min(MLSTM_CHUNK, seq)
    nc = seq // L
    s_idx = lax.broadcasted_iota(jnp.int32, (L, L), 0)
    t_idx = lax.broadcasted_iota(jnp.int32, (L, L), 1)
    before = (s_idx <= t_idx, s_idx >= t_idx)
    tri = (jnp.where(before[1], 1.0, 0.0).astype(BF16),
           jnp.where(before[0], 1.0, 0.0).astype(BF16))
    chains = [(bi, d, h) for bi in range(nb) for d in range(2) for h in range(N_MLSTM)]

    state = {}
    for (bi, d, h) in chains:
        if seeded:
            state[bi, d, h] = (c0_ref[bi, d, h], n0_ref[bi, d, h:h + 1, :], m0_ref[bi, d:d + 1, h:h + 1])
        else:
            state[bi, d, h] = (None, None, jnp.zeros((1, 1), F32))

    h_t = {}
    for c in range(nc):
        terms = {}
        for bi in range(nb):
            for d in range(2):
                r0 = bi * seq + (c if d == 0 else nc - 1 - c) * L
                g_t = gt_ref[0:N_GATE, r0:r0 + L]
                hi = g_t.astype(BF16)
                rem = g_t - hi.astype(F32)
                mid = rem.astype(BF16)
                lo = (rem - mid.astype(F32)).astype(BF16)
                pieces = _bdot_nt(jnp.concatenate([hi, mid, lo], axis=0), tri[d])
                cum_t = pieces[0:N_GATE] + pieces[N_GATE:2 * N_GATE] + pieces[2 * N_GATE:]
                c_t = (g_t - pltpu.roll(cum_t, N_GATE - N_MLSTM, 0)) * LOG2E
                col = jnp.concatenate([c_t, jnp.zeros((LANES - N_GATE, L), F32)], axis=0).T
                terms[bi, d] = (r0, col, g_t, cum_t)

        new_state = {}
        need_update = emit_state or c < nc - 1
        group = [(d, h) for d in range(2) for h in range(N_MLSTM)]
        for bi in range(nb):
            has_state = state[bi, 0, 0][0] is not None
            qs, ks, vts, cbs, s0s, i_rows, b_rows, m_prevs = [], [], [], [], [], [], [], []
            for d, h in group:
                r0, col, g_t, cum_t = terms[bi, d]
                ci = 2 * d * N_MLSTM + h
                cf = ci + N_MLSTM
                hs = slice(h * DH_MLSTM, (h + 1) * DH_MLSTM)
                qs.append(mq_ref[r0:r0 + L, hs])
                ks.append(mk_ref[r0:r0 + L, hs])
                vts.append(vt_ref[hs, r0:r0 + L])
                cbs.append(jnp.where(before[d], col[:, ci:ci + 1], -jnp.inf))
                s0s.append(_bdot_nt(ks[-1], qs[-1]))
                i_rows.append(g_t[ci:ci + 1, :])
                b_rows.append(cum_t[cf:cf + 1, :])
                m_prevs.append(jnp.broadcast_to(state[bi, d, h][2], (1, L)))
            cb = jnp.concatenate(cbs, axis=1)
            i_row, b_row = jnp.concatenate(i_rows, axis=1), jnp.concatenate(b_rows, axis=1)
            m_prev = jnp.concatenate(m_prevs, axis=1)
            m2_prev = m_prev * LOG2E
            m2_row = jnp.maximum(jnp.max(cb, axis=0, keepdims=True), m2_prev)
            s_t = jnp.concatenate(s0s, axis=1) * jnp.exp2(cb - m2_row)
            den = jnp.sum(s_t, axis=0, keepdims=True)
            s_bf = s_t.astype(BF16)
            m_t = b_row + m2_row * LN2
            floor = jnp.exp(-m_t)
            if has_state:
                sp = jnp.exp2(m2_prev - m2_row)
            if need_update:
                tot, new = [], []
                for j, (d, h) in enumerate(group):
                    last = j * L + (L - 1 if d == 0 else 0)
                    tot.append(jnp.broadcast_to(b_row[:, last:last + 1], (1, L)))
                    new.append(jnp.broadcast_to(m_t[:, last:last + 1], (1, L)))
                b_tot, m_new = jnp.concatenate(tot, axis=1), jnp.concatenate(new, axis=1)
                wl = jnp.exp(b_tot + (i_row - b_row) - m_new)
                if has_state:
                    decay = jnp.exp(b_tot + m_prev - m_new)
            for j, (d, h) in enumerate(group):
                cols = slice(j * L, (j + 1) * L)
                c_prev, n_prev, _ = state[bi, d, h]
                num_t = jnp.dot(vts[j].astype(BF16), s_bf[:, cols], preferred_element_type=F32)
                den_j = den[:, cols]
                if has_state:
                    cn = jnp.concatenate([c_prev, jnp.broadcast_to(n_prev, (SUBLANES, DH_MLSTM))], axis=0)
                    cq = _bdot_nt(cn, qs[j])
                    num_t = num_t + sp[:, cols] * cq[:DH_MLSTM]
                    den_j = den_j + sp[:, cols] * cq[DH_MLSTM:DH_MLSTM + 1]
                h_t[bi, d, h, c] = num_t / jnp.maximum(jnp.abs(den_j), floor[:, cols])
                if need_update:
                    wl_j = wl[:, cols]
                    vw = jnp.concatenate([vts[j] * wl_j, jnp.broadcast_to(wl_j, (SUBLANES, L))], axis=0)
                    upd = jnp.dot(vw.astype(BF16), ks[j], preferred_element_type=F32)
                    c_new, n_new = upd[:DH_MLSTM], upd[DH_MLSTM:DH_MLSTM + 1]
                    if has_state:
                        decay_j = decay[:, j * L:j * L + 1]
                        c_new = decay_j * c_prev + c_new
                        n_new = decay_j * n_prev + n_new
                    new_state[bi, d, h] = (c_new, n_new, m_new[:, j * L:j * L + 1])
        state = new_state

    g_m = g_ref[...]
    for bi in range(nb):
        for h in range(N_MLSTM):
            hs = slice(h * DH_MLSTM, (h + 1) * DH_MLSTM)
            fwd = [h_t[bi, 0, h, c] for c in range(nc)]
            bwd = [h_t[bi, 1, h, nc - 1 - c] for c in range(nc)]
            hsum = (fwd[0] if nc == 1 else jnp.concatenate(fwd, axis=1)) \
                + (bwd[0] if nc == 1 else jnp.concatenate(bwd, axis=1))
            y = hsum * lax.rsqrt(jnp.mean(hsum * hsum, axis=0, keepdims=True) + EPS)
            rows = slice(bi * seq, (bi + 1) * seq)
            o_ref[rows, hs] = (y.T * g_m * _sigmoid(mo_ref[rows, hs])).astype(o_ref.dtype)

    if emit_state:
        for (bi, d, h) in chains:
            c_fin, n_fin, m_fin = state[bi, d, h]
            c1_ref[bi, d, h] = c_fin
            n1_ref[bi, d, h:h + 1, :] = n_fin
            m1_ref[bi, d:d + 1, h:h + 1] = m_fin


def _mlstm_call(mq, mk, vt, mo, gates_t, g_m, state, alias_prev, st, *, layer, latent):
    if latent:
        batch, seq, nb, row0 = st.dec_batch, st.dec_seq, 1, st.n_ctx
    else:
        batch, seq, nb, row0 = st.batch, st.seq, 4, 0
    emit_state = not latent
    rows = nb * seq
    b0 = row0 // rows
    tok_spec = pl.BlockSpec((rows, W_MLSTM), lambda b: (b0 + b, 0))
    in_specs = [tok_spec, tok_spec,
                pl.BlockSpec((W_MLSTM, rows), lambda b: (0, b0 + b)),
                tok_spec,
                pl.BlockSpec((LANES, rows), lambda b: (0, b0 + b)),
                pl.BlockSpec((None, 1, DH_MLSTM), lambda b: (layer, 0, 0))]
    args = [mq, mk, vt, mo, gates_t, g_m]
    state_specs = [
        pl.BlockSpec((nb, None, 2, N_MLSTM, DH_MLSTM, DH_MLSTM), lambda b: (b, layer, 0, 0, 0, 0)),
        pl.BlockSpec((nb, None, 2, N_MLSTM, DH_MLSTM), lambda b: (b, layer, 0, 0, 0)),
        pl.BlockSpec((nb, None, 2, N_MLSTM), lambda b: (b, layer, 0, 0)),
    ]
    if state is not None:
        in_specs += state_specs
        args += list(state)
    alias_in = list(alias_prev) if alias_prev is not None else []
    n_in = len(args)
    in_specs += [pl.BlockSpec(memory_space=pl.ANY)] * len(alias_in)
    args += alias_in
    out_specs = [tok_spec]
    out_shape = [jax.ShapeDtypeStruct((st.n_all, W_MLSTM), BF16)]
    if emit_state:
        out_specs += state_specs
        out_shape += [
            jax.ShapeDtypeStruct((batch, DEPTH, 2, N_MLSTM, DH_MLSTM, DH_MLSTM), F32),
            jax.ShapeDtypeStruct((batch, DEPTH, 2, N_MLSTM, DH_MLSTM), F32),
            jax.ShapeDtypeStruct((batch, DEPTH, 2, N_MLSTM), F32),
        ]
    first_out = 1 if emit_state else 0
    return pl.pallas_call(
        functools.partial(_mlstm_kernel, seq=seq, nb=nb, seeded=state is not None, emit_state=emit_state,
                          n_alias=len(alias_in)),
        grid=(batch // nb,),
        in_specs=in_specs,
        out_specs=out_specs,
        out_shape=out_shape,
        input_output_aliases={n_in + j: first_out + j for j in range(len(alias_in))},
        compiler_params=_cparams(1),
        name="mlstm",
    )(*args)


def _merge_kernel(x_ref, a_ref, f_ref, m_ref, mod_ref, g_ref, wg_ref, wa_ref, wf_ref, wm_ref, wo_ref, o_ref):
    x = x_ref[...]
    mod = mod_ref[...]
    u = _modulated_norm(x, g_ref[...], mod, 3).astype(BF16)
    merged = None
    for j, (br_ref, w_ref) in enumerate(((a_ref, wa_ref), (f_ref, wf_ref), (m_ref, wm_ref))):
        gate = _sigmoid(jnp.dot(u, wg_ref[:, j * D_MODEL:(j + 1) * D_MODEL], preferred_element_type=F32))
        term = gate * _bdot(br_ref[...], w_ref[...])
        merged = term if merged is None else merged + term
    out = _bdot(merged, wo_ref[...])
    o_ref[...] = x + mod[5:6] * out


def _merge_call(x, a, f, m, mods, g_norm, wg, wa, wf, wm, wo, st, *, layer):
    tm = TM_WIDE
    n = st.n_all
    br_spec = pl.BlockSpec((tm, W_ATTN), lambda i: (i, 0))
    return pl.pallas_call(
        _merge_kernel,
        grid=(n // tm,),
        in_specs=[
            pl.BlockSpec((tm, D_MODEL), lambda i: (i, 0)),
            br_spec, br_spec, br_spec,
            st.mod_spec(tm),
            pl.BlockSpec((None, None, 1, D_MODEL), lambda i: (layer, 1, 0, 0)),
            _layer_spec((D_MODEL, 3 * D_MODEL), layer),
            _layer_spec((W_ATTN, D_MODEL), layer), _layer_spec((W_FOUR, D_MODEL), layer),
            _layer_spec((W_MLSTM, D_MODEL), layer),
            _layer_spec((D_MODEL, D_MODEL), layer),
        ],
        out_specs=pl.BlockSpec((tm, D_MODEL), lambda i: (i, 0)),
        out_shape=jax.ShapeDtypeStruct((n, D_MODEL), F32),
        compiler_params=_cparams(1),
        name="merge",
    )(x, a, f, m, mods, g_norm, wg, wa, wf, wm, wo)


def _rope_tables(n_tok):
    tok = np.arange(n_tok)
    inv = ROPE_BASE ** (-np.arange(ROPE_AXIS_PAIRS, dtype=np.float32) / ROPE_AXIS_PAIRS)
    ang = np.concatenate([(tok // GRID_W).astype(np.float32)[:, None] * inv,
                          (tok % GRID_W).astype(np.float32)[:, None] * inv], axis=-1).astype(np.float32)
    c, s = np.cos(ang), np.sin(ang)
    cos_t = np.concatenate([c, c, c, c], axis=-1)
    sin_t = np.concatenate([-s, s, -s, s], axis=-1)
    return jnp.asarray(cos_t, F32), jnp.asarray(sin_t, F32)


def kernel(x_prompt, x_sample, cache_k, cache_v, state_C, state_n, state_m, c, c_ctx, w_ada, b_ada, g_norm,
           w_ffn1_in, w_ffn1_out, w_ffn2_in, w_ffn2_out, w_in, b_mgate, attn_lambda, g_attn_sub, g_mlstm,
           w_branch_gate, w_br_attn, w_br_four, w_br_mlstm, w_out, g_final):
    batch, seq, _ = x_prompt.shape
    dec_batch, dec_seq, _ = x_sample.shape
    st = _Stream(batch, seq, dec_batch, dec_seq)
    cond = jnp.zeros((N_COND, D_MODEL), F32).at[0].set(c_ctx).at[1:1 + dec_batch].set(c)
    mods_all = _ada_call(cond, w_ada, b_ada).reshape(DEPTH, N_COND, N_MOD, D_MODEL)

    w_gate = jnp.pad(w_in[:, :, P_MAIN:], ((0, 0), (0, 0), (0, LANES - N_GATE)))
    b_gate_t = jnp.pad(b_mgate, ((0, 0), (0, LANES - N_GATE)))[:, :, None]
    g_norm4 = g_norm[:, :, None, :]
    g_fin = g_final[None, :]
    ffn_w = ((w_ffn1_in.astype(BF16), w_ffn1_out.astype(BF16)), (w_ffn2_in.astype(BF16), w_ffn2_out.astype(BF16)))
    w_all = w_in[:, :, :P_MAIN].astype(BF16)
    w_bg, w_ba, w_bf, w_bm, w_o = (t.astype(BF16) for t in
                                   (w_branch_gate, w_br_attn, w_br_four, w_br_mlstm, w_out))
    g_sub3 = g_attn_sub[:, None, :]
    g_m3 = g_mlstm[:, None, :]
    cos_t, sin_t = _rope_tables(dec_seq)
    lat_state = (state_C, state_n, state_m)

    xs = (x_prompt.reshape(batch * seq, D_MODEL), x_sample.reshape(dec_batch * dec_seq, D_MODEL))
    kv, states = None, None
    for l in range(DEPTH):
        mods = mods_all[l]
        lam_init = 0.8 - 0.6 * math.exp(-0.3 * l)
        (x,) = _ffn_call(xs, mods, g_norm4, g_fin, *ffn_w[0], st, layer=l, sub=0, final=False)
        q, k, ck, cv, zf, mq, mk, mo, vat, vt, gates_t = _proj_call(
            x, mods, g_norm4, w_all, w_gate, b_gate_t, cos_t, sin_t, kv, st, layer=l)
        kv = (ck, cv)
        a = _attn_call(q, k, vat, None, attn_lambda, g_sub3, None, st, layer=l, lam_init=lam_init,
                       latent=False)
        a = _attn_call(q, k, vat, (cache_k, cache_v), attn_lambda, g_sub3, a, st, layer=l,
                       lam_init=lam_init, latent=True)
        f = _four_call(zf, None, st, latent=False)
        f = _four_call(zf, f, st, latent=True)
        m, *states = _mlstm_call(mq, mk, vt, mo, gates_t, g_m3, None, states, st, layer=l, latent=False)
        (m,) = _mlstm_call(mq, mk, vt, mo, gates_t, g_m3, lat_state, (m,), st, layer=l, latent=True)
        x = _merge_call(x, a, f, m, mods, g_norm4, w_bg, w_ba, w_bf, w_bm, w_o, st, layer=l)
        xs = tuple(_ffn_call((x,), mods, g_norm4, g_fin, *ffn_w[1], st, layer=l, sub=2,
                             final=(l == DEPTH - 1)))
    y_prompt, y_sample = xs
    return (y_prompt.reshape(batch, seq, D_MODEL), y_sample.reshape(dec_batch, dec_seq, D_MODEL),
            *kv, *states)
```

```python
import functools
import math

import numpy as np
import jax
import jax.numpy as jnp
from jax import lax
from jax.experimental import pallas as pl
from jax.experimental.pallas import tpu as pltpu

D_MODEL = 1024
DEPTH = 2
GRID_W = 64
N_ATTN = 4
DH_ATTN = 64
DV_ATTN = 2 * DH_ATTN
W_ATTN = N_ATTN * DV_ATTN
N_FOUR = 4
DG_FOUR = 128
W_FOUR = N_FOUR * DG_FOUR
N_MLSTM = 4
DH_MLSTM = 128
W_MLSTM = N_MLSTM * DH_MLSTM
N_GATE = 4 * N_MLSTM
LANES = 128
SUBLANES = 16
MV_LO = 3 * W_ATTN + W_FOUR + 2 * W_MLSTM
MV_HI = MV_LO + W_MLSTM
P_MAIN = MV_HI + W_MLSTM
D_FF = 2816
N_MOD = 9
N_COND = 8
ROPE_BASE = 10000.0
ROPE_AXIS_PAIRS = DH_ATTN // 4
ATTN_SCALE = DH_ATTN ** -0.5
LOG2E = 1.4426950408889634
LN2 = 0.6931471805599453
MLSTM_K_SCALE = DH_MLSTM ** -0.5
EPS = 1e-6
MLSTM_CHUNK = 256
VMEM_LIMIT = 56 * 1024 * 1024
TM_WIDE = 1024
TM_PROJ = 512

F32 = jnp.float32
BF16 = jnp.bfloat16


def _cparams(n_grid):
    return pltpu.CompilerParams(dimension_semantics=("arbitrary",) * n_grid,
                                vmem_limit_bytes=VMEM_LIMIT)


def _bdot(a, b):
    return jnp.dot(a.astype(BF16), b.astype(BF16), preferred_element_type=F32)


def _bdot_nt(a, b):
    return lax.dot_general(a.astype(BF16), b.astype(BF16), (((1,), (1,)), ((), ())),
                           preferred_element_type=F32)


def _sigmoid(x):
    return 1.0 / (1.0 + jnp.exp(-x))


def _log_sigmoid(x):
    return jnp.minimum(x, 0.0) - jnp.log1p(jnp.exp(-jnp.abs(x)))


def _rms(x, g):
    return x * lax.rsqrt(jnp.mean(x * x, axis=-1, keepdims=True) + EPS) * g


def _modulated_norm(x, g, mod, base):
    return _rms(x, g) * (1.0 + mod[base + 1:base + 2]) + mod[base:base + 1]


def _layer_spec(shape, layer):
    return pl.BlockSpec((None,) + shape, lambda *_: (layer,) + (0,) * len(shape),
                        pipeline_mode=pl.Buffered(1))


class _Stream:
    def __init__(self, batch, seq, dec_batch, dec_seq):
        self.batch, self.seq, self.dec_batch, self.dec_seq = batch, seq, dec_batch, dec_seq
        self.n_ctx = batch * seq
        self.n_all = self.n_ctx + dec_batch * dec_seq

    def ctx_tiles(self, tm):
        if self.n_ctx % tm or self.dec_seq % tm:
            raise ValueError("row tile must divide the context rows and one latent sequence")
        return self.n_ctx // tm

    def mod_spec(self, tm):
        n_pt, tps = self.ctx_tiles(tm), self.dec_seq // tm
        return pl.BlockSpec((None, N_MOD, D_MODEL),
                            lambda i: (jnp.where(i < n_pt, 0, 1 + (i - n_pt) // tps), 0, 0))


def _ada_kernel(c_ref, w_ref, b_ref, o_ref):
    c = c_ref[...]
    s = c * _sigmoid(c)
    o_ref[...] = _bdot(s, w_ref[...]) + b_ref[...]


def _ada_call(cond, w_ada, b_ada):
    tn = 2304
    n_out = N_MOD * D_MODEL
    return pl.pallas_call(
        _ada_kernel,
        grid=(DEPTH, n_out // tn),
        in_specs=[
            pl.BlockSpec((N_COND, D_MODEL), lambda l, j: (0, 0)),
            pl.BlockSpec((None, D_MODEL, tn), lambda l, j: (l, 0, j)),
            pl.BlockSpec((None, 1, tn), lambda l, j: (l, 0, j)),
        ],
        out_specs=pl.BlockSpec((None, N_COND, tn), lambda l, j: (l, 0, j)),
        out_shape=jax.ShapeDtypeStruct((DEPTH, N_COND, n_out), F32),
        compiler_params=_cparams(2),
        name="adaln",
    )(cond, w_ada, b_ada.reshape(DEPTH, 1, n_out))


FF_CHUNKS = (768, 768, 768, 512)


def _ffn_kernel(*refs, base, n_x, final, n_pt):
    x_refs = refs[:n_x]
    mod_ref, g_ref, gf_ref, win_ref, wout_ref = refs[n_x:n_x + 5]
    o_refs = refs[n_x + 5:]
    i = pl.program_id(0)
    if n_x == 2:
        x = jnp.where(i < n_pt, x_refs[0][...], x_refs[1][...])
    else:
        x = x_refs[0][...]
    mod = mod_ref[...]
    u = _modulated_norm(x, g_ref[...], mod, base).astype(BF16)
    y = None
    lo = 0
    for width in FF_CHUNKS:
        a = jnp.dot(u, win_ref[:, lo:lo + width], preferred_element_type=F32)
        g = jnp.dot(u, win_ref[:, D_FF + lo:D_FF + lo + width], preferred_element_type=F32)
        hh = (a * _sigmoid(a) * g).astype(BF16)
        part = jnp.dot(hh, wout_ref[lo:lo + width, :], preferred_element_type=F32)
        y = part if y is None else y + part
        lo += width
    xn = x + 0.5 * mod[base + 2:base + 3] * y
    if not final:
        o_refs[0][...] = xn
    else:
        xn = _rms(xn, gf_ref[...])

        @pl.when(i < n_pt)
        def _():
            o_refs[0][...] = xn

        @pl.when(i >= n_pt)
        def _():
            o_refs[1][...] = xn


def _ffn_call(xs, mods, g_norm, g_final, w_in, w_out, st, *, layer, sub, final):
    tm = TM_WIDE
    n_pt = st.ctx_tiles(tm)
    n_tiles = st.n_all // tm
    ctx_spec = pl.BlockSpec((tm, D_MODEL), lambda i: (jnp.minimum(i, n_pt - 1), 0))
    lat_spec = pl.BlockSpec((tm, D_MODEL), lambda i: (jnp.maximum(i - n_pt, 0), 0))
    all_spec = pl.BlockSpec((tm, D_MODEL), lambda i: (i, 0))
    x_specs = [ctx_spec, lat_spec] if len(xs) == 2 else [all_spec]
    if final:
        out_specs = [ctx_spec, lat_spec]
        out_shape = [jax.ShapeDtypeStruct((st.n_ctx, D_MODEL), F32),
                     jax.ShapeDtypeStruct((st.n_all - st.n_ctx, D_MODEL), F32)]
    else:
        out_specs = [all_spec]
        out_shape = [jax.ShapeDtypeStruct((st.n_all, D_MODEL), F32)]
    return pl.pallas_call(
        functools.partial(_ffn_kernel, base=3 * sub, n_x=len(xs), final=final, n_pt=n_pt),
        grid=(n_tiles,),
        in_specs=x_specs + [
            st.mod_spec(tm),
            pl.BlockSpec((None, None, 1, D_MODEL), lambda i: (layer, sub, 0, 0)),
            pl.BlockSpec((1, D_MODEL), lambda i: (0, 0)),
            _layer_spec((D_MODEL, 2 * D_FF), layer),
            _layer_spec((D_FF, D_MODEL), layer),
        ],
        out_specs=out_specs,
        out_shape=out_shape,
        compiler_params=_cparams(1),
        name="ffn",
    )(*xs, mods, g_norm, g_final, w_in, w_out)


def _swap32(x):
    lane = lax.broadcasted_iota(jnp.int32, x.shape, 1)
    return jnp.where((lane & (DH_ATTN // 2)) == 0,
                     pltpu.roll(x, LANES - DH_ATTN // 2, 1), pltpu.roll(x, DH_ATTN // 2, 1))


def _proj_kernel(*refs, n_alias, n_pt, seqs_per_tile, seq):
    (x_ref, mod_ref, g_ref, wt_ref, bgt_ref, cos_ref, sin_ref) = refs[:7]
    (q_ref, k_ref, ck_ref, cv_ref, zf_ref, mq_ref, mk_ref, mo_ref, vat_ref, vt_ref, gt_ref) = refs[7 + n_alias:]
    i = pl.program_id(0)
    is_latent = i >= n_pt
    x = x_ref[...]
    u = _modulated_norm(x, g_ref[...], mod_ref[...], 3).astype(BF16)
    za = _bdot_nt(u, wt_ref[:3 * W_ATTN, :])
    zb = _bdot_nt(u, wt_ref[3 * W_ATTN:MV_LO, :])
    zo = _bdot_nt(u, wt_ref[MV_HI:P_MAIN, :])
    cos_t = cos_ref[...]
    sin_t = sin_ref[...]
    plain_k = []
    for h in range(N_ATTN):
        zq = za[:, h * DV_ATTN:(h + 1) * DV_ATTN]
        zk = za[:, W_ATTN + h * DV_ATTN:W_ATTN + (h + 1) * DV_ATTN]
        plain_k.append(zk)
        q_ref[h] = jnp.where(is_latent, zq * cos_t + _swap32(zq) * sin_t, zq).astype(q_ref.dtype)
        k_ref[h] = jnp.where(is_latent, zk * cos_t + _swap32(zk) * sin_t, zk).astype(k_ref.dtype)

    @pl.when(i < n_pt)
    def _():
        for h in range(N_ATTN):
            zv = za[:, 2 * W_ATTN + h * DV_ATTN:2 * W_ATTN + (h + 1) * DV_ATTN]
            for s in range(seqs_per_tile):
                rs = slice(s * seq, (s + 1) * seq)
                ck_ref[s, h] = plain_k[h][rs]
                cv_ref[s, h] = zv[rs]

    zf_ref[...] = zb[:, :W_FOUR].astype(zf_ref.dtype)
    off = W_FOUR
    mq_ref[...] = zb[:, off:off + W_MLSTM].astype(mq_ref.dtype)
    mk_ref[...] = (zb[:, off + W_MLSTM:off + 2 * W_MLSTM] * MLSTM_K_SCALE).astype(mk_ref.dtype)
    mo_ref[...] = zo
    vat_ref[...] = _bdot_nt(wt_ref[2 * W_ATTN:3 * W_ATTN, :], u).astype(vat_ref.dtype)
    vt_ref[...] = _bdot_nt(wt_ref[MV_LO:MV_HI, :], u)
    gpt = _bdot_nt(wt_ref[P_MAIN:, :], u) + bgt_ref[...]
    sub = lax.broadcasted_iota(jnp.int32, gpt.shape, 0)
    gt_ref[...] = jnp.where((sub & N_MLSTM) != 0, _log_sigmoid(gpt), gpt)


def _proj_call(x, mods, g_norm, w_t, bgt, cos_t, sin_t, kv_prev, st, *, layer):
    tm = TM_PROJ
    n_pt = st.ctx_tiles(tm)
    if tm % st.seq:
        raise ValueError("a context row tile must hold whole sequences")
    seqs_per_tile, tps = tm // st.seq, st.dec_seq // tm
    n = st.n_all
    head_shape = jax.ShapeDtypeStruct((N_ATTN, n, DV_ATTN), BF16)
    head_spec = pl.BlockSpec((N_ATTN, tm, DV_ATTN), lambda i: (0, i, 0))
    kv_shape = jax.ShapeDtypeStruct((st.batch, DEPTH, N_ATTN, st.seq, DV_ATTN), F32)
    kv_spec = pl.BlockSpec((seqs_per_tile, None, N_ATTN, st.seq, DV_ATTN),
                           lambda i: (jnp.minimum(i, n_pt - 1), layer, 0, 0, 0))
    tok_shape = jax.ShapeDtypeStruct((n, W_ATTN), F32)
    tok_bf16 = jax.ShapeDtypeStruct((n, W_ATTN), BF16)
    tok_spec = pl.BlockSpec((tm, W_ATTN), lambda i: (i, 0))
    rope_spec = pl.BlockSpec((tm, LANES), lambda i: (jnp.maximum(i - n_pt, 0) % tps, 0))
    alias_in = list(kv_prev) if kv_prev is not None else []
    n_in = 7
    return pl.pallas_call(
        functools.partial(_proj_kernel, n_alias=len(alias_in), n_pt=n_pt, seqs_per_tile=seqs_per_tile,
                          seq=st.seq),
        grid=(n // tm,),
        in_specs=[
            pl.BlockSpec((tm, D_MODEL), lambda i: (i, 0)),
            st.mod_spec(tm),
            pl.BlockSpec((None, None, 1, D_MODEL), lambda i: (layer, 1, 0, 0)),
            _layer_spec((w_t.shape[1], D_MODEL), layer),
            _layer_spec((N_GATE, 1), layer),
            rope_spec, rope_spec,
        ] + [pl.BlockSpec(memory_space=pl.ANY)] * len(alias_in),
        out_specs=[head_spec, head_spec, kv_spec, kv_spec, tok_spec, tok_spec, tok_spec, tok_spec,
                   pl.BlockSpec((W_ATTN, tm), lambda i: (0, i)),
                   pl.BlockSpec((W_MLSTM, tm), lambda i: (0, i)),
                   pl.BlockSpec((N_GATE, tm), lambda i: (0, i))],
        out_shape=[head_shape, head_shape, kv_shape, kv_shape, tok_bf16, tok_bf16, tok_bf16, tok_shape,
                   jax.ShapeDtypeStruct((W_ATTN, n), BF16),
                   jax.ShapeDtypeStruct((W_MLSTM, n), F32),
                   jax.ShapeDtypeStruct((N_GATE, n), F32)],
        input_output_aliases={n_in + j: 2 + j for j in range(len(alias_in))},
        compiler_params=_cparams(1),
        name="mixer_proj",
    )(x, mods, g_norm, w_t, bgt, cos_t, sin_t, *alias_in)


def _attn_kernel(*refs, lam_init, cached, nb, tq, seq):
    q_ref, k_ref, vt_ref = refs[:3]
    pos = 3
    if cached:
        ck_ref, cv_ref = refs[pos:pos + 2]
        pos += 2
    lam_ref, g_ref = refs[pos:pos + 2]
    o_ref = refs[-1]
    lp = lam_ref[...]
    lam = (jnp.exp(jnp.sum(lp[0:1] * lp[1:2], axis=-1, keepdims=True))
           - jnp.exp(jnp.sum(lp[2:3] * lp[3:4], axis=-1, keepdims=True)) + lam_init)
    g_sub = g_ref[...]
    lane = lax.broadcasted_iota(jnp.int32, (tq, DV_ATTN), 1)
    heads = range(N_ATTN)
    results = []
    for bi in range(nb):
        ks = slice(bi * seq, (bi + 1) * seq)
        keys = [[k_ref[h, ks] for h in heads]]
        vals_t = [[vt_ref[h * DV_ATTN:(h + 1) * DV_ATTN, ks] for h in heads]]
        if cached:
            keys.insert(0, [ck_ref[h].astype(BF16) for h in heads])
            vals_t.insert(0, [cv_ref[h].T.astype(BF16) for h in heads])
        qh = [q_ref[h, bi * tq:(bi + 1) * tq].astype(F32) * (ATTN_SCALE * LOG2E) for h in heads]
        exps, dens = [], []
        for first_map in (True, False):
            qm = [jnp.where((lane < DH_ATTN) == first_map, q, 0.0).astype(BF16) for q in qh]
            s = [jnp.concatenate([_bdot_nt(kg[h], qm[h]) for h in heads], axis=1) for kg in keys]
            m = functools.reduce(jnp.maximum, [jnp.max(si, axis=0, keepdims=True) for si in s])
            e = [jnp.exp2(si - m) for si in s]
            exps.append(e)
            dens.append(functools.reduce(jnp.add, [jnp.sum(ei, axis=0, keepdims=True) for ei in e]))
        ratio = lam * dens[0] / dens[1]
        inv = 1.0 / dens[0]
        probs = [(e1 - e2 * ratio).astype(BF16) for e1, e2 in zip(*exps)]
        for h in heads:
            cols = slice(h * tq, (h + 1) * tq)
            o_t = None
            for pj, vg in zip(probs, vals_t):
                part = jnp.dot(vg[h], pj[:, cols], preferred_element_type=F32)
                o_t = part if o_t is None else o_t + part
            o_t = o_t * inv[:, cols]
            y_t = o_t * lax.rsqrt(jnp.mean(o_t * o_t, axis=0, keepdims=True) + EPS)
            results.append((bi, h, (y_t.T * g_sub * (1.0 - lam_init)).astype(o_ref.dtype)))
    for bi, h, y in results:
        o_ref[bi * tq:(bi + 1) * tq, h * DV_ATTN:(h + 1) * DV_ATTN] = y


def _attn_call(q, k, v_t, cache, lam_p, g_sub, out_prev, st, *, layer, lam_init, latent):
    n = st.n_all
    if latent:
        batch, seq, nb, tq = st.dec_batch, st.dec_seq, 1, st.dec_seq // 2
        row0 = st.n_ctx
    else:
        batch, seq, nb, tq = st.batch, st.seq, 4, st.seq
        row0 = 0
    nq = seq // tq
    if nq > 1 and nb > 1:
        raise ValueError("query blocks of several sequences are not contiguous rows")
    q0, s0 = row0 // (nb * tq), row0 // (nb * seq)
    in_specs = [pl.BlockSpec((N_ATTN, nb * tq, DV_ATTN), lambda b, i: (0, q0 + b * nq + i, 0)),
                pl.BlockSpec((N_ATTN, nb * seq, DV_ATTN), lambda b, i: (0, s0 + b, 0)),
                pl.BlockSpec((W_ATTN, nb * seq), lambda b, i: (0, s0 + b))]
    args = [q, k, v_t]
    if cache is not None:
        past = cache[0].shape[3]
        c_spec = pl.BlockSpec((None, None, N_ATTN, past, DV_ATTN), lambda b, i: (b, layer, 0, 0, 0))
        in_specs += [c_spec, c_spec]
        args += list(cache)
    in_specs += [pl.BlockSpec((None, 4, DH_ATTN), lambda b, i: (layer, 0, 0)),
                 pl.BlockSpec((None, 1, DV_ATTN), lambda b, i: (layer, 0, 0))]
    args += [lam_p, g_sub]
    aliases = {}
    if out_prev is not None:
        aliases = {len(args): 0}
        in_specs.append(pl.BlockSpec(memory_space=pl.ANY))
        args.append(out_prev)
    return pl.pallas_call(
        functools.partial(_attn_kernel, lam_init=lam_init, cached=cache is not None, nb=nb, tq=tq, seq=seq),
        grid=(batch // nb, nq),
        in_specs=in_specs,
        out_specs=pl.BlockSpec((nb * tq, W_ATTN), lambda b, i: (q0 + b * nq + i, 0)),
        out_shape=jax.ShapeDtypeStruct((n, W_ATTN), BF16),
        input_output_aliases=aliases,
        compiler_params=_cparams(2),
        name="diff_attn",
    )(*args)


def _dft_tables(seq):
    def cs(n):
        j = np.arange(n)
        ang = 2.0 * np.pi * ((j[:, None] * j[None, :]) % n) / n
        return np.cos(ang) / math.sqrt(n), np.sin(ang) / math.sqrt(n)

    cd, sd = cs(DG_FOUR)
    ct, st = cs(seq)
    w_d = jnp.asarray(np.concatenate([cd, sd], axis=1), F32)
    return w_d.astype(BF16), jnp.asarray(ct, F32).astype(BF16), jnp.asarray(-st, F32).astype(BF16)


def _four_kernel(*refs, nb, seq):
    z_ref, wd_ref, ct_ref, st_ref = refs[:4]
    o_ref = refs[-1]
    y_cos, y_sin = [], []
    for gidx in range(N_FOUR):
        y = jnp.dot(z_ref[:, gidx * DG_FOUR:(gidx + 1) * DG_FOUR], wd_ref[...], preferred_element_type=F32)
        y_cos.append(y[:, :DG_FOUR].astype(BF16))
        y_sin.append(y[:, DG_FOUR:].astype(BF16))
    y_cos = jnp.concatenate(y_cos, axis=1)
    y_sin = jnp.concatenate(y_sin, axis=1)
    for s in range(nb):
        rows = slice(s * seq, (s + 1) * seq)
        o_ref[rows, :] = (jnp.dot(ct_ref[...], y_cos[rows], preferred_element_type=F32)
                          + jnp.dot(st_ref[...], y_sin[rows], preferred_element_type=F32)).astype(o_ref.dtype)


def _four_call(zf, out_prev, st, *, latent):
    if latent:
        batch, seq, nb, row0 = st.dec_batch, st.dec_seq, 1, st.n_ctx
    else:
        batch, seq, nb, row0 = st.batch, st.seq, 4, 0
    b0 = row0 // (nb * seq)
    w_d, ct, s_t = _dft_tables(seq)
    row_spec = pl.BlockSpec((nb * seq, W_FOUR), lambda b: (b0 + b, 0))
    in_specs = [row_spec,
                pl.BlockSpec((DG_FOUR, 2 * DG_FOUR), lambda b: (0, 0)),
                pl.BlockSpec((seq, seq), lambda b: (0, 0)),
                pl.BlockSpec((seq, seq), lambda b: (0, 0))]
    args = [zf, w_d, ct, s_t]
    aliases = {}
    if out_prev is not None:
        aliases = {len(args): 0}
        in_specs.append(pl.BlockSpec(memory_space=pl.ANY))
        args.append(out_prev)
    return pl.pallas_call(
        functools.partial(_four_kernel, nb=nb, seq=seq),
        grid=(batch // nb,),
        in_specs=in_specs,
        out_specs=row_spec,
        out_shape=jax.ShapeDtypeStruct((st.n_all, W_FOUR), BF16),
        input_output_aliases=aliases,
        compiler_params=_cparams(1),
        name="fourier_mix",
    )(*args)


def _mlstm_kernel(*refs, seq, nb, seeded, emit_state, n_alias):
    mq_ref, mk_ref, vt_ref, mo_ref, gt_ref, g_ref = refs[:6]
    pos = 6
    if seeded:
        c0_ref, n0_ref, m0_ref = refs[pos:pos + 3]
        pos += 3
    pos += n_alias
    o_ref = refs[pos]
    if emit_state:
        c1_ref, n1_ref, m1_ref = refs[pos + 1:pos + 4]

    L = min(MLSTM_CHUNK, seq)
    nc = seq // L
    s_idx = lax.broadcasted_iota(jnp.int32, (L, L), 0)
    t_idx = lax.broadcasted_iota(jnp.int32, (L, L), 1)
    before = (s_idx <= t_idx, s_idx >= t_idx)
    tri = (jnp.where(before[1], 1.0, 0.0).astype(BF16),
           jnp.where(before[0], 1.0, 0.0).astype(BF16))
    chains = [(bi, d, h) for bi in range(nb) for d in range(2) for h in range(N_MLSTM)]

    state = {}
    for (bi, d, h) in chains:
        if seeded:
            state[bi, d, h] = (c0_ref[bi, d, h], n0_ref[bi, d, h:h + 1, :], m0_ref[bi, d:d + 1, h:h + 1])
        else:
            state[bi, d, h] = (None, None, jnp.zeros((1, 1), F32))

    h_t = {}
    for c in range(nc):
        terms = {}
        for bi in range(nb):
            for d in range(2):
                r0 = bi * seq + (c if d == 0 else nc - 1 - c) * L
                g_t = gt_ref[:, r0:r0 + L]
                hi = g_t.astype(BF16)
                rem = g_t - hi.astype(F32)
                mid = rem.astype(BF16)
                lo = (rem - mid.astype(F32)).astype(BF16)
                pieces = _bdot_nt(jnp.concatenate([hi, mid, lo], axis=0), tri[d])
                cum_t = pieces[0:N_GATE] + pieces[N_GATE:2 * N_GATE] + pieces[2 * N_GATE:]
                c_t = (g_t - pltpu.roll(cum_t, N_GATE - N_MLSTM, 0)) * LOG2E
                col = jnp.concatenate([c_t, jnp.zeros((LANES - N_GATE, L), F32)], axis=0).T
                terms[bi, d] = (r0, col, g_t, cum_t)

        new_state = {}
        need_update = emit_state or c < nc - 1
        group = [(d, h) for d in range(2) for h in range(N_MLSTM)]
        for bi in range(nb):
            has_state = state[bi, 0, 0][0] is not None
            qs, ks, vts, cbs, s0s, i_rows, b_rows, m_prevs = [], [], [], [], [], [], [], []
            for d, h in group:
                r0, col, g_t, cum_t = terms[bi, d]
                ci = 2 * d * N_MLSTM + h
                cf = ci + N_MLSTM
                hs = slice(h * DH_MLSTM, (h + 1) * DH_MLSTM)
                qs.append(mq_ref[r0:r0 + L, hs])
                ks.append(mk_ref[r0:r0 + L, hs])
                vts.append(vt_ref[hs, r0:r0 + L])
                cbs.append(jnp.where(before[d], col[:, ci:ci + 1], -jnp.inf))
                s0s.append(_bdot_nt(ks[-1], qs[-1]))
                i_rows.append(g_t[ci:ci + 1, :])
                b_rows.append(cum_t[cf:cf + 1, :])
                m_prevs.append(jnp.broadcast_to(state[bi, d, h][2], (1, L)))
            cb = jnp.concatenate(cbs, axis=1)
            i_row, b_row = jnp.concatenate(i_rows, axis=1), jnp.concatenate(b_rows, axis=1)
            m_prev = jnp.concatenate(m_prevs, axis=1)
            m2_prev = m_prev * LOG2E
            m2_row = jnp.maximum(jnp.max(cb, axis=0, keepdims=True), m2_prev)
            s_t = jnp.concatenate(s0s, axis=1) * jnp.exp2(cb - m2_row)
            den = jnp.sum(s_t, axis=0, keepdims=True)
            s_bf = s_t.astype(BF16)
            m_t = b_row + m2_row * LN2
            floor = jnp.exp(-m_t)
            if has_state:
                sp = jnp.exp2(m2_prev - m2_row)
            if need_update:
                tot, new = [], []
                for j, (d, h) in enumerate(group):
                    last = j * L + (L - 1 if d == 0 else 0)
                    tot.append(jnp.broadcast_to(b_row[:, last:last + 1], (1, L)))
                    new.append(jnp.broadcast_to(m_t[:, last:last + 1], (1, L)))
                b_tot, m_new = jnp.concatenate(tot, axis=1), jnp.concatenate(new, axis=1)
                wl = jnp.exp(b_tot + (i_row - b_row) - m_new)
                if has_state:
                    decay = jnp.exp(b_tot + m_prev - m_new)
            for j, (d, h) in enumerate(group):
                cols = slice(j * L, (j + 1) * L)
                c_prev, n_prev, _ = state[bi, d, h]
                num_t = jnp.dot(vts[j].astype(BF16), s_bf[:, cols], preferred_element_type=F32)
                den_j = den[:, cols]
                if has_state:
                    cn = jnp.concatenate([c_prev, jnp.broadcast_to(n_prev, (SUBLANES, DH_MLSTM))], axis=0)
                    cq = _bdot_nt(cn, qs[j])
                    num_t = num_t + sp[:, cols] * cq[:DH_MLSTM]
                    den_j = den_j + sp[:, cols] * cq[DH_MLSTM:DH_MLSTM + 1]
                h_t[bi, d, h, c] = num_t / jnp.maximum(jnp.abs(den_j), floor[:, cols])
                if need_update:
                    wl_j = wl[:, cols]
                    vw = jnp.concatenate([vts[j] * wl_j, jnp.broadcast_to(wl_j, (SUBLANES, L))], axis=0)
                    upd = jnp.dot(vw.astype(BF16), ks[j], preferred_element_type=F32)
                    c_new, n_new = upd[:DH_MLSTM], upd[DH_MLSTM:DH_MLSTM + 1]
                    if has_state:
                        decay_j = decay[:, j * L:j * L + 1]
                        c_new = decay_j * c_prev + c_new
                        n_new = decay_j * n_prev + n_new
                    new_state[bi, d, h] = (c_new, n_new, m_new[:, j * L:j * L + 1])
        state = new_state

    g_m = g_ref[...]
    for bi in range(nb):
        for h in range(N_MLSTM):
            hs = slice(h * DH_MLSTM, (h + 1) * DH_MLSTM)
            fwd = [h_t[bi, 0, h, c] for c in range(nc)]
            bwd = [h_t[bi, 1, h, nc - 1 - c] for c in range(nc)]
            hsum = (fwd[0] if nc == 1 else jnp.concatenate(fwd, axis=1)) \
                + (bwd[0] if nc == 1 else jnp.concatenate(bwd, axis=1))
            y = hsum * lax.rsqrt(jnp.mean(hsum * hsum, axis=0, keepdims=True) + EPS)
            rows = slice(bi * seq, (bi + 1) * seq)
            o_ref[rows, hs] = (y.T * g_m * _sigmoid(mo_ref[rows, hs])).astype(o_ref.dtype)

    if emit_state:
        for (bi, d, h) in chains:
            c_fin, n_fin, m_fin = state[bi, d, h]
            c1_ref[bi, d, h] = c_fin
            n1_ref[bi, d, h:h + 1, :] = n_fin
            m1_ref[bi, d:d + 1, h:h + 1] = m_fin


def _mlstm_call(mq, mk, vt, mo, gates_t, g_m, state, alias_prev, st, *, layer, latent):
    if latent:
        batch, seq, nb, row0 = st.dec_batch, st.dec_seq, 1, st.n_ctx
    else:
        batch, seq, nb, row0 = st.batch, st.seq, 4, 0
    emit_state = not latent
    rows = nb * seq
    b0 = row0 // rows
    tok_spec = pl.BlockSpec((rows, W_MLSTM), lambda b: (b0 + b, 0))
    in_specs = [tok_spec, tok_spec,
                pl.BlockSpec((W_MLSTM, rows), lambda b: (0, b0 + b)),
                tok_spec,
                pl.BlockSpec((N_GATE, rows), lambda b: (0, b0 + b)),
                pl.BlockSpec((None, 1, DH_MLSTM), lambda b: (layer, 0, 0))]
    args = [mq, mk, vt, mo, gates_t, g_m]
    state_specs = [
        pl.BlockSpec((nb, None, 2, N_MLSTM, DH_MLSTM, DH_MLSTM), lambda b: (b, layer, 0, 0, 0, 0)),
        pl.BlockSpec((nb, None, 2, N_MLSTM, DH_MLSTM), lambda b: (b, layer, 0, 0, 0)),
        pl.BlockSpec((nb, None, 2, N_MLSTM), lambda b: (b, layer, 0, 0)),
    ]
    if state is not None:
        in_specs += state_specs
        args += list(state)
    alias_in = list(alias_prev) if alias_prev is not None else []
    n_in = len(args)
    in_specs += [pl.BlockSpec(memory_space=pl.ANY)] * len(alias_in)
    args += alias_in
    out_specs = [tok_spec]
    out_shape = [jax.ShapeDtypeStruct((st.n_all, W_MLSTM), BF16)]
    if emit_state:
        out_specs += state_specs
        out_shape += [
            jax.ShapeDtypeStruct((batch, DEPTH, 2, N_MLSTM, DH_MLSTM, DH_MLSTM), F32),
            jax.ShapeDtypeStruct((batch, DEPTH, 2, N_MLSTM, DH_MLSTM), F32),
            jax.ShapeDtypeStruct((batch, DEPTH, 2, N_MLSTM), F32),
        ]
    first_out = 1 if emit_state else 0
    return pl.pallas_call(
        functools.partial(_mlstm_kernel, seq=seq, nb=nb, seeded=state is not None, emit_state=emit_state,
                          n_alias=len(alias_in)),
        grid=(batch // nb,),
        in_specs=in_specs,
        out_specs=out_specs,
        out_shape=out_shape,
        input_output_aliases={n_in + j: first_out + j for j in range(len(alias_in))},
        compiler_params=_cparams(1),
        name="mlstm",
    )(*args)


def _merge_kernel(x_ref, a_ref, f_ref, m_ref, mod_ref, g_ref, wg_ref, wa_ref, wf_ref, wm_ref, wo_ref, o_ref):
    x = x_ref[...]
    mod = mod_ref[...]
    u = _modulated_norm(x, g_ref[...], mod, 3).astype(BF16)
    merged = None
    for j, (br_ref, w_ref) in enumerate(((a_ref, wa_ref), (f_ref, wf_ref), (m_ref, wm_ref))):
        gate = _sigmoid(jnp.dot(u, wg_ref[:, j * D_MODEL:(j + 1) * D_MODEL], preferred_element_type=F32))
        term = gate * _bdot(br_ref[...], w_ref[...])
        merged = term if merged is None else merged + term
    out = _bdot(merged, wo_ref[...])
    o_ref[...] = x + mod[5:6] * out


def _merge_call(x, a, f, m, mods, g_norm, wg, wa, wf, wm, wo, st, *, layer):
    tm = TM_WIDE
    n = st.n_all
    br_spec = pl.BlockSpec((tm, W_ATTN), lambda i: (i, 0))
    return pl.pallas_call(
        _merge_kernel,
        grid=(n // tm,),
        in_specs=[
            pl.BlockSpec((tm, D_MODEL), lambda i: (i, 0)),
            br_spec, br_spec, br_spec,
            st.mod_spec(tm),
            pl.BlockSpec((None, None, 1, D_MODEL), lambda i: (layer, 1, 0, 0)),
            _layer_spec((D_MODEL, 3 * D_MODEL), layer),
            _layer_spec((W_ATTN, D_MODEL), layer), _layer_spec((W_FOUR, D_MODEL), layer),
            _layer_spec((W_MLSTM, D_MODEL), layer),
            _layer_spec((D_MODEL, D_MODEL), layer),
        ],
        out_specs=pl.BlockSpec((tm, D_MODEL), lambda i: (i, 0)),
        out_shape=jax.ShapeDtypeStruct((n, D_MODEL), F32),
        compiler_params=_cparams(1),
        name="merge",
    )(x, a, f, m, mods, g_norm, wg, wa, wf, wm, wo)


def _rope_tables(n_tok):
    tok = np.arange(n_tok)
    inv = ROPE_BASE ** (-np.arange(ROPE_AXIS_PAIRS, dtype=np.float32) / ROPE_AXIS_PAIRS)
    ang = np.concatenate([(tok // GRID_W).astype(np.float32)[:, None] * inv,
                          (tok % GRID_W).astype(np.float32)[:, None] * inv], axis=-1).astype(np.float32)
    c, s = np.cos(ang), np.sin(ang)
    cos_t = np.concatenate([c, c, c, c], axis=-1)
    sin_t = np.concatenate([-s, s, -s, s], axis=-1)
    return jnp.asarray(cos_t, F32), jnp.asarray(sin_t, F32)


def kernel(x_prompt, x_sample, cache_k, cache_v, state_C, state_n, state_m, c, c_ctx, w_ada, b_ada, g_norm,
           w_ffn1_in, w_ffn1_out, w_ffn2_in, w_ffn2_out, w_in, b_mgate, attn_lambda, g_attn_sub, g_mlstm,
           w_branch_gate, w_br_attn, w_br_four, w_br_mlstm, w_out, g_final):
    batch, seq, _ = x_prompt.shape
    dec_batch, dec_seq, _ = x_sample.shape
    st = _Stream(batch, seq, dec_batch, dec_seq)
    cond = jnp.zeros((N_COND, D_MODEL), F32).at[0].set(c_ctx).at[1:1 + dec_batch].set(c)
    mods_all = _ada_call(cond, w_ada, b_ada).reshape(DEPTH, N_COND, N_MOD, D_MODEL)

    b_gate_t = b_mgate[:, :, None]
    g_norm4 = g_norm[:, :, None, :]
    g_fin = g_final[None, :]
    ffn_w = ((w_ffn1_in.astype(BF16), w_ffn1_out.astype(BF16)), (w_ffn2_in.astype(BF16), w_ffn2_out.astype(BF16)))
    w_t = jnp.swapaxes(w_in, 1, 2).astype(BF16)
    w_bg, w_ba, w_bf, w_bm, w_o = (t.astype(BF16) for t in
                                   (w_branch_gate, w_br_attn, w_br_four, w_br_mlstm, w_out))
    g_sub3 = g_attn_sub[:, None, :]
    g_m3 = g_mlstm[:, None, :]
    cos_t, sin_t = _rope_tables(dec_seq)
    lat_state = (state_C, state_n, state_m)

    xs = (x_prompt.reshape(batch * seq, D_MODEL), x_sample.reshape(dec_batch * dec_seq, D_MODEL))
    kv, states = None, None
    for l in range(DEPTH):
        mods = mods_all[l]
        lam_init = 0.8 - 0.6 * math.exp(-0.3 * l)
        (x,) = _ffn_call(xs, mods, g_norm4, g_fin, *ffn_w[0], st, layer=l, sub=0, final=False)
        q, k, ck, cv, zf, mq, mk, mo, vat, vt, gates_t = _proj_call(
            x, mods, g_norm4, w_t, b_gate_t, cos_t, sin_t, kv, st, layer=l)
        kv = (ck, cv)
        a = _attn_call(q, k, vat, None, attn_lambda, g_sub3, None, st, layer=l, lam_init=lam_init,
                       latent=False)
        a = _attn_call(q, k, vat, (cache_k, cache_v), attn_lambda, g_sub3, a, st, layer=l,
                       lam_init=lam_init, latent=True)
        f = _four_call(zf, None, st, latent=False)
        f = _four_call(zf, f, st, latent=True)
        m, *states = _mlstm_call(mq, mk, vt, mo, gates_t, g_m3, None, states, st, layer=l, latent=False)
        (m,) = _mlstm_call(mq, mk, vt, mo, gates_t, g_m3, lat_state, (m,), st, layer=l, latent=True)
        x = _merge_call(x, a, f, m, mods, g_norm4, w_bg, w_ba, w_bf, w_bm, w_o, st, layer=l)
        xs = tuple(_ffn_call((x,), mods, g_norm4, g_fin, *ffn_w[1], st, layer=l, sub=2,
                             final=(l == DEPTH - 1)))
    y_prompt, y_sample = xs
    return (y_prompt.reshape(batch, seq, D_MODEL), y_sample.reshape(dec_batch, dec_seq, D_MODEL),
            *kv, *states)
```

```python
import functools
import math

import numpy as np
import jax
import jax.numpy as jnp
from jax import lax
from jax.experimental import pallas as pl
from jax.experimental.pallas import tpu as pltpu

D_MODEL = 1024
DEPTH = 2
GRID_W = 64
N_ATTN = 4
DH_ATTN = 64
DV_ATTN = 2 * DH_ATTN
W_ATTN = N_ATTN * DV_ATTN
N_FOUR = 4
DG_FOUR = 128
W_FOUR = N_FOUR * DG_FOUR
N_MLSTM = 4
DH_MLSTM = 128
W_MLSTM = N_MLSTM * DH_MLSTM
N_GATE = 4 * N_MLSTM
LANES = 128
SUBLANES = 16
MV_LO = 3 * W_ATTN + W_FOUR + 2 * W_MLSTM
MV_HI = MV_LO + W_MLSTM
P_MAIN = MV_HI + W_MLSTM
D_FF = 2816
N_MOD = 9
N_COND = 8
ROPE_BASE = 10000.0
ROPE_AXIS_PAIRS = DH_ATTN // 4
ATTN_SCALE = DH_ATTN ** -0.5
LOG2E = 1.4426950408889634
LN2 = 0.6931471805599453
MLSTM_K_SCALE = DH_MLSTM ** -0.5
EPS = 1e-6
MLSTM_CHUNK = 256
VMEM_LIMIT = 56 * 1024 * 1024
TM_WIDE = 1024
TM_PROJ = 512

F32 = jnp.float32
BF16 = jnp.bfloat16


def _cparams(n_grid):
    return pltpu.CompilerParams(dimension_semantics=("arbitrary",) * n_grid,
                                vmem_limit_bytes=VMEM_LIMIT)


def _bdot(a, b):
    return jnp.dot(a.astype(BF16), b.astype(BF16), preferred_element_type=F32)


def _bdot_nt(a, b):
    return lax.dot_general(a.astype(BF16), b.astype(BF16), (((1,), (1,)), ((), ())),
                           preferred_element_type=F32)


def _sigmoid(x):
    return 1.0 / (1.0 + jnp.exp(-x))


def _log_sigmoid(x):
    return jnp.minimum(x, 0.0) - jnp.log1p(jnp.exp(-jnp.abs(x)))


def _rms(x, g):
    return x * lax.rsqrt(jnp.mean(x * x, axis=-1, keepdims=True) + EPS) * g


def _modulated_norm(x, g, mod, base):
    return _rms(x, g) * (1.0 + mod[base + 1:base + 2]) + mod[base:base + 1]


def _layer_spec(shape, layer):
    if layer is None:
        return pl.BlockSpec(shape, lambda *_: (0,) * len(shape), pipeline_mode=pl.Buffered(1))
    return pl.BlockSpec((None,) + shape, lambda *_: (layer,) + (0,) * len(shape),
                        pipeline_mode=pl.Buffered(1))


class _CastJobs:
    def __init__(self, jobs, steps, step_index):
        self.jobs, self.steps, self.step_index = list(jobs), steps, step_index

    def __len__(self):
        return len(self.jobs)

    def _rows(self, w):
        rows = w.shape[1] // self.steps
        if rows * self.steps != w.shape[1] or rows % SUBLANES:
            raise ValueError("weight rows do not split into bf16 row tiles over the grid")
        return rows

    def in_specs(self):
        return [pl.BlockSpec((None, self._rows(w), w.shape[2]),
                             lambda *g, layer=layer: (layer, self.step_index(*g), 0))
                for w, layer in self.jobs]

    def out_specs(self):
        return [pl.BlockSpec((self._rows(w), w.shape[2]), lambda *g: (self.step_index(*g), 0))
                for w, _ in self.jobs]

    def out_shapes(self):
        return [jax.ShapeDtypeStruct(w.shape[1:], BF16) for w, _ in self.jobs]

    def args(self):
        return [w for w, _ in self.jobs]

    def wrap(self, body, n_in, n_out):
        n = len(self.jobs)
        if n == 0:
            return body

        def kernel(*refs):
            ins, cast_in = refs[:n_in], refs[n_in:n_in + n]
            outs = refs[n_in + n:n_in + n + n_out]
            cast_out = refs[n_in + n + n_out:n_in + 2 * n + n_out]
            for src, dst in zip(cast_in, cast_out):
                dst[...] = src[...].astype(dst.dtype)
            body(*ins, *outs, *refs[n_in + 2 * n + n_out:])

        return kernel


class _Stream:
    def __init__(self, batch, seq, dec_batch, dec_seq):
        self.batch, self.seq, self.dec_batch, self.dec_seq = batch, seq, dec_batch, dec_seq
        self.n_ctx = batch * seq
        self.n_all = self.n_ctx + dec_batch * dec_seq

    def ctx_tiles(self, tm):
        if self.n_ctx % tm or self.dec_seq % tm:
            raise ValueError("row tile must divide the context rows and one latent sequence")
        return self.n_ctx // tm

    def mod_spec(self, tm):
        n_pt, tps = self.ctx_tiles(tm), self.dec_seq // tm
        return pl.BlockSpec((None, N_MOD, D_MODEL),
                            lambda i: (jnp.where(i < n_pt, 0, 1 + (i - n_pt) // tps), 0, 0))


def _ada_kernel(c_ref, w_ref, b_ref, o_ref):
    c = c_ref[...]
    s = c * _sigmoid(c)
    o_ref[...] = _bdot(s, w_ref[...]) + b_ref[...]


def _ada_call(cond, w_ada, b_ada):
    tn = 2304
    n_out = N_MOD * D_MODEL
    return pl.pallas_call(
        _ada_kernel,
        grid=(DEPTH, n_out // tn),
        in_specs=[
            pl.BlockSpec((N_COND, D_MODEL), lambda l, j: (0, 0)),
            pl.BlockSpec((None, D_MODEL, tn), lambda l, j: (l, 0, j)),
            pl.BlockSpec((None, 1, tn), lambda l, j: (l, 0, j)),
        ],
        out_specs=pl.BlockSpec((None, N_COND, tn), lambda l, j: (l, 0, j)),
        out_shape=jax.ShapeDtypeStruct((DEPTH, N_COND, n_out), F32),
        compiler_params=_cparams(2),
        name="adaln",
    )(cond, w_ada, b_ada.reshape(DEPTH, 1, n_out))


FF_CHUNKS = (768, 768, 768, 512)


def _ffn_kernel(*refs, base, n_x, final, n_pt):
    x_refs = refs[:n_x]
    mod_ref, g_ref, gf_ref, win_ref, wout_ref = refs[n_x:n_x + 5]
    o_refs = refs[n_x + 5:]
    i = pl.program_id(0)
    if n_x == 2:
        x = jnp.where(i < n_pt, x_refs[0][...], x_refs[1][...])
    else:
        x = x_refs[0][...]
    mod = mod_ref[...]
    u = _modulated_norm(x, g_ref[...], mod, base).astype(BF16)
    y = None
    lo = 0
    for width in FF_CHUNKS:
        a = jnp.dot(u, win_ref[:, lo:lo + width], preferred_element_type=F32)
        g = jnp.dot(u, win_ref[:, D_FF + lo:D_FF + lo + width], preferred_element_type=F32)
        hh = (a * _sigmoid(a) * g).astype(BF16)
        part = jnp.dot(hh, wout_ref[lo:lo + width, :], preferred_element_type=F32)
        y = part if y is None else y + part
        lo += width
    xn = x + 0.5 * mod[base + 2:base + 3] * y
    if not final:
        o_refs[0][...] = xn
    else:
        xn = _rms(xn, gf_ref[...])

        @pl.when(i < n_pt)
        def _():
            o_refs[0][...] = xn

        @pl.when(i >= n_pt)
        def _():
            o_refs[1][...] = xn


def _ffn_call(xs, mods, g_norm, g_final, w_in, w_out, st, *, layer, sub, final):
    tm = TM_WIDE
    n_pt = st.ctx_tiles(tm)
    n_tiles = st.n_all // tm
    ctx_spec = pl.BlockSpec((tm, D_MODEL), lambda i: (jnp.minimum(i, n_pt - 1), 0))
    lat_spec = pl.BlockSpec((tm, D_MODEL), lambda i: (jnp.maximum(i - n_pt, 0), 0))
    all_spec = pl.BlockSpec((tm, D_MODEL), lambda i: (i, 0))
    x_specs = [ctx_spec, lat_spec] if len(xs) == 2 else [all_spec]
    if final:
        out_specs = [ctx_spec, lat_spec]
        out_shape = [jax.ShapeDtypeStruct((st.n_ctx, D_MODEL), F32),
                     jax.ShapeDtypeStruct((st.n_all - st.n_ctx, D_MODEL), F32)]
    else:
        out_specs = [all_spec]
        out_shape = [jax.ShapeDtypeStruct((st.n_all, D_MODEL), F32)]
    return pl.pallas_call(
        functools.partial(_ffn_kernel, base=3 * sub, n_x=len(xs), final=final, n_pt=n_pt),
        grid=(n_tiles,),
        in_specs=x_specs + [
            st.mod_spec(tm),
            pl.BlockSpec((None, None, 1, D_MODEL), lambda i: (layer, sub, 0, 0)),
            pl.BlockSpec((1, D_MODEL), lambda i: (0, 0)),
            _layer_spec((D_MODEL, 2 * D_FF), None),
            _layer_spec((D_FF, D_MODEL), None),
        ],
        out_specs=out_specs,
        out_shape=out_shape,
        compiler_params=_cparams(1),
        name="ffn",
    )(*xs, mods, g_norm, g_final, w_in, w_out)


def _swap32(x):
    lane = lax.broadcasted_iota(jnp.int32, x.shape, 1)
    return jnp.where((lane & (DH_ATTN // 2)) == 0,
                     pltpu.roll(x, LANES - DH_ATTN // 2, 1), pltpu.roll(x, DH_ATTN // 2, 1))


def _proj_kernel(*refs, n_alias, n_pt, seqs_per_tile, seq):
    (x_ref, mod_ref, g_ref, wt_ref, bgt_ref, cos_ref, sin_ref) = refs[:7]
    (q_ref, k_ref, ck_ref, cv_ref, zf_ref, mq_ref, mk_ref, mo_ref, vat_ref, vt_ref, gt_ref) = refs[7 + n_alias:]
    i = pl.program_id(0)
    is_latent = i >= n_pt
    x = x_ref[...]
    u = _modulated_norm(x, g_ref[...], mod_ref[...], 3).astype(BF16)
    za = _bdot_nt(u, wt_ref[:3 * W_ATTN, :])
    zb = _bdot_nt(u, wt_ref[3 * W_ATTN:MV_LO, :])
    zo = _bdot_nt(u, wt_ref[MV_HI:P_MAIN, :])
    cos_t = cos_ref[...]
    sin_t = sin_ref[...]
    plain_k = []
    for h in range(N_ATTN):
        zq = za[:, h * DV_ATTN:(h + 1) * DV_ATTN]
        zk = za[:, W_ATTN + h * DV_ATTN:W_ATTN + (h + 1) * DV_ATTN]
        plain_k.append(zk)
        q_ref[h] = jnp.where(is_latent, zq * cos_t + _swap32(zq) * sin_t, zq).astype(q_ref.dtype)
        k_ref[h] = jnp.where(is_latent, zk * cos_t + _swap32(zk) * sin_t, zk).astype(k_ref.dtype)

    @pl.when(i < n_pt)
    def _():
        for h in range(N_ATTN):
            zv = za[:, 2 * W_ATTN + h * DV_ATTN:2 * W_ATTN + (h + 1) * DV_ATTN]
            for s in range(seqs_per_tile):
                rs = slice(s * seq, (s + 1) * seq)
                ck_ref[s, h] = plain_k[h][rs]
                cv_ref[s, h] = zv[rs]

    zf_ref[...] = zb[:, :W_FOUR].astype(zf_ref.dtype)
    off = W_FOUR
    mq_ref[...] = zb[:, off:off + W_MLSTM].astype(mq_ref.dtype)
    mk_ref[...] = (zb[:, off + W_MLSTM:off + 2 * W_MLSTM] * MLSTM_K_SCALE).astype(mk_ref.dtype)
    mo_ref[...] = zo
    vat_ref[...] = _bdot_nt(wt_ref[2 * W_ATTN:3 * W_ATTN, :], u).astype(vat_ref.dtype)
    vt_ref[...] = _bdot_nt(wt_ref[MV_LO:MV_HI, :], u)
    gpt = _bdot_nt(wt_ref[P_MAIN:, :], u) + bgt_ref[...]
    sub = lax.broadcasted_iota(jnp.int32, gpt.shape, 0)
    gt_ref[...] = jnp.where((sub & N_MLSTM) != 0, _log_sigmoid(gpt), gpt)


def _proj_call(x, mods, g_norm, w_t, bgt, cos_t, sin_t, kv_prev, st, *, layer):
    tm = TM_PROJ
    n_pt = st.ctx_tiles(tm)
    if tm % st.seq:
        raise ValueError("a context row tile must hold whole sequences")
    seqs_per_tile, tps = tm // st.seq, st.dec_seq // tm
    n = st.n_all
    head_shape = jax.ShapeDtypeStruct((N_ATTN, n, DV_ATTN), BF16)
    head_spec = pl.BlockSpec((N_ATTN, tm, DV_ATTN), lambda i: (0, i, 0))
    kv_shape = jax.ShapeDtypeStruct((st.batch, DEPTH, N_ATTN, st.seq, DV_ATTN), F32)
    kv_spec = pl.BlockSpec((seqs_per_tile, None, N_ATTN, st.seq, DV_ATTN),
                           lambda i: (jnp.minimum(i, n_pt - 1), layer, 0, 0, 0))
    tok_shape = jax.ShapeDtypeStruct((n, W_ATTN), F32)
    tok_bf16 = jax.ShapeDtypeStruct((n, W_ATTN), BF16)
    tok_spec = pl.BlockSpec((tm, W_ATTN), lambda i: (i, 0))
    rope_spec = pl.BlockSpec((tm, LANES), lambda i: (jnp.maximum(i - n_pt, 0) % tps, 0))
    alias_in = list(kv_prev) if kv_prev is not None else []
    n_in = 7
    return pl.pallas_call(
        functools.partial(_proj_kernel, n_alias=len(alias_in), n_pt=n_pt, seqs_per_tile=seqs_per_tile,
                          seq=st.seq),
        grid=(n // tm,),
        in_specs=[
            pl.BlockSpec((tm, D_MODEL), lambda i: (i, 0)),
            st.mod_spec(tm),
            pl.BlockSpec((None, None, 1, D_MODEL), lambda i: (layer, 1, 0, 0)),
            _layer_spec((w_t.shape[1], D_MODEL), layer),
            _layer_spec((N_GATE, 1), layer),
            rope_spec, rope_spec,
        ] + [pl.BlockSpec(memory_space=pl.ANY)] * len(alias_in),
        out_specs=[head_spec, head_spec, kv_spec, kv_spec, tok_spec, tok_spec, tok_spec, tok_spec,
                   pl.BlockSpec((W_ATTN, tm), lambda i: (0, i)),
                   pl.BlockSpec((W_MLSTM, tm), lambda i: (0, i)),
                   pl.BlockSpec((N_GATE, tm), lambda i: (0, i))],
        out_shape=[head_shape, head_shape, kv_shape, kv_shape, tok_bf16, tok_bf16, tok_bf16, tok_shape,
                   jax.ShapeDtypeStruct((W_ATTN, n), BF16),
                   jax.ShapeDtypeStruct((W_MLSTM, n), F32),
                   jax.ShapeDtypeStruct((N_GATE, n), F32)],
        input_output_aliases={n_in + j: 2 + j for j in range(len(alias_in))},
        compiler_params=_cparams(1),
        name="mixer_proj",
    )(x, mods, g_norm, w_t, bgt, cos_t, sin_t, *alias_in)


def _attn_kernel(*refs, lam_init, cached, nb, tq, seq):
    q_ref, k_ref, vt_ref = refs[:3]
    pos = 3
    if cached:
        ck_ref, cv_ref = refs[pos:pos + 2]
        pos += 2
    lam_ref, g_ref = refs[pos:pos + 2]
    o_ref = refs[-1]
    lp = lam_ref[...]
    lam = (jnp.exp(jnp.sum(lp[0:1] * lp[1:2], axis=-1, keepdims=True))
           - jnp.exp(jnp.sum(lp[2:3] * lp[3:4], axis=-1, keepdims=True)) + lam_init)
    g_sub = g_ref[...]
    lane = lax.broadcasted_iota(jnp.int32, (tq, DV_ATTN), 1)
    heads = range(N_ATTN)
    results = []
    for bi in range(nb):
        ks = slice(bi * seq, (bi + 1) * seq)
        keys = [[k_ref[h, ks] for h in heads]]
        vals_t = [[vt_ref[h * DV_ATTN:(h + 1) * DV_ATTN, ks] for h in heads]]
        if cached:
            keys.insert(0, [ck_ref[h].astype(BF16) for h in heads])
            vals_t.insert(0, [cv_ref[h].T.astype(BF16) for h in heads])
        qh = [q_ref[h, bi * tq:(bi + 1) * tq].astype(F32) * (ATTN_SCALE * LOG2E) for h in heads]
        exps, dens = [], []
        for first_map in (True, False):
            qm = [jnp.where((lane < DH_ATTN) == first_map, q, 0.0).astype(BF16) for q in qh]
            s = [jnp.concatenate([_bdot_nt(kg[h], qm[h]) for h in heads], axis=1) for kg in keys]
            m = functools.reduce(jnp.maximum, [jnp.max(si, axis=0, keepdims=True) for si in s])
            e = [jnp.exp2(si - m) for si in s]
            exps.append(e)
            dens.append(functools.reduce(jnp.add, [jnp.sum(ei, axis=0, keepdims=True) for ei in e]))
        ratio = lam * dens[0] / dens[1]
        inv = 1.0 / dens[0]
        probs = [(e1 - e2 * ratio).astype(BF16) for e1, e2 in zip(*exps)]
        for h in heads:
            cols = slice(h * tq, (h + 1) * tq)
            o_t = None
            for pj, vg in zip(probs, vals_t):
                part = jnp.dot(vg[h], pj[:, cols], preferred_element_type=F32)
                o_t = part if o_t is None else o_t + part
            o_t = o_t * inv[:, cols]
            y_t = o_t * lax.rsqrt(jnp.mean(o_t * o_t, axis=0, keepdims=True) + EPS)
            results.append((bi, h, (y_t.T * g_sub * (1.0 - lam_init)).astype(o_ref.dtype)))
    for bi, h, y in results:
        o_ref[bi * tq:(bi + 1) * tq, h * DV_ATTN:(h + 1) * DV_ATTN] = y


def _attn_call(q, k, v_t, cache, lam_p, g_sub, out_prev, cast_w, st, *, layer, lam_init, latent):
    n = st.n_all
    if latent:
        batch, seq, nb, tq = st.dec_batch, st.dec_seq, 1, st.dec_seq // 2
        row0 = st.n_ctx
    else:
        batch, seq, nb, tq = st.batch, st.seq, 4, st.seq
        row0 = 0
    nq = seq // tq
    if nq > 1 and nb > 1:
        raise ValueError("query blocks of several sequences are not contiguous rows")
    q0, s0 = row0 // (nb * tq), row0 // (nb * seq)
    in_specs = [pl.BlockSpec((N_ATTN, nb * tq, DV_ATTN), lambda b, i: (0, q0 + b * nq + i, 0)),
                pl.BlockSpec((N_ATTN, nb * seq, DV_ATTN), lambda b, i: (0, s0 + b, 0)),
                pl.BlockSpec((W_ATTN, nb * seq), lambda b, i: (0, s0 + b))]
    args = [q, k, v_t]
    if cache is not None:
        past = cache[0].shape[3]
        c_spec = pl.BlockSpec((None, None, N_ATTN, past, DV_ATTN), lambda b, i: (b, layer, 0, 0, 0))
        in_specs += [c_spec, c_spec]
        args += list(cache)
    in_specs += [pl.BlockSpec((None, 4, DH_ATTN), lambda b, i: (layer, 0, 0)),
                 pl.BlockSpec((None, 1, DV_ATTN), lambda b, i: (layer, 0, 0))]
    args += [lam_p, g_sub]
    aliases = {}
    if out_prev is not None:
        aliases = {len(args): 0}
        in_specs.append(pl.BlockSpec(memory_space=pl.ANY))
        args.append(out_prev)
    casts = _CastJobs(cast_w, (batch // nb) * nq, lambda b, i: b * nq + i)
    body = functools.partial(_attn_kernel, lam_init=lam_init, cached=cache is not None, nb=nb, tq=tq, seq=seq)
    return pl.pallas_call(
        casts.wrap(body, len(args), 1),
        grid=(batch // nb, nq),
        in_specs=in_specs + casts.in_specs(),
        out_specs=[pl.BlockSpec((nb * tq, W_ATTN), lambda b, i: (q0 + b * nq + i, 0))] + casts.out_specs(),
        out_shape=[jax.ShapeDtypeStruct((n, W_ATTN), BF16)] + casts.out_shapes(),
        input_output_aliases=aliases,
        compiler_params=_cparams(2),
        name="diff_attn",
    )(*args, *casts.args())


def _dft_tables(seq):
    def cs(n):
        j = np.arange(n)
        ang = 2.0 * np.pi * ((j[:, None] * j[None, :]) % n) / n
        return np.cos(ang) / math.sqrt(n), np.sin(ang) / math.sqrt(n)

    cd, sd = cs(DG_FOUR)
    ct, st = cs(seq)
    w_d = jnp.asarray(np.concatenate([cd, sd], axis=1), F32)
    return w_d.astype(BF16), jnp.asarray(ct, F32).astype(BF16), jnp.asarray(-st, F32).astype(BF16)


def _four_kernel(*refs, nb, seq):
    z_ref, wd_ref, ct_ref, st_ref = refs[:4]
    o_ref = refs[-1]
    y_cos, y_sin = [], []
    for gidx in range(N_FOUR):
        y = jnp.dot(z_ref[:, gidx * DG_FOUR:(gidx + 1) * DG_FOUR], wd_ref[...], preferred_element_type=F32)
        y_cos.append(y[:, :DG_FOUR].astype(BF16))
        y_sin.append(y[:, DG_FOUR:].astype(BF16))
    y_cos = jnp.concatenate(y_cos, axis=1)
    y_sin = jnp.concatenate(y_sin, axis=1)
    for s in range(nb):
        rows = slice(s * seq, (s + 1) * seq)
        o_ref[rows, :] = (jnp.dot(ct_ref[...], y_cos[rows], preferred_element_type=F32)
                          + jnp.dot(st_ref[...], y_sin[rows], preferred_element_type=F32)).astype(o_ref.dtype)


def _four_call(zf, out_prev, cast_w, st, *, latent):
    if latent:
        batch, seq, nb, row0 = st.dec_batch, st.dec_seq, 1, st.n_ctx
    else:
        batch, seq, nb, row0 = st.batch, st.seq, 4, 0
    b0 = row0 // (nb * seq)
    w_d, ct, s_t = _dft_tables(seq)
    row_spec = pl.BlockSpec((nb * seq, W_FOUR), lambda b: (b0 + b, 0))
    in_specs = [row_spec,
                pl.BlockSpec((DG_FOUR, 2 * DG_FOUR), lambda b: (0, 0)),
                pl.BlockSpec((seq, seq), lambda b: (0, 0)),
                pl.BlockSpec((seq, seq), lambda b: (0, 0))]
    args = [zf, w_d, ct, s_t]
    aliases = {}
    if out_prev is not None:
        aliases = {len(args): 0}
        in_specs.append(pl.BlockSpec(memory_space=pl.ANY))
        args.append(out_prev)
    casts = _CastJobs(cast_w, batch // nb, lambda b: b)
    return pl.pallas_call(
        casts.wrap(functools.partial(_four_kernel, nb=nb, seq=seq), len(args), 1),
        grid=(batch // nb,),
        in_specs=in_specs + casts.in_specs(),
        out_specs=[row_spec] + casts.out_specs(),
        out_shape=[jax.ShapeDtypeStruct((st.n_all, W_FOUR), BF16)] + casts.out_shapes(),
        input_output_aliases=aliases,
        compiler_params=_cparams(1),
        name="fourier_mix",
    )(*args, *casts.args())


def _mlstm_kernel(*refs, seq, nb, seeded, emit_state, n_alias):
    mq_ref, mk_ref, vt_ref, mo_ref, gt_ref, g_ref = refs[:6]
    pos = 6
    if seeded:
        c0_ref, n0_ref, m0_ref = refs[pos:pos + 3]
        pos += 3
    pos += n_alias
    o_ref = refs[pos]
    if emit_state:
        c1_ref, n1_ref, m1_ref = refs[pos + 1:pos + 4]

    L = min(MLSTM_CHUNK, seq)
    nc = seq // L
    s_idx = lax.broadcasted_iota(jnp.int32, (L, L), 0)
    t_idx = lax.broadcasted_iota(jnp.int32, (L, L), 1)
    before = (s_idx <= t_idx, s_idx >= t_idx)
    tri = (jnp.where(before[1], 1.0, 0.0).astype(BF16),
           jnp.where(before[0], 1.0, 0.0).astype(BF16))
    chains = [(bi, d, h) for bi in range(nb) for d in range(2) for h in range(N_MLSTM)]

    state = {}
    for (bi, d, h) in chains:
        if seeded:
            state[bi, d, h] = (c0_ref[bi, d, h], n0_ref[bi, d, h:h + 1, :], m0_ref[bi, d:d + 1, h:h + 1])
        else:
            state[bi, d, h] = (None, None, jnp.zeros((1, 1), F32))

    h_t = {}
    for c in range(nc):
        terms = {}
        for bi in range(nb):
            for d in range(2):
                r0 = bi * seq + (c if d == 0 else nc - 1 - c) * L
                g_t = gt_ref[:, r0:r0 + L]
                hi = g_t.astype(BF16)
                rem = g_t - hi.astype(F32)
                mid = rem.astype(BF16)
                lo = (rem - mid.astype(F32)).astype(BF16)
                pieces = _bdot_nt(jnp.concatenate([hi, mid, lo], axis=0), tri[d])
                cum_t = pieces[0:N_GATE] + pieces[N_GATE:2 * N_GATE] + pieces[2 * N_GATE:]
                c_t = (g_t - pltpu.roll(cum_t, N_GATE - N_MLSTM, 0)) * LOG2E
                col = jnp.concatenate([c_t, jnp.zeros((LANES - N_GATE, L), F32)], axis=0).T
                terms[bi, d] = (r0, col, g_t, cum_t)

        new_state = {}
        need_update = emit_state or c < nc - 1
        group = [(d, h) for d in range(2) for h in range(N_MLSTM)]
        for bi in range(nb):
            has_state = state[bi, 0, 0][0] is not None
            qs, ks, vts, cbs, s0s, i_rows, b_rows, m_prevs = [], [], [], [], [], [], [], []
            for d, h in group:
                r0, col, g_t, cum_t = terms[bi, d]
                ci = 2 * d * N_MLSTM + h
                cf = ci + N_MLSTM
                hs = slice(h * DH_MLSTM, (h + 1) * DH_MLSTM)
                qs.append(mq_ref[r0:r0 + L, hs])
                ks.append(mk_ref[r0:r0 + L, hs])
                vts.append(vt_ref[hs, r0:r0 + L])
                cbs.append(jnp.where(before[d], col[:, ci:ci + 1], -jnp.inf))
                s0s.append(_bdot_nt(ks[-1], qs[-1]))
                i_rows.append(g_t[ci:ci + 1, :])
                b_rows.append(cum_t[cf:cf + 1, :])
                m_prevs.append(jnp.broadcast_to(state[bi, d, h][2], (1, L)))
            cb = jnp.concatenate(cbs, axis=1)
            i_row, b_row = jnp.concatenate(i_rows, axis=1), jnp.concatenate(b_rows, axis=1)
            m_prev = jnp.concatenate(m_prevs, axis=1)
            m2_prev = m_prev * LOG2E
            m2_row = jnp.maximum(jnp.max(cb, axis=0, keepdims=True), m2_prev)
            s_t = jnp.concatenate(s0s, axis=1) * jnp.exp2(cb - m2_row)
            den = jnp.sum(s_t, axis=0, keepdims=True)
            s_bf = s_t.astype(BF16)
            m_t = b_row + m2_row * LN2
            floor = jnp.exp(-m_t)
            if has_state:
                sp = jnp.exp2(m2_prev - m2_row)
            if need_update:
                tot, new = [], []
                for j, (d, h) in enumerate(group):
                    last = j * L + (L - 1 if d == 0 else 0)
                    tot.append(jnp.broadcast_to(b_row[:, last:last + 1], (1, L)))
                    new.append(jnp.broadcast_to(m_t[:, last:last + 1], (1, L)))
                b_tot, m_new = jnp.concatenate(tot, axis=1), jnp.concatenate(new, axis=1)
                wl = jnp.exp(b_tot + (i_row - b_row) - m_new)
                if has_state:
                    decay = jnp.exp(b_tot + m_prev - m_new)
            for j, (d, h) in enumerate(group):
                cols = slice(j * L, (j + 1) * L)
                c_prev, n_prev, _ = state[bi, d, h]
                num_t = jnp.dot(vts[j].astype(BF16), s_bf[:, cols], preferred_element_type=F32)
                den_j = den[:, cols]
                if has_state:
                    cn = jnp.concatenate([c_prev, jnp.broadcast_to(n_prev, (SUBLANES, DH_MLSTM))], axis=0)
                    cq = _bdot_nt(cn, qs[j])
                    num_t = num_t + sp[:, cols] * cq[:DH_MLSTM]
                    den_j = den_j + sp[:, cols] * cq[DH_MLSTM:DH_MLSTM + 1]
                h_t[bi, d, h, c] = num_t / jnp.maximum(jnp.abs(den_j), floor[:, cols])
                if need_update:
                    wl_j = wl[:, cols]
                    vw = jnp.concatenate([vts[j] * wl_j, jnp.broadcast_to(wl_j, (SUBLANES, L))], axis=0)
                    upd = jnp.dot(vw.astype(BF16), ks[j], preferred_element_type=F32)
                    c_new, n_new = upd[:DH_MLSTM], upd[DH_MLSTM:DH_MLSTM + 1]
                    if has_state:
                        decay_j = decay[:, j * L:j * L + 1]
                        c_new = decay_j * c_prev + c_new
                        n_new = decay_j * n_prev + n_new
                    new_state[bi, d, h] = (c_new, n_new, m_new[:, j * L:j * L + 1])
        state = new_state

    g_m = g_ref[...]
    for bi in range(nb):
        for h in range(N_MLSTM):
            hs = slice(h * DH_MLSTM, (h + 1) * DH_MLSTM)
            fwd = [h_t[bi, 0, h, c] for c in range(nc)]
            bwd = [h_t[bi, 1, h, nc - 1 - c] for c in range(nc)]
            hsum = (fwd[0] if nc == 1 else jnp.concatenate(fwd, axis=1)) \
                + (bwd[0] if nc == 1 else jnp.concatenate(bwd, axis=1))
            y = hsum * lax.rsqrt(jnp.mean(hsum * hsum, axis=0, keepdims=True) + EPS)
            rows = slice(bi * seq, (bi + 1) * seq)
            o_ref[rows, hs] = (y.T * g_m * _sigmoid(mo_ref[rows, hs])).astype(o_ref.dtype)

    if emit_state:
        for (bi, d, h) in chains:
            c_fin, n_fin, m_fin = state[bi, d, h]
            c1_ref[bi, d, h] = c_fin
            n1_ref[bi, d, h:h + 1, :] = n_fin
            m1_ref[bi, d:d + 1, h:h + 1] = m_fin


def _mlstm_call(mq, mk, vt, mo, gates_t, g_m, state, alias_prev, cast_w, st, *, layer, latent):
    if latent:
        batch, seq, nb, row0 = st.dec_batch, st.dec_seq, 1, st.n_ctx
    else:
        batch, seq, nb, row0 = st.batch, st.seq, 4, 0
    emit_state = not latent
    rows = nb * seq
    b0 = row0 // rows
    tok_spec = pl.BlockSpec((rows, W_MLSTM), lambda b: (b0 + b, 0))
    in_specs = [tok_spec, tok_spec,
                pl.BlockSpec((W_MLSTM, rows), lambda b: (0, b0 + b)),
                tok_spec,
                pl.BlockSpec((N_GATE, rows), lambda b: (0, b0 + b)),
                pl.BlockSpec((None, 1, DH_MLSTM), lambda b: (layer, 0, 0))]
    args = [mq, mk, vt, mo, gates_t, g_m]
    state_specs = [
        pl.BlockSpec((nb, None, 2, N_MLSTM, DH_MLSTM, DH_MLSTM), lambda b: (b, layer, 0, 0, 0, 0)),
        pl.BlockSpec((nb, None, 2, N_MLSTM, DH_MLSTM), lambda b: (b, layer, 0, 0, 0)),
        pl.BlockSpec((nb, None, 2, N_MLSTM), lambda b: (b, layer, 0, 0)),
    ]
    if state is not None:
        in_specs += state_specs
        args += list(state)
    alias_in = list(alias_prev) if alias_prev is not None else []
    n_in = len(args)
    in_specs += [pl.BlockSpec(memory_space=pl.ANY)] * len(alias_in)
    args += alias_in
    out_specs = [tok_spec]
    out_shape = [jax.ShapeDtypeStruct((st.n_all, W_MLSTM), BF16)]
    if emit_state:
        out_specs += state_specs
        out_shape += [
            jax.ShapeDtypeStruct((batch, DEPTH, 2, N_MLSTM, DH_MLSTM, DH_MLSTM), F32),
            jax.ShapeDtypeStruct((batch, DEPTH, 2, N_MLSTM, DH_MLSTM), F32),
            jax.ShapeDtypeStruct((batch, DEPTH, 2, N_MLSTM), F32),
        ]
    first_out = 1 if emit_state else 0
    casts = _CastJobs(cast_w, batch // nb, lambda b: b)
    body = functools.partial(_mlstm_kernel, seq=seq, nb=nb, seeded=state is not None, emit_state=emit_state,
                             n_alias=len(alias_in))
    return pl.pallas_call(
        casts.wrap(body, len(args), len(out_specs)),
        grid=(batch // nb,),
        in_specs=in_specs + casts.in_specs(),
        out_specs=out_specs + casts.out_specs(),
        out_shape=out_shape + casts.out_shapes(),
        input_output_aliases={n_in + j: first_out + j for j in range(len(alias_in))},
        compiler_params=_cparams(1),
        name="mlstm",
    )(*args, *casts.args())


def _merge_kernel(x_ref, a_ref, f_ref, m_ref, mod_ref, g_ref, wg_ref, wa_ref, wf_ref, wm_ref, wo_ref, o_ref):
    x = x_ref[...]
    mod = mod_ref[...]
    u = _modulated_norm(x, g_ref[...], mod, 3).astype(BF16)
    merged = None
    for j, (br_ref, w_ref) in enumerate(((a_ref, wa_ref), (f_ref, wf_ref), (m_ref, wm_ref))):
        gate = _sigmoid(jnp.dot(u, wg_ref[:, j * D_MODEL:(j + 1) * D_MODEL], preferred_element_type=F32))
        term = gate * _bdot(br_ref[...], w_ref[...])
        merged = term if merged is None else merged + term
    out = _bdot(merged, wo_ref[...])
    o_ref[...] = x + mod[5:6] * out


def _merge_call(x, a, f, m, mods, g_norm, wg, wa, wf, wm, wo, st, *, layer):
    tm = TM_WIDE
    n = st.n_all
    br_spec = pl.BlockSpec((tm, W_ATTN), lambda i: (i, 0))
    return pl.pallas_call(
        _merge_kernel,
        grid=(n // tm,),
        in_specs=[
            pl.BlockSpec((tm, D_MODEL), lambda i: (i, 0)),
            br_spec, br_spec, br_spec,
            st.mod_spec(tm),
            pl.BlockSpec((None, None, 1, D_MODEL), lambda i: (layer, 1, 0, 0)),
            _layer_spec((D_MODEL, 3 * D_MODEL), None),
            _layer_spec((W_ATTN, D_MODEL), None), _layer_spec((W_FOUR, D_MODEL), None),
            _layer_spec((W_MLSTM, D_MODEL), None),
            _layer_spec((D_MODEL, D_MODEL), None),
        ],
        out_specs=pl.BlockSpec((tm, D_MODEL), lambda i: (i, 0)),
        out_shape=jax.ShapeDtypeStruct((n, D_MODEL), F32),
        compiler_params=_cparams(1),
        name="merge",
    )(x, a, f, m, mods, g_norm, wg, wa, wf, wm, wo)


def _rope_tables(n_tok):
    tok = np.arange(n_tok)
    inv = ROPE_BASE ** (-np.arange(ROPE_AXIS_PAIRS, dtype=np.float32) / ROPE_AXIS_PAIRS)
    ang = np.concatenate([(tok // GRID_W).astype(np.float32)[:, None] * inv,
                          (tok % GRID_W).astype(np.float32)[:, None] * inv], axis=-1).astype(np.float32)
    c, s = np.cos(ang), np.sin(ang)
    cos_t = np.concatenate([c, c, c, c], axis=-1)
    sin_t = np.concatenate([-s, s, -s, s], axis=-1)
    return jnp.asarray(cos_t, F32), jnp.asarray(sin_t, F32)


def kernel(x_prompt, x_sample, cache_k, cache_v, state_C, state_n, state_m, c, c_ctx, w_ada, b_ada, g_norm,
           w_ffn1_in, w_ffn1_out, w_ffn2_in, w_ffn2_out, w_in, b_mgate, attn_lambda, g_attn_sub, g_mlstm,
           w_branch_gate, w_br_attn, w_br_four, w_br_mlstm, w_out, g_final):
    batch, seq, _ = x_prompt.shape
    dec_batch, dec_seq, _ = x_sample.shape
    st = _Stream(batch, seq, dec_batch, dec_seq)
    cond = jnp.zeros((N_COND, D_MODEL), F32).at[0].set(c_ctx).at[1:1 + dec_batch].set(c)
    mods_all = _ada_call(cond, w_ada, b_ada).reshape(DEPTH, N_COND, N_MOD, D_MODEL)

    b_gate_t = b_mgate[:, :, None]
    g_norm4 = g_norm[:, :, None, :]
    g_fin = g_final[None, :]
    w_t = jnp.swapaxes(w_in, 1, 2).astype(BF16)
    g_sub3 = g_attn_sub[:, None, :]
    g_m3 = g_mlstm[:, None, :]
    cos_t, sin_t = _rope_tables(dec_seq)
    lat_state = (state_C, state_n, state_m)

    xs = (x_prompt.reshape(batch * seq, D_MODEL), x_sample.reshape(dec_batch * dec_seq, D_MODEL))
    kv, states = None, None
    ffn1_w = (w_ffn1_in[0].astype(BF16), w_ffn1_out[0].astype(BF16))
    for l in range(DEPTH):
        mods = mods_all[l]
        lam_init = 0.8 - 0.6 * math.exp(-0.3 * l)
        nxt = l + 1 < DEPTH
        (x,) = _ffn_call(xs, mods, g_norm4, g_fin, *ffn1_w, st, layer=l, sub=0, final=False)
        q, k, ck, cv, zf, mq, mk, mo, vat, vt, gates_t = _proj_call(
            x, mods, g_norm4, w_t, b_gate_t, cos_t, sin_t, kv, st, layer=l)
        kv = (ck, cv)
        a, ffn2_in = _attn_call(q, k, vat, None, attn_lambda, g_sub3, None, [(w_ffn2_in, l)], st, layer=l,
                                lam_init=lam_init, latent=False)
        a, ffn2_out = _attn_call(q, k, vat, (cache_k, cache_v), attn_lambda, g_sub3, a, [(w_ffn2_out, l)], st,
                                 layer=l, lam_init=lam_init, latent=True)
        f, *nxt_in = _four_call(zf, None, [(w_ffn1_in, l + 1)] if nxt else [], st, latent=False)
        f, w_bg = _four_call(zf, f, [(w_branch_gate, l)], st, latent=True)
        m, c_fin, n_fin, m_fin, *nxt_out = _mlstm_call(
            mq, mk, vt, mo, gates_t, g_m3, None, states, [(w_ffn1_out, l + 1)] if nxt else [], st,
            layer=l, latent=False)
        states = [c_fin, n_fin, m_fin]
        m, w_ba, w_bf, w_bm, w_o = _mlstm_call(
            mq, mk, vt, mo, gates_t, g_m3, lat_state, (m,),
            [(w_br_attn, l), (w_br_four, l), (w_br_mlstm, l), (w_out, l)], st, layer=l, latent=True)
        x = _merge_call(x, a, f, m, mods, g_norm4, w_bg, w_ba, w_bf, w_bm, w_o, st, layer=l)
        xs = tuple(_ffn_call((x,), mods, g_norm4, g_fin, ffn2_in, ffn2_out, st, layer=l, sub=2,
                             final=(l == DEPTH - 1)))
        if nxt:
            ffn1_w = (nxt_in[0], nxt_out[0])
    y_prompt, y_sample = xs
    return (y_prompt.reshape(batch, seq, D_MODEL), y_sample.reshape(dec_batch, dec_seq, D_MODEL),
            *kv, *states)
```

```python
import functools
import math

import numpy as np
import jax
import jax.numpy as jnp
from jax import lax
from jax.experimental import pallas as pl
from jax.experimental.pallas import tpu as pltpu

D_MODEL = 1024
DEPTH = 2
GRID_W = 64
N_ATTN = 4
DH_ATTN = 64
DV_ATTN = 2 * DH_ATTN
W_ATTN = N_ATTN * DV_ATTN
N_FOUR = 4
DG_FOUR = 128
W_FOUR = N_FOUR * DG_FOUR
N_MLSTM = 4
DH_MLSTM = 128
W_MLSTM = N_MLSTM * DH_MLSTM
N_GATE = 4 * N_MLSTM
LANES = 128
SUBLANES = 16
MV_LO = 3 * W_ATTN + W_FOUR + 2 * W_MLSTM
MV_HI = MV_LO + W_MLSTM
P_MAIN = MV_HI + W_MLSTM
D_FF = 2816
N_MOD = 9
N_COND = 8
ROPE_BASE = 10000.0
ROPE_AXIS_PAIRS = DH_ATTN // 4
ATTN_SCALE = DH_ATTN ** -0.5
LOG2E = 1.4426950408889634
LN2 = 0.6931471805599453
MLSTM_K_SCALE = DH_MLSTM ** -0.5
EPS = 1e-6
MLSTM_CHUNK = 256
VMEM_LIMIT = 56 * 1024 * 1024
TM_WIDE = 1024
TM_PROJ = 512

F32 = jnp.float32
BF16 = jnp.bfloat16


def _cparams(n_grid):
    return pltpu.CompilerParams(dimension_semantics=("arbitrary",) * n_grid,
                                vmem_limit_bytes=VMEM_LIMIT)


def _bdot(a, b):
    return jnp.dot(a.astype(BF16), b.astype(BF16), preferred_element_type=F32)


def _bdot_nt(a, b):
    return lax.dot_general(a.astype(BF16), b.astype(BF16), (((1,), (1,)), ((), ())),
                           preferred_element_type=F32)


def _sigmoid(x):
    return 1.0 / (1.0 + jnp.exp(-x))


def _log_sigmoid(x):
    return jnp.minimum(x, 0.0) - jnp.log1p(jnp.exp(-jnp.abs(x)))


def _rms(x, g):
    return x * lax.rsqrt(jnp.mean(x * x, axis=-1, keepdims=True) + EPS) * g


def _modulated_norm(x, g, mod, base):
    return _rms(x, g) * (1.0 + mod[base + 1:base + 2]) + mod[base:base + 1]


def _layer_spec(shape, layer):
    if layer is None:
        return pl.BlockSpec(shape, lambda *_: (0,) * len(shape), pipeline_mode=pl.Buffered(1))
    return pl.BlockSpec((None,) + shape, lambda *_: (layer,) + (0,) * len(shape),
                        pipeline_mode=pl.Buffered(1))


class _CastJobs:
    def __init__(self, jobs, steps, step_index):
        self.jobs, self.steps, self.step_index = list(jobs), steps, step_index

    def __len__(self):
        return len(self.jobs)

    def _rows(self, w):
        rows = w.shape[1] // self.steps
        if rows * self.steps != w.shape[1] or rows % SUBLANES:
            raise ValueError("weight rows do not split into bf16 row tiles over the grid")
        return rows

    def in_specs(self):
        return [pl.BlockSpec((None, self._rows(w), w.shape[2]),
                             lambda *g, layer=layer: (layer, self.step_index(*g), 0))
                for w, layer in self.jobs]

    def out_specs(self):
        return [pl.BlockSpec((self._rows(w), w.shape[2]), lambda *g: (self.step_index(*g), 0))
                for w, _ in self.jobs]

    def out_shapes(self):
        return [jax.ShapeDtypeStruct(w.shape[1:], BF16) for w, _ in self.jobs]

    def args(self):
        return [w for w, _ in self.jobs]

    def wrap(self, body, n_in, n_out):
        n = len(self.jobs)
        if n == 0:
            return body

        def kernel(*refs):
            ins, cast_in = refs[:n_in], refs[n_in:n_in + n]
            outs = refs[n_in + n:n_in + n + n_out]
            cast_out = refs[n_in + n + n_out:n_in + 2 * n + n_out]
            for src, dst in zip(cast_in, cast_out):
                dst[...] = src[...].astype(dst.dtype)
            body(*ins, *outs, *refs[n_in + 2 * n + n_out:])

        return kernel


class _Stream:
    def __init__(self, batch, seq, dec_batch, dec_seq):
        self.batch, self.seq, self.dec_batch, self.dec_seq = batch, seq, dec_batch, dec_seq
        self.n_ctx = batch * seq
        self.n_all = self.n_ctx + dec_batch * dec_seq

    def ctx_tiles(self, tm):
        if self.n_ctx % tm or self.dec_seq % tm:
            raise ValueError("row tile must divide the context rows and one latent sequence")
        return self.n_ctx // tm

    def mod_spec(self, tm):
        n_pt, tps = self.ctx_tiles(tm), self.dec_seq // tm
        return pl.BlockSpec((None, N_MOD, D_MODEL),
                            lambda i: (jnp.where(i < n_pt, 0, 1 + (i - n_pt) // tps), 0, 0))


def _ada_kernel(c_ref, w_ref, b_ref, o_ref):
    c = c_ref[...]
    s = c * _sigmoid(c)
    o_ref[...] = _bdot(s, w_ref[...]) + b_ref[...]


def _ada_call(cond, w_ada, b_ada):
    tn = 2304
    n_out = N_MOD * D_MODEL
    return pl.pallas_call(
        _ada_kernel,
        grid=(DEPTH, n_out // tn),
        in_specs=[
            pl.BlockSpec((N_COND, D_MODEL), lambda l, j: (0, 0)),
            pl.BlockSpec((None, D_MODEL, tn), lambda l, j: (l, 0, j)),
            pl.BlockSpec((None, 1, tn), lambda l, j: (l, 0, j)),
        ],
        out_specs=pl.BlockSpec((None, N_COND, tn), lambda l, j: (l, 0, j)),
        out_shape=jax.ShapeDtypeStruct((DEPTH, N_COND, n_out), F32),
        compiler_params=_cparams(2),
        name="adaln",
    )(cond, w_ada, b_ada.reshape(DEPTH, 1, n_out))


FF_CHUNKS = (768, 768, 768, 512)


def _ffn_kernel(*refs, base, n_x, final, n_pt):
    x_refs = refs[:n_x]
    mod_ref, g_ref, gf_ref, win_ref, wout_ref = refs[n_x:n_x + 5]
    o_refs = refs[n_x + 5:]
    i = pl.program_id(0)
    if n_x == 2:
        x = jnp.where(i < n_pt, x_refs[0][...], x_refs[1][...])
    else:
        x = x_refs[0][...]
    mod = mod_ref[...]
    u = _modulated_norm(x, g_ref[...], mod, base).astype(BF16)
    y = None
    lo = 0
    for width in FF_CHUNKS:
        a = jnp.dot(u, win_ref[:, lo:lo + width], preferred_element_type=F32)
        g = jnp.dot(u, win_ref[:, D_FF + lo:D_FF + lo + width], preferred_element_type=F32)
        hh = (a * _sigmoid(a) * g).astype(BF16)
        part = jnp.dot(hh, wout_ref[lo:lo + width, :], preferred_element_type=F32)
        y = part if y is None else y + part
        lo += width
    xn = x + 0.5 * mod[base + 2:base + 3] * y
    if not final:
        o_refs[0][...] = xn
    else:
        xn = _rms(xn, gf_ref[...])

        @pl.when(i < n_pt)
        def _():
            o_refs[0][...] = xn

        @pl.when(i >= n_pt)
        def _():
            o_refs[1][...] = xn


def _ffn_call(xs, mods, g_norm, g_final, w_in, w_out, st, *, layer, sub, final):
    tm = TM_WIDE
    n_pt = st.ctx_tiles(tm)
    n_tiles = st.n_all // tm
    ctx_spec = pl.BlockSpec((tm, D_MODEL), lambda i: (jnp.minimum(i, n_pt - 1), 0))
    lat_spec = pl.BlockSpec((tm, D_MODEL), lambda i: (jnp.maximum(i - n_pt, 0), 0))
    all_spec = pl.BlockSpec((tm, D_MODEL), lambda i: (i, 0))
    x_specs = [ctx_spec, lat_spec] if len(xs) == 2 else [all_spec]
    if final:
        out_specs = [ctx_spec, lat_spec]
        out_shape = [jax.ShapeDtypeStruct((st.n_ctx, D_MODEL), F32),
                     jax.ShapeDtypeStruct((st.n_all - st.n_ctx, D_MODEL), F32)]
    else:
        out_specs = [all_spec]
        out_shape = [jax.ShapeDtypeStruct((st.n_all, D_MODEL), F32)]
    return pl.pallas_call(
        functools.partial(_ffn_kernel, base=3 * sub, n_x=len(xs), final=final, n_pt=n_pt),
        grid=(n_tiles,),
        in_specs=x_specs + [
            st.mod_spec(tm),
            pl.BlockSpec((None, None, 1, D_MODEL), lambda i: (layer, sub, 0, 0)),
            pl.BlockSpec((1, D_MODEL), lambda i: (0, 0)),
            _layer_spec((D_MODEL, 2 * D_FF), None),
            _layer_spec((D_FF, D_MODEL), None),
        ],
        out_specs=out_specs,
        out_shape=out_shape,
        compiler_params=_cparams(1),
        name="ffn",
    )(*xs, mods, g_norm, g_final, w_in, w_out)


def _swap32(x):
    lane = lax.broadcasted_iota(jnp.int32, x.shape, 1)
    return jnp.where((lane & (DH_ATTN // 2)) == 0,
                     pltpu.roll(x, LANES - DH_ATTN // 2, 1), pltpu.roll(x, DH_ATTN // 2, 1))


def _proj_kernel(*refs, n_alias, n_pt, seqs_per_tile, seq):
    (x_ref, mod_ref, g_ref, wt_ref, bgt_ref, cos_ref, sin_ref) = refs[:7]
    (q_ref, k_ref, ck_ref, cv_ref, zf_ref, mq_ref, mk_ref, mo_ref, vat_ref, vt_ref, gt_ref) = refs[7 + n_alias:]
    i = pl.program_id(0)
    is_latent = i >= n_pt
    x = x_ref[...]
    u = _modulated_norm(x, g_ref[...], mod_ref[...], 3).astype(BF16)
    za = _bdot_nt(u, wt_ref[:3 * W_ATTN, :])
    zb = _bdot_nt(u, wt_ref[3 * W_ATTN:MV_LO, :])
    zo = _bdot_nt(u, wt_ref[MV_HI:P_MAIN, :])
    cos_t = cos_ref[...]
    sin_t = sin_ref[...]
    plain_k = []
    for h in range(N_ATTN):
        zq = za[:, h * DV_ATTN:(h + 1) * DV_ATTN]
        zk = za[:, W_ATTN + h * DV_ATTN:W_ATTN + (h + 1) * DV_ATTN]
        plain_k.append(zk)
        q_ref[h] = jnp.where(is_latent, zq * cos_t + _swap32(zq) * sin_t, zq).astype(q_ref.dtype)
        k_ref[h] = jnp.where(is_latent, zk * cos_t + _swap32(zk) * sin_t, zk).astype(k_ref.dtype)

    @pl.when(i < n_pt)
    def _():
        for h in range(N_ATTN):
            zv = za[:, 2 * W_ATTN + h * DV_ATTN:2 * W_ATTN + (h + 1) * DV_ATTN]
            for s in range(seqs_per_tile):
                rs = slice(s * seq, (s + 1) * seq)
                ck_ref[s, h] = plain_k[h][rs]
                cv_ref[s, h] = zv[rs]

    zf_ref[...] = zb[:, :W_FOUR].astype(zf_ref.dtype)
    off = W_FOUR
    mq_ref[...] = zb[:, off:off + W_MLSTM].astype(mq_ref.dtype)
    mk_ref[...] = (zb[:, off + W_MLSTM:off + 2 * W_MLSTM] * MLSTM_K_SCALE).astype(mk_ref.dtype)
    mo_ref[...] = zo
    vat_ref[...] = _bdot_nt(wt_ref[2 * W_ATTN:3 * W_ATTN, :], u).astype(vat_ref.dtype)
    vt_ref[...] = _bdot_nt(wt_ref[MV_LO:MV_HI, :], u)
    gpt = _bdot_nt(wt_ref[P_MAIN:, :], u) + bgt_ref[...]
    sub = lax.broadcasted_iota(jnp.int32, gpt.shape, 0)
    gt_ref[...] = jnp.where((sub & N_MLSTM) != 0, _log_sigmoid(gpt), gpt)


def _proj_call(x, mods, g_norm, w_t, bgt, cos_t, sin_t, kv_prev, st, *, layer):
    tm = TM_PROJ
    n_pt = st.ctx_tiles(tm)
    if tm % st.seq:
        raise ValueError("a context row tile must hold whole sequences")
    seqs_per_tile, tps = tm // st.seq, st.dec_seq // tm
    n = st.n_all
    head_shape = jax.ShapeDtypeStruct((N_ATTN, n, DV_ATTN), BF16)
    head_spec = pl.BlockSpec((N_ATTN, tm, DV_ATTN), lambda i: (0, i, 0))
    kv_shape = jax.ShapeDtypeStruct((st.batch, DEPTH, N_ATTN, st.seq, DV_ATTN), F32)
    kv_spec = pl.BlockSpec((seqs_per_tile, None, N_ATTN, st.seq, DV_ATTN),
                           lambda i: (jnp.minimum(i, n_pt - 1), layer, 0, 0, 0))
    tok_shape = jax.ShapeDtypeStruct((n, W_ATTN), F32)
    tok_bf16 = jax.ShapeDtypeStruct((n, W_ATTN), BF16)
    tok_spec = pl.BlockSpec((tm, W_ATTN), lambda i: (i, 0))
    rope_spec = pl.BlockSpec((tm, LANES), lambda i: (jnp.maximum(i - n_pt, 0) % tps, 0))
    alias_in = list(kv_prev) if kv_prev is not None else []
    n_in = 7
    return pl.pallas_call(
        functools.partial(_proj_kernel, n_alias=len(alias_in), n_pt=n_pt, seqs_per_tile=seqs_per_tile,
                          seq=st.seq),
        grid=(n // tm,),
        in_specs=[
            pl.BlockSpec((tm, D_MODEL), lambda i: (i, 0)),
            st.mod_spec(tm),
            pl.BlockSpec((None, None, 1, D_MODEL), lambda i: (layer, 1, 0, 0)),
            _layer_spec((w_t.shape[1], D_MODEL), layer),
            _layer_spec((N_GATE, 1), layer),
            rope_spec, rope_spec,
        ] + [pl.BlockSpec(memory_space=pl.ANY)] * len(alias_in),
        out_specs=[head_spec, head_spec, kv_spec, kv_spec, tok_spec, tok_spec, tok_spec, tok_spec,
                   pl.BlockSpec((W_ATTN, tm), lambda i: (0, i)),
                   pl.BlockSpec((W_MLSTM, tm), lambda i: (0, i)),
                   pl.BlockSpec((N_GATE, tm), lambda i: (0, i))],
        out_shape=[head_shape, head_shape, kv_shape, kv_shape, tok_bf16, tok_bf16, tok_bf16, tok_shape,
                   jax.ShapeDtypeStruct((W_ATTN, n), BF16),
                   jax.ShapeDtypeStruct((W_MLSTM, n), F32),
                   jax.ShapeDtypeStruct((N_GATE, n), F32)],
        input_output_aliases={n_in + j: 2 + j for j in range(len(alias_in))},
        compiler_params=_cparams(1),
        name="mixer_proj",
    )(x, mods, g_norm, w_t, bgt, cos_t, sin_t, *alias_in)


def _attn_kernel(*refs, lam_init, cached, nb, tq, seq):
    q_ref, k_ref, vt_ref = refs[:3]
    pos = 3
    if cached:
        ck_ref, cv_ref = refs[pos:pos + 2]
        pos += 2
    lam_ref, g_ref = refs[pos:pos + 2]
    o_ref = refs[-1]
    lp = lam_ref[...]
    lam = (jnp.exp(jnp.sum(lp[0:1] * lp[1:2], axis=-1, keepdims=True))
           - jnp.exp(jnp.sum(lp[2:3] * lp[3:4], axis=-1, keepdims=True)) + lam_init)
    g_sub = g_ref[...]
    lane = lax.broadcasted_iota(jnp.int32, (tq, DV_ATTN), 1)
    heads = range(N_ATTN)
    results = []
    for bi in range(nb):
        ks = slice(bi * seq, (bi + 1) * seq)
        keys = [[k_ref[h, ks] for h in heads]]
        vals_t = [[vt_ref[h * DV_ATTN:(h + 1) * DV_ATTN, ks] for h in heads]]
        if cached:
            keys.insert(0, [ck_ref[h].astype(BF16) for h in heads])
            vals_t.insert(0, [cv_ref[h].T.astype(BF16) for h in heads])
        qh = [q_ref[h, bi * tq:(bi + 1) * tq].astype(F32) * (ATTN_SCALE * LOG2E) for h in heads]
        exps, dens = [], []
        for first_map in (True, False):
            qm = [jnp.where((lane < DH_ATTN) == first_map, q, 0.0).astype(BF16) for q in qh]
            s = [jnp.concatenate([_bdot_nt(kg[h], qm[h]) for h in heads], axis=1) for kg in keys]
            m = functools.reduce(jnp.maximum, [jnp.max(si, axis=0, keepdims=True) for si in s])
            e = [jnp.exp2(si - m) for si in s]
            exps.append(e)
            dens.append(functools.reduce(jnp.add, [jnp.sum(ei, axis=0, keepdims=True) for ei in e]))
        ratio = lam * dens[0] / dens[1]
        inv = 1.0 / dens[0]
        probs = [(e1 - e2 * ratio).astype(BF16) for e1, e2 in zip(*exps)]
        for h in heads:
            cols = slice(h * tq, (h + 1) * tq)
            o_t = None
            for pj, vg in zip(probs, vals_t):
                part = jnp.dot(vg[h], pj[:, cols], preferred_element_type=F32)
                o_t = part if o_t is None else o_t + part
            o_t = o_t * inv[:, cols]
            y_t = o_t * lax.rsqrt(jnp.mean(o_t * o_t, axis=0, keepdims=True) + EPS)
            results.append((bi, h, (y_t.T * g_sub * (1.0 - lam_init)).astype(o_ref.dtype)))
    for bi, h, y in results:
        o_ref[bi * tq:(bi + 1) * tq, h * DV_ATTN:(h + 1) * DV_ATTN] = y


def _attn_four_kernel(*refs, n_attn, attn_kw, four_kw, once_per_seq):
    a_ref, f_ref = refs[-2:]
    _attn_kernel(*refs[:n_attn], a_ref, **attn_kw)
    four = functools.partial(_four_kernel, *refs[n_attn:n_attn + 4], f_ref, **four_kw)
    if once_per_seq:
        pl.when(pl.program_id(1) == 0)(four)
    else:
        four()


def _attn_four_call(q, k, v_t, cache, lam_p, g_sub, zf, outs_prev, cast_w, st, *, layer, lam_init, latent):
    n = st.n_all
    if latent:
        batch, seq, nb, tq = st.dec_batch, st.dec_seq, 1, st.dec_seq // 2
        row0 = st.n_ctx
    else:
        batch, seq, nb, tq = st.batch, st.seq, 4, st.seq
        row0 = 0
    nq = seq // tq
    if nq > 1 and nb > 1:
        raise ValueError("query blocks of several sequences are not contiguous rows")
    q0, s0 = row0 // (nb * tq), row0 // (nb * seq)
    in_specs = [pl.BlockSpec((N_ATTN, nb * tq, DV_ATTN), lambda b, i: (0, q0 + b * nq + i, 0)),
                pl.BlockSpec((N_ATTN, nb * seq, DV_ATTN), lambda b, i: (0, s0 + b, 0)),
                pl.BlockSpec((W_ATTN, nb * seq), lambda b, i: (0, s0 + b))]
    args = [q, k, v_t]
    if cache is not None:
        past = cache[0].shape[3]
        c_spec = pl.BlockSpec((None, None, N_ATTN, past, DV_ATTN), lambda b, i: (b, layer, 0, 0, 0))
        in_specs += [c_spec, c_spec]
        args += list(cache)
    in_specs += [pl.BlockSpec((None, 4, DH_ATTN), lambda b, i: (layer, 0, 0)),
                 pl.BlockSpec((None, 1, DV_ATTN), lambda b, i: (layer, 0, 0))]
    args += [lam_p, g_sub]
    n_attn = len(args)
    w_d, ct, s_t = _dft_tables(seq)
    seq_rows = pl.BlockSpec((nb * seq, W_FOUR), lambda b, i: (s0 + b, 0))
    in_specs += [seq_rows,
                 pl.BlockSpec((DG_FOUR, 2 * DG_FOUR), lambda b, i: (0, 0)),
                 pl.BlockSpec((seq, seq), lambda b, i: (0, 0)),
                 pl.BlockSpec((seq, seq), lambda b, i: (0, 0))]
    args += [zf, w_d, ct, s_t]
    aliases = {}
    if outs_prev is not None:
        aliases = {len(args): 0, len(args) + 1: 1}
        in_specs += [pl.BlockSpec(memory_space=pl.ANY)] * 2
        args += list(outs_prev)
    casts = _CastJobs(cast_w, (batch // nb) * nq, lambda b, i: b * nq + i)
    body = functools.partial(
        _attn_four_kernel, n_attn=n_attn, once_per_seq=nq > 1,
        attn_kw=dict(lam_init=lam_init, cached=cache is not None, nb=nb, tq=tq, seq=seq),
        four_kw=dict(nb=nb, seq=seq))
    return pl.pallas_call(
        casts.wrap(body, len(args), 2),
        grid=(batch // nb, nq),
        in_specs=in_specs + casts.in_specs(),
        out_specs=[pl.BlockSpec((nb * tq, W_ATTN), lambda b, i: (q0 + b * nq + i, 0)), seq_rows]
        + casts.out_specs(),
        out_shape=[jax.ShapeDtypeStruct((n, W_ATTN), BF16), jax.ShapeDtypeStruct((n, W_FOUR), BF16)]
        + casts.out_shapes(),
        input_output_aliases=aliases,
        compiler_params=_cparams(2),
        name="attn_fourier",
    )(*args, *casts.args())


def _dft_tables(seq):
    def cs(n):
        j = np.arange(n)
        ang = 2.0 * np.pi * ((j[:, None] * j[None, :]) % n) / n
        return np.cos(ang) / math.sqrt(n), np.sin(ang) / math.sqrt(n)

    cd, sd = cs(DG_FOUR)
    ct, st = cs(seq)
    w_d = jnp.asarray(np.concatenate([cd, sd], axis=1), F32)
    return w_d.astype(BF16), jnp.asarray(ct, F32).astype(BF16), jnp.asarray(-st, F32).astype(BF16)


def _four_kernel(*refs, nb, seq):
    z_ref, wd_ref, ct_ref, st_ref = refs[:4]
    o_ref = refs[-1]
    y_cos, y_sin = [], []
    for gidx in range(N_FOUR):
        y = jnp.dot(z_ref[:, gidx * DG_FOUR:(gidx + 1) * DG_FOUR], wd_ref[...], preferred_element_type=F32)
        y_cos.append(y[:, :DG_FOUR].astype(BF16))
        y_sin.append(y[:, DG_FOUR:].astype(BF16))
    y_cos = jnp.concatenate(y_cos, axis=1)
    y_sin = jnp.concatenate(y_sin, axis=1)
    for s in range(nb):
        rows = slice(s * seq, (s + 1) * seq)
        o_ref[rows, :] = (jnp.dot(ct_ref[...], y_cos[rows], preferred_element_type=F32)
                          + jnp.dot(st_ref[...], y_sin[rows], preferred_element_type=F32)).astype(o_ref.dtype)


def _mlstm_kernel(*refs, seq, nb, seeded, emit_state, n_alias):
    mq_ref, mk_ref, vt_ref, mo_ref, gt_ref, g_ref = refs[:6]
    pos = 6
    if seeded:
        c0_ref, n0_ref, m0_ref = refs[pos:pos + 3]
        pos += 3
    pos += n_alias
    o_ref = refs[pos]
    if emit_state:
        c1_ref, n1_ref, m1_ref = refs[pos + 1:pos + 4]

    L = min(MLSTM_CHUNK, seq)
    nc = seq // L
    s_idx = lax.broadcasted_iota(jnp.int32, (L, L), 0)
    t_idx = lax.broadcasted_iota(jnp.int32, (L, L), 1)
    before = (s_idx <= t_idx, s_idx >= t_idx)
    tri = (jnp.where(before[1], 1.0, 0.0).astype(BF16),
           jnp.where(before[0], 1.0, 0.0).astype(BF16))
    chains = [(bi, d, h) for bi in range(nb) for d in range(2) for h in range(N_MLSTM)]

    state = {}
    for (bi, d, h) in chains:
        if seeded:
            state[bi, d, h] = (c0_ref[bi, d, h], n0_ref[bi, d, h:h + 1, :], m0_ref[bi, d:d + 1, h:h + 1])
        else:
            state[bi, d, h] = (None, None, jnp.zeros((1, 1), F32))

    h_t = {}
    for c in range(nc):
        terms = {}
        for bi in range(nb):
            for d in range(2):
                r0 = bi * seq + (c if d == 0 else nc - 1 - c) * L
                g_t = gt_ref[:, r0:r0 + L]
                hi = g_t.astype(BF16)
                rem = g_t - hi.astype(F32)
                mid = rem.astype(BF16)
                lo = (rem - mid.astype(F32)).astype(BF16)
                pieces = _bdot_nt(jnp.concatenate([hi, mid, lo], axis=0), tri[d])
                cum_t = pieces[0:N_GATE] + pieces[N_GATE:2 * N_GATE] + pieces[2 * N_GATE:]
                c_t = (g_t - pltpu.roll(cum_t, N_GATE - N_MLSTM, 0)) * LOG2E
                col = jnp.concatenate([c_t, jnp.zeros((LANES - N_GATE, L), F32)], axis=0).T
                terms[bi, d] = (r0, col, g_t, cum_t)

        new_state = {}
        need_update = emit_state or c < nc - 1
        group = [(d, h) for d in range(2) for h in range(N_MLSTM)]
        for bi in range(nb):
            has_state = state[bi, 0, 0][0] is not None
            qs, ks, vts, cbs, s0s, i_rows, b_rows, m_prevs = [], [], [], [], [], [], [], []
            for d, h in group:
                r0, col, g_t, cum_t = terms[bi, d]
                ci = 2 * d * N_MLSTM + h
                cf = ci + N_MLSTM
                hs = slice(h * DH_MLSTM, (h + 1) * DH_MLSTM)
                qs.append(mq_ref[r0:r0 + L, hs])
                ks.append(mk_ref[r0:r0 + L, hs])
                vts.append(vt_ref[hs, r0:r0 + L])
                cbs.append(jnp.where(before[d], col[:, ci:ci + 1], -jnp.inf))
                s0s.append(_bdot_nt(ks[-1], qs[-1]))
                i_rows.append(g_t[ci:ci + 1, :])
                b_rows.append(cum_t[cf:cf + 1, :])
                m_prevs.append(jnp.broadcast_to(state[bi, d, h][2], (1, L)))
            cb = jnp.concatenate(cbs, axis=1)
            i_row, b_row = jnp.concatenate(i_rows, axis=1), jnp.concatenate(b_rows, axis=1)
            m_prev = jnp.concatenate(m_prevs, axis=1)
            m2_prev = m_prev * LOG2E
            m2_row = jnp.maximum(jnp.max(cb, axis=0, keepdims=True), m2_prev)
            s_t = jnp.concatenate(s0s, axis=1) * jnp.exp2(cb - m2_row)
            den = jnp.sum(s_t, axis=0, keepdims=True)
            s_bf = s_t.astype(BF16)
            m_t = b_row + m2_row * LN2
            floor = jnp.exp(-m_t)
            if has_state:
                sp = jnp.exp2(m2_prev - m2_row)
            if need_update:
                tot, new = [], []
                for j, (d, h) in enumerate(group):
                    last = j * L + (L - 1 if d == 0 else 0)
                    tot.append(jnp.broadcast_to(b_row[:, last:last + 1], (1, L)))
                    new.append(jnp.broadcast_to(m_t[:, last:last + 1], (1, L)))
                b_tot, m_new = jnp.concatenate(tot, axis=1), jnp.concatenate(new, axis=1)
                wl = jnp.exp(b_tot + (i_row - b_row) - m_new)
                if has_state:
                    decay = jnp.exp(b_tot + m_prev - m_new)
            for j, (d, h) in enumerate(group):
                cols = slice(j * L, (j + 1) * L)
                c_prev, n_prev, _ = state[bi, d, h]
                num_t = jnp.dot(vts[j].astype(BF16), s_bf[:, cols], preferred_element_type=F32)
                den_j = den[:, cols]
                if has_state:
                    cn = jnp.concatenate([c_prev, jnp.broadcast_to(n_prev, (SUBLANES, DH_MLSTM))], axis=0)
                    cq = _bdot_nt(cn, qs[j])
                    num_t = num_t + sp[:, cols] * cq[:DH_MLSTM]
                    den_j = den_j + sp[:, cols] * cq[DH_MLSTM:DH_MLSTM + 1]
                h_t[bi, d, h, c] = num_t / jnp.maximum(jnp.abs(den_j), floor[:, cols])
                if need_update:
                    wl_j = wl[:, cols]
                    vw = jnp.concatenate([vts[j] * wl_j, jnp.broadcast_to(wl_j, (SUBLANES, L))], axis=0)
                    upd = jnp.dot(vw.astype(BF16), ks[j], preferred_element_type=F32)
                    c_new, n_new = upd[:DH_MLSTM], upd[DH_MLSTM:DH_MLSTM + 1]
                    if has_state:
                        decay_j = decay[:, j * L:j * L + 1]
                        c_new = decay_j * c_prev + c_new
                        n_new = decay_j * n_prev + n_new
                    new_state[bi, d, h] = (c_new, n_new, m_new[:, j * L:j * L + 1])
        state = new_state

    g_m = g_ref[...]
    for bi in range(nb):
        for h in range(N_MLSTM):
            hs = slice(h * DH_MLSTM, (h + 1) * DH_MLSTM)
            fwd = [h_t[bi, 0, h, c] for c in range(nc)]
            bwd = [h_t[bi, 1, h, nc - 1 - c] for c in range(nc)]
            hsum = (fwd[0] if nc == 1 else jnp.concatenate(fwd, axis=1)) \
                + (bwd[0] if nc == 1 else jnp.concatenate(bwd, axis=1))
            y = hsum * lax.rsqrt(jnp.mean(hsum * hsum, axis=0, keepdims=True) + EPS)
            rows = slice(bi * seq, (bi + 1) * seq)
            o_ref[rows, hs] = (y.T * g_m * _sigmoid(mo_ref[rows, hs])).astype(o_ref.dtype)

    if emit_state:
        for (bi, d, h) in chains:
            c_fin, n_fin, m_fin = state[bi, d, h]
            c1_ref[bi, d, h] = c_fin
            n1_ref[bi, d, h:h + 1, :] = n_fin
            m1_ref[bi, d:d + 1, h:h + 1] = m_fin


def _mlstm_call(mq, mk, vt, mo, gates_t, g_m, state, alias_prev, cast_w, st, *, layer, latent):
    if latent:
        batch, seq, nb, row0 = st.dec_batch, st.dec_seq, 1, st.n_ctx
    else:
        batch, seq, nb, row0 = st.batch, st.seq, 4, 0
    emit_state = not latent
    rows = nb * seq
    b0 = row0 // rows
    tok_spec = pl.BlockSpec((rows, W_MLSTM), lambda b: (b0 + b, 0))
    in_specs = [tok_spec, tok_spec,
                pl.BlockSpec((W_MLSTM, rows), lambda b: (0, b0 + b)),
                tok_spec,
                pl.BlockSpec((N_GATE, rows), lambda b: (0, b0 + b)),
                pl.BlockSpec((None, 1, DH_MLSTM), lambda b: (layer, 0, 0))]
    args = [mq, mk, vt, mo, gates_t, g_m]
    state_specs = [
        pl.BlockSpec((nb, None, 2, N_MLSTM, DH_MLSTM, DH_MLSTM), lambda b: (b, layer, 0, 0, 0, 0)),
        pl.BlockSpec((nb, None, 2, N_MLSTM, DH_MLSTM), lambda b: (b, layer, 0, 0, 0)),
        pl.BlockSpec((nb, None, 2, N_MLSTM), lambda b: (b, layer, 0, 0)),
    ]
    if state is not None:
        in_specs += state_specs
        args += list(state)
    alias_in = list(alias_prev) if alias_prev is not None else []
    n_in = len(args)
    in_specs += [pl.BlockSpec(memory_space=pl.ANY)] * len(alias_in)
    args += alias_in
    out_specs = [tok_spec]
    out_shape = [jax.ShapeDtypeStruct((st.n_all, W_MLSTM), BF16)]
    if emit_state:
        out_specs += state_specs
        out_shape += [
            jax.ShapeDtypeStruct((batch, DEPTH, 2, N_MLSTM, DH_MLSTM, DH_MLSTM), F32),
            jax.ShapeDtypeStruct((batch, DEPTH, 2, N_MLSTM, DH_MLSTM), F32),
            jax.ShapeDtypeStruct((batch, DEPTH, 2, N_MLSTM), F32),
        ]
    first_out = 1 if emit_state else 0
    casts = _CastJobs(cast_w, batch // nb, lambda b: b)
    body = functools.partial(_mlstm_kernel, seq=seq, nb=nb, seeded=state is not None, emit_state=emit_state,
                             n_alias=len(alias_in))
    return pl.pallas_call(
        casts.wrap(body, len(args), len(out_specs)),
        grid=(batch // nb,),
        in_specs=in_specs + casts.in_specs(),
        out_specs=out_specs + casts.out_specs(),
        out_shape=out_shape + casts.out_shapes(),
        input_output_aliases={n_in + j: first_out + j for j in range(len(alias_in))},
        compiler_params=_cparams(1),
        name="mlstm",
    )(*args, *casts.args())


def _merge_kernel(x_ref, a_ref, f_ref, m_ref, mod_ref, g_ref, wg_ref, wa_ref, wf_ref, wm_ref, wo_ref, o_ref):
    x = x_ref[...]
    mod = mod_ref[...]
    u = _modulated_norm(x, g_ref[...], mod, 3).astype(BF16)
    merged = None
    for j, (br_ref, w_ref) in enumerate(((a_ref, wa_ref), (f_ref, wf_ref), (m_ref, wm_ref))):
        gate = _sigmoid(jnp.dot(u, wg_ref[:, j * D_MODEL:(j + 1) * D_MODEL], preferred_element_type=F32))
        term = gate * _bdot(br_ref[...], w_ref[...])
        merged = term if merged is None else merged + term
    out = _bdot(merged, wo_ref[...])
    o_ref[...] = x + mod[5:6] * out


def _merge_call(x, a, f, m, mods, g_norm, wg, wa, wf, wm, wo, st, *, layer):
    tm = TM_WIDE
    n = st.n_all
    br_spec = pl.BlockSpec((tm, W_ATTN), lambda i: (i, 0))
    return pl.pallas_call(
        _merge_kernel,
        grid=(n // tm,),
        in_specs=[
            pl.BlockSpec((tm, D_MODEL), lambda i: (i, 0)),
            br_spec, br_spec, br_spec,
            st.mod_spec(tm),
            pl.BlockSpec((None, None, 1, D_MODEL), lambda i: (layer, 1, 0, 0)),
            _layer_spec((D_MODEL, 3 * D_MODEL), None),
            _layer_spec((W_ATTN, D_MODEL), None), _layer_spec((W_FOUR, D_MODEL), None),
            _layer_spec((W_MLSTM, D_MODEL), None),
            _layer_spec((D_MODEL, D_MODEL), None),
        ],
        out_specs=pl.BlockSpec((tm, D_MODEL), lambda i: (i, 0)),
        out_shape=jax.ShapeDtypeStruct((n, D_MODEL), F32),
        compiler_params=_cparams(1),
        name="merge",
    )(x, a, f, m, mods, g_norm, wg, wa, wf, wm, wo)


def _rope_tables(n_tok):
    tok = np.arange(n_tok)
    inv = ROPE_BASE ** (-np.arange(ROPE_AXIS_PAIRS, dtype=np.float32) / ROPE_AXIS_PAIRS)
    ang = np.concatenate([(tok // GRID_W).astype(np.float32)[:, None] * inv,
                          (tok % GRID_W).astype(np.float32)[:, None] * inv], axis=-1).astype(np.float32)
    c, s = np.cos(ang), np.sin(ang)
    cos_t = np.concatenate([c, c, c, c], axis=-1)
    sin_t = np.concatenate([-s, s, -s, s], axis=-1)
    return jnp.asarray(cos_t, F32), jnp.asarray(sin_t, F32)


def kernel(x_prompt, x_sample, cache_k, cache_v, state_C, state_n, state_m, c, c_ctx, w_ada, b_ada, g_norm,
           w_ffn1_in, w_ffn1_out, w_ffn2_in, w_ffn2_out, w_in, b_mgate, attn_lambda, g_attn_sub, g_mlstm,
           w_branch_gate, w_br_attn, w_br_four, w_br_mlstm, w_out, g_final):
    batch, seq, _ = x_prompt.shape
    dec_batch, dec_seq, _ = x_sample.shape
    st = _Stream(batch, seq, dec_batch, dec_seq)
    cond = jnp.zeros((N_COND, D_MODEL), F32).at[0].set(c_ctx).at[1:1 + dec_batch].set(c)
    mods_all = _ada_call(cond, w_ada, b_ada).reshape(DEPTH, N_COND, N_MOD, D_MODEL)

    b_gate_t = b_mgate[:, :, None]
    g_norm4 = g_norm[:, :, None, :]
    g_fin = g_final[None, :]
    w_t = jnp.swapaxes(w_in, 1, 2).astype(BF16)
    g_sub3 = g_attn_sub[:, None, :]
    g_m3 = g_mlstm[:, None, :]
    cos_t, sin_t = _rope_tables(dec_seq)
    lat_state = (state_C, state_n, state_m)

    xs = (x_prompt.reshape(batch * seq, D_MODEL), x_sample.reshape(dec_batch * dec_seq, D_MODEL))
    kv, states = None, None
    ffn1_w = (w_ffn1_in[0].astype(BF16), w_ffn1_out[0].astype(BF16))
    for l in range(DEPTH):
        mods = mods_all[l]
        lam_init = 0.8 - 0.6 * math.exp(-0.3 * l)
        nxt = l + 1 < DEPTH
        (x,) = _ffn_call(xs, mods, g_norm4, g_fin, *ffn1_w, st, layer=l, sub=0, final=False)
        q, k, ck, cv, zf, mq, mk, mo, vat, vt, gates_t = _proj_call(
            x, mods, g_norm4, w_t, b_gate_t, cos_t, sin_t, kv, st, layer=l)
        kv = (ck, cv)
        a, f, ffn2_in, *nxt_in = _attn_four_call(
            q, k, vat, None, attn_lambda, g_sub3, zf, None,
            [(w_ffn2_in, l)] + ([(w_ffn1_in, l + 1)] if nxt else []), st,
            layer=l, lam_init=lam_init, latent=False)
        a, f, ffn2_out, w_bg = _attn_four_call(
            q, k, vat, (cache_k, cache_v), attn_lambda, g_sub3, zf, (a, f),
            [(w_ffn2_out, l), (w_branch_gate, l)], st, layer=l, lam_init=lam_init, latent=True)
        m, c_fin, n_fin, m_fin, *nxt_out = _mlstm_call(
            mq, mk, vt, mo, gates_t, g_m3, None, states, [(w_ffn1_out, l + 1)] if nxt else [], st,
            layer=l, latent=False)
        states = [c_fin, n_fin, m_fin]
        m, w_ba, w_bf, w_bm, w_o = _mlstm_call(
            mq, mk, vt, mo, gates_t, g_m3, lat_state, (m,),
            [(w_br_attn, l), (w_br_four, l), (w_br_mlstm, l), (w_out, l)], st, layer=l, latent=True)
        x = _merge_call(x, a, f, m, mods, g_norm4, w_bg, w_ba, w_bf, w_bm, w_o, st, layer=l)
        xs = tuple(_ffn_call((x,), mods, g_norm4, g_fin, ffn2_in, ffn2_out, st, layer=l, sub=2,
                             final=(l == DEPTH - 1)))
        if nxt:
            ffn1_w = (nxt_in[0], nxt_out[0])
    y_prompt, y_sample = xs
    return (y_prompt.reshape(batch, seq, D_MODEL), y_sample.reshape(dec_batch, dec_seq, D_MODEL),
            *kv, *states)
```

```python
import functools
import math

import numpy as np
import jax
import jax.numpy as jnp
from jax import lax
from jax.experimental import pallas as pl
from jax.experimental.pallas import tpu as pltpu

D_MODEL = 1024
DEPTH = 2
GRID_W = 64
N_ATTN = 4
DH_ATTN = 64
DV_ATTN = 2 * DH_ATTN
W_ATTN = N_ATTN * DV_ATTN
N_FOUR = 4
DG_FOUR = 128
W_FOUR = N_FOUR * DG_FOUR
N_MLSTM = 4
DH_MLSTM = 128
W_MLSTM = N_MLSTM * DH_MLSTM
N_GATE = 4 * N_MLSTM
LANES = 128
SUBLANES = 16
MV_LO = 3 * W_ATTN + W_FOUR + 2 * W_MLSTM
MV_HI = MV_LO + W_MLSTM
P_MAIN = MV_HI + W_MLSTM
D_FF = 2816
N_MOD = 9
N_COND = 8
ROPE_BASE = 10000.0
ROPE_AXIS_PAIRS = DH_ATTN // 4
ATTN_SCALE = DH_ATTN ** -0.5
LOG2E = 1.4426950408889634
LN2 = 0.6931471805599453
MLSTM_K_SCALE = DH_MLSTM ** -0.5
EPS = 1e-6
MLSTM_CHUNK = 256
VMEM_LIMIT = 56 * 1024 * 1024
TM_WIDE = 1024
TM_PROJ = 1024

F32 = jnp.float32
BF16 = jnp.bfloat16


def _cparams(n_grid):
    return pltpu.CompilerParams(dimension_semantics=("arbitrary",) * n_grid,
                                vmem_limit_bytes=VMEM_LIMIT)


def _bdot(a, b):
    return jnp.dot(a.astype(BF16), b.astype(BF16), preferred_element_type=F32)


def _bdot_nt(a, b):
    return lax.dot_general(a.astype(BF16), b.astype(BF16), (((1,), (1,)), ((), ())),
                           preferred_element_type=F32)


def _sigmoid(x):
    return 1.0 / (1.0 + jnp.exp(-x))


def _log_sigmoid(x):
    return jnp.minimum(x, 0.0) - jnp.log1p(jnp.exp(-jnp.abs(x)))


def _rms(x, g):
    return x * lax.rsqrt(jnp.mean(x * x, axis=-1, keepdims=True) + EPS) * g


def _modulated_norm(x, g, mod, base):
    return _rms(x, g) * (1.0 + mod[base + 1:base + 2]) + mod[base:base + 1]


def _layer_spec(shape, layer):
    if layer is None:
        return pl.BlockSpec(shape, lambda *_: (0,) * len(shape), pipeline_mode=pl.Buffered(1))
    return pl.BlockSpec((None,) + shape, lambda *_: (layer,) + (0,) * len(shape),
                        pipeline_mode=pl.Buffered(1))


class _CastJobs:
    def __init__(self, jobs, steps, step_index):
        self.jobs, self.steps, self.step_index = list(jobs), steps, step_index

    def __len__(self):
        return len(self.jobs)

    def _rows(self, w):
        rows = w.shape[1] // self.steps
        if rows * self.steps != w.shape[1] or rows % SUBLANES:
            raise ValueError("weight rows do not split into bf16 row tiles over the grid")
        return rows

    def in_specs(self):
        return [pl.BlockSpec((None, self._rows(w), w.shape[2]),
                             lambda *g, layer=layer: (layer, self.step_index(*g), 0))
                for w, layer in self.jobs]

    def out_specs(self):
        return [pl.BlockSpec((self._rows(w), w.shape[2]), lambda *g: (self.step_index(*g), 0))
                for w, _ in self.jobs]

    def out_shapes(self):
        return [jax.ShapeDtypeStruct(w.shape[1:], BF16) for w, _ in self.jobs]

    def args(self):
        return [w for w, _ in self.jobs]

    def wrap(self, body, n_in, n_out):
        n = len(self.jobs)
        if n == 0:
            return body

        def kernel(*refs):
            ins, cast_in = refs[:n_in], refs[n_in:n_in + n]
            outs = refs[n_in + n:n_in + n + n_out]
            cast_out = refs[n_in + n + n_out:n_in + 2 * n + n_out]
            for src, dst in zip(cast_in, cast_out):
                dst[...] = src[...].astype(dst.dtype)
            body(*ins, *outs, *refs[n_in + 2 * n + n_out:])

        return kernel


class _Stream:
    def __init__(self, batch, seq, dec_batch, dec_seq):
        self.batch, self.seq, self.dec_batch, self.dec_seq = batch, seq, dec_batch, dec_seq
        self.n_ctx = batch * seq
        self.n_all = self.n_ctx + dec_batch * dec_seq

    def ctx_tiles(self, tm):
        if self.n_ctx % tm or self.dec_seq % tm:
            raise ValueError("row tile must divide the context rows and one latent sequence")
        return self.n_ctx // tm

    def mod_spec(self, tm):
        n_pt, tps = self.ctx_tiles(tm), self.dec_seq // tm
        return pl.BlockSpec((None, N_MOD, D_MODEL),
                            lambda i: (jnp.where(i < n_pt, 0, 1 + (i - n_pt) // tps), 0, 0))


def _ada_kernel(c_ref, w_ref, b_ref, o_ref):
    c = c_ref[...]
    s = c * _sigmoid(c)
    o_ref[...] = _bdot(s, w_ref[...]) + b_ref[...]


def _ada_call(cond, w_ada, b_ada):
    tn = 2304
    n_out = N_MOD * D_MODEL
    return pl.pallas_call(
        _ada_kernel,
        grid=(DEPTH, n_out // tn),
        in_specs=[
            pl.BlockSpec((N_COND, D_MODEL), lambda l, j: (0, 0)),
            pl.BlockSpec((None, D_MODEL, tn), lambda l, j: (l, 0, j)),
            pl.BlockSpec((None, 1, tn), lambda l, j: (l, 0, j)),
        ],
        out_specs=pl.BlockSpec((None, N_COND, tn), lambda l, j: (l, 0, j)),
        out_shape=jax.ShapeDtypeStruct((DEPTH, N_COND, n_out), F32),
        compiler_params=_cparams(2),
        name="adaln",
    )(cond, w_ada, b_ada.reshape(DEPTH, 1, n_out))


FF_CHUNKS = (768, 768, 768, 512)


def _ffn_kernel(*refs, base, n_x, final, n_pt):
    x_refs = refs[:n_x]
    mod_ref, g_ref, gf_ref, win_ref, wout_ref = refs[n_x:n_x + 5]
    o_refs = refs[n_x + 5:]
    i = pl.program_id(0)
    if n_x == 2:
        x = jnp.where(i < n_pt, x_refs[0][...], x_refs[1][...])
    else:
        x = x_refs[0][...]
    mod = mod_ref[...]
    u = _modulated_norm(x, g_ref[...], mod, base).astype(BF16)
    y = None
    lo = 0
    for width in FF_CHUNKS:
        a = jnp.dot(u, win_ref[:, lo:lo + width], preferred_element_type=F32)
        g = jnp.dot(u, win_ref[:, D_FF + lo:D_FF + lo + width], preferred_element_type=F32)
        hh = (a * _sigmoid(a) * g).astype(BF16)
        part = jnp.dot(hh, wout_ref[lo:lo + width, :], preferred_element_type=F32)
        y = part if y is None else y + part
        lo += width
    xn = x + 0.5 * mod[base + 2:base + 3] * y
    if not final:
        o_refs[0][...] = xn
    else:
        xn = _rms(xn, gf_ref[...])

        @pl.when(i < n_pt)
        def _():
            o_refs[0][...] = xn

        @pl.when(i >= n_pt)
        def _():
            o_refs[1][...] = xn


def _ffn_call(xs, mods, g_norm, g_final, w_in, w_out, st, *, layer, sub, final):
    tm = TM_WIDE
    n_pt = st.ctx_tiles(tm)
    n_tiles = st.n_all // tm
    ctx_spec = pl.BlockSpec((tm, D_MODEL), lambda i: (jnp.minimum(i, n_pt - 1), 0))
    lat_spec = pl.BlockSpec((tm, D_MODEL), lambda i: (jnp.maximum(i - n_pt, 0), 0))
    all_spec = pl.BlockSpec((tm, D_MODEL), lambda i: (i, 0))
    x_specs = [ctx_spec, lat_spec] if len(xs) == 2 else [all_spec]
    if final:
        out_specs = [ctx_spec, lat_spec]
        out_shape = [jax.ShapeDtypeStruct((st.n_ctx, D_MODEL), F32),
                     jax.ShapeDtypeStruct((st.n_all - st.n_ctx, D_MODEL), F32)]
    else:
        out_specs = [all_spec]
        out_shape = [jax.ShapeDtypeStruct((st.n_all, D_MODEL), F32)]
    return pl.pallas_call(
        functools.partial(_ffn_kernel, base=3 * sub, n_x=len(xs), final=final, n_pt=n_pt),
        grid=(n_tiles,),
        in_specs=x_specs + [
            st.mod_spec(tm),
            pl.BlockSpec((None, None, 1, D_MODEL), lambda i: (layer, sub, 0, 0)),
            pl.BlockSpec((1, D_MODEL), lambda i: (0, 0)),
            _layer_spec((D_MODEL, 2 * D_FF), None),
            _layer_spec((D_FF, D_MODEL), None),
        ],
        out_specs=out_specs,
        out_shape=out_shape,
        compiler_params=_cparams(1),
        name="ffn",
    )(*xs, mods, g_norm, g_final, w_in, w_out)


def _swap32(x):
    lane = lax.broadcasted_iota(jnp.int32, x.shape, 1)
    return jnp.where((lane & (DH_ATTN // 2)) == 0,
                     pltpu.roll(x, LANES - DH_ATTN // 2, 1), pltpu.roll(x, DH_ATTN // 2, 1))


def _proj_kernel(*refs, n_alias, n_pt, seqs_per_tile, seq):
    (x_ref, mod_ref, g_ref, wt_ref, bgt_ref, cos_ref, sin_ref) = refs[:7]
    (q_ref, k_ref, ck_ref, cv_ref, zf_ref, mq_ref, mk_ref, mo_ref, vat_ref, vt_ref, gt_ref) = refs[7 + n_alias:]
    i = pl.program_id(0)
    is_latent = i >= n_pt
    x = x_ref[...]
    u = _modulated_norm(x, g_ref[...], mod_ref[...], 3).astype(BF16)
    za = _bdot_nt(u, wt_ref[:3 * W_ATTN, :])
    cos_t = cos_ref[...]
    sin_t = sin_ref[...]
    plain_k = []
    for h in range(N_ATTN):
        zq = za[:, h * DV_ATTN:(h + 1) * DV_ATTN]
        zk = za[:, W_ATTN + h * DV_ATTN:W_ATTN + (h + 1) * DV_ATTN]
        plain_k.append(zk)
        q_ref[h] = jnp.where(is_latent, zq * cos_t + _swap32(zq) * sin_t, zq).astype(q_ref.dtype)
        k_ref[h] = jnp.where(is_latent, zk * cos_t + _swap32(zk) * sin_t, zk).astype(k_ref.dtype)

    @pl.when(i < n_pt)
    def _():
        for h in range(N_ATTN):
            zv = za[:, 2 * W_ATTN + h * DV_ATTN:2 * W_ATTN + (h + 1) * DV_ATTN]
            for s in range(seqs_per_tile):
                rs = slice(s * seq, (s + 1) * seq)
                ck_ref[s, h] = plain_k[h][rs]
                cv_ref[s, h] = zv[rs]

    zb = _bdot_nt(u, wt_ref[3 * W_ATTN:MV_LO, :])
    zf_ref[...] = zb[:, :W_FOUR].astype(zf_ref.dtype)
    off = W_FOUR
    mq_ref[...] = zb[:, off:off + W_MLSTM].astype(mq_ref.dtype)
    mk_ref[...] = (zb[:, off + W_MLSTM:off + 2 * W_MLSTM] * MLSTM_K_SCALE).astype(mk_ref.dtype)
    mo_ref[...] = _bdot_nt(u, wt_ref[MV_HI:P_MAIN, :])
    vat_ref[...] = _bdot_nt(wt_ref[2 * W_ATTN:3 * W_ATTN, :], u).astype(vat_ref.dtype)
    vt_ref[...] = _bdot_nt(wt_ref[MV_LO:MV_HI, :], u)
    gpt = _bdot_nt(wt_ref[P_MAIN:, :], u) + bgt_ref[...]
    sub = lax.broadcasted_iota(jnp.int32, gpt.shape, 0)
    gt_ref[...] = jnp.where((sub & N_MLSTM) != 0, _log_sigmoid(gpt), gpt)


def _proj_call(x, mods, g_norm, w_t, bgt, cos_t, sin_t, kv_prev, st, *, layer):
    tm = TM_PROJ
    n_pt = st.ctx_tiles(tm)
    if tm % st.seq:
        raise ValueError("a context row tile must hold whole sequences")
    seqs_per_tile, tps = tm // st.seq, st.dec_seq // tm
    n = st.n_all
    head_shape = jax.ShapeDtypeStruct((N_ATTN, n, DV_ATTN), BF16)
    head_spec = pl.BlockSpec((N_ATTN, tm, DV_ATTN), lambda i: (0, i, 0))
    kv_shape = jax.ShapeDtypeStruct((st.batch, DEPTH, N_ATTN, st.seq, DV_ATTN), F32)
    kv_spec = pl.BlockSpec((seqs_per_tile, None, N_ATTN, st.seq, DV_ATTN),
                           lambda i: (jnp.minimum(i, n_pt - 1), layer, 0, 0, 0))
    tok_shape = jax.ShapeDtypeStruct((n, W_ATTN), F32)
    tok_bf16 = jax.ShapeDtypeStruct((n, W_ATTN), BF16)
    tok_spec = pl.BlockSpec((tm, W_ATTN), lambda i: (i, 0))
    rope_spec = pl.BlockSpec((tm, LANES), lambda i: (jnp.maximum(i - n_pt, 0) % tps, 0))
    alias_in = list(kv_prev) if kv_prev is not None else []
    n_in = 7
    return pl.pallas_call(
        functools.partial(_proj_kernel, n_alias=len(alias_in), n_pt=n_pt, seqs_per_tile=seqs_per_tile,
                          seq=st.seq),
        grid=(n // tm,),
        in_specs=[
            pl.BlockSpec((tm, D_MODEL), lambda i: (i, 0)),
            st.mod_spec(tm),
            pl.BlockSpec((None, None, 1, D_MODEL), lambda i: (layer, 1, 0, 0)),
            _layer_spec((w_t.shape[1], D_MODEL), layer),
            _layer_spec((N_GATE, 1), layer),
            rope_spec, rope_spec,
        ] + [pl.BlockSpec(memory_space=pl.ANY)] * len(alias_in),
        out_specs=[head_spec, head_spec, kv_spec, kv_spec, tok_spec, tok_spec, tok_spec, tok_spec,
                   pl.BlockSpec((W_ATTN, tm), lambda i: (0, i)),
                   pl.BlockSpec((W_MLSTM, tm), lambda i: (0, i)),
                   pl.BlockSpec((N_GATE, tm), lambda i: (0, i))],
        out_shape=[head_shape, head_shape, kv_shape, kv_shape, tok_bf16, tok_bf16, tok_bf16, tok_shape,
                   jax.ShapeDtypeStruct((W_ATTN, n), BF16),
                   jax.ShapeDtypeStruct((W_MLSTM, n), F32),
                   jax.ShapeDtypeStruct((N_GATE, n), F32)],
        input_output_aliases={n_in + j: 2 + j for j in range(len(alias_in))},
        compiler_params=_cparams(1),
        name="mixer_proj",
    )(x, mods, g_norm, w_t, bgt, cos_t, sin_t, *alias_in)


def _attn_kernel(*refs, lam_init, cached, nb, tq, seq):
    q_ref, k_ref, vt_ref = refs[:3]
    pos = 3
    if cached:
        ck_ref, cv_ref = refs[pos:pos + 2]
        pos += 2
    lam_ref, g_ref = refs[pos:pos + 2]
    o_ref = refs[-1]
    lp = lam_ref[...]
    lam = (jnp.exp(jnp.sum(lp[0:1] * lp[1:2], axis=-1, keepdims=True))
           - jnp.exp(jnp.sum(lp[2:3] * lp[3:4], axis=-1, keepdims=True)) + lam_init)
    g_sub = g_ref[...]
    lane = lax.broadcasted_iota(jnp.int32, (tq, DV_ATTN), 1)
    heads = range(N_ATTN)
    results = []
    for bi in range(nb):
        ks = slice(bi * seq, (bi + 1) * seq)
        keys = [[k_ref[h, ks] for h in heads]]
        vals_t = [[vt_ref[h * DV_ATTN:(h + 1) * DV_ATTN, ks] for h in heads]]
        if cached:
            keys.insert(0, [ck_ref[h].astype(BF16) for h in heads])
            vals_t.insert(0, [cv_ref[h].T.astype(BF16) for h in heads])
        qh = [q_ref[h, bi * tq:(bi + 1) * tq].astype(F32) * (ATTN_SCALE * LOG2E) for h in heads]
        exps, dens = [], []
        for first_map in (True, False):
            qm = [jnp.where((lane < DH_ATTN) == first_map, q, 0.0).astype(BF16) for q in qh]
            s = [jnp.concatenate([_bdot_nt(kg[h], qm[h]) for h in heads], axis=1) for kg in keys]
            m = functools.reduce(jnp.maximum, [jnp.max(si, axis=0, keepdims=True) for si in s])
            e = [jnp.exp2(si - m) for si in s]
            exps.append(e)
            dens.append(functools.reduce(jnp.add, [jnp.sum(ei, axis=0, keepdims=True) for ei in e]))
        ratio = lam * dens[0] / dens[1]
        inv = 1.0 / dens[0]
        probs = [(e1 - e2 * ratio).astype(BF16) for e1, e2 in zip(*exps)]
        for h in heads:
            cols = slice(h * tq, (h + 1) * tq)
            o_t = None
            for pj, vg in zip(probs, vals_t):
                part = jnp.dot(vg[h], pj[:, cols], preferred_element_type=F32)
                o_t = part if o_t is None else o_t + part
            o_t = o_t * inv[:, cols]
            y_t = o_t * lax.rsqrt(jnp.mean(o_t * o_t, axis=0, keepdims=True) + EPS)
            results.append((bi, h, (y_t.T * g_sub * (1.0 - lam_init)).astype(o_ref.dtype)))
    for bi, h, y in results:
        o_ref[bi * tq:(bi + 1) * tq, h * DV_ATTN:(h + 1) * DV_ATTN] = y


def _attn_four_kernel(*refs, n_attn, attn_kw, four_kw, once_per_seq):
    a_ref, f_ref = refs[-2:]
    _attn_kernel(*refs[:n_attn], a_ref, **attn_kw)
    four = functools.partial(_four_kernel, *refs[n_attn:n_attn + 4], f_ref, **four_kw)
    if once_per_seq:
        pl.when(pl.program_id(1) == 0)(four)
    else:
        four()


def _attn_four_call(q, k, v_t, cache, lam_p, g_sub, zf, outs_prev, cast_w, st, *, layer, lam_init, latent):
    n = st.n_all
    if latent:
        batch, seq, nb, tq = st.dec_batch, st.dec_seq, 1, st.dec_seq // 2
        row0 = st.n_ctx
    else:
        batch, seq, nb, tq = st.batch, st.seq, 4, st.seq
        row0 = 0
    nq = seq // tq
    if nq > 1 and nb > 1:
        raise ValueError("query blocks of several sequences are not contiguous rows")
    q0, s0 = row0 // (nb * tq), row0 // (nb * seq)
    in_specs = [pl.BlockSpec((N_ATTN, nb * tq, DV_ATTN), lambda b, i: (0, q0 + b * nq + i, 0)),
                pl.BlockSpec((N_ATTN, nb * seq, DV_ATTN), lambda b, i: (0, s0 + b, 0)),
                pl.BlockSpec((W_ATTN, nb * seq), lambda b, i: (0, s0 + b))]
    args = [q, k, v_t]
    if cache is not None:
        past = cache[0].shape[3]
        c_spec = pl.BlockSpec((None, None, N_ATTN, past, DV_ATTN), lambda b, i: (b, layer, 0, 0, 0))
        in_specs += [c_spec, c_spec]
        args += list(cache)
    in_specs += [pl.BlockSpec((None, 4, DH_ATTN), lambda b, i: (layer, 0, 0)),
                 pl.BlockSpec((None, 1, DV_ATTN), lambda b, i: (layer, 0, 0))]
    args += [lam_p, g_sub]
    n_attn = len(args)
    w_d, ct, s_t = _dft_tables(seq)
    seq_rows = pl.BlockSpec((nb * seq, W_FOUR), lambda b, i: (s0 + b, 0))
    in_specs += [seq_rows,
                 pl.BlockSpec((DG_FOUR, 2 * DG_FOUR), lambda b, i: (0, 0)),
                 pl.BlockSpec((seq, seq), lambda b, i: (0, 0)),
                 pl.BlockSpec((seq, seq), lambda b, i: (0, 0))]
    args += [zf, w_d, ct, s_t]
    aliases = {}
    if outs_prev is not None:
        aliases = {len(args): 0, len(args) + 1: 1}
        in_specs += [pl.BlockSpec(memory_space=pl.ANY)] * 2
        args += list(outs_prev)
    casts = _CastJobs(cast_w, (batch // nb) * nq, lambda b, i: b * nq + i)
    body = functools.partial(
        _attn_four_kernel, n_attn=n_attn, once_per_seq=nq > 1,
        attn_kw=dict(lam_init=lam_init, cached=cache is not None, nb=nb, tq=tq, seq=seq),
        four_kw=dict(nb=nb, seq=seq))
    return pl.pallas_call(
        casts.wrap(body, len(args), 2),
        grid=(batch // nb, nq),
        in_specs=in_specs + casts.in_specs(),
        out_specs=[pl.BlockSpec((nb * tq, W_ATTN), lambda b, i: (q0 + b * nq + i, 0)), seq_rows]
        + casts.out_specs(),
        out_shape=[jax.ShapeDtypeStruct((n, W_ATTN), BF16), jax.ShapeDtypeStruct((n, W_FOUR), BF16)]
        + casts.out_shapes(),
        input_output_aliases=aliases,
        compiler_params=_cparams(2),
        name="attn_fourier",
    )(*args, *casts.args())


def _dft_tables(seq):
    def cs(n):
        j = np.arange(n)
        ang = 2.0 * np.pi * ((j[:, None] * j[None, :]) % n) / n
        return np.cos(ang) / math.sqrt(n), np.sin(ang) / math.sqrt(n)

    cd, sd = cs(DG_FOUR)
    ct, st = cs(seq)
    w_d = jnp.asarray(np.concatenate([cd, sd], axis=1), F32)
    return w_d.astype(BF16), jnp.asarray(ct, F32).astype(BF16), jnp.asarray(-st, F32).astype(BF16)


def _four_kernel(*refs, nb, seq):
    z_ref, wd_ref, ct_ref, st_ref = refs[:4]
    o_ref = refs[-1]
    y_cos, y_sin = [], []
    for gidx in range(N_FOUR):
        y = jnp.dot(z_ref[:, gidx * DG_FOUR:(gidx + 1) * DG_FOUR], wd_ref[...], preferred_element_type=F32)
        y_cos.append(y[:, :DG_FOUR].astype(BF16))
        y_sin.append(y[:, DG_FOUR:].astype(BF16))
    y_cos = jnp.concatenate(y_cos, axis=1)
    y_sin = jnp.concatenate(y_sin, axis=1)
    for s in range(nb):
        rows = slice(s * seq, (s + 1) * seq)
        o_ref[rows, :] = (jnp.dot(ct_ref[...], y_cos[rows], preferred_element_type=F32)
                          + jnp.dot(st_ref[...], y_sin[rows], preferred_element_type=F32)).astype(o_ref.dtype)


def _mlstm_kernel(*refs, seq, nb, seeded, emit_state, n_alias):
    mq_ref, mk_ref, vt_ref, mo_ref, gt_ref, g_ref = refs[:6]
    pos = 6
    if seeded:
        c0_ref, n0_ref, m0_ref = refs[pos:pos + 3]
        pos += 3
    pos += n_alias
    o_ref = refs[pos]
    if emit_state:
        c1_ref, n1_ref, m1_ref = refs[pos + 1:pos + 4]

    L = min(MLSTM_CHUNK, seq)
    nc = seq // L
    s_idx = lax.broadcasted_iota(jnp.int32, (L, L), 0)
    t_idx = lax.broadcasted_iota(jnp.int32, (L, L), 1)
    before = (s_idx <= t_idx, s_idx >= t_idx)
    tri = (jnp.where(before[1], 1.0, 0.0).astype(BF16),
           jnp.where(before[0], 1.0, 0.0).astype(BF16))
    chains = [(bi, d, h) for bi in range(nb) for d in range(2) for h in range(N_MLSTM)]

    state = {}
    for (bi, d, h) in chains:
        if seeded:
            state[bi, d, h] = (c0_ref[bi, d, h], n0_ref[bi, d, h:h + 1, :], m0_ref[bi, d:d + 1, h:h + 1])
        else:
            state[bi, d, h] = (None, None, jnp.zeros((1, 1), F32))

    h_t = {}
    for c in range(nc):
        terms = {}
        for bi in range(nb):
            for d in range(2):
                r0 = bi * seq + (c if d == 0 else nc - 1 - c) * L
                g_t = gt_ref[:, r0:r0 + L]
                hi = g_t.astype(BF16)
                rem = g_t - hi.astype(F32)
                mid = rem.astype(BF16)
                lo = (rem - mid.astype(F32)).astype(BF16)
                pieces = _bdot_nt(jnp.concatenate([hi, mid, lo], axis=0), tri[d])
                cum_t = pieces[0:N_GATE] + pieces[N_GATE:2 * N_GATE] + pieces[2 * N_GATE:]
                c_t = (g_t - pltpu.roll(cum_t, N_GATE - N_MLSTM, 0)) * LOG2E
                col = jnp.concatenate([c_t, jnp.zeros((LANES - N_GATE, L), F32)], axis=0).T
                terms[bi, d] = (r0, col, g_t, cum_t)

        new_state = {}
        need_update = emit_state or c < nc - 1
        group = [(d, h) for d in range(2) for h in range(N_MLSTM)]
        for bi in range(nb):
            has_state = state[bi, 0, 0][0] is not None
            qs, ks, vts, cbs, s0s, i_rows, b_rows, m_prevs = [], [], [], [], [], [], [], []
            for d, h in group:
                r0, col, g_t, cum_t = terms[bi, d]
                ci = 2 * d * N_MLSTM + h
                cf = ci + N_MLSTM
                hs = slice(h * DH_MLSTM, (h + 1) * DH_MLSTM)
                qs.append(mq_ref[r0:r0 + L, hs])
                ks.append(mk_ref[r0:r0 + L, hs])
                vts.append(vt_ref[hs, r0:r0 + L])
                cbs.append(jnp.where(before[d], col[:, ci:ci + 1], -jnp.inf))
                s0s.append(_bdot_nt(ks[-1], qs[-1]))
                i_rows.append(g_t[ci:ci + 1, :])
                b_rows.append(cum_t[cf:cf + 1, :])
                m_prevs.append(jnp.broadcast_to(state[bi, d, h][2], (1, L)))
            cb = jnp.concatenate(cbs, axis=1)
            i_row, b_row = jnp.concatenate(i_rows, axis=1), jnp.concatenate(b_rows, axis=1)
            m_prev = jnp.concatenate(m_prevs, axis=1)
            m2_prev = m_prev * LOG2E
            m2_row = jnp.maximum(jnp.max(cb, axis=0, keepdims=True), m2_prev)
            s_t = jnp.concatenate(s0s, axis=1) * jnp.exp2(cb - m2_row)
            den = jnp.sum(s_t, axis=0, keepdims=True)
            s_bf = s_t.astype(BF16)
            m_t = b_row + m2_row * LN2
            floor = jnp.exp(-m_t)
            if has_state:
                sp = jnp.exp2(m2_prev - m2_row)
            if need_update:
                tot, new = [], []
                for j, (d, h) in enumerate(group):
                    last = j * L + (L - 1 if d == 0 else 0)
                    tot.append(jnp.broadcast_to(b_row[:, last:last + 1], (1, L)))
                    new.append(jnp.broadcast_to(m_t[:, last:last + 1], (1, L)))
                b_tot, m_new = jnp.concatenate(tot, axis=1), jnp.concatenate(new, axis=1)
                wl = jnp.exp(b_tot + (i_row - b_row) - m_new)
                if has_state:
                    decay = jnp.exp(b_tot + m_prev - m_new)
            for j, (d, h) in enumerate(group):
                cols = slice(j * L, (j + 1) * L)
                c_prev, n_prev, _ = state[bi, d, h]
                num_t = jnp.dot(vts[j].astype(BF16), s_bf[:, cols], preferred_element_type=F32)
                den_j = den[:, cols]
                if has_state:
                    cn = jnp.concatenate([c_prev, jnp.broadcast_to(n_prev, (SUBLANES, DH_MLSTM))], axis=0)
                    cq = _bdot_nt(cn, qs[j])
                    num_t = num_t + sp[:, cols] * cq[:DH_MLSTM]
                    den_j = den_j + sp[:, cols] * cq[DH_MLSTM:DH_MLSTM + 1]
                h_t[bi, d, h, c] = num_t / jnp.maximum(jnp.abs(den_j), floor[:, cols])
                if need_update:
                    wl_j = wl[:, cols]
                    vw = jnp.concatenate([vts[j] * wl_j, jnp.broadcast_to(wl_j, (SUBLANES, L))], axis=0)
                    upd = jnp.dot(vw.astype(BF16), ks[j], preferred_element_type=F32)
                    c_new, n_new = upd[:DH_MLSTM], upd[DH_MLSTM:DH_MLSTM + 1]
                    if has_state:
                        decay_j = decay[:, j * L:j * L + 1]
                        c_new = decay_j * c_prev + c_new
                        n_new = decay_j * n_prev + n_new
                    new_state[bi, d, h] = (c_new, n_new, m_new[:, j * L:j * L + 1])
        state = new_state

    g_m = g_ref[...]
    for bi in range(nb):
        for h in range(N_MLSTM):
            hs = slice(h * DH_MLSTM, (h + 1) * DH_MLSTM)
            fwd = [h_t[bi, 0, h, c] for c in range(nc)]
            bwd = [h_t[bi, 1, h, nc - 1 - c] for c in range(nc)]
            hsum = (fwd[0] if nc == 1 else jnp.concatenate(fwd, axis=1)) \
                + (bwd[0] if nc == 1 else jnp.concatenate(bwd, axis=1))
            y = hsum * lax.rsqrt(jnp.mean(hsum * hsum, axis=0, keepdims=True) + EPS)
            rows = slice(bi * seq, (bi + 1) * seq)
            o_ref[rows, hs] = (y.T * g_m * _sigmoid(mo_ref[rows, hs])).astype(o_ref.dtype)

    if emit_state:
        for (bi, d, h) in chains:
            c_fin, n_fin, m_fin = state[bi, d, h]
            c1_ref[bi, d, h] = c_fin
            n1_ref[bi, d, h:h + 1, :] = n_fin
            m1_ref[bi, d:d + 1, h:h + 1] = m_fin


def _mlstm_call(mq, mk, vt, mo, gates_t, g_m, state, alias_prev, cast_w, st, *, layer, latent):
    if latent:
        batch, seq, nb, row0 = st.dec_batch, st.dec_seq, 1, st.n_ctx
    else:
        batch, seq, nb, row0 = st.batch, st.seq, 4, 0
    emit_state = not latent
    rows = nb * seq
    b0 = row0 // rows
    tok_spec = pl.BlockSpec((rows, W_MLSTM), lambda b: (b0 + b, 0))
    in_specs = [tok_spec, tok_spec,
                pl.BlockSpec((W_MLSTM, rows), lambda b: (0, b0 + b)),
                tok_spec,
                pl.BlockSpec((N_GATE, rows), lambda b: (0, b0 + b)),
                pl.BlockSpec((None, 1, DH_MLSTM), lambda b: (layer, 0, 0))]
    args = [mq, mk, vt, mo, gates_t, g_m]
    state_specs = [
        pl.BlockSpec((nb, None, 2, N_MLSTM, DH_MLSTM, DH_MLSTM), lambda b: (b, layer, 0, 0, 0, 0)),
        pl.BlockSpec((nb, None, 2, N_MLSTM, DH_MLSTM), lambda b: (b, layer, 0, 0, 0)),
        pl.BlockSpec((nb, None, 2, N_MLSTM), lambda b: (b, layer, 0, 0)),
    ]
    if state is not None:
        in_specs += state_specs
        args += list(state)
    alias_in = list(alias_prev) if alias_prev is not None else []
    n_in = len(args)
    in_specs += [pl.BlockSpec(memory_space=pl.ANY)] * len(alias_in)
    args += alias_in
    out_specs = [tok_spec]
    out_shape = [jax.ShapeDtypeStruct((st.n_all, W_MLSTM), BF16)]
    if emit_state:
        out_specs += state_specs
        out_shape += [
            jax.ShapeDtypeStruct((batch, DEPTH, 2, N_MLSTM, DH_MLSTM, DH_MLSTM), F32),
            jax.ShapeDtypeStruct((batch, DEPTH, 2, N_MLSTM, DH_MLSTM), F32),
            jax.ShapeDtypeStruct((batch, DEPTH, 2, N_MLSTM), F32),
        ]
    first_out = 1 if emit_state else 0
    casts = _CastJobs(cast_w, batch // nb, lambda b: b)
    body = functools.partial(_mlstm_kernel, seq=seq, nb=nb, seeded=state is not None, emit_state=emit_state,
                             n_alias=len(alias_in))
    return pl.pallas_call(
        casts.wrap(body, len(args), len(out_specs)),
        grid=(batch // nb,),
        in_specs=in_specs + casts.in_specs(),
        out_specs=out_specs + casts.out_specs(),
        out_shape=out_shape + casts.out_shapes(),
        input_output_aliases={n_in + j: first_out + j for j in range(len(alias_in))},
        compiler_params=_cparams(1),
        name="mlstm",
    )(*args, *casts.args())


def _merge_kernel(x_ref, a_ref, f_ref, m_ref, mod_ref, g_ref, wg_ref, wa_ref, wf_ref, wm_ref, wo_ref, o_ref):
    x = x_ref[...]
    mod = mod_ref[...]
    u = _modulated_norm(x, g_ref[...], mod, 3).astype(BF16)
    merged = None
    for j, (br_ref, w_ref) in enumerate(((a_ref, wa_ref), (f_ref, wf_ref), (m_ref, wm_ref))):
        gate = _sigmoid(jnp.dot(u, wg_ref[:, j * D_MODEL:(j + 1) * D_MODEL], preferred_element_type=F32))
        term = gate * _bdot(br_ref[...], w_ref[...])
        merged = term if merged is None else merged + term
    out = _bdot(merged, wo_ref[...])
    o_ref[...] = x + mod[5:6] * out


def _merge_call(x, a, f, m, mods, g_norm, wg, wa, wf, wm, wo, st, *, layer):
    tm = TM_WIDE
    n = st.n_all
    br_spec = pl.BlockSpec((tm, W_ATTN), lambda i: (i, 0))
    return pl.pallas_call(
        _merge_kernel,
        grid=(n // tm,),
        in_specs=[
            pl.BlockSpec((tm, D_MODEL), lambda i: (i, 0)),
            br_spec, br_spec, br_spec,
            st.mod_spec(tm),
            pl.BlockSpec((None, None, 1, D_MODEL), lambda i: (layer, 1, 0, 0)),
            _layer_spec((D_MODEL, 3 * D_MODEL), None),
            _layer_spec((W_ATTN, D_MODEL), None), _layer_spec((W_FOUR, D_MODEL), None),
            _layer_spec((W_MLSTM, D_MODEL), None),
            _layer_spec((D_MODEL, D_MODEL), None),
        ],
        out_specs=pl.BlockSpec((tm, D_MODEL), lambda i: (i, 0)),
        out_shape=jax.ShapeDtypeStruct((n, D_MODEL), F32),
        compiler_params=_cparams(1),
        name="merge",
    )(x, a, f, m, mods, g_norm, wg, wa, wf, wm, wo)


def _rope_tables(n_tok):
    tok = np.arange(n_tok)
    inv = ROPE_BASE ** (-np.arange(ROPE_AXIS_PAIRS, dtype=np.float32) / ROPE_AXIS_PAIRS)
    ang = np.concatenate([(tok // GRID_W).astype(np.float32)[:, None] * inv,
                          (tok % GRID_W).astype(np.float32)[:, None] * inv], axis=-1).astype(np.float32)
    c, s = np.cos(ang), np.sin(ang)
    cos_t = np.concatenate([c, c, c, c], axis=-1)
    sin_t = np.concatenate([-s, s, -s, s], axis=-1)
    return jnp.asarray(cos_t, F32), jnp.asarray(sin_t, F32)


def kernel(x_prompt, x_sample, cache_k, cache_v, state_C, state_n, state_m, c, c_ctx, w_ada, b_ada, g_norm,
           w_ffn1_in, w_ffn1_out, w_ffn2_in, w_ffn2_out, w_in, b_mgate, attn_lambda, g_attn_sub, g_mlstm,
           w_branch_gate, w_br_attn, w_br_four, w_br_mlstm, w_out, g_final):
    batch, seq, _ = x_prompt.shape
    dec_batch, dec_seq, _ = x_sample.shape
    st = _Stream(batch, seq, dec_batch, dec_seq)
    cond = jnp.zeros((N_COND, D_MODEL), F32).at[0].set(c_ctx).at[1:1 + dec_batch].set(c)
    mods_all = _ada_call(cond, w_ada, b_ada).reshape(DEPTH, N_COND, N_MOD, D_MODEL)

    b_gate_t = b_mgate[:, :, None]
    g_norm4 = g_norm[:, :, None, :]
    g_fin = g_final[None, :]
    w_t = jnp.swapaxes(w_in, 1, 2).astype(BF16)
    g_sub3 = g_attn_sub[:, None, :]
    g_m3 = g_mlstm[:, None, :]
    cos_t, sin_t = _rope_tables(dec_seq)
    lat_state = (state_C, state_n, state_m)

    xs = (x_prompt.reshape(batch * seq, D_MODEL), x_sample.reshape(dec_batch * dec_seq, D_MODEL))
    kv, states = None, None
    ffn1_w = (w_ffn1_in[0].astype(BF16), w_ffn1_out[0].astype(BF16))
    for l in range(DEPTH):
        mods = mods_all[l]
        lam_init = 0.8 - 0.6 * math.exp(-0.3 * l)
        nxt = l + 1 < DEPTH
        (x,) = _ffn_call(xs, mods, g_norm4, g_fin, *ffn1_w, st, layer=l, sub=0, final=False)
        q, k, ck, cv, zf, mq, mk, mo, vat, vt, gates_t = _proj_call(
            x, mods, g_norm4, w_t, b_gate_t, cos_t, sin_t, kv, st, layer=l)
        kv = (ck, cv)
        a, f, ffn2_in, *nxt_in = _attn_four_call(
            q, k, vat, None, attn_lambda, g_sub3, zf, None,
            [(w_ffn2_in, l)] + ([(w_ffn1_in, l + 1)] if nxt else []), st,
            layer=l, lam_init=lam_init, latent=False)
        a, f, ffn2_out, w_bg = _attn_four_call(
            q, k, vat, (cache_k, cache_v), attn_lambda, g_sub3, zf, (a, f),
            [(w_ffn2_out, l), (w_branch_gate, l)], st, layer=l, lam_init=lam_init, latent=True)
        m, c_fin, n_fin, m_fin, *nxt_out = _mlstm_call(
            mq, mk, vt, mo, gates_t, g_m3, None, states, [(w_ffn1_out, l + 1)] if nxt else [], st,
            layer=l, latent=False)
        states = [c_fin, n_fin, m_fin]
        m, w_ba, w_bf, w_bm, w_o = _mlstm_call(
            mq, mk, vt, mo, gates_t, g_m3, lat_state, (m,),
            [(w_br_attn, l), (w_br_four, l), (w_br_mlstm, l), (w_out, l)], st, layer=l, latent=True)
        x = _merge_call(x, a, f, m, mods, g_norm4, w_bg, w_ba, w_bf, w_bm, w_o, st, layer=l)
        xs = tuple(_ffn_call((x,), mods, g_norm4, g_fin, ffn2_in, ffn2_out, st, layer=l, sub=2,
                             final=(l == DEPTH - 1)))
        if nxt:
            ffn1_w = (nxt_in[0], nxt_out[0])
    y_prompt, y_sample = xs
    return (y_prompt.reshape(batch, seq, D_MODEL), y_sample.reshape(dec_batch, dec_seq, D_MODEL),
            *kv, *states)
```

```python
import functools
import math

import numpy as np
import jax
import jax.numpy as jnp
from jax import lax
from jax.experimental import pallas as pl
from jax.experimental.pallas import tpu as pltpu

D_MODEL = 1024
DEPTH = 2
GRID_W = 64
N_ATTN = 4
DH_ATTN = 64
DV_ATTN = 2 * DH_ATTN
W_ATTN = N_ATTN * DV_ATTN
N_FOUR = 4
DG_FOUR = 128
W_FOUR = N_FOUR * DG_FOUR
N_MLSTM = 4
DH_MLSTM = 128
W_MLSTM = N_MLSTM * DH_MLSTM
N_GATE = 4 * N_MLSTM
LANES = 128
SUBLANES = 16
MV_LO = 3 * W_ATTN + W_FOUR + 2 * W_MLSTM
MV_HI = MV_LO + W_MLSTM
P_MAIN = MV_HI + W_MLSTM
D_FF = 2816
N_MOD = 9
N_COND = 8
ROPE_BASE = 10000.0
ROPE_AXIS_PAIRS = DH_ATTN // 4
ATTN_SCALE = DH_ATTN ** -0.5
LOG2E = 1.4426950408889634
LN2 = 0.6931471805599453
MLSTM_K_SCALE = DH_MLSTM ** -0.5
EPS = 1e-6
MLSTM_CHUNK = 256
VMEM_LIMIT = 56 * 1024 * 1024
TM_WIDE = 1024
TM_PROJ = 1024

F32 = jnp.float32
BF16 = jnp.bfloat16


def _cparams(n_grid):
    return pltpu.CompilerParams(dimension_semantics=("arbitrary",) * n_grid,
                                vmem_limit_bytes=VMEM_LIMIT)


def _bdot(a, b):
    return jnp.dot(a.astype(BF16), b.astype(BF16), preferred_element_type=F32)


def _bdot_nt(a, b):
    return lax.dot_general(a.astype(BF16), b.astype(BF16), (((1,), (1,)), ((), ())),
                           preferred_element_type=F32)


def _sigmoid(x):
    return 1.0 / (1.0 + jnp.exp(-x))


def _log_sigmoid(x):
    return jnp.minimum(x, 0.0) - jnp.log1p(jnp.exp(-jnp.abs(x)))


def _rms(x, g):
    return x * lax.rsqrt(jnp.mean(x * x, axis=-1, keepdims=True) + EPS) * g


def _modulated_norm(x, g, mod, base):
    return _rms(x, g) * (1.0 + mod[base + 1:base + 2]) + mod[base:base + 1]


def _layer_spec(shape, layer):
    if layer is None:
        return pl.BlockSpec(shape, lambda *_: (0,) * len(shape), pipeline_mode=pl.Buffered(1))
    return pl.BlockSpec((None,) + shape, lambda *_: (layer,) + (0,) * len(shape),
                        pipeline_mode=pl.Buffered(1))


class _CastJobs:
    def __init__(self, jobs, steps, step_index):
        self.jobs, self.steps, self.step_index = list(jobs), steps, step_index

    def __len__(self):
        return len(self.jobs)

    def _rows(self, w):
        rows = w.shape[1] // self.steps
        if rows * self.steps != w.shape[1] or rows % SUBLANES:
            raise ValueError("weight rows do not split into bf16 row tiles over the grid")
        return rows

    def in_specs(self):
        return [pl.BlockSpec((None, self._rows(w), w.shape[2]),
                             lambda *g, layer=layer: (layer, self.step_index(*g), 0))
                for w, layer in self.jobs]

    def out_specs(self):
        return [pl.BlockSpec((self._rows(w), w.shape[2]), lambda *g: (self.step_index(*g), 0))
                for w, _ in self.jobs]

    def out_shapes(self):
        return [jax.ShapeDtypeStruct(w.shape[1:], BF16) for w, _ in self.jobs]

    def args(self):
        return [w for w, _ in self.jobs]

    def wrap(self, body, n_in, n_out):
        n = len(self.jobs)
        if n == 0:
            return body

        def kernel(*refs):
            ins, cast_in = refs[:n_in], refs[n_in:n_in + n]
            outs = refs[n_in + n:n_in + n + n_out]
            cast_out = refs[n_in + n + n_out:n_in + 2 * n + n_out]
            for src, dst in zip(cast_in, cast_out):
                dst[...] = src[...].astype(dst.dtype)
            body(*ins, *outs, *refs[n_in + 2 * n + n_out:])

        return kernel


class _Stream:
    def __init__(self, batch, seq, dec_batch, dec_seq):
        self.batch, self.seq, self.dec_batch, self.dec_seq = batch, seq, dec_batch, dec_seq
        self.n_ctx = batch * seq
        self.n_all = self.n_ctx + dec_batch * dec_seq

    def ctx_tiles(self, tm):
        if self.n_ctx % tm or self.dec_seq % tm:
            raise ValueError("row tile must divide the context rows and one latent sequence")
        return self.n_ctx // tm

    def mod_spec(self, tm):
        n_pt, tps = self.ctx_tiles(tm), self.dec_seq // tm
        return pl.BlockSpec((None, N_MOD, D_MODEL),
                            lambda i: (jnp.where(i < n_pt, 0, 1 + (i - n_pt) // tps), 0, 0))


def _ada_kernel(c_ref, w_ref, b_ref, o_ref):
    c = c_ref[...]
    s = c * _sigmoid(c)
    o_ref[...] = _bdot(s, w_ref[...]) + b_ref[...]


def _ada_call(cond, w_ada, b_ada):
    tn = 2304
    n_out = N_MOD * D_MODEL
    return pl.pallas_call(
        _ada_kernel,
        grid=(DEPTH, n_out // tn),
        in_specs=[
            pl.BlockSpec((N_COND, D_MODEL), lambda l, j: (0, 0)),
            pl.BlockSpec((None, D_MODEL, tn), lambda l, j: (l, 0, j)),
            pl.BlockSpec((None, 1, tn), lambda l, j: (l, 0, j)),
        ],
        out_specs=pl.BlockSpec((None, N_COND, tn), lambda l, j: (l, 0, j)),
        out_shape=jax.ShapeDtypeStruct((DEPTH, N_COND, n_out), F32),
        compiler_params=_cparams(2),
        name="adaln",
    )(cond, w_ada, b_ada.reshape(DEPTH, 1, n_out))


FF_CHUNKS = (768, 768, 768, 512)


def _ffn_kernel(*refs, base, n_x, final, n_pt):
    x_refs = refs[:n_x]
    mod_ref, g_ref, gf_ref, win_ref, wout_ref = refs[n_x:n_x + 5]
    o_refs = refs[n_x + 5:]
    i = pl.program_id(0)
    if n_x == 2:
        x = jnp.where(i < n_pt, x_refs[0][...], x_refs[1][...])
    else:
        x = x_refs[0][...]
    mod = mod_ref[...]
    u = _modulated_norm(x, g_ref[...], mod, base).astype(BF16)
    y = None
    lo = 0
    for width in FF_CHUNKS:
        a = jnp.dot(u, win_ref[:, lo:lo + width], preferred_element_type=F32)
        g = jnp.dot(u, win_ref[:, D_FF + lo:D_FF + lo + width], preferred_element_type=F32)
        hh = (a * _sigmoid(a) * g).astype(BF16)
        part = jnp.dot(hh, wout_ref[lo:lo + width, :], preferred_element_type=F32)
        y = part if y is None else y + part
        lo += width
    xn = x + 0.5 * mod[base + 2:base + 3] * y
    if not final:
        o_refs[0][...] = xn
    else:
        xn = _rms(xn, gf_ref[...])

        @pl.when(i < n_pt)
        def _():
            o_refs[0][...] = xn

        @pl.when(i >= n_pt)
        def _():
            o_refs[1][...] = xn


def _ffn_call(xs, mods, g_norm, g_final, w_in, w_out, st, *, layer, sub, final):
    tm = TM_WIDE
    n_pt = st.ctx_tiles(tm)
    n_tiles = st.n_all // tm
    ctx_spec = pl.BlockSpec((tm, D_MODEL), lambda i: (jnp.minimum(i, n_pt - 1), 0))
    lat_spec = pl.BlockSpec((tm, D_MODEL), lambda i: (jnp.maximum(i - n_pt, 0), 0))
    all_spec = pl.BlockSpec((tm, D_MODEL), lambda i: (i, 0))
    x_specs = [ctx_spec, lat_spec] if len(xs) == 2 else [all_spec]
    if final:
        out_specs = [ctx_spec, lat_spec]
        out_shape = [jax.ShapeDtypeStruct((st.n_ctx, D_MODEL), F32),
                     jax.ShapeDtypeStruct((st.n_all - st.n_ctx, D_MODEL), F32)]
    else:
        out_specs = [all_spec]
        out_shape = [jax.ShapeDtypeStruct((st.n_all, D_MODEL), F32)]
    return pl.pallas_call(
        functools.partial(_ffn_kernel, base=3 * sub, n_x=len(xs), final=final, n_pt=n_pt),
        grid=(n_tiles,),
        in_specs=x_specs + [
            st.mod_spec(tm),
            pl.BlockSpec((None, None, 1, D_MODEL), lambda i: (layer, sub, 0, 0)),
            pl.BlockSpec((1, D_MODEL), lambda i: (0, 0)),
            _layer_spec((D_MODEL, 2 * D_FF), None),
            _layer_spec((D_FF, D_MODEL), None),
        ],
        out_specs=out_specs,
        out_shape=out_shape,
        compiler_params=_cparams(1),
        name="ffn",
    )(*xs, mods, g_norm, g_final, w_in, w_out)


def _swap32(x):
    lane = lax.broadcasted_iota(jnp.int32, x.shape, 1)
    return jnp.where((lane & (DH_ATTN // 2)) == 0,
                     pltpu.roll(x, LANES - DH_ATTN // 2, 1), pltpu.roll(x, DH_ATTN // 2, 1))


def _proj_kernel(*refs, n_alias, n_pt, seqs_per_tile, seq):
    (x_ref, mod_ref, g_ref, wt_ref, bgt_ref, cos_ref, sin_ref) = refs[:7]
    (q_ref, k_ref, ck_ref, cv_ref, zf_ref, mq_ref, mk_ref, mo_ref, vat_ref, vt_ref, gt_ref) = refs[7 + n_alias:]
    i = pl.program_id(0)
    is_latent = i >= n_pt
    x = x_ref[...]
    u = _modulated_norm(x, g_ref[...], mod_ref[...], 3).astype(BF16)
    za = _bdot_nt(u, wt_ref[:3 * W_ATTN, :])
    cos_t = cos_ref[...]
    sin_t = sin_ref[...]
    plain_k = []
    for h in range(N_ATTN):
        zq = za[:, h * DV_ATTN:(h + 1) * DV_ATTN]
        zk = za[:, W_ATTN + h * DV_ATTN:W_ATTN + (h + 1) * DV_ATTN]
        plain_k.append(zk)
        q_ref[h] = jnp.where(is_latent, zq * cos_t + _swap32(zq) * sin_t, zq).astype(q_ref.dtype)
        k_ref[h] = jnp.where(is_latent, zk * cos_t + _swap32(zk) * sin_t, zk).astype(k_ref.dtype)

    @pl.when(i < n_pt)
    def _():
        for h in range(N_ATTN):
            zv = za[:, 2 * W_ATTN + h * DV_ATTN:2 * W_ATTN + (h + 1) * DV_ATTN]
            for s in range(seqs_per_tile):
                rs = slice(s * seq, (s + 1) * seq)
                ck_ref[s, h] = plain_k[h][rs]
                cv_ref[s, h] = zv[rs]

    zb = _bdot_nt(u, wt_ref[3 * W_ATTN:MV_LO, :])
    zf_ref[...] = zb[:, :W_FOUR].astype(zf_ref.dtype)
    off = W_FOUR
    mq_ref[...] = zb[:, off:off + W_MLSTM].astype(mq_ref.dtype)
    mk_ref[...] = (zb[:, off + W_MLSTM:off + 2 * W_MLSTM] * MLSTM_K_SCALE).astype(mk_ref.dtype)
    mo_ref[...] = _bdot_nt(u, wt_ref[MV_HI:P_MAIN, :])
    vat_ref[...] = _bdot_nt(wt_ref[2 * W_ATTN:3 * W_ATTN, :], u).astype(vat_ref.dtype)
    vt_ref[...] = _bdot_nt(wt_ref[MV_LO:MV_HI, :], u)
    gpt = _bdot_nt(wt_ref[P_MAIN:, :], u) + bgt_ref[...]
    sub = lax.broadcasted_iota(jnp.int32, gpt.shape, 0)
    gt_ref[...] = jnp.where((sub & N_MLSTM) != 0, _log_sigmoid(gpt), gpt)


def _proj_call(x, mods, g_norm, w_t, bgt, cos_t, sin_t, kv_prev, st, *, layer):
    tm = TM_PROJ
    n_pt = st.ctx_tiles(tm)
    if tm % st.seq:
        raise ValueError("a context row tile must hold whole sequences")
    seqs_per_tile, tps = tm // st.seq, st.dec_seq // tm
    n = st.n_all
    head_shape = jax.ShapeDtypeStruct((N_ATTN, n, DV_ATTN), BF16)
    head_spec = pl.BlockSpec((N_ATTN, tm, DV_ATTN), lambda i: (0, i, 0))
    kv_shape = jax.ShapeDtypeStruct((st.batch, DEPTH, N_ATTN, st.seq, DV_ATTN), F32)
    kv_spec = pl.BlockSpec((seqs_per_tile, None, N_ATTN, st.seq, DV_ATTN),
                           lambda i: (jnp.minimum(i, n_pt - 1), layer, 0, 0, 0))
    tok_shape = jax.ShapeDtypeStruct((n, W_ATTN), F32)
    tok_bf16 = jax.ShapeDtypeStruct((n, W_ATTN), BF16)
    tok_spec = pl.BlockSpec((tm, W_ATTN), lambda i: (i, 0))
    rope_spec = pl.BlockSpec((tm, LANES), lambda i: (jnp.maximum(i - n_pt, 0) % tps, 0))
    alias_in = list(kv_prev) if kv_prev is not None else []
    n_in = 7
    return pl.pallas_call(
        functools.partial(_proj_kernel, n_alias=len(alias_in), n_pt=n_pt, seqs_per_tile=seqs_per_tile,
                          seq=st.seq),
        grid=(n // tm,),
        in_specs=[
            pl.BlockSpec((tm, D_MODEL), lambda i: (i, 0)),
            st.mod_spec(tm),
            pl.BlockSpec((None, None, 1, D_MODEL), lambda i: (layer, 1, 0, 0)),
            _layer_spec((w_t.shape[1], D_MODEL), layer),
            _layer_spec((N_GATE, 1), layer),
            rope_spec, rope_spec,
        ] + [pl.BlockSpec(memory_space=pl.ANY)] * len(alias_in),
        out_specs=[head_spec, head_spec, kv_spec, kv_spec, tok_spec, tok_spec, tok_spec, tok_spec,
                   pl.BlockSpec((W_ATTN, tm), lambda i: (0, i)),
                   pl.BlockSpec((W_MLSTM, tm), lambda i: (0, i)),
                   pl.BlockSpec((N_GATE, tm), lambda i: (0, i))],
        out_shape=[head_shape, head_shape, kv_shape, kv_shape, tok_bf16, tok_bf16, tok_bf16, tok_shape,
                   jax.ShapeDtypeStruct((W_ATTN, n), BF16),
                   jax.ShapeDtypeStruct((W_MLSTM, n), F32),
                   jax.ShapeDtypeStruct((N_GATE, n), F32)],
        input_output_aliases={n_in + j: 2 + j for j in range(len(alias_in))},
        compiler_params=_cparams(1),
        name="mixer_proj",
    )(x, mods, g_norm, w_t, bgt, cos_t, sin_t, *alias_in)


def _attn_kernel(*refs, lam_init, cached, nb, tq, seq):
    q_ref, k_ref, vt_ref = refs[:3]
    pos = 3
    if cached:
        ck_ref, cv_ref = refs[pos:pos + 2]
        pos += 2
    lam_ref, g_ref = refs[pos:pos + 2]
    o_ref = refs[-1]
    lp = lam_ref[...]
    lam = (jnp.exp(jnp.sum(lp[0:1] * lp[1:2], axis=-1, keepdims=True))
           - jnp.exp(jnp.sum(lp[2:3] * lp[3:4], axis=-1, keepdims=True)) + lam_init)
    g_sub = g_ref[...]
    lane = lax.broadcasted_iota(jnp.int32, (tq, DV_ATTN), 1)
    heads = range(N_ATTN)
    results = []
    for bi in range(nb):
        ks = slice(bi * seq, (bi + 1) * seq)
        keys = [[k_ref[h, ks] for h in heads]]
        vals_t = [[vt_ref[h * DV_ATTN:(h + 1) * DV_ATTN, ks] for h in heads]]
        if cached:
            keys.insert(0, [ck_ref[h].astype(BF16) for h in heads])
            vals_t.insert(0, [cv_ref[h].T.astype(BF16) for h in heads])
        qh = [q_ref[h, bi * tq:(bi + 1) * tq].astype(F32) * (ATTN_SCALE * LOG2E) for h in heads]
        exps, dens = [], []
        for first_map in (True, False):
            qm = [jnp.where((lane < DH_ATTN) == first_map, q, 0.0).astype(BF16) for q in qh]
            s = [jnp.concatenate([_bdot_nt(kg[h], qm[h]) for h in heads], axis=1) for kg in keys]
            m = functools.reduce(jnp.maximum, [jnp.max(si, axis=0, keepdims=True) for si in s])
            e = [jnp.exp2(si - m) for si in s]
            exps.append(e)
            dens.append(functools.reduce(jnp.add, [jnp.sum(ei, axis=0, keepdims=True) for ei in e]))
        ratio = lam * dens[0] / dens[1]
        inv = 1.0 / dens[0]
        probs = [(e1 - e2 * ratio).astype(BF16) for e1, e2 in zip(*exps)]
        for h in heads:
            cols = slice(h * tq, (h + 1) * tq)
            o_t = None
            for pj, vg in zip(probs, vals_t):
                part = jnp.dot(vg[h], pj[:, cols], preferred_element_type=F32)
                o_t = part if o_t is None else o_t + part
            o_t = o_t * inv[:, cols]
            y_t = o_t * lax.rsqrt(jnp.mean(o_t * o_t, axis=0, keepdims=True) + EPS)
            results.append((bi, h, (y_t.T * g_sub * (1.0 - lam_init)).astype(o_ref.dtype)))
    for bi, h, y in results:
        o_ref[bi * tq:(bi + 1) * tq, h * DV_ATTN:(h + 1) * DV_ATTN] = y


def _attn_four_kernel(*refs, n_attn, attn_kw, four_kw, once_per_seq):
    a_ref, f_ref = refs[-2:]
    _attn_kernel(*refs[:n_attn], a_ref, **attn_kw)
    four = functools.partial(_four_kernel, *refs[n_attn:n_attn + 4], f_ref, **four_kw)
    if once_per_seq:
        pl.when(pl.program_id(1) == 0)(four)
    else:
        four()


def _attn_four_call(q, k, v_t, cache, lam_p, g_sub, zf, outs_prev, cast_w, st, *, layer, lam_init, latent):
    n = st.n_all
    if latent:
        batch, seq, nb, tq = st.dec_batch, st.dec_seq, 1, st.dec_seq // 2
        row0 = st.n_ctx
    else:
        batch, seq, nb, tq = st.batch, st.seq, 4, st.seq
        row0 = 0
    nq = seq // tq
    if nq > 1 and nb > 1:
        raise ValueError("query blocks of several sequences are not contiguous rows")
    q0, s0 = row0 // (nb * tq), row0 // (nb * seq)
    in_specs = [pl.BlockSpec((N_ATTN, nb * tq, DV_ATTN), lambda b, i: (0, q0 + b * nq + i, 0)),
                pl.BlockSpec((N_ATTN, nb * seq, DV_ATTN), lambda b, i: (0, s0 + b, 0)),
                pl.BlockSpec((W_ATTN, nb * seq), lambda b, i: (0, s0 + b))]
    args = [q, k, v_t]
    if cache is not None:
        past = cache[0].shape[3]
        c_spec = pl.BlockSpec((None, None, N_ATTN, past, DV_ATTN), lambda b, i: (b, layer, 0, 0, 0))
        in_specs += [c_spec, c_spec]
        args += list(cache)
    in_specs += [pl.BlockSpec((None, 4, DH_ATTN), lambda b, i: (layer, 0, 0)),
                 pl.BlockSpec((None, 1, DV_ATTN), lambda b, i: (layer, 0, 0))]
    args += [lam_p, g_sub]
    n_attn = len(args)
    w_d, ct, s_t = _dft_tables(seq)
    seq_rows = pl.BlockSpec((nb * seq, W_FOUR), lambda b, i: (s0 + b, 0))
    in_specs += [seq_rows,
                 pl.BlockSpec((DG_FOUR, 2 * DG_FOUR), lambda b, i: (0, 0)),
                 pl.BlockSpec((seq, seq), lambda b, i: (0, 0)),
                 pl.BlockSpec((seq, seq), lambda b, i: (0, 0))]
    args += [zf, w_d, ct, s_t]
    aliases = {}
    if outs_prev is not None:
        aliases = {len(args): 0, len(args) + 1: 1}
        in_specs += [pl.BlockSpec(memory_space=pl.ANY)] * 2
        args += list(outs_prev)
    casts = _CastJobs(cast_w, (batch // nb) * nq, lambda b, i: b * nq + i)
    body = functools.partial(
        _attn_four_kernel, n_attn=n_attn, once_per_seq=nq > 1,
        attn_kw=dict(lam_init=lam_init, cached=cache is not None, nb=nb, tq=tq, seq=seq),
        four_kw=dict(nb=nb, seq=seq))
    return pl.pallas_call(
        casts.wrap(body, len(args), 2),
        grid=(batch // nb, nq),
        in_specs=in_specs + casts.in_specs(),
        out_specs=[pl.BlockSpec((nb * tq, W_ATTN), lambda b, i: (q0 + b * nq + i, 0)), seq_rows]
        + casts.out_specs(),
        out_shape=[jax.ShapeDtypeStruct((n, W_ATTN), BF16), jax.ShapeDtypeStruct((n, W_FOUR), BF16)]
        + casts.out_shapes(),
        input_output_aliases=aliases,
        compiler_params=_cparams(2),
        name="attn_fourier",
    )(*args, *casts.args())


def _dft_tables(seq):
    def cs(n):
        j = np.arange(n)
        ang = 2.0 * np.pi * ((j[:, None] * j[None, :]) % n) / n
        return np.cos(ang) / math.sqrt(n), np.sin(ang) / math.sqrt(n)

    cd, sd = cs(DG_FOUR)
    ct, st = cs(seq)
    w_d = jnp.asarray(np.concatenate([cd, sd], axis=1), F32)
    return w_d.astype(BF16), jnp.asarray(ct, F32).astype(BF16), jnp.asarray(-st, F32).astype(BF16)


def _four_kernel(*refs, nb, seq):
    z_ref, wd_ref, ct_ref, st_ref = refs[:4]
    o_ref = refs[-1]
    y_cos, y_sin = [], []
    for gidx in range(N_FOUR):
        y = jnp.dot(z_ref[:, gidx * DG_FOUR:(gidx + 1) * DG_FOUR], wd_ref[...], preferred_element_type=F32)
        y_cos.append(y[:, :DG_FOUR].astype(BF16))
        y_sin.append(y[:, DG_FOUR:].astype(BF16))
    y_cos = jnp.concatenate(y_cos, axis=1)
    y_sin = jnp.concatenate(y_sin, axis=1)
    for s in range(nb):
        rows = slice(s * seq, (s + 1) * seq)
        o_ref[rows, :] = (jnp.dot(ct_ref[...], y_cos[rows], preferred_element_type=F32)
                          + jnp.dot(st_ref[...], y_sin[rows], preferred_element_type=F32)).astype(o_ref.dtype)


def _mlstm_kernel(*refs, seq, nb, seeded, emit_state, n_alias):
    mq_ref, mk_ref, vt_ref, mo_ref, gt_ref, g_ref = refs[:6]
    pos = 6
    if seeded:
        c0_ref, n0_ref, m0_ref = refs[pos:pos + 3]
        pos += 3
    pos += n_alias
    o_ref = refs[pos]
    if emit_state:
        c1_ref, n1_ref, m1_ref = refs[pos + 1:pos + 4]

    L = min(MLSTM_CHUNK, seq)
    nc = seq // L
    s_idx = lax.broadcasted_iota(jnp.int32, (L, L), 0)
    t_idx = lax.broadcasted_iota(jnp.int32, (L, L), 1)
    before = (s_idx <= t_idx, s_idx >= t_idx)
    tri = (jnp.where(before[1], 1.0, 0.0).astype(BF16),
           jnp.where(before[0], 1.0, 0.0).astype(BF16))
    chains = [(bi, d, h) for bi in range(nb) for d in range(2) for h in range(N_MLSTM)]

    state = {}
    for (bi, d, h) in chains:
        if seeded:
            state[bi, d, h] = (c0_ref[bi, d, h], n0_ref[bi, d, h:h + 1, :], m0_ref[bi, d:d + 1, h:h + 1])
        else:
            state[bi, d, h] = (None, None, jnp.zeros((1, 1), F32))

    terms = {}
    for c in range(nc):
        for bi in range(nb):
            for d in range(2):
                r0 = bi * seq + (c if d == 0 else nc - 1 - c) * L
                g_t = gt_ref[:, r0:r0 + L]
                hi = g_t.astype(BF16)
                rem = g_t - hi.astype(F32)
                mid = rem.astype(BF16)
                lo = (rem - mid.astype(F32)).astype(BF16)
                pieces = _bdot_nt(jnp.concatenate([hi, mid, lo], axis=0), tri[d])
                cum_t = pieces[0:N_GATE] + pieces[N_GATE:2 * N_GATE] + pieces[2 * N_GATE:]
                c_t = (g_t - pltpu.roll(cum_t, N_GATE - N_MLSTM, 0)) * LOG2E
                col = jnp.concatenate([c_t, jnp.zeros((LANES - N_GATE, L), F32)], axis=0).T
                terms[c, bi, d] = (r0, col, g_t, cum_t)

    h_t = {}
    for c in range(nc):
        new_state = {}
        need_update = emit_state or c < nc - 1
        group = [(d, h) for d in range(2) for h in range(N_MLSTM)]
        for bi in range(nb):
            has_state = state[bi, 0, 0][0] is not None
            qs, ks, vts, cbs, s0s, i_rows, b_rows, m_prevs = [], [], [], [], [], [], [], []
            for d, h in group:
                r0, col, g_t, cum_t = terms[c, bi, d]
                ci = 2 * d * N_MLSTM + h
                cf = ci + N_MLSTM
                hs = slice(h * DH_MLSTM, (h + 1) * DH_MLSTM)
                qs.append(mq_ref[r0:r0 + L, hs])
                ks.append(mk_ref[r0:r0 + L, hs])
                vts.append(vt_ref[hs, r0:r0 + L])
                cbs.append(jnp.where(before[d], col[:, ci:ci + 1], -jnp.inf))
                s0s.append(_bdot_nt(ks[-1], qs[-1]))
                i_rows.append(g_t[ci:ci + 1, :])
                b_rows.append(cum_t[cf:cf + 1, :])
                m_prevs.append(jnp.broadcast_to(state[bi, d, h][2], (1, L)))
            cb = jnp.concatenate(cbs, axis=1)
            i_row, b_row = jnp.concatenate(i_rows, axis=1), jnp.concatenate(b_rows, axis=1)
            m_prev = jnp.concatenate(m_prevs, axis=1)
            m2_prev = m_prev * LOG2E
            m2_row = jnp.maximum(jnp.max(cb, axis=0, keepdims=True), m2_prev)
            s_t = jnp.concatenate(s0s, axis=1) * jnp.exp2(cb - m2_row)
            den = jnp.sum(s_t, axis=0, keepdims=True)
            s_bf = s_t.astype(BF16)
            m_t = b_row + m2_row * LN2
            floor = jnp.exp(-m_t)
            if has_state:
                sp = jnp.exp2(m2_prev - m2_row)
            if need_update:
                tot, new = [], []
                for j, (d, h) in enumerate(group):
                    last = j * L + (L - 1 if d == 0 else 0)
                    tot.append(jnp.broadcast_to(b_row[:, last:last + 1], (1, L)))
                    new.append(jnp.broadcast_to(m_t[:, last:last + 1], (1, L)))
                b_tot, m_new = jnp.concatenate(tot, axis=1), jnp.concatenate(new, axis=1)
                wl = jnp.exp(b_tot + (i_row - b_row) - m_new)
                if has_state:
                    decay = jnp.exp(b_tot + m_prev - m_new)
            for j, (d, h) in enumerate(group):
                cols = slice(j * L, (j + 1) * L)
                c_prev, n_prev, _ = state[bi, d, h]
                num_t = jnp.dot(vts[j].astype(BF16), s_bf[:, cols], preferred_element_type=F32)
                den_j = den[:, cols]
                if has_state:
                    cn = jnp.concatenate([c_prev, jnp.broadcast_to(n_prev, (SUBLANES, DH_MLSTM))], axis=0)
                    cq = _bdot_nt(cn, qs[j])
                    num_t = num_t + sp[:, cols] * cq[:DH_MLSTM]
                    den_j = den_j + sp[:, cols] * cq[DH_MLSTM:DH_MLSTM + 1]
                h_t[bi, d, h, c] = num_t / jnp.maximum(jnp.abs(den_j), floor[:, cols])
                if need_update:
                    wl_j = wl[:, cols]
                    vw = jnp.concatenate([vts[j] * wl_j, jnp.broadcast_to(wl_j, (SUBLANES, L))], axis=0)
                    upd = jnp.dot(vw.astype(BF16), ks[j], preferred_element_type=F32)
                    c_new, n_new = upd[:DH_MLSTM], upd[DH_MLSTM:DH_MLSTM + 1]
                    if has_state:
                        decay_j = decay[:, j * L:j * L + 1]
                        c_new = decay_j * c_prev + c_new
                        n_new = decay_j * n_prev + n_new
                    new_state[bi, d, h] = (c_new, n_new, m_new[:, j * L:j * L + 1])
        state = new_state

    g_m = g_ref[...]
    for bi in range(nb):
        for h in range(N_MLSTM):
            hs = slice(h * DH_MLSTM, (h + 1) * DH_MLSTM)
            fwd = [h_t[bi, 0, h, c] for c in range(nc)]
            bwd = [h_t[bi, 1, h, nc - 1 - c] for c in range(nc)]
            hsum = (fwd[0] if nc == 1 else jnp.concatenate(fwd, axis=1)) \
                + (bwd[0] if nc == 1 else jnp.concatenate(bwd, axis=1))
            y = hsum * lax.rsqrt(jnp.mean(hsum * hsum, axis=0, keepdims=True) + EPS)
            rows = slice(bi * seq, (bi + 1) * seq)
            o_ref[rows, hs] = (y.T * g_m * _sigmoid(mo_ref[rows, hs])).astype(o_ref.dtype)

    if emit_state:
        for (bi, d, h) in chains:
            c_fin, n_fin, m_fin = state[bi, d, h]
            c1_ref[bi, d, h] = c_fin
            n1_ref[bi, d, h:h + 1, :] = n_fin
            m1_ref[bi, d:d + 1, h:h + 1] = m_fin


def _mlstm_call(mq, mk, vt, mo, gates_t, g_m, state, alias_prev, cast_w, st, *, layer, latent):
    if latent:
        batch, seq, nb, row0 = st.dec_batch, st.dec_seq, 1, st.n_ctx
    else:
        batch, seq, nb, row0 = st.batch, st.seq, 4, 0
    emit_state = not latent
    rows = nb * seq
    b0 = row0 // rows
    tok_spec = pl.BlockSpec((rows, W_MLSTM), lambda b: (b0 + b, 0))
    in_specs = [tok_spec, tok_spec,
                pl.BlockSpec((W_MLSTM, rows), lambda b: (0, b0 + b)),
                tok_spec,
                pl.BlockSpec((N_GATE, rows), lambda b: (0, b0 + b)),
                pl.BlockSpec((None, 1, DH_MLSTM), lambda b: (layer, 0, 0))]
    args = [mq, mk, vt, mo, gates_t, g_m]
    state_specs = [
        pl.BlockSpec((nb, None, 2, N_MLSTM, DH_MLSTM, DH_MLSTM), lambda b: (b, layer, 0, 0, 0, 0)),
        pl.BlockSpec((nb, None, 2, N_MLSTM, DH_MLSTM), lambda b: (b, layer, 0, 0, 0)),
        pl.BlockSpec((nb, None, 2, N_MLSTM), lambda b: (b, layer, 0, 0)),
    ]
    if state is not None:
        in_specs += state_specs
        args += list(state)
    alias_in = list(alias_prev) if alias_prev is not None else []
    n_in = len(args)
    in_specs += [pl.BlockSpec(memory_space=pl.ANY)] * len(alias_in)
    args += alias_in
    out_specs = [tok_spec]
    out_shape = [jax.ShapeDtypeStruct((st.n_all, W_MLSTM), BF16)]
    if emit_state:
        out_specs += state_specs
        out_shape += [
            jax.ShapeDtypeStruct((batch, DEPTH, 2, N_MLSTM, DH_MLSTM, DH_MLSTM), F32),
            jax.ShapeDtypeStruct((batch, DEPTH, 2, N_MLSTM, DH_MLSTM), F32),
            jax.ShapeDtypeStruct((batch, DEPTH, 2, N_MLSTM), F32),
        ]
    first_out = 1 if emit_state else 0
    casts = _CastJobs(cast_w, batch // nb, lambda b: b)
    body = functools.partial(_mlstm_kernel, seq=seq, nb=nb, seeded=state is not None, emit_state=emit_state,
                             n_alias=len(alias_in))
    return pl.pallas_call(
        casts.wrap(body, len(args), len(out_specs)),
        grid=(batch // nb,),
        in_specs=in_specs + casts.in_specs(),
        out_specs=out_specs + casts.out_specs(),
        out_shape=out_shape + casts.out_shapes(),
        input_output_aliases={n_in + j: first_out + j for j in range(len(alias_in))},
        compiler_params=_cparams(1),
        name="mlstm",
    )(*args, *casts.args())


def _merge_kernel(x_ref, a_ref, f_ref, m_ref, mod_ref, g_ref, wg_ref, wa_ref, wf_ref, wm_ref, wo_ref, o_ref):
    x = x_ref[...]
    mod = mod_ref[...]
    u = _modulated_norm(x, g_ref[...], mod, 3).astype(BF16)
    merged = None
    for j, (br_ref, w_ref) in enumerate(((a_ref, wa_ref), (f_ref, wf_ref), (m_ref, wm_ref))):
        gate = _sigmoid(jnp.dot(u, wg_ref[:, j * D_MODEL:(j + 1) * D_MODEL], preferred_element_type=F32))
        term = gate * _bdot(br_ref[...], w_ref[...])
        merged = term if merged is None else merged + term
    out = _bdot(merged, wo_ref[...])
    o_ref[...] = x + mod[5:6] * out


def _merge_call(x, a, f, m, mods, g_norm, wg, wa, wf, wm, wo, st, *, layer):
    tm = TM_WIDE
    n = st.n_all
    br_spec = pl.BlockSpec((tm, W_ATTN), lambda i: (i, 0))
    return pl.pallas_call(
        _merge_kernel,
        grid=(n // tm,),
        in_specs=[
            pl.BlockSpec((tm, D_MODEL), lambda i: (i, 0)),
            br_spec, br_spec, br_spec,
            st.mod_spec(tm),
            pl.BlockSpec((None, None, 1, D_MODEL), lambda i: (layer, 1, 0, 0)),
            _layer_spec((D_MODEL, 3 * D_MODEL), None),
            _layer_spec((W_ATTN, D_MODEL), None), _layer_spec((W_FOUR, D_MODEL), None),
            _layer_spec((W_MLSTM, D_MODEL), None),
            _layer_spec((D_MODEL, D_MODEL), None),
        ],
        out_specs=pl.BlockSpec((tm, D_MODEL), lambda i: (i, 0)),
        out_shape=jax.ShapeDtypeStruct((n, D_MODEL), F32),
        compiler_params=_cparams(1),
        name="merge",
    )(x, a, f, m, mods, g_norm, wg, wa, wf, wm, wo)


def _rope_tables(n_tok):
    tok = np.arange(n_tok)
    inv = ROPE_BASE ** (-np.arange(ROPE_AXIS_PAIRS, dtype=np.float32) / ROPE_AXIS_PAIRS)
    ang = np.concatenate([(tok // GRID_W).astype(np.float32)[:, None] * inv,
                          (tok % GRID_W).astype(np.float32)[:, None] * inv], axis=-1).astype(np.float32)
    c, s = np.cos(ang), np.sin(ang)
    cos_t = np.concatenate([c, c, c, c], axis=-1)
    sin_t = np.concatenate([-s, s, -s, s], axis=-1)
    return jnp.asarray(cos_t, F32), jnp.asarray(sin_t, F32)


def kernel(x_prompt, x_sample, cache_k, cache_v, state_C, state_n, state_m, c, c_ctx, w_ada, b_ada, g_norm,
           w_ffn1_in, w_ffn1_out, w_ffn2_in, w_ffn2_out, w_in, b_mgate, attn_lambda, g_attn_sub, g_mlstm,
           w_branch_gate, w_br_attn, w_br_four, w_br_mlstm, w_out, g_final):
    batch, seq, _ = x_prompt.shape
    dec_batch, dec_seq, _ = x_sample.shape
    st = _Stream(batch, seq, dec_batch, dec_seq)
    cond = jnp.zeros((N_COND, D_MODEL), F32).at[0].set(c_ctx).at[1:1 + dec_batch].set(c)
    mods_all = _ada_call(cond, w_ada, b_ada).reshape(DEPTH, N_COND, N_MOD, D_MODEL)

    b_gate_t = b_mgate[:, :, None]
    g_norm4 = g_norm[:, :, None, :]
    g_fin = g_final[None, :]
    w_t = jnp.swapaxes(w_in, 1, 2).astype(BF16)
    g_sub3 = g_attn_sub[:, None, :]
    g_m3 = g_mlstm[:, None, :]
    cos_t, sin_t = _rope_tables(dec_seq)
    lat_state = (state_C, state_n, state_m)

    xs = (x_prompt.reshape(batch * seq, D_MODEL), x_sample.reshape(dec_batch * dec_seq, D_MODEL))
    kv, states = None, None
    ffn1_w = (w_ffn1_in[0].astype(BF16), w_ffn1_out[0].astype(BF16))
    for l in range(DEPTH):
        mods = mods_all[l]
        lam_init = 0.8 - 0.6 * math.exp(-0.3 * l)
        nxt = l + 1 < DEPTH
        (x,) = _ffn_call(xs, mods, g_norm4, g_fin, *ffn1_w, st, layer=l, sub=0, final=False)
        q, k, ck, cv, zf, mq, mk, mo, vat, vt, gates_t = _proj_call(
            x, mods, g_norm4, w_t, b_gate_t, cos_t, sin_t, kv, st, layer=l)
        kv = (ck, cv)
        a, f, ffn2_in = _attn_four_call(
            q, k, vat, None, attn_lambda, g_sub3, zf, None, [(w_ffn2_in, l)], st,
            layer=l, lam_init=lam_init, latent=False)
        a, f, ffn2_out, w_bg, *nxt_in = _attn_four_call(
            q, k, vat, (cache_k, cache_v), attn_lambda, g_sub3, zf, (a, f),
            [(w_ffn2_out, l), (w_branch_gate, l)] + ([(w_ffn1_in, l + 1)] if nxt else []), st,
            layer=l, lam_init=lam_init, latent=True)
        m, c_fin, n_fin, m_fin, *nxt_out = _mlstm_call(
            mq, mk, vt, mo, gates_t, g_m3, None, states, [(w_ffn1_out, l + 1)] if nxt else [], st,
            layer=l, latent=False)
        states = [c_fin, n_fin, m_fin]
        m, w_ba, w_bf, w_bm, w_o = _mlstm_call(
            mq, mk, vt, mo, gates_t, g_m3, lat_state, (m,),
            [(w_br_attn, l), (w_br_four, l), (w_br_mlstm, l), (w_out, l)], st, layer=l, latent=True)
        x = _merge_call(x, a, f, m, mods, g_norm4, w_bg, w_ba, w_bf, w_bm, w_o, st, layer=l)
        xs = tuple(_ffn_call((x,), mods, g_norm4, g_fin, ffn2_in, ffn2_out, st, layer=l, sub=2,
                             final=(l == DEPTH - 1)))
        if nxt:
            ffn1_w = (nxt_in[0], nxt_out[0])
    y_prompt, y_sample = xs
    return (y_prompt.reshape(batch, seq, D_MODEL), y_sample.reshape(dec_batch, dec_seq, D_MODEL),
            *kv, *states)
```

```python
import functools
import math

import numpy as np
import jax
import jax.numpy as jnp
from jax import lax
from jax.experimental import pallas as pl
from jax.experimental.pallas import tpu as pltpu

D_MODEL = 1024
DEPTH = 2
GRID_W = 64
N_ATTN = 4
DH_ATTN = 64
DV_ATTN = 2 * DH_ATTN
W_ATTN = N_ATTN * DV_ATTN
N_FOUR = 4
DG_FOUR = 128
W_FOUR = N_FOUR * DG_FOUR
N_MLSTM = 4
DH_MLSTM = 128
W_MLSTM = N_MLSTM * DH_MLSTM
N_GATE = 4 * N_MLSTM
LANES = 128
SUBLANES = 16
MV_LO = 3 * W_ATTN + W_FOUR + 2 * W_MLSTM
MV_HI = MV_LO + W_MLSTM
P_MAIN = MV_HI + W_MLSTM
D_FF = 2816
N_MOD = 9
N_COND = 8
ROPE_BASE = 10000.0
ROPE_AXIS_PAIRS = DH_ATTN // 4
ATTN_SCALE = DH_ATTN ** -0.5
LOG2E = 1.4426950408889634
LN2 = 0.6931471805599453
MLSTM_K_SCALE = DH_MLSTM ** -0.5
EPS = 1e-6
MLSTM_CHUNK = 256
VMEM_LIMIT = 56 * 1024 * 1024
TM_WIDE = 1024
TM_PROJ = 1024

F32 = jnp.float32
BF16 = jnp.bfloat16


def _cparams(n_grid):
    return pltpu.CompilerParams(dimension_semantics=("arbitrary",) * n_grid,
                                vmem_limit_bytes=VMEM_LIMIT)


def _bdot(a, b):
    return jnp.dot(a.astype(BF16), b.astype(BF16), preferred_element_type=F32)


def _bdot_nt(a, b):
    return lax.dot_general(a.astype(BF16), b.astype(BF16), (((1,), (1,)), ((), ())),
                           preferred_element_type=F32)


def _sigmoid(x):
    return 1.0 / (1.0 + jnp.exp(-x))


def _log_sigmoid(x):
    return jnp.minimum(x, 0.0) - jnp.log1p(jnp.exp(-jnp.abs(x)))


def _rms(x, g):
    return x * lax.rsqrt(jnp.mean(x * x, axis=-1, keepdims=True) + EPS) * g


def _modulated_norm(x, g, mod, base):
    gain = g * (1.0 + mod[base + 1:base + 2])
    return x * lax.rsqrt(jnp.mean(x * x, axis=-1, keepdims=True) + EPS) * gain + mod[base:base + 1]


def _layer_spec(shape, layer):
    if layer is None:
        return pl.BlockSpec(shape, lambda *_: (0,) * len(shape), pipeline_mode=pl.Buffered(1))
    return pl.BlockSpec((None,) + shape, lambda *_: (layer,) + (0,) * len(shape),
                        pipeline_mode=pl.Buffered(1))


class _CastJobs:
    def __init__(self, jobs, steps, step_index):
        self.jobs, self.steps, self.step_index = list(jobs), steps, step_index

    def __len__(self):
        return len(self.jobs)

    def _rows(self, w):
        rows = w.shape[1] // self.steps
        if rows * self.steps != w.shape[1] or rows % SUBLANES:
            raise ValueError("weight rows do not split into bf16 row tiles over the grid")
        return rows

    def in_specs(self):
        return [pl.BlockSpec((None, self._rows(w), w.shape[2]),
                             lambda *g, layer=layer: (layer, self.step_index(*g), 0))
                for w, layer in self.jobs]

    def out_specs(self):
        return [pl.BlockSpec((self._rows(w), w.shape[2]), lambda *g: (self.step_index(*g), 0))
                for w, _ in self.jobs]

    def out_shapes(self):
        return [jax.ShapeDtypeStruct(w.shape[1:], BF16) for w, _ in self.jobs]

    def args(self):
        return [w for w, _ in self.jobs]

    def wrap(self, body, n_in, n_out):
        n = len(self.jobs)
        if n == 0:
            return body

        def kernel(*refs):
            ins, cast_in = refs[:n_in], refs[n_in:n_in + n]
            outs = refs[n_in + n:n_in + n + n_out]
            cast_out = refs[n_in + n + n_out:n_in + 2 * n + n_out]
            for src, dst in zip(cast_in, cast_out):
                dst[...] = src[...].astype(dst.dtype)
            body(*ins, *outs, *refs[n_in + 2 * n + n_out:])

        return kernel


class _Stream:
    def __init__(self, batch, seq, dec_batch, dec_seq):
        self.batch, self.seq, self.dec_batch, self.dec_seq = batch, seq, dec_batch, dec_seq
        self.n_ctx = batch * seq
        self.n_all = self.n_ctx + dec_batch * dec_seq

    def ctx_tiles(self, tm):
        if self.n_ctx % tm or self.dec_seq % tm:
            raise ValueError("row tile must divide the context rows and one latent sequence")
        return self.n_ctx // tm

    def mod_spec(self, tm):
        n_pt, tps = self.ctx_tiles(tm), self.dec_seq // tm
        return pl.BlockSpec((None, N_MOD, D_MODEL),
                            lambda i: (jnp.where(i < n_pt, 0, 1 + (i - n_pt) // tps), 0, 0))


def _ada_kernel(c_ref, w_ref, b_ref, o_ref):
    c = c_ref[...]
    s = c * _sigmoid(c)
    o_ref[...] = _bdot(s, w_ref[...]) + b_ref[...]


def _ada_call(cond, w_ada, b_ada):
    tn = 2304
    n_out = N_MOD * D_MODEL
    return pl.pallas_call(
        _ada_kernel,
        grid=(DEPTH, n_out // tn),
        in_specs=[
            pl.BlockSpec((N_COND, D_MODEL), lambda l, j: (0, 0)),
            pl.BlockSpec((None, D_MODEL, tn), lambda l, j: (l, 0, j)),
            pl.BlockSpec((None, 1, tn), lambda l, j: (l, 0, j)),
        ],
        out_specs=pl.BlockSpec((None, N_COND, tn), lambda l, j: (l, 0, j)),
        out_shape=jax.ShapeDtypeStruct((DEPTH, N_COND, n_out), F32),
        compiler_params=_cparams(2),
        name="adaln",
    )(cond, w_ada, b_ada.reshape(DEPTH, 1, n_out))


FF_CHUNKS = (768, 768, 768, 512)


def _ffn_kernel(*refs, base, n_x, final, n_pt):
    x_refs = refs[:n_x]
    mod_ref, g_ref, gf_ref, win_ref, wout_ref = refs[n_x:n_x + 5]
    o_refs = refs[n_x + 5:]
    i = pl.program_id(0)
    if n_x == 2:
        x = jnp.where(i < n_pt, x_refs[0][...], x_refs[1][...])
    else:
        x = x_refs[0][...]
    mod = mod_ref[...]
    u = _modulated_norm(x, g_ref[...], mod, base).astype(BF16)
    y = None
    lo = 0
    for width in FF_CHUNKS:
        a = jnp.dot(u, win_ref[:, lo:lo + width], preferred_element_type=F32)
        g = jnp.dot(u, win_ref[:, D_FF + lo:D_FF + lo + width], preferred_element_type=F32)
        hh = (a * _sigmoid(a) * g).astype(BF16)
        part = jnp.dot(hh, wout_ref[lo:lo + width, :], preferred_element_type=F32)
        y = part if y is None else y + part
        lo += width
    xn = x + 0.5 * mod[base + 2:base + 3] * y
    if not final:
        o_refs[0][...] = xn
    else:
        xn = _rms(xn, gf_ref[...])

        @pl.when(i < n_pt)
        def _():
            o_refs[0][...] = xn

        @pl.when(i >= n_pt)
        def _():
            o_refs[1][...] = xn


def _ffn_call(xs, mods, g_norm, g_final, w_in, w_out, st, *, layer, sub, final):
    tm = TM_WIDE
    n_pt = st.ctx_tiles(tm)
    n_tiles = st.n_all // tm
    ctx_spec = pl.BlockSpec((tm, D_MODEL), lambda i: (jnp.minimum(i, n_pt - 1), 0))
    lat_spec = pl.BlockSpec((tm, D_MODEL), lambda i: (jnp.maximum(i - n_pt, 0), 0))
    all_spec = pl.BlockSpec((tm, D_MODEL), lambda i: (i, 0))
    x_specs = [ctx_spec, lat_spec] if len(xs) == 2 else [all_spec]
    if final:
        out_specs = [ctx_spec, lat_spec]
        out_shape = [jax.ShapeDtypeStruct((st.n_ctx, D_MODEL), F32),
                     jax.ShapeDtypeStruct((st.n_all - st.n_ctx, D_MODEL), F32)]
    else:
        out_specs = [all_spec]
        out_shape = [jax.ShapeDtypeStruct((st.n_all, D_MODEL), F32)]
    return pl.pallas_call(
        functools.partial(_ffn_kernel, base=3 * sub, n_x=len(xs), final=final, n_pt=n_pt),
        grid=(n_tiles,),
        in_specs=x_specs + [
            st.mod_spec(tm),
            pl.BlockSpec((None, None, 1, D_MODEL), lambda i: (layer, sub, 0, 0)),
            pl.BlockSpec((1, D_MODEL), lambda i: (0, 0)),
            _layer_spec((D_MODEL, 2 * D_FF), None),
            _layer_spec((D_FF, D_MODEL), None),
        ],
        out_specs=out_specs,
        out_shape=out_shape,
        compiler_params=_cparams(1),
        name="ffn",
    )(*xs, mods, g_norm, g_final, w_in, w_out)


def _swap32(x):
    lane = lax.broadcasted_iota(jnp.int32, x.shape, 1)
    return jnp.where((lane & (DH_ATTN // 2)) == 0,
                     pltpu.roll(x, LANES - DH_ATTN // 2, 1), pltpu.roll(x, DH_ATTN // 2, 1))


def _proj_kernel(*refs, n_alias, n_pt, seqs_per_tile, seq):
    (x_ref, mod_ref, g_ref, wt_ref, bgt_ref, cos_ref, sin_ref) = refs[:7]
    (q_ref, k_ref, ck_ref, cv_ref, zf_ref, mq_ref, mk_ref, mo_ref, vat_ref, vt_ref, gt_ref) = refs[7 + n_alias:]
    i = pl.program_id(0)
    is_latent = i >= n_pt
    x = x_ref[...]
    u = _modulated_norm(x, g_ref[...], mod_ref[...], 3).astype(BF16)
    za = _bdot_nt(u, wt_ref[:3 * W_ATTN, :])
    cos_t = cos_ref[...]
    sin_t = sin_ref[...]
    plain_k = []
    for h in range(N_ATTN):
        zq = za[:, h * DV_ATTN:(h + 1) * DV_ATTN]
        zk = za[:, W_ATTN + h * DV_ATTN:W_ATTN + (h + 1) * DV_ATTN]
        plain_k.append(zk)
        q_ref[h] = jnp.where(is_latent, zq * cos_t + _swap32(zq) * sin_t, zq).astype(q_ref.dtype)
        k_ref[h] = jnp.where(is_latent, zk * cos_t + _swap32(zk) * sin_t, zk).astype(k_ref.dtype)

    @pl.when(i < n_pt)
    def _():
        for h in range(N_ATTN):
            zv = za[:, 2 * W_ATTN + h * DV_ATTN:2 * W_ATTN + (h + 1) * DV_ATTN]
            for s in range(seqs_per_tile):
                rs = slice(s * seq, (s + 1) * seq)
                ck_ref[s, h] = plain_k[h][rs]
                cv_ref[s, h] = zv[rs]

    zb = _bdot_nt(u, wt_ref[3 * W_ATTN:MV_LO, :])
    zf_ref[...] = zb[:, :W_FOUR].astype(zf_ref.dtype)
    off = W_FOUR
    mq_ref[...] = zb[:, off:off + W_MLSTM].astype(mq_ref.dtype)
    mk_ref[...] = (zb[:, off + W_MLSTM:off + 2 * W_MLSTM] * MLSTM_K_SCALE).astype(mk_ref.dtype)
    mo_ref[...] = _bdot_nt(u, wt_ref[MV_HI:P_MAIN, :])
    vat_ref[...] = _bdot_nt(wt_ref[2 * W_ATTN:3 * W_ATTN, :], u).astype(vat_ref.dtype)
    vt_ref[...] = _bdot_nt(wt_ref[MV_LO:MV_HI, :], u)
    gpt = _bdot_nt(wt_ref[P_MAIN:, :], u) + bgt_ref[...]
    sub = lax.broadcasted_iota(jnp.int32, gpt.shape, 0)
    gt_ref[...] = jnp.where((sub & N_MLSTM) != 0, _log_sigmoid(gpt), gpt)


def _proj_call(x, mods, g_norm, w_t, bgt, cos_t, sin_t, kv_prev, st, *, layer):
    tm = TM_PROJ
    n_pt = st.ctx_tiles(tm)
    if tm % st.seq:
        raise ValueError("a context row tile must hold whole sequences")
    seqs_per_tile, tps = tm // st.seq, st.dec_seq // tm
    n = st.n_all
    head_shape = jax.ShapeDtypeStruct((N_ATTN, n, DV_ATTN), BF16)
    head_spec = pl.BlockSpec((N_ATTN, tm, DV_ATTN), lambda i: (0, i, 0))
    kv_shape = jax.ShapeDtypeStruct((st.batch, DEPTH, N_ATTN, st.seq, DV_ATTN), F32)
    kv_spec = pl.BlockSpec((seqs_per_tile, None, N_ATTN, st.seq, DV_ATTN),
                           lambda i: (jnp.minimum(i, n_pt - 1), layer, 0, 0, 0))
    tok_shape = jax.ShapeDtypeStruct((n, W_ATTN), F32)
    tok_bf16 = jax.ShapeDtypeStruct((n, W_ATTN), BF16)
    tok_spec = pl.BlockSpec((tm, W_ATTN), lambda i: (i, 0))
    rope_spec = pl.BlockSpec((tm, LANES), lambda i: (jnp.maximum(i - n_pt, 0) % tps, 0))
    alias_in = list(kv_prev) if kv_prev is not None else []
    n_in = 7
    return pl.pallas_call(
        functools.partial(_proj_kernel, n_alias=len(alias_in), n_pt=n_pt, seqs_per_tile=seqs_per_tile,
                          seq=st.seq),
        grid=(n // tm,),
        in_specs=[
            pl.BlockSpec((tm, D_MODEL), lambda i: (i, 0)),
            st.mod_spec(tm),
            pl.BlockSpec((None, None, 1, D_MODEL), lambda i: (layer, 1, 0, 0)),
            _layer_spec((w_t.shape[1], D_MODEL), layer),
            _layer_spec((N_GATE, 1), layer),
            rope_spec, rope_spec,
        ] + [pl.BlockSpec(memory_space=pl.ANY)] * len(alias_in),
        out_specs=[head_spec, head_spec, kv_spec, kv_spec, tok_spec, tok_spec, tok_spec, tok_spec,
                   pl.BlockSpec((W_ATTN, tm), lambda i: (0, i)),
                   pl.BlockSpec((W_MLSTM, tm), lambda i: (0, i)),
                   pl.BlockSpec((N_GATE, tm), lambda i: (0, i))],
        out_shape=[head_shape, head_shape, kv_shape, kv_shape, tok_bf16, tok_bf16, tok_bf16, tok_shape,
                   jax.ShapeDtypeStruct((W_ATTN, n), BF16),
                   jax.ShapeDtypeStruct((W_MLSTM, n), F32),
                   jax.ShapeDtypeStruct((N_GATE, n), F32)],
        input_output_aliases={n_in + j: 2 + j for j in range(len(alias_in))},
        compiler_params=_cparams(1),
        name="mixer_proj",
    )(x, mods, g_norm, w_t, bgt, cos_t, sin_t, *alias_in)


def _attn_kernel(*refs, lam_init, cached, nb, tq, seq):
    q_ref, k_ref, vt_ref = refs[:3]
    pos = 3
    if cached:
        ck_ref, cv_ref = refs[pos:pos + 2]
        pos += 2
    lam_ref, g_ref = refs[pos:pos + 2]
    o_ref = refs[-1]
    lp = lam_ref[...]
    lam = (jnp.exp(jnp.sum(lp[0:1] * lp[1:2], axis=-1, keepdims=True))
           - jnp.exp(jnp.sum(lp[2:3] * lp[3:4], axis=-1, keepdims=True)) + lam_init)
    g_sub = g_ref[...]
    lane = lax.broadcasted_iota(jnp.int32, (tq, DV_ATTN), 1)
    heads = range(N_ATTN)
    results = []
    for bi in range(nb):
        ks = slice(bi * seq, (bi + 1) * seq)
        keys = [[k_ref[h, ks] for h in heads]]
        vals_t = [[vt_ref[h * DV_ATTN:(h + 1) * DV_ATTN, ks] for h in heads]]
        if cached:
            keys.insert(0, [ck_ref[h].astype(BF16) for h in heads])
            vals_t.insert(0, [cv_ref[h].T.astype(BF16) for h in heads])
        qh = [q_ref[h, bi * tq:(bi + 1) * tq].astype(F32) * (ATTN_SCALE * LOG2E) for h in heads]
        exps, dens = [], []
        for first_map in (True, False):
            qm = [jnp.where((lane < DH_ATTN) == first_map, q, 0.0).astype(BF16) for q in qh]
            s = [jnp.concatenate([_bdot_nt(kg[h], qm[h]) for h in heads], axis=1) for kg in keys]
            m = functools.reduce(jnp.maximum, [jnp.max(si, axis=0, keepdims=True) for si in s])
            e = [jnp.exp2(si - m) for si in s]
            exps.append(e)
            dens.append(functools.reduce(jnp.add, [jnp.sum(ei, axis=0, keepdims=True) for ei in e]))
        ratio = lam * dens[0] / dens[1]
        inv = 1.0 / dens[0]
        probs = [(e1 - e2 * ratio).astype(BF16) for e1, e2 in zip(*exps)]
        for h in heads:
            cols = slice(h * tq, (h + 1) * tq)
            o_t = None
            for pj, vg in zip(probs, vals_t):
                part = jnp.dot(vg[h], pj[:, cols], preferred_element_type=F32)
                o_t = part if o_t is None else o_t + part
            o_t = o_t * inv[:, cols]
            y_t = o_t * lax.rsqrt(jnp.mean(o_t * o_t, axis=0, keepdims=True) + EPS)
            results.append((bi, h, (y_t.T * g_sub * (1.0 - lam_init)).astype(o_ref.dtype)))
    for bi, h, y in results:
        o_ref[bi * tq:(bi + 1) * tq, h * DV_ATTN:(h + 1) * DV_ATTN] = y


def _attn_four_kernel(*refs, n_attn, attn_kw, four_kw, once_per_seq):
    a_ref, f_ref = refs[-2:]
    _attn_kernel(*refs[:n_attn], a_ref, **attn_kw)
    four = functools.partial(_four_kernel, *refs[n_attn:n_attn + 4], f_ref, **four_kw)
    if once_per_seq:
        pl.when(pl.program_id(1) == 0)(four)
    else:
        four()


def _attn_four_call(q, k, v_t, cache, lam_p, g_sub, zf, outs_prev, cast_w, st, *, layer, lam_init, latent):
    n = st.n_all
    if latent:
        batch, seq, nb, tq = st.dec_batch, st.dec_seq, 1, st.dec_seq // 2
        row0 = st.n_ctx
    else:
        batch, seq, nb, tq = st.batch, st.seq, 4, st.seq
        row0 = 0
    nq = seq // tq
    if nq > 1 and nb > 1:
        raise ValueError("query blocks of several sequences are not contiguous rows")
    q0, s0 = row0 // (nb * tq), row0 // (nb * seq)
    in_specs = [pl.BlockSpec((N_ATTN, nb * tq, DV_ATTN), lambda b, i: (0, q0 + b * nq + i, 0)),
                pl.BlockSpec((N_ATTN, nb * seq, DV_ATTN), lambda b, i: (0, s0 + b, 0)),
                pl.BlockSpec((W_ATTN, nb * seq), lambda b, i: (0, s0 + b))]
    args = [q, k, v_t]
    if cache is not None:
        past = cache[0].shape[3]
        c_spec = pl.BlockSpec((None, None, N_ATTN, past, DV_ATTN), lambda b, i: (b, layer, 0, 0, 0))
        in_specs += [c_spec, c_spec]
        args += list(cache)
    in_specs += [pl.BlockSpec((None, 4, DH_ATTN), lambda b, i: (layer, 0, 0)),
                 pl.BlockSpec((None, 1, DV_ATTN), lambda b, i: (layer, 0, 0))]
    args += [lam_p, g_sub]
    n_attn = len(args)
    w_d, ct, s_t = _dft_tables(seq)
    seq_rows = pl.BlockSpec((nb * seq, W_FOUR), lambda b, i: (s0 + b, 0))
    in_specs += [seq_rows,
                 pl.BlockSpec((DG_FOUR, 2 * DG_FOUR), lambda b, i: (0, 0)),
                 pl.BlockSpec((seq, seq), lambda b, i: (0, 0)),
                 pl.BlockSpec((seq, seq), lambda b, i: (0, 0))]
    args += [zf, w_d, ct, s_t]
    aliases = {}
    if outs_prev is not None:
        aliases = {len(args): 0, len(args) + 1: 1}
        in_specs += [pl.BlockSpec(memory_space=pl.ANY)] * 2
        args += list(outs_prev)
    casts = _CastJobs(cast_w, (batch // nb) * nq, lambda b, i: b * nq + i)
    body = functools.partial(
        _attn_four_kernel, n_attn=n_attn, once_per_seq=nq > 1,
        attn_kw=dict(lam_init=lam_init, cached=cache is not None, nb=nb, tq=tq, seq=seq),
        four_kw=dict(nb=nb, seq=seq))
    return pl.pallas_call(
        casts.wrap(body, len(args), 2),
        grid=(batch // nb, nq),
        in_specs=in_specs + casts.in_specs(),
        out_specs=[pl.BlockSpec((nb * tq, W_ATTN), lambda b, i: (q0 + b * nq + i, 0)), seq_rows]
        + casts.out_specs(),
        out_shape=[jax.ShapeDtypeStruct((n, W_ATTN), BF16), jax.ShapeDtypeStruct((n, W_FOUR), BF16)]
        + casts.out_shapes(),
        input_output_aliases=aliases,
        compiler_params=_cparams(2),
        name="attn_fourier",
    )(*args, *casts.args())


def _dft_tables(seq):
    def cs(n):
        j = np.arange(n)
        ang = 2.0 * np.pi * ((j[:, None] * j[None, :]) % n) / n
        return np.cos(ang) / math.sqrt(n), np.sin(ang) / math.sqrt(n)

    cd, sd = cs(DG_FOUR)
    ct, st = cs(seq)
    w_d = jnp.asarray(np.concatenate([cd, sd], axis=1), F32)
    return w_d.astype(BF16), jnp.asarray(ct, F32).astype(BF16), jnp.asarray(-st, F32).astype(BF16)


def _four_kernel(*refs, nb, seq):
    z_ref, wd_ref, ct_ref, st_ref = refs[:4]
    o_ref = refs[-1]
    y_cos, y_sin = [], []
    for gidx in range(N_FOUR):
        y = jnp.dot(z_ref[:, gidx * DG_FOUR:(gidx + 1) * DG_FOUR], wd_ref[...], preferred_element_type=F32)
        y_cos.append(y[:, :DG_FOUR].astype(BF16))
        y_sin.append(y[:, DG_FOUR:].astype(BF16))
    y_cos = jnp.concatenate(y_cos, axis=1)
    y_sin = jnp.concatenate(y_sin, axis=1)
    for s in range(nb):
        rows = slice(s * seq, (s + 1) * seq)
        o_ref[rows, :] = (jnp.dot(ct_ref[...], y_cos[rows], preferred_element_type=F32)
                          + jnp.dot(st_ref[...], y_sin[rows], preferred_element_type=F32)).astype(o_ref.dtype)


def _mlstm_kernel(*refs, seq, nb, seeded, emit_state, n_alias):
    mq_ref, mk_ref, vt_ref, mo_ref, gt_ref, g_ref = refs[:6]
    pos = 6
    if seeded:
        c0_ref, n0_ref, m0_ref = refs[pos:pos + 3]
        pos += 3
    pos += n_alias
    o_ref = refs[pos]
    if emit_state:
        c1_ref, n1_ref, m1_ref = refs[pos + 1:pos + 4]

    L = min(MLSTM_CHUNK, seq)
    nc = seq // L
    s_idx = lax.broadcasted_iota(jnp.int32, (L, L), 0)
    t_idx = lax.broadcasted_iota(jnp.int32, (L, L), 1)
    before = (s_idx <= t_idx, s_idx >= t_idx)
    tri = (jnp.where(before[1], 1.0, 0.0).astype(BF16),
           jnp.where(before[0], 1.0, 0.0).astype(BF16))
    chains = [(bi, d, h) for bi in range(nb) for d in range(2) for h in range(N_MLSTM)]

    state = {}
    for (bi, d, h) in chains:
        if seeded:
            state[bi, d, h] = (c0_ref[bi, d, h], n0_ref[bi, d, h:h + 1, :], m0_ref[bi, d:d + 1, h:h + 1])
        else:
            state[bi, d, h] = (None, None, jnp.zeros((1, 1), F32))

    terms = {}
    for c in range(nc):
        for bi in range(nb):
            for d in range(2):
                r0 = bi * seq + (c if d == 0 else nc - 1 - c) * L
                g_t = gt_ref[:, r0:r0 + L]
                hi = g_t.astype(BF16)
                rem = g_t - hi.astype(F32)
                mid = rem.astype(BF16)
                lo = (rem - mid.astype(F32)).astype(BF16)
                pieces = _bdot_nt(jnp.concatenate([hi, mid, lo], axis=0), tri[d])
                cum_t = pieces[0:N_GATE] + pieces[N_GATE:2 * N_GATE] + pieces[2 * N_GATE:]
                c_t = (g_t - pltpu.roll(cum_t, N_GATE - N_MLSTM, 0)) * LOG2E
                col = jnp.concatenate([c_t, jnp.zeros((LANES - N_GATE, L), F32)], axis=0).T
                terms[c, bi, d] = (r0, col, g_t, cum_t)

    h_t = {}
    for c in range(nc):
        new_state = {}
        need_update = emit_state or c < nc - 1
        group = [(d, h) for d in range(2) for h in range(N_MLSTM)]
        for bi in range(nb):
            has_state = state[bi, 0, 0][0] is not None
            qs, ks, vts, cbs, s0s, i_rows, b_rows, m_prevs = [], [], [], [], [], [], [], []
            for d, h in group:
                r0, col, g_t, cum_t = terms[c, bi, d]
                ci = 2 * d * N_MLSTM + h
                cf = ci + N_MLSTM
                hs = slice(h * DH_MLSTM, (h + 1) * DH_MLSTM)
                qs.append(mq_ref[r0:r0 + L, hs])
                ks.append(mk_ref[r0:r0 + L, hs])
                vts.append(vt_ref[hs, r0:r0 + L])
                cbs.append(jnp.where(before[d], col[:, ci:ci + 1], -jnp.inf))
                s0s.append(_bdot_nt(ks[-1], qs[-1]))
                i_rows.append(g_t[ci:ci + 1, :])
                b_rows.append(cum_t[cf:cf + 1, :])
                m_prevs.append(jnp.broadcast_to(state[bi, d, h][2], (1, L)))
            cb = jnp.concatenate(cbs, axis=1)
            i_row, b_row = jnp.concatenate(i_rows, axis=1), jnp.concatenate(b_rows, axis=1)
            m_prev = jnp.concatenate(m_prevs, axis=1)
            m2_prev = m_prev * LOG2E
            m2_row = jnp.maximum(jnp.max(cb, axis=0, keepdims=True), m2_prev)
            s_t = jnp.concatenate(s0s, axis=1) * jnp.exp2(cb - m2_row)
            den = jnp.sum(s_t, axis=0, keepdims=True)
            s_bf = s_t.astype(BF16)
            m_t = b_row + m2_row * LN2
            floor = jnp.exp(-m_t)
            if has_state:
                sp = jnp.exp2(m2_prev - m2_row)
            if need_update:
                tot, new = [], []
                for j, (d, h) in enumerate(group):
                    last = j * L + (L - 1 if d == 0 else 0)
                    tot.append(jnp.broadcast_to(b_row[:, last:last + 1], (1, L)))
                    new.append(jnp.broadcast_to(m_t[:, last:last + 1], (1, L)))
                b_tot, m_new = jnp.concatenate(tot, axis=1), jnp.concatenate(new, axis=1)
                wl = jnp.exp(b_tot + (i_row - b_row) - m_new)
                if has_state:
                    decay = jnp.exp(b_tot + m_prev - m_new)
            for j, (d, h) in enumerate(group):
                cols = slice(j * L, (j + 1) * L)
                c_prev, n_prev, _ = state[bi, d, h]
                num_t = jnp.dot(vts[j].astype(BF16), s_bf[:, cols], preferred_element_type=F32)
                den_j = den[:, cols]
                if has_state:
                    cn = jnp.concatenate([c_prev, jnp.broadcast_to(n_prev, (SUBLANES, DH_MLSTM))], axis=0)
                    cq = _bdot_nt(cn, qs[j])
                    num_t = num_t + sp[:, cols] * cq[:DH_MLSTM]
                    den_j = den_j + sp[:, cols] * cq[DH_MLSTM:DH_MLSTM + 1]
                h_t[bi, d, h, c] = num_t / jnp.maximum(jnp.abs(den_j), floor[:, cols])
                if need_update:
                    wl_j = wl[:, cols]
                    vw = jnp.concatenate([vts[j] * wl_j, jnp.broadcast_to(wl_j, (SUBLANES, L))], axis=0)
                    upd = jnp.dot(vw.astype(BF16), ks[j], preferred_element_type=F32)
                    c_new, n_new = upd[:DH_MLSTM], upd[DH_MLSTM:DH_MLSTM + 1]
                    if has_state:
                        decay_j = decay[:, j * L:j * L + 1]
                        c_new = decay_j * c_prev + c_new
                        n_new = decay_j * n_prev + n_new
                    new_state[bi, d, h] = (c_new, n_new, m_new[:, j * L:j * L + 1])
        state = new_state

    g_m = g_ref[...]
    for bi in range(nb):
        for h in range(N_MLSTM):
            hs = slice(h * DH_MLSTM, (h + 1) * DH_MLSTM)
            fwd = [h_t[bi, 0, h, c] for c in range(nc)]
            bwd = [h_t[bi, 1, h, nc - 1 - c] for c in range(nc)]
            hsum = (fwd[0] if nc == 1 else jnp.concatenate(fwd, axis=1)) \
                + (bwd[0] if nc == 1 else jnp.concatenate(bwd, axis=1))
            y = hsum * lax.rsqrt(jnp.mean(hsum * hsum, axis=0, keepdims=True) + EPS)
            rows = slice(bi * seq, (bi + 1) * seq)
            o_ref[rows, hs] = (y.T * g_m * _sigmoid(mo_ref[rows, hs])).astype(o_ref.dtype)

    if emit_state:
        for (bi, d, h) in chains:
            c_fin, n_fin, m_fin = state[bi, d, h]
            c1_ref[bi, d, h] = c_fin
            n1_ref[bi, d, h:h + 1, :] = n_fin
            m1_ref[bi, d:d + 1, h:h + 1] = m_fin


def _mlstm_call(mq, mk, vt, mo, gates_t, g_m, state, alias_prev, cast_w, st, *, layer, latent):
    if latent:
        batch, seq, nb, row0 = st.dec_batch, st.dec_seq, 1, st.n_ctx
    else:
        batch, seq, nb, row0 = st.batch, st.seq, 8, 0
    emit_state = not latent
    rows = nb * seq
    b0 = row0 // rows
    tok_spec = pl.BlockSpec((rows, W_MLSTM), lambda b: (b0 + b, 0))
    in_specs = [tok_spec, tok_spec,
                pl.BlockSpec((W_MLSTM, rows), lambda b: (0, b0 + b)),
                tok_spec,
                pl.BlockSpec((N_GATE, rows), lambda b: (0, b0 + b)),
                pl.BlockSpec((None, 1, DH_MLSTM), lambda b: (layer, 0, 0))]
    args = [mq, mk, vt, mo, gates_t, g_m]
    state_specs = [
        pl.BlockSpec((nb, None, 2, N_MLSTM, DH_MLSTM, DH_MLSTM), lambda b: (b, layer, 0, 0, 0, 0)),
        pl.BlockSpec((nb, None, 2, N_MLSTM, DH_MLSTM), lambda b: (b, layer, 0, 0, 0)),
        pl.BlockSpec((nb, None, 2, N_MLSTM), lambda b: (b, layer, 0, 0)),
    ]
    if state is not None:
        in_specs += state_specs
        args += list(state)
    alias_in = list(alias_prev) if alias_prev is not None else []
    n_in = len(args)
    in_specs += [pl.BlockSpec(memory_space=pl.ANY)] * len(alias_in)
    args += alias_in
    out_specs = [tok_spec]
    out_shape = [jax.ShapeDtypeStruct((st.n_all, W_MLSTM), BF16)]
    if emit_state:
        out_specs += state_specs
        out_shape += [
            jax.ShapeDtypeStruct((batch, DEPTH, 2, N_MLSTM, DH_MLSTM, DH_MLSTM), F32),
            jax.ShapeDtypeStruct((batch, DEPTH, 2, N_MLSTM, DH_MLSTM), F32),
            jax.ShapeDtypeStruct((batch, DEPTH, 2, N_MLSTM), F32),
        ]
    first_out = 1 if emit_state else 0
    casts = _CastJobs(cast_w, batch // nb, lambda b: b)
    body = functools.partial(_mlstm_kernel, seq=seq, nb=nb, seeded=state is not None, emit_state=emit_state,
                             n_alias=len(alias_in))
    return pl.pallas_call(
        casts.wrap(body, len(args), len(out_specs)),
        grid=(batch // nb,),
        in_specs=in_specs + casts.in_specs(),
        out_specs=out_specs + casts.out_specs(),
        out_shape=out_shape + casts.out_shapes(),
        input_output_aliases={n_in + j: first_out + j for j in range(len(alias_in))},
        compiler_params=_cparams(1),
        name="mlstm",
    )(*args, *casts.args())


def _merge_kernel(x_ref, a_ref, f_ref, m_ref, mod_ref, g_ref, wg_ref, wa_ref, wf_ref, wm_ref, wo_ref, o_ref):
    x = x_ref[...]
    mod = mod_ref[...]
    u = _modulated_norm(x, g_ref[...], mod, 3).astype(BF16)
    merged = None
    for j, (br_ref, w_ref) in enumerate(((a_ref, wa_ref), (f_ref, wf_ref), (m_ref, wm_ref))):
        gate = _sigmoid(jnp.dot(u, wg_ref[:, j * D_MODEL:(j + 1) * D_MODEL], preferred_element_type=F32))
        term = gate * _bdot(br_ref[...], w_ref[...])
        merged = term if merged is None else merged + term
    out = _bdot(merged, wo_ref[...])
    o_ref[...] = x + mod[5:6] * out


def _merge_call(x, a, f, m, mods, g_norm, wg, wa, wf, wm, wo, st, *, layer):
    tm = TM_WIDE
    n = st.n_all
    br_spec = pl.BlockSpec((tm, W_ATTN), lambda i: (i, 0))
    return pl.pallas_call(
        _merge_kernel,
        grid=(n // tm,),
        in_specs=[
            pl.BlockSpec((tm, D_MODEL), lambda i: (i, 0)),
            br_spec, br_spec, br_spec,
            st.mod_spec(tm),
            pl.BlockSpec((None, None, 1, D_MODEL), lambda i: (layer, 1, 0, 0)),
            _layer_spec((D_MODEL, 3 * D_MODEL), None),
            _layer_spec((W_ATTN, D_MODEL), None), _layer_spec((W_FOUR, D_MODEL), None),
            _layer_spec((W_MLSTM, D_MODEL), None),
            _layer_spec((D_MODEL, D_MODEL), None),
        ],
        out_specs=pl.BlockSpec((tm, D_MODEL), lambda i: (i, 0)),
        out_shape=jax.ShapeDtypeStruct((n, D_MODEL), F32),
        compiler_params=_cparams(1),
        name="merge",
    )(x, a, f, m, mods, g_norm, wg, wa, wf, wm, wo)


def _rope_tables(n_tok):
    tok = np.arange(n_tok)
    inv = ROPE_BASE ** (-np.arange(ROPE_AXIS_PAIRS, dtype=np.float32) / ROPE_AXIS_PAIRS)
    ang = np.concatenate([(tok // GRID_W).astype(np.float32)[:, None] * inv,
                          (tok % GRID_W).astype(np.float32)[:, None] * inv], axis=-1).astype(np.float32)
    c, s = np.cos(ang), np.sin(ang)
    cos_t = np.concatenate([c, c, c, c], axis=-1)
    sin_t = np.concatenate([-s, s, -s, s], axis=-1)
    return jnp.asarray(cos_t, F32), jnp.asarray(sin_t, F32)


def kernel(x_prompt, x_sample, cache_k, cache_v, state_C, state_n, state_m, c, c_ctx, w_ada, b_ada, g_norm,
           w_ffn1_in, w_ffn1_out, w_ffn2_in, w_ffn2_out, w_in, b_mgate, attn_lambda, g_attn_sub, g_mlstm,
           w_branch_gate, w_br_attn, w_br_four, w_br_mlstm, w_out, g_final):
    batch, seq, _ = x_prompt.shape
    dec_batch, dec_seq, _ = x_sample.shape
    st = _Stream(batch, seq, dec_batch, dec_seq)
    cond = jnp.zeros((N_COND, D_MODEL), F32).at[0].set(c_ctx).at[1:1 + dec_batch].set(c)
    mods_all = _ada_call(cond, w_ada, b_ada).reshape(DEPTH, N_COND, N_MOD, D_MODEL)

    b_gate_t = b_mgate[:, :, None]
    g_norm4 = g_norm[:, :, None, :]
    g_fin = g_final[None, :]
    w_t = jnp.swapaxes(w_in, 1, 2).astype(BF16)
    g_sub3 = g_attn_sub[:, None, :]
    g_m3 = g_mlstm[:, None, :]
    cos_t, sin_t = _rope_tables(dec_seq)
    lat_state = (state_C, state_n, state_m)

    xs = (x_prompt.reshape(batch * seq, D_MODEL), x_sample.reshape(dec_batch * dec_seq, D_MODEL))
    kv, states = None, None
    ffn1_w = (w_ffn1_in[0].astype(BF16), w_ffn1_out[0].astype(BF16))
    for l in range(DEPTH):
        mods = mods_all[l]
        lam_init = 0.8 - 0.6 * math.exp(-0.3 * l)
        nxt = l + 1 < DEPTH
        (x,) = _ffn_call(xs, mods, g_norm4, g_fin, *ffn1_w, st, layer=l, sub=0, final=False)
        q, k, ck, cv, zf, mq, mk, mo, vat, vt, gates_t = _proj_call(
            x, mods, g_norm4, w_t, b_gate_t, cos_t, sin_t, kv, st, layer=l)
        kv = (ck, cv)
        a, f, ffn2_in = _attn_four_call(
            q, k, vat, None, attn_lambda, g_sub3, zf, None, [(w_ffn2_in, l)], st,
            layer=l, lam_init=lam_init, latent=False)
        a, f, ffn2_out, w_bg, *nxt_in = _attn_four_call(
            q, k, vat, (cache_k, cache_v), attn_lambda, g_sub3, zf, (a, f),
            [(w_ffn2_out, l), (w_branch_gate, l)] + ([(w_ffn1_in, l + 1)] if nxt else []), st,
            layer=l, lam_init=lam_init, latent=True)
        m, c_fin, n_fin, m_fin, *nxt_out = _mlstm_call(
            mq, mk, vt, mo, gates_t, g_m3, None, states, [(w_ffn1_out, l + 1)] if nxt else [], st,
            layer=l, latent=False)
        states = [c_fin, n_fin, m_fin]
        m, w_ba, w_bf, w_bm, w_o = _mlstm_call(
            mq, mk, vt, mo, gates_t, g_m3, lat_state, (m,),
            [(w_br_attn, l), (w_br_four, l), (w_br_mlstm, l), (w_out, l)], st, layer=l, latent=True)
        x = _merge_call(x, a, f, m, mods, g_norm4, w_bg, w_ba, w_bf, w_bm, w_o, st, layer=l)
        xs = tuple(_ffn_call((x,), mods, g_norm4, g_fin, ffn2_in, ffn2_out, st, layer=l, sub=2,
                             final=(l == DEPTH - 1)))
        if nxt:
            ffn1_w = (nxt_in[0], nxt_out[0])
    y_prompt, y_sample = xs
    return (y_prompt.reshape(batch, seq, D_MODEL), y_sample.reshape(dec_batch, dec_seq, D_MODEL),
            *kv, *states)
```

```python
import functools
import math

import numpy as np
import jax
import jax.numpy as jnp
from jax import lax
from jax.experimental import pallas as pl
from jax.experimental.pallas import tpu as pltpu

D_MODEL = 1024
DEPTH = 2
GRID_W = 64
N_ATTN = 4
DH_ATTN = 64
DV_ATTN = 2 * DH_ATTN
W_ATTN = N_ATTN * DV_ATTN
N_FOUR = 4
DG_FOUR = 128
W_FOUR = N_FOUR * DG_FOUR
N_MLSTM = 4
DH_MLSTM = 128
W_MLSTM = N_MLSTM * DH_MLSTM
N_GATE = 4 * N_MLSTM
LANES = 128
SUBLANES = 16
MV_LO = 3 * W_ATTN + W_FOUR + 2 * W_MLSTM
MV_HI = MV_LO + W_MLSTM
P_MAIN = MV_HI + W_MLSTM
D_FF = 2816
N_MOD = 9
N_COND = 8
ROPE_BASE = 10000.0
ROPE_AXIS_PAIRS = DH_ATTN // 4
ATTN_SCALE = DH_ATTN ** -0.5
LOG2E = 1.4426950408889634
LN2 = 0.6931471805599453
MLSTM_K_SCALE = DH_MLSTM ** -0.5
EPS = 1e-6
MLSTM_CHUNK = 256
VMEM_LIMIT = 56 * 1024 * 1024
TM_WIDE = 1024
TM_PROJ = 1024

F32 = jnp.float32
BF16 = jnp.bfloat16


def _cparams(n_grid):
    return pltpu.CompilerParams(dimension_semantics=("arbitrary",) * n_grid,
                                vmem_limit_bytes=VMEM_LIMIT)


def _bdot(a, b):
    return jnp.dot(a.astype(BF16), b.astype(BF16), preferred_element_type=F32)


def _bdot_nt(a, b):
    return lax.dot_general(a.astype(BF16), b.astype(BF16), (((1,), (1,)), ((), ())),
                           preferred_element_type=F32)


def _sigmoid(x):
    return 1.0 / (1.0 + jnp.exp(-x))


def _log_sigmoid(x):
    return jnp.minimum(x, 0.0) - jnp.log1p(jnp.exp(-jnp.abs(x)))


def _rms(x, g):
    return x * lax.rsqrt(jnp.mean(x * x, axis=-1, keepdims=True) + EPS) * g


def _modulated_norm(x, g, mod, base):
    gain = g * (1.0 + mod[base + 1:base + 2])
    return x * lax.rsqrt(jnp.mean(x * x, axis=-1, keepdims=True) + EPS) * gain + mod[base:base + 1]


def _layer_spec(shape, layer):
    if layer is None:
        return pl.BlockSpec(shape, lambda *_: (0,) * len(shape), pipeline_mode=pl.Buffered(1))
    return pl.BlockSpec((None,) + shape, lambda *_: (layer,) + (0,) * len(shape),
                        pipeline_mode=pl.Buffered(1))


class _CastJobs:
    def __init__(self, jobs, steps, step_index):
        self.jobs, self.steps, self.step_index = list(jobs), steps, step_index

    def __len__(self):
        return len(self.jobs)

    def _rows(self, w):
        rows = w.shape[1] // self.steps
        if rows * self.steps != w.shape[1] or rows % SUBLANES:
            raise ValueError("weight rows do not split into bf16 row tiles over the grid")
        return rows

    def in_specs(self):
        return [pl.BlockSpec((None, self._rows(w), w.shape[2]),
                             lambda *g, layer=layer: (layer, self.step_index(*g), 0))
                for w, layer in self.jobs]

    def out_specs(self):
        return [pl.BlockSpec((self._rows(w), w.shape[2]), lambda *g: (self.step_index(*g), 0))
                for w, _ in self.jobs]

    def out_shapes(self):
        return [jax.ShapeDtypeStruct(w.shape[1:], BF16) for w, _ in self.jobs]

    def args(self):
        return [w for w, _ in self.jobs]

    def wrap(self, body, n_in, n_out):
        n = len(self.jobs)
        if n == 0:
            return body

        def kernel(*refs):
            ins, cast_in = refs[:n_in], refs[n_in:n_in + n]
            outs = refs[n_in + n:n_in + n + n_out]
            cast_out = refs[n_in + n + n_out:n_in + 2 * n + n_out]
            for src, dst in zip(cast_in, cast_out):
                dst[...] = src[...].astype(dst.dtype)
            body(*ins, *outs, *refs[n_in + 2 * n + n_out:])

        return kernel


class _Stream:
    def __init__(self, batch, seq, dec_batch, dec_seq):
        self.batch, self.seq, self.dec_batch, self.dec_seq = batch, seq, dec_batch, dec_seq
        self.n_ctx = batch * seq
        self.n_all = self.n_ctx + dec_batch * dec_seq

    def ctx_tiles(self, tm):
        if self.n_ctx % tm or self.dec_seq % tm:
            raise ValueError("row tile must divide the context rows and one latent sequence")
        return self.n_ctx // tm

    def mod_spec(self, tm):
        n_pt, tps = self.ctx_tiles(tm), self.dec_seq // tm
        return pl.BlockSpec((None, N_MOD, D_MODEL),
                            lambda i: (jnp.where(i < n_pt, 0, 1 + (i - n_pt) // tps), 0, 0))


def _ada_kernel(c_ref, w_ref, b_ref, o_ref):
    c = c_ref[...]
    s = c * _sigmoid(c)
    o_ref[...] = _bdot(s, w_ref[...]) + b_ref[...]


def _ada_call(cond, w_ada, b_ada, cast_w):
    tn = 2304
    n_out = N_MOD * D_MODEL
    nj = n_out // tn
    casts = _CastJobs(cast_w, DEPTH * nj, lambda l, j: l * nj + j)
    return pl.pallas_call(
        casts.wrap(_ada_kernel, 3, 1),
        grid=(DEPTH, nj),
        in_specs=[
            pl.BlockSpec((N_COND, D_MODEL), lambda l, j: (0, 0)),
            pl.BlockSpec((None, D_MODEL, tn), lambda l, j: (l, 0, j)),
            pl.BlockSpec((None, 1, tn), lambda l, j: (l, 0, j)),
        ] + casts.in_specs(),
        out_specs=[pl.BlockSpec((None, N_COND, tn), lambda l, j: (l, 0, j))] + casts.out_specs(),
        out_shape=[jax.ShapeDtypeStruct((DEPTH, N_COND, n_out), F32)] + casts.out_shapes(),
        compiler_params=_cparams(2),
        name="adaln",
    )(cond, w_ada, b_ada.reshape(DEPTH, 1, n_out), *casts.args())


FF_CHUNKS = (768, 768, 768, 512)


def _ffn_kernel(*refs, base, n_x, final, n_pt):
    x_refs = refs[:n_x]
    mod_ref, g_ref, gf_ref, win_ref, wout_ref = refs[n_x:n_x + 5]
    o_refs = refs[n_x + 5:]
    i = pl.program_id(0)
    if n_x == 2:
        x = jnp.where(i < n_pt, x_refs[0][...], x_refs[1][...])
    else:
        x = x_refs[0][...]
    mod = mod_ref[...]
    u = _modulated_norm(x, g_ref[...], mod, base).astype(BF16)
    y = None
    lo = 0
    for width in FF_CHUNKS:
        a = jnp.dot(u, win_ref[:, lo:lo + width], preferred_element_type=F32)
        g = jnp.dot(u, win_ref[:, D_FF + lo:D_FF + lo + width], preferred_element_type=F32)
        hh = (a * _sigmoid(a) * g).astype(BF16)
        part = jnp.dot(hh, wout_ref[lo:lo + width, :], preferred_element_type=F32)
        y = part if y is None else y + part
        lo += width
    xn = x + 0.5 * mod[base + 2:base + 3] * y
    if not final:
        o_refs[0][...] = xn
    else:
        xn = _rms(xn, gf_ref[...])

        @pl.when(i < n_pt)
        def _():
            o_refs[0][...] = xn

        @pl.when(i >= n_pt)
        def _():
            o_refs[1][...] = xn


def _ffn_call(xs, mods, g_norm, g_final, w_in, w_out, st, *, layer, sub, final):
    tm = TM_WIDE
    n_pt = st.ctx_tiles(tm)
    n_tiles = st.n_all // tm
    ctx_spec = pl.BlockSpec((tm, D_MODEL), lambda i: (jnp.minimum(i, n_pt - 1), 0))
    lat_spec = pl.BlockSpec((tm, D_MODEL), lambda i: (jnp.maximum(i - n_pt, 0), 0))
    all_spec = pl.BlockSpec((tm, D_MODEL), lambda i: (i, 0))
    x_specs = [ctx_spec, lat_spec] if len(xs) == 2 else [all_spec]
    if final:
        out_specs = [ctx_spec, lat_spec]
        out_shape = [jax.ShapeDtypeStruct((st.n_ctx, D_MODEL), F32),
                     jax.ShapeDtypeStruct((st.n_all - st.n_ctx, D_MODEL), F32)]
    else:
        out_specs = [all_spec]
        out_shape = [jax.ShapeDtypeStruct((st.n_all, D_MODEL), F32)]
    return pl.pallas_call(
        functools.partial(_ffn_kernel, base=3 * sub, n_x=len(xs), final=final, n_pt=n_pt),
        grid=(n_tiles,),
        in_specs=x_specs + [
            st.mod_spec(tm),
            pl.BlockSpec((None, None, 1, D_MODEL), lambda i: (layer, sub, 0, 0)),
            pl.BlockSpec((1, D_MODEL), lambda i: (0, 0)),
            _layer_spec((D_MODEL, 2 * D_FF), None),
            _layer_spec((D_FF, D_MODEL), None),
        ],
        out_specs=out_specs,
        out_shape=out_shape,
        compiler_params=_cparams(1),
        name="ffn",
    )(*xs, mods, g_norm, g_final, w_in, w_out)


def _swap32(x):
    lane = lax.broadcasted_iota(jnp.int32, x.shape, 1)
    return jnp.where((lane & (DH_ATTN // 2)) == 0,
                     pltpu.roll(x, LANES - DH_ATTN // 2, 1), pltpu.roll(x, DH_ATTN // 2, 1))


def _proj_kernel(*refs, n_alias, n_pt, seqs_per_tile, seq):
    (x_ref, mod_ref, g_ref, wt_ref, bgt_ref, cos_ref, sin_ref) = refs[:7]
    (q_ref, k_ref, ck_ref, cv_ref, zf_ref, mq_ref, mk_ref, mo_ref, vat_ref, vt_ref, gt_ref) = refs[7 + n_alias:]
    i = pl.program_id(0)
    is_latent = i >= n_pt
    x = x_ref[...]
    u = _modulated_norm(x, g_ref[...], mod_ref[...], 3).astype(BF16)
    za = _bdot_nt(u, wt_ref[:3 * W_ATTN, :])
    cos_t = cos_ref[...]
    sin_t = sin_ref[...]
    plain_k = []
    for h in range(N_ATTN):
        zq = za[:, h * DV_ATTN:(h + 1) * DV_ATTN]
        zk = za[:, W_ATTN + h * DV_ATTN:W_ATTN + (h + 1) * DV_ATTN]
        plain_k.append(zk)
        q_ref[h] = jnp.where(is_latent, zq * cos_t + _swap32(zq) * sin_t, zq).astype(q_ref.dtype)
        k_ref[h] = jnp.where(is_latent, zk * cos_t + _swap32(zk) * sin_t, zk).astype(k_ref.dtype)

    @pl.when(i < n_pt)
    def _():
        for h in range(N_ATTN):
            zv = za[:, 2 * W_ATTN + h * DV_ATTN:2 * W_ATTN + (h + 1) * DV_ATTN]
            for s in range(seqs_per_tile):
                rs = slice(s * seq, (s + 1) * seq)
                ck_ref[s, h] = plain_k[h][rs]
                cv_ref[s, h] = zv[rs]

    zb = _bdot_nt(u, wt_ref[3 * W_ATTN:MV_LO, :])
    zf_ref[...] = zb[:, :W_FOUR].astype(zf_ref.dtype)
    off = W_FOUR
    mq_ref[...] = zb[:, off:off + W_MLSTM].astype(mq_ref.dtype)
    mk_ref[...] = (zb[:, off + W_MLSTM:off + 2 * W_MLSTM] * MLSTM_K_SCALE).astype(mk_ref.dtype)
    mo_ref[...] = _bdot_nt(u, wt_ref[MV_HI:P_MAIN, :])
    vat_ref[...] = _bdot_nt(wt_ref[2 * W_ATTN:3 * W_ATTN, :], u).astype(vat_ref.dtype)
    vt_ref[...] = _bdot_nt(wt_ref[MV_LO:MV_HI, :], u)
    gpt = _bdot_nt(wt_ref[P_MAIN:, :], u) + bgt_ref[...]
    sub = lax.broadcasted_iota(jnp.int32, gpt.shape, 0)
    gt_ref[...] = jnp.where((sub & N_MLSTM) != 0, _log_sigmoid(gpt), gpt)


def _proj_call(x, mods, g_norm, w_t, bgt, cos_t, sin_t, kv_prev, st, *, layer):
    tm = TM_PROJ
    n_pt = st.ctx_tiles(tm)
    if tm % st.seq:
        raise ValueError("a context row tile must hold whole sequences")
    seqs_per_tile, tps = tm // st.seq, st.dec_seq // tm
    n = st.n_all
    head_shape = jax.ShapeDtypeStruct((N_ATTN, n, DV_ATTN), BF16)
    head_spec = pl.BlockSpec((N_ATTN, tm, DV_ATTN), lambda i: (0, i, 0))
    kv_shape = jax.ShapeDtypeStruct((st.batch, DEPTH, N_ATTN, st.seq, DV_ATTN), F32)
    kv_spec = pl.BlockSpec((seqs_per_tile, None, N_ATTN, st.seq, DV_ATTN),
                           lambda i: (jnp.minimum(i, n_pt - 1), layer, 0, 0, 0))
    tok_shape = jax.ShapeDtypeStruct((n, W_ATTN), F32)
    tok_bf16 = jax.ShapeDtypeStruct((n, W_ATTN), BF16)
    tok_spec = pl.BlockSpec((tm, W_ATTN), lambda i: (i, 0))
    rope_spec = pl.BlockSpec((tm, LANES), lambda i: (jnp.maximum(i - n_pt, 0) % tps, 0))
    alias_in = list(kv_prev) if kv_prev is not None else []
    n_in = 7
    return pl.pallas_call(
        functools.partial(_proj_kernel, n_alias=len(alias_in), n_pt=n_pt, seqs_per_tile=seqs_per_tile,
                          seq=st.seq),
        grid=(n // tm,),
        in_specs=[
            pl.BlockSpec((tm, D_MODEL), lambda i: (i, 0)),
            st.mod_spec(tm),
            pl.BlockSpec((None, None, 1, D_MODEL), lambda i: (layer, 1, 0, 0)),
            _layer_spec((w_t.shape[1], D_MODEL), layer),
            _layer_spec((N_GATE, 1), layer),
            rope_spec, rope_spec,
        ] + [pl.BlockSpec(memory_space=pl.ANY)] * len(alias_in),
        out_specs=[head_spec, head_spec, kv_spec, kv_spec, tok_spec, tok_spec, tok_spec, tok_spec,
                   pl.BlockSpec((W_ATTN, tm), lambda i: (0, i)),
                   pl.BlockSpec((W_MLSTM, tm), lambda i: (0, i)),
                   pl.BlockSpec((N_GATE, tm), lambda i: (0, i))],
        out_shape=[head_shape, head_shape, kv_shape, kv_shape, tok_bf16, tok_bf16, tok_bf16, tok_shape,
                   jax.ShapeDtypeStruct((W_ATTN, n), BF16),
                   jax.ShapeDtypeStruct((W_MLSTM, n), F32),
                   jax.ShapeDtypeStruct((N_GATE, n), F32)],
        input_output_aliases={n_in + j: 2 + j for j in range(len(alias_in))},
        compiler_params=_cparams(1),
        name="mixer_proj",
    )(x, mods, g_norm, w_t, bgt, cos_t, sin_t, *alias_in)


def _attn_kernel(*refs, lam_init, cached, nb, tq, seq):
    q_ref, k_ref, vt_ref = refs[:3]
    pos = 3
    if cached:
        ck_ref, cv_ref = refs[pos:pos + 2]
        pos += 2
    lam_ref, g_ref = refs[pos:pos + 2]
    o_ref = refs[-1]
    lp = lam_ref[...]
    lam = (jnp.exp(jnp.sum(lp[0:1] * lp[1:2], axis=-1, keepdims=True))
           - jnp.exp(jnp.sum(lp[2:3] * lp[3:4], axis=-1, keepdims=True)) + lam_init)
    g_sub = g_ref[...]
    lane = lax.broadcasted_iota(jnp.int32, (tq, DV_ATTN), 1)
    heads = range(N_ATTN)
    results = []
    for bi in range(nb):
        ks = slice(bi * seq, (bi + 1) * seq)
        keys = [[k_ref[h, ks] for h in heads]]
        vals_t = [[vt_ref[h * DV_ATTN:(h + 1) * DV_ATTN, ks] for h in heads]]
        if cached:
            keys.insert(0, [ck_ref[h].astype(BF16) for h in heads])
            vals_t.insert(0, [cv_ref[h].T.astype(BF16) for h in heads])
        qh = [q_ref[h, bi * tq:(bi + 1) * tq].astype(F32) * (ATTN_SCALE * LOG2E) for h in heads]
        exps, dens = [], []
        for first_map in (True, False):
            qm = [jnp.where((lane < DH_ATTN) == first_map, q, 0.0).astype(BF16) for q in qh]
            s = [jnp.concatenate([_bdot_nt(kg[h], qm[h]) for h in heads], axis=1) for kg in keys]
            m = functools.reduce(jnp.maximum, [jnp.max(si, axis=0, keepdims=True) for si in s])
            e = [jnp.exp2(si - m) for si in s]
            exps.append(e)
            dens.append(functools.reduce(jnp.add, [jnp.sum(ei, axis=0, keepdims=True) for ei in e]))
        ratio = lam * dens[0] / dens[1]
        inv = 1.0 / dens[0]
        probs = [(e1 - e2 * ratio).astype(BF16) for e1, e2 in zip(*exps)]
        for h in heads:
            cols = slice(h * tq, (h + 1) * tq)
            o_t = None
            for pj, vg in zip(probs, vals_t):
                part = jnp.dot(vg[h], pj[:, cols], preferred_element_type=F32)
                o_t = part if o_t is None else o_t + part
            o_t = o_t * inv[:, cols]
            y_t = o_t * lax.rsqrt(jnp.mean(o_t * o_t, axis=0, keepdims=True) + EPS)
            results.append((bi, h, (y_t.T * g_sub * (1.0 - lam_init)).astype(o_ref.dtype)))
    for bi, h, y in results:
        o_ref[bi * tq:(bi + 1) * tq, h * DV_ATTN:(h + 1) * DV_ATTN] = y


def _attn_four_kernel(*refs, n_attn, attn_kw, four_kw, once_per_seq):
    a_ref, f_ref = refs[-2:]
    _attn_kernel(*refs[:n_attn], a_ref, **attn_kw)
    four = functools.partial(_four_kernel, *refs[n_attn:n_attn + 4], f_ref, **four_kw)
    if once_per_seq:
        pl.when(pl.program_id(1) == 0)(four)
    else:
        four()


def _attn_four_call(q, k, v_t, cache, lam_p, g_sub, zf, outs_prev, cast_w, st, *, layer, lam_init, latent):
    n = st.n_all
    if latent:
        batch, seq, nb, tq = st.dec_batch, st.dec_seq, 1, st.dec_seq // 2
        row0 = st.n_ctx
    else:
        batch, seq, nb, tq = st.batch, st.seq, 4, st.seq
        row0 = 0
    nq = seq // tq
    if nq > 1 and nb > 1:
        raise ValueError("query blocks of several sequences are not contiguous rows")
    q0, s0 = row0 // (nb * tq), row0 // (nb * seq)
    in_specs = [pl.BlockSpec((N_ATTN, nb * tq, DV_ATTN), lambda b, i: (0, q0 + b * nq + i, 0)),
                pl.BlockSpec((N_ATTN, nb * seq, DV_ATTN), lambda b, i: (0, s0 + b, 0)),
                pl.BlockSpec((W_ATTN, nb * seq), lambda b, i: (0, s0 + b))]
    args = [q, k, v_t]
    if cache is not None:
        past = cache[0].shape[3]
        c_spec = pl.BlockSpec((None, None, N_ATTN, past, DV_ATTN), lambda b, i: (b, layer, 0, 0, 0))
        in_specs += [c_spec, c_spec]
        args += list(cache)
    in_specs += [pl.BlockSpec((None, 4, DH_ATTN), lambda b, i: (layer, 0, 0)),
                 pl.BlockSpec((None, 1, DV_ATTN), lambda b, i: (layer, 0, 0))]
    args += [lam_p, g_sub]
    n_attn = len(args)
    w_d, ct, s_t = _dft_tables(seq)
    seq_rows = pl.BlockSpec((nb * seq, W_FOUR), lambda b, i: (s0 + b, 0))
    in_specs += [seq_rows,
                 pl.BlockSpec((DG_FOUR, 2 * DG_FOUR), lambda b, i: (0, 0)),
                 pl.BlockSpec((seq, seq), lambda b, i: (0, 0)),
                 pl.BlockSpec((seq, seq), lambda b, i: (0, 0))]
    args += [zf, w_d, ct, s_t]
    aliases = {}
    if outs_prev is not None:
        aliases = {len(args): 0, len(args) + 1: 1}
        in_specs += [pl.BlockSpec(memory_space=pl.ANY)] * 2
        args += list(outs_prev)
    casts = _CastJobs(cast_w, (batch // nb) * nq, lambda b, i: b * nq + i)
    body = functools.partial(
        _attn_four_kernel, n_attn=n_attn, once_per_seq=nq > 1,
        attn_kw=dict(lam_init=lam_init, cached=cache is not None, nb=nb, tq=tq, seq=seq),
        four_kw=dict(nb=nb, seq=seq))
    return pl.pallas_call(
        casts.wrap(body, len(args), 2),
        grid=(batch // nb, nq),
        in_specs=in_specs + casts.in_specs(),
        out_specs=[pl.BlockSpec((nb * tq, W_ATTN), lambda b, i: (q0 + b * nq + i, 0)), seq_rows]
        + casts.out_specs(),
        out_shape=[jax.ShapeDtypeStruct((n, W_ATTN), BF16), jax.ShapeDtypeStruct((n, W_FOUR), BF16)]
        + casts.out_shapes(),
        input_output_aliases=aliases,
        compiler_params=_cparams(2),
        name="attn_fourier",
    )(*args, *casts.args())


def _dft_tables(seq):
    def cs(n):
        j = np.arange(n)
        ang = 2.0 * np.pi * ((j[:, None] * j[None, :]) % n) / n
        return np.cos(ang) / math.sqrt(n), np.sin(ang) / math.sqrt(n)

    cd, sd = cs(DG_FOUR)
    ct, st = cs(seq)
    w_d = jnp.asarray(np.concatenate([cd, sd], axis=1), F32)
    return w_d.astype(BF16), jnp.asarray(ct, F32).astype(BF16), jnp.asarray(-st, F32).astype(BF16)


def _four_kernel(*refs, nb, seq):
    z_ref, wd_ref, ct_ref, st_ref = refs[:4]
    o_ref = refs[-1]
    y_cos, y_sin = [], []
    for gidx in range(N_FOUR):
        y = jnp.dot(z_ref[:, gidx * DG_FOUR:(gidx + 1) * DG_FOUR], wd_ref[...], preferred_element_type=F32)
        y_cos.append(y[:, :DG_FOUR].astype(BF16))
        y_sin.append(y[:, DG_FOUR:].astype(BF16))
    y_cos = jnp.concatenate(y_cos, axis=1)
    y_sin = jnp.concatenate(y_sin, axis=1)
    for s in range(nb):
        rows = slice(s * seq, (s + 1) * seq)
        o_ref[rows, :] = (jnp.dot(ct_ref[...], y_cos[rows], preferred_element_type=F32)
                          + jnp.dot(st_ref[...], y_sin[rows], preferred_element_type=F32)).astype(o_ref.dtype)


def _mlstm_kernel(*refs, seq, nb, seeded, emit_state, n_alias):
    mq_ref, mk_ref, vt_ref, mo_ref, gt_ref, g_ref = refs[:6]
    pos = 6
    if seeded:
        c0_ref, n0_ref, m0_ref = refs[pos:pos + 3]
        pos += 3
    pos += n_alias
    o_ref = refs[pos]
    if emit_state:
        c1_ref, n1_ref, m1_ref = refs[pos + 1:pos + 4]

    L = min(MLSTM_CHUNK, seq)
    nc = seq // L
    s_idx = lax.broadcasted_iota(jnp.int32, (L, L), 0)
    t_idx = lax.broadcasted_iota(jnp.int32, (L, L), 1)
    before = (s_idx <= t_idx, s_idx >= t_idx)
    tri = (jnp.where(before[1], 1.0, 0.0).astype(BF16),
           jnp.where(before[0], 1.0, 0.0).astype(BF16))
    chains = [(bi, d, h) for bi in range(nb) for d in range(2) for h in range(N_MLSTM)]

    state = {}
    for (bi, d, h) in chains:
        if seeded:
            state[bi, d, h] = (c0_ref[bi, d, h], n0_ref[bi, d, h:h + 1, :], m0_ref[bi, d:d + 1, h:h + 1])
        else:
            state[bi, d, h] = (None, None, jnp.zeros((1, 1), F32))

    terms = {}
    for c in range(nc):
        for bi in range(nb):
            for d in range(2):
                r0 = bi * seq + (c if d == 0 else nc - 1 - c) * L
                g_t = gt_ref[:, r0:r0 + L]
                hi = g_t.astype(BF16)
                rem = g_t - hi.astype(F32)
                mid = rem.astype(BF16)
                lo = (rem - mid.astype(F32)).astype(BF16)
                pieces = _bdot_nt(jnp.concatenate([hi, mid, lo], axis=0), tri[d])
                cum_t = pieces[0:N_GATE] + pieces[N_GATE:2 * N_GATE] + pieces[2 * N_GATE:]
                c_t = (g_t - pltpu.roll(cum_t, N_GATE - N_MLSTM, 0)) * LOG2E
                col = jnp.concatenate([c_t, jnp.zeros((LANES - N_GATE, L), F32)], axis=0).T
                terms[c, bi, d] = (r0, col, g_t, cum_t)

    h_t = {}
    for c in range(nc):
        new_state = {}
        need_update = emit_state or c < nc - 1
        group = [(d, h) for d in range(2) for h in range(N_MLSTM)]
        for bi in range(nb):
            has_state = state[bi, 0, 0][0] is not None
            qs, ks, vts, cbs, s0s, i_rows, b_rows, m_prevs = [], [], [], [], [], [], [], []
            for d, h in group:
                r0, col, g_t, cum_t = terms[c, bi, d]
                ci = 2 * d * N_MLSTM + h
                cf = ci + N_MLSTM
                hs = slice(h * DH_MLSTM, (h + 1) * DH_MLSTM)
                qs.append(mq_ref[r0:r0 + L, hs])
                ks.append(mk_ref[r0:r0 + L, hs])
                vts.append(vt_ref[hs, r0:r0 + L])
                cbs.append(jnp.where(before[d], col[:, ci:ci + 1], -jnp.inf))
                s0s.append(_bdot_nt(ks[-1], qs[-1]))
                i_rows.append(g_t[ci:ci + 1, :])
                b_rows.append(cum_t[cf:cf + 1, :])
                m_prevs.append(jnp.broadcast_to(state[bi, d, h][2], (1, L)))
            cb = jnp.concatenate(cbs, axis=1)
            i_row, b_row = jnp.concatenate(i_rows, axis=1), jnp.concatenate(b_rows, axis=1)
            m_prev = jnp.concatenate(m_prevs, axis=1)
            m2_prev = m_prev * LOG2E
            m2_row = jnp.maximum(jnp.max(cb, axis=0, keepdims=True), m2_prev)
            s_t = jnp.concatenate(s0s, axis=1) * jnp.exp2(cb - m2_row)
            den = jnp.sum(s_t, axis=0, keepdims=True)
            s_bf = s_t.astype(BF16)
            m_t = b_row + m2_row * LN2
            floor = jnp.exp(-m_t)
            if has_state:
                sp = jnp.exp2(m2_prev - m2_row)
            if need_update:
                tot, new = [], []
                for j, (d, h) in enumerate(group):
                    last = j * L + (L - 1 if d == 0 else 0)
                    tot.append(jnp.broadcast_to(b_row[:, last:last + 1], (1, L)))
                    new.append(jnp.broadcast_to(m_t[:, last:last + 1], (1, L)))
                b_tot, m_new = jnp.concatenate(tot, axis=1), jnp.concatenate(new, axis=1)
                wl = jnp.exp(b_tot + (i_row - b_row) - m_new)
                if has_state:
                    decay = jnp.exp(b_tot + m_prev - m_new)
            for j, (d, h) in enumerate(group):
                cols = slice(j * L, (j + 1) * L)
                c_prev, n_prev, _ = state[bi, d, h]
                num_t = jnp.dot(vts[j].astype(BF16), s_bf[:, cols], preferred_element_type=F32)
                den_j = den[:, cols]
                if has_state:
                    cn = jnp.concatenate([c_prev, jnp.broadcast_to(n_prev, (SUBLANES, DH_MLSTM))], axis=0)
                    cq = _bdot_nt(cn, qs[j])
                    num_t = num_t + sp[:, cols] * cq[:DH_MLSTM]
                    den_j = den_j + sp[:, cols] * cq[DH_MLSTM:DH_MLSTM + 1]
                h_t[bi, d, h, c] = num_t / jnp.maximum(jnp.abs(den_j), floor[:, cols])
                if need_update:
                    wl_j = wl[:, cols]
                    vw = jnp.concatenate([vts[j] * wl_j, jnp.broadcast_to(wl_j, (SUBLANES, L))], axis=0)
                    upd = jnp.dot(vw.astype(BF16), ks[j], preferred_element_type=F32)
                    c_new, n_new = upd[:DH_MLSTM], upd[DH_MLSTM:DH_MLSTM + 1]
                    if has_state:
                        decay_j = decay[:, j * L:j * L + 1]
                        c_new = decay_j * c_prev + c_new
                        n_new = decay_j * n_prev + n_new
                    new_state[bi, d, h] = (c_new, n_new, m_new[:, j * L:j * L + 1])
        state = new_state

    g_m = g_ref[...]
    for bi in range(nb):
        for h in range(N_MLSTM):
            hs = slice(h * DH_MLSTM, (h + 1) * DH_MLSTM)
            fwd = [h_t[bi, 0, h, c] for c in range(nc)]
            bwd = [h_t[bi, 1, h, nc - 1 - c] for c in range(nc)]
            hsum = (fwd[0] if nc == 1 else jnp.concatenate(fwd, axis=1)) \
                + (bwd[0] if nc == 1 else jnp.concatenate(bwd, axis=1))
            y = hsum * lax.rsqrt(jnp.mean(hsum * hsum, axis=0, keepdims=True) + EPS)
            rows = slice(bi * seq, (bi + 1) * seq)
            o_ref[rows, hs] = (y.T * g_m * _sigmoid(mo_ref[rows, hs])).astype(o_ref.dtype)

    if emit_state:
        for (bi, d, h) in chains:
            c_fin, n_fin, m_fin = state[bi, d, h]
            c1_ref[bi, d, h] = c_fin
            n1_ref[bi, d, h:h + 1, :] = n_fin
            m1_ref[bi, d:d + 1, h:h + 1] = m_fin


def _mlstm_call(mq, mk, vt, mo, gates_t, g_m, state, alias_prev, cast_w, st, *, layer, latent):
    if latent:
        batch, seq, nb, row0 = st.dec_batch, st.dec_seq, 1, st.n_ctx
    else:
        batch, seq, nb, row0 = st.batch, st.seq, 4, 0
    emit_state = not latent
    rows = nb * seq
    b0 = row0 // rows
    tok_spec = pl.BlockSpec((rows, W_MLSTM), lambda b: (b0 + b, 0))
    in_specs = [tok_spec, tok_spec,
                pl.BlockSpec((W_MLSTM, rows), lambda b: (0, b0 + b)),
                tok_spec,
                pl.BlockSpec((N_GATE, rows), lambda b: (0, b0 + b)),
                pl.BlockSpec((None, 1, DH_MLSTM), lambda b: (layer, 0, 0))]
    args = [mq, mk, vt, mo, gates_t, g_m]
    state_specs = [
        pl.BlockSpec((nb, None, 2, N_MLSTM, DH_MLSTM, DH_MLSTM), lambda b: (b, layer, 0, 0, 0, 0)),
        pl.BlockSpec((nb, None, 2, N_MLSTM, DH_MLSTM), lambda b: (b, layer, 0, 0, 0)),
        pl.BlockSpec((nb, None, 2, N_MLSTM), lambda b: (b, layer, 0, 0)),
    ]
    if state is not None:
        in_specs += state_specs
        args += list(state)
    alias_in = list(alias_prev) if alias_prev is not None else []
    n_in = len(args)
    in_specs += [pl.BlockSpec(memory_space=pl.ANY)] * len(alias_in)
    args += alias_in
    out_specs = [tok_spec]
    out_shape = [jax.ShapeDtypeStruct((st.n_all, W_MLSTM), BF16)]
    if emit_state:
        out_specs += state_specs
        out_shape += [
            jax.ShapeDtypeStruct((batch, DEPTH, 2, N_MLSTM, DH_MLSTM, DH_MLSTM), F32),
            jax.ShapeDtypeStruct((batch, DEPTH, 2, N_MLSTM, DH_MLSTM), F32),
            jax.ShapeDtypeStruct((batch, DEPTH, 2, N_MLSTM), F32),
        ]
    first_out = 1 if emit_state else 0
    casts = _CastJobs(cast_w, batch // nb, lambda b: b)
    body = functools.partial(_mlstm_kernel, seq=seq, nb=nb, seeded=state is not None, emit_state=emit_state,
                             n_alias=len(alias_in))
    return pl.pallas_call(
        casts.wrap(body, len(args), len(out_specs)),
        grid=(batch // nb,),
        in_specs=in_specs + casts.in_specs(),
        out_specs=out_specs + casts.out_specs(),
        out_shape=out_shape + casts.out_shapes(),
        input_output_aliases={n_in + j: first_out + j for j in range(len(alias_in))},
        compiler_params=_cparams(1),
        name="mlstm",
    )(*args, *casts.args())


def _merge_kernel(x_ref, a_ref, f_ref, m_ref, mod_ref, g_ref, wg_ref, wa_ref, wf_ref, wm_ref, wo_ref, o_ref):
    x = x_ref[...]
    mod = mod_ref[...]
    u = _modulated_norm(x, g_ref[...], mod, 3).astype(BF16)
    merged = None
    for j, (br_ref, w_ref) in enumerate(((a_ref, wa_ref), (f_ref, wf_ref), (m_ref, wm_ref))):
        gate = _sigmoid(jnp.dot(u, wg_ref[:, j * D_MODEL:(j + 1) * D_MODEL], preferred_element_type=F32))
        term = gate * _bdot(br_ref[...], w_ref[...])
        merged = term if merged is None else merged + term
    out = _bdot(merged, wo_ref[...])
    o_ref[...] = x + mod[5:6] * out


def _merge_call(x, a, f, m, mods, g_norm, wg, wa, wf, wm, wo, st, *, layer):
    tm = TM_WIDE
    n = st.n_all
    br_spec = pl.BlockSpec((tm, W_ATTN), lambda i: (i, 0))
    return pl.pallas_call(
        _merge_kernel,
        grid=(n // tm,),
        in_specs=[
            pl.BlockSpec((tm, D_MODEL), lambda i: (i, 0)),
            br_spec, br_spec, br_spec,
            st.mod_spec(tm),
            pl.BlockSpec((None, None, 1, D_MODEL), lambda i: (layer, 1, 0, 0)),
            _layer_spec((D_MODEL, 3 * D_MODEL), None),
            _layer_spec((W_ATTN, D_MODEL), None), _layer_spec((W_FOUR, D_MODEL), None),
            _layer_spec((W_MLSTM, D_MODEL), None),
            _layer_spec((D_MODEL, D_MODEL), None),
        ],
        out_specs=pl.BlockSpec((tm, D_MODEL), lambda i: (i, 0)),
        out_shape=jax.ShapeDtypeStruct((n, D_MODEL), F32),
        compiler_params=_cparams(1),
        name="merge",
    )(x, a, f, m, mods, g_norm, wg, wa, wf, wm, wo)


def _rope_tables(n_tok):
    tok = np.arange(n_tok)
    inv = ROPE_BASE ** (-np.arange(ROPE_AXIS_PAIRS, dtype=np.float32) / ROPE_AXIS_PAIRS)
    ang = np.concatenate([(tok // GRID_W).astype(np.float32)[:, None] * inv,
                          (tok % GRID_W).astype(np.float32)[:, None] * inv], axis=-1).astype(np.float32)
    c, s = np.cos(ang), np.sin(ang)
    cos_t = np.concatenate([c, c, c, c], axis=-1)
    sin_t = np.concatenate([-s, s, -s, s], axis=-1)
    return jnp.asarray(cos_t, F32), jnp.asarray(sin_t, F32)


def kernel(x_prompt, x_sample, cache_k, cache_v, state_C, state_n, state_m, c, c_ctx, w_ada, b_ada, g_norm,
           w_ffn1_in, w_ffn1_out, w_ffn2_in, w_ffn2_out, w_in, b_mgate, attn_lambda, g_attn_sub, g_mlstm,
           w_branch_gate, w_br_attn, w_br_four, w_br_mlstm, w_out, g_final):
    batch, seq, _ = x_prompt.shape
    dec_batch, dec_seq, _ = x_sample.shape
    st = _Stream(batch, seq, dec_batch, dec_seq)
    cond = jnp.zeros((N_COND, D_MODEL), F32).at[0].set(c_ctx).at[1:1 + dec_batch].set(c)
    mods_flat, *ffn1_w = _ada_call(cond, w_ada, b_ada, [(w_ffn1_in, 0), (w_ffn1_out, 0)])
    mods_all = mods_flat.reshape(DEPTH, N_COND, N_MOD, D_MODEL)

    b_gate_t = b_mgate[:, :, None]
    g_norm4 = g_norm[:, :, None, :]
    g_fin = g_final[None, :]
    w_t = jnp.swapaxes(w_in, 1, 2).astype(BF16)
    g_sub3 = g_attn_sub[:, None, :]
    g_m3 = g_mlstm[:, None, :]
    cos_t, sin_t = _rope_tables(dec_seq)
    lat_state = (state_C, state_n, state_m)

    xs = (x_prompt.reshape(batch * seq, D_MODEL), x_sample.reshape(dec_batch * dec_seq, D_MODEL))
    kv, states = None, None
    for l in range(DEPTH):
        mods = mods_all[l]
        lam_init = 0.8 - 0.6 * math.exp(-0.3 * l)
        nxt = l + 1 < DEPTH
        (x,) = _ffn_call(xs, mods, g_norm4, g_fin, *ffn1_w, st, layer=l, sub=0, final=False)
        q, k, ck, cv, zf, mq, mk, mo, vat, vt, gates_t = _proj_call(
            x, mods, g_norm4, w_t, b_gate_t, cos_t, sin_t, kv, st, layer=l)
        kv = (ck, cv)
        a, f, ffn2_in = _attn_four_call(
            q, k, vat, None, attn_lambda, g_sub3, zf, None, [(w_ffn2_in, l)], st,
            layer=l, lam_init=lam_init, latent=False)
        a, f, ffn2_out, w_bg, *nxt_in = _attn_four_call(
            q, k, vat, (cache_k, cache_v), attn_lambda, g_sub3, zf, (a, f),
            [(w_ffn2_out, l), (w_branch_gate, l)] + ([(w_ffn1_in, l + 1)] if nxt else []), st,
            layer=l, lam_init=lam_init, latent=True)
        m, c_fin, n_fin, m_fin, *nxt_out = _mlstm_call(
            mq, mk, vt, mo, gates_t, g_m3, None, states, [(w_ffn1_out, l + 1)] if nxt else [], st,
            layer=l, latent=False)
        states = [c_fin, n_fin, m_fin]
        m, w_ba, w_bf, w_bm, w_o = _mlstm_call(
            mq, mk, vt, mo, gates_t, g_m3, lat_state, (m,),
            [(w_br_attn, l), (w_br_four, l), (w_br_mlstm, l), (w_out, l)], st, layer=l, latent=True)
        x = _merge_call(x, a, f, m, mods, g_norm4, w_bg, w_ba, w_bf, w_bm, w_o, st, layer=l)
        xs = tuple(_ffn_call((x,), mods, g_norm4, g_fin, ffn2_in, ffn2_out, st, layer=l, sub=2,
                             final=(l == DEPTH - 1)))
        if nxt:
            ffn1_w = (nxt_in[0], nxt_out[0])
    y_prompt, y_sample = xs
    return (y_prompt.reshape(batch, seq, D_MODEL), y_sample.reshape(dec_batch, dec_seq, D_MODEL),
            *kv, *states)
```

```python
import functools
import math

import numpy as np
import jax
import jax.numpy as jnp
from jax import lax
from jax.experimental import pallas as pl
from jax.experimental.pallas import tpu as pltpu

D_MODEL = 1024
DEPTH = 2
GRID_W = 64
N_ATTN = 4
DH_ATTN = 64
DV_ATTN = 2 * DH_ATTN
W_ATTN = N_ATTN * DV_ATTN
N_FOUR = 4
DG_FOUR = 128
W_FOUR = N_FOUR * DG_FOUR
N_MLSTM = 4
DH_MLSTM = 128
W_MLSTM = N_MLSTM * DH_MLSTM
N_GATE = 4 * N_MLSTM
LANES = 128
SUBLANES = 16
MV_LO = 3 * W_ATTN + W_FOUR + 2 * W_MLSTM
MV_HI = MV_LO + W_MLSTM
P_MAIN = MV_HI + W_MLSTM
D_FF = 2816
N_MOD = 9
N_COND = 8
ROPE_BASE = 10000.0
ROPE_AXIS_PAIRS = DH_ATTN // 4
ATTN_SCALE = DH_ATTN ** -0.5
LOG2E = 1.4426950408889634
LN2 = 0.6931471805599453
MLSTM_K_SCALE = DH_MLSTM ** -0.5
EPS = 1e-6
MLSTM_CHUNK = 256
VMEM_LIMIT = 56 * 1024 * 1024
TM_WIDE = 1024
TM_PROJ = 1024

F32 = jnp.float32
BF16 = jnp.bfloat16


def _cparams(n_grid):
    return pltpu.CompilerParams(dimension_semantics=("arbitrary",) * n_grid,
                                vmem_limit_bytes=VMEM_LIMIT)


def _bdot(a, b):
    return jnp.dot(a.astype(BF16), b.astype(BF16), preferred_element_type=F32)


def _bdot_nt(a, b):
    return lax.dot_general(a.astype(BF16), b.astype(BF16), (((1,), (1,)), ((), ())),
                           preferred_element_type=F32)


def _sigmoid(x):
    return 1.0 / (1.0 + jnp.exp(-x))


def _log_sigmoid(x):
    return jnp.minimum(x, 0.0) - jnp.log1p(jnp.exp(-jnp.abs(x)))


def _rms(x, g):
    return x * lax.rsqrt(jnp.mean(x * x, axis=-1, keepdims=True) + EPS) * g


def _modulated_norm(x, g, mod, base):
    gain = g * (1.0 + mod[base + 1:base + 2])
    return x * lax.rsqrt(jnp.mean(x * x, axis=-1, keepdims=True) + EPS) * gain + mod[base:base + 1]


def _layer_spec(shape, layer):
    if layer is None:
        return pl.BlockSpec(shape, lambda *_: (0,) * len(shape), pipeline_mode=pl.Buffered(1))
    return pl.BlockSpec((None,) + shape, lambda *_: (layer,) + (0,) * len(shape),
                        pipeline_mode=pl.Buffered(1))


class _CastJobs:
    def __init__(self, jobs, steps, step_index):
        self.jobs, self.steps, self.step_index = list(jobs), steps, step_index

    def __len__(self):
        return len(self.jobs)

    def _rows(self, w):
        rows = w.shape[1] // self.steps
        if rows * self.steps != w.shape[1] or rows % SUBLANES:
            raise ValueError("weight rows do not split into bf16 row tiles over the grid")
        return rows

    def in_specs(self):
        return [pl.BlockSpec((None, self._rows(w), w.shape[2]),
                             lambda *g, layer=layer: (layer, self.step_index(*g), 0))
                for w, layer in self.jobs]

    def out_specs(self):
        return [pl.BlockSpec((self._rows(w), w.shape[2]), lambda *g: (self.step_index(*g), 0))
                for w, _ in self.jobs]

    def out_shapes(self):
        return [jax.ShapeDtypeStruct(w.shape[1:], BF16) for w, _ in self.jobs]

    def args(self):
        return [w for w, _ in self.jobs]

    def wrap(self, body, n_in, n_out):
        n = len(self.jobs)
        if n == 0:
            return body

        def kernel(*refs):
            ins, cast_in = refs[:n_in], refs[n_in:n_in + n]
            outs = refs[n_in + n:n_in + n + n_out]
            cast_out = refs[n_in + n + n_out:n_in + 2 * n + n_out]
            for src, dst in zip(cast_in, cast_out):
                dst[...] = src[...].astype(dst.dtype)
            body(*ins, *outs, *refs[n_in + 2 * n + n_out:])

        return kernel


class _Stream:
    def __init__(self, batch, seq, dec_batch, dec_seq):
        self.batch, self.seq, self.dec_batch, self.dec_seq = batch, seq, dec_batch, dec_seq
        self.n_ctx = batch * seq
        self.n_all = self.n_ctx + dec_batch * dec_seq

    def ctx_tiles(self, tm):
        if self.n_ctx % tm or self.dec_seq % tm:
            raise ValueError("row tile must divide the context rows and one latent sequence")
        return self.n_ctx // tm

    def mod_spec(self, tm):
        n_pt, tps = self.ctx_tiles(tm), self.dec_seq // tm
        return pl.BlockSpec((None, N_MOD, D_MODEL),
                            lambda i: (jnp.where(i < n_pt, 0, 1 + (i - n_pt) // tps), 0, 0))


def _ada_kernel(c_ref, w_ref, b_ref, o_ref):
    c = c_ref[...]
    s = c * _sigmoid(c)
    o_ref[...] = _bdot(s, w_ref[...]) + b_ref[...]


def _ada_call(cond, w_ada, b_ada, cast_w):
    tn = 2304
    n_out = N_MOD * D_MODEL
    nj = n_out // tn
    casts = _CastJobs(cast_w, DEPTH * nj, lambda l, j: l * nj + j)
    return pl.pallas_call(
        casts.wrap(_ada_kernel, 3, 1),
        grid=(DEPTH, nj),
        in_specs=[
            pl.BlockSpec((N_COND, D_MODEL), lambda l, j: (0, 0)),
            pl.BlockSpec((None, D_MODEL, tn), lambda l, j: (l, 0, j)),
            pl.BlockSpec((None, 1, tn), lambda l, j: (l, 0, j)),
        ] + casts.in_specs(),
        out_specs=[pl.BlockSpec((None, N_COND, tn), lambda l, j: (l, 0, j))] + casts.out_specs(),
        out_shape=[jax.ShapeDtypeStruct((DEPTH, N_COND, n_out), F32)] + casts.out_shapes(),
        compiler_params=_cparams(2),
        name="adaln",
    )(cond, w_ada, b_ada.reshape(DEPTH, 1, n_out), *casts.args())


FF_CHUNKS = (768, 768, 768, 512)


def _ffn_kernel(*refs, base, n_x, final, n_pt):
    x_refs = refs[:n_x]
    mod_ref, g_ref, gf_ref, win_ref, wout_ref = refs[n_x:n_x + 5]
    o_refs = refs[n_x + 5:]
    i = pl.program_id(0)
    if n_x == 2:
        x = jnp.where(i < n_pt, x_refs[0][...], x_refs[1][...])
    else:
        x = x_refs[0][...]
    mod = mod_ref[...]
    u = _modulated_norm(x, g_ref[...], mod, base).astype(BF16)
    y = None
    lo = 0
    for width in FF_CHUNKS:
        a = jnp.dot(u, win_ref[:, lo:lo + width], preferred_element_type=F32)
        g = jnp.dot(u, win_ref[:, D_FF + lo:D_FF + lo + width], preferred_element_type=F32)
        hh = (a * _sigmoid(a) * g).astype(BF16)
        part = jnp.dot(hh, wout_ref[lo:lo + width, :], preferred_element_type=F32)
        y = part if y is None else y + part
        lo += width
    xn = x + 0.5 * mod[base + 2:base + 3] * y
    if not final:
        o_refs[0][...] = xn
    else:
        xn = _rms(xn, gf_ref[...])

        @pl.when(i < n_pt)
        def _():
            o_refs[0][...] = xn

        @pl.when(i >= n_pt)
        def _():
            o_refs[1][...] = xn


def _ffn_call(xs, mods, g_norm, g_final, w_in, w_out, st, *, layer, sub, final):
    tm = TM_WIDE
    n_pt = st.ctx_tiles(tm)
    n_tiles = st.n_all // tm
    ctx_spec = pl.BlockSpec((tm, D_MODEL), lambda i: (jnp.minimum(i, n_pt - 1), 0))
    lat_spec = pl.BlockSpec((tm, D_MODEL), lambda i: (jnp.maximum(i - n_pt, 0), 0))
    all_spec = pl.BlockSpec((tm, D_MODEL), lambda i: (i, 0))
    x_specs = [ctx_spec, lat_spec] if len(xs) == 2 else [all_spec]
    if final:
        out_specs = [ctx_spec, lat_spec]
        out_shape = [jax.ShapeDtypeStruct((st.n_ctx, D_MODEL), F32),
                     jax.ShapeDtypeStruct((st.n_all - st.n_ctx, D_MODEL), F32)]
    else:
        out_specs = [all_spec]
        out_shape = [jax.ShapeDtypeStruct((st.n_all, D_MODEL), F32)]
    return pl.pallas_call(
        functools.partial(_ffn_kernel, base=3 * sub, n_x=len(xs), final=final, n_pt=n_pt),
        grid=(n_tiles,),
        in_specs=x_specs + [
            st.mod_spec(tm),
            pl.BlockSpec((None, None, 1, D_MODEL), lambda i: (layer, sub, 0, 0)),
            pl.BlockSpec((1, D_MODEL), lambda i: (0, 0)),
            _layer_spec((D_MODEL, 2 * D_FF), None),
            _layer_spec((D_FF, D_MODEL), None),
        ],
        out_specs=out_specs,
        out_shape=out_shape,
        compiler_params=_cparams(1),
        name="ffn",
    )(*xs, mods, g_norm, g_final, w_in, w_out)


def _swap32(x):
    lane = lax.broadcasted_iota(jnp.int32, x.shape, 1)
    return jnp.where((lane & (DH_ATTN // 2)) == 0,
                     pltpu.roll(x, LANES - DH_ATTN // 2, 1), pltpu.roll(x, DH_ATTN // 2, 1))


def _proj_kernel(*refs, n_alias, n_pt, seqs_per_tile, seq):
    (x_ref, mod_ref, g_ref, wt_ref, bgt_ref, cos_ref, sin_ref) = refs[:7]
    (q_ref, k_ref, ck_ref, cv_ref, zf_ref, mq_ref, mk_ref, mo_ref, vat_ref, vt_ref, gt_ref) = refs[7 + n_alias:]
    i = pl.program_id(0)
    is_latent = i >= n_pt
    x = x_ref[...]
    u = _modulated_norm(x, g_ref[...], mod_ref[...], 3).astype(BF16)
    za = _bdot_nt(u, wt_ref[:3 * W_ATTN, :])
    cos_t = cos_ref[...]
    sin_t = sin_ref[...]
    plain_k = []
    for h in range(N_ATTN):
        zq = za[:, h * DV_ATTN:(h + 1) * DV_ATTN]
        zk = za[:, W_ATTN + h * DV_ATTN:W_ATTN + (h + 1) * DV_ATTN]
        plain_k.append(zk)
        q_ref[h] = jnp.where(is_latent, zq * cos_t + _swap32(zq) * sin_t, zq).astype(q_ref.dtype)
        k_ref[h] = jnp.where(is_latent, zk * cos_t + _swap32(zk) * sin_t, zk).astype(k_ref.dtype)

    @pl.when(i < n_pt)
    def _():
        for h in range(N_ATTN):
            zv = za[:, 2 * W_ATTN + h * DV_ATTN:2 * W_ATTN + (h + 1) * DV_ATTN]
            for s in range(seqs_per_tile):
                rs = slice(s * seq, (s + 1) * seq)
                ck_ref[s, h] = plain_k[h][rs]
                cv_ref[s, h] = zv[rs]

    zb = _bdot_nt(u, wt_ref[3 * W_ATTN:MV_LO, :])
    zf_ref[...] = zb[:, :W_FOUR].astype(zf_ref.dtype)
    off = W_FOUR
    mq_ref[...] = zb[:, off:off + W_MLSTM].astype(mq_ref.dtype)
    mk_ref[...] = (zb[:, off + W_MLSTM:off + 2 * W_MLSTM] * MLSTM_K_SCALE).astype(mk_ref.dtype)
    mo_ref[...] = _bdot_nt(u, wt_ref[MV_HI:P_MAIN, :])
    vat_ref[...] = _bdot_nt(wt_ref[2 * W_ATTN:3 * W_ATTN, :], u).astype(vat_ref.dtype)
    vt_ref[...] = _bdot_nt(wt_ref[MV_LO:MV_HI, :], u)
    gpt = _bdot_nt(wt_ref[P_MAIN:, :], u) + bgt_ref[...]
    sub = lax.broadcasted_iota(jnp.int32, gpt.shape, 0)
    gt_ref[...] = jnp.where((sub & N_MLSTM) != 0, _log_sigmoid(gpt), gpt)


def _proj_call(x, mods, g_norm, w_t, bgt, cos_t, sin_t, kv_prev, st, *, layer):
    tm = TM_PROJ
    n_pt = st.ctx_tiles(tm)
    if tm % st.seq:
        raise ValueError("a context row tile must hold whole sequences")
    seqs_per_tile, tps = tm // st.seq, st.dec_seq // tm
    n = st.n_all
    head_shape = jax.ShapeDtypeStruct((N_ATTN, n, DV_ATTN), BF16)
    head_spec = pl.BlockSpec((N_ATTN, tm, DV_ATTN), lambda i: (0, i, 0))
    kv_shape = jax.ShapeDtypeStruct((st.batch, DEPTH, N_ATTN, st.seq, DV_ATTN), F32)
    kv_spec = pl.BlockSpec((seqs_per_tile, None, N_ATTN, st.seq, DV_ATTN),
                           lambda i: (jnp.minimum(i, n_pt - 1), layer, 0, 0, 0))
    tok_shape = jax.ShapeDtypeStruct((n, W_ATTN), F32)
    tok_bf16 = jax.ShapeDtypeStruct((n, W_ATTN), BF16)
    tok_spec = pl.BlockSpec((tm, W_ATTN), lambda i: (i, 0))
    rope_spec = pl.BlockSpec((tm, LANES), lambda i: (jnp.maximum(i - n_pt, 0) % tps, 0))
    alias_in = list(kv_prev) if kv_prev is not None else []
    n_in = 7
    return pl.pallas_call(
        functools.partial(_proj_kernel, n_alias=len(alias_in), n_pt=n_pt, seqs_per_tile=seqs_per_tile,
                          seq=st.seq),
        grid=(n // tm,),
        in_specs=[
            pl.BlockSpec((tm, D_MODEL), lambda i: (i, 0)),
            st.mod_spec(tm),
            pl.BlockSpec((None, None, 1, D_MODEL), lambda i: (layer, 1, 0, 0)),
            _layer_spec((w_t.shape[1], D_MODEL), layer),
            _layer_spec((N_GATE, 1), layer),
            rope_spec, rope_spec,
        ] + [pl.BlockSpec(memory_space=pl.ANY)] * len(alias_in),
        out_specs=[head_spec, head_spec, kv_spec, kv_spec, tok_spec, tok_spec, tok_spec, tok_spec,
                   pl.BlockSpec((W_ATTN, tm), lambda i: (0, i)),
                   pl.BlockSpec((W_MLSTM, tm), lambda i: (0, i)),
                   pl.BlockSpec((N_GATE, tm), lambda i: (0, i))],
        out_shape=[head_shape, head_shape, kv_shape, kv_shape, tok_bf16, tok_bf16, tok_bf16, tok_shape,
                   jax.ShapeDtypeStruct((W_ATTN, n), BF16),
                   jax.ShapeDtypeStruct((W_MLSTM, n), F32),
                   jax.ShapeDtypeStruct((N_GATE, n), F32)],
        input_output_aliases={n_in + j: 2 + j for j in range(len(alias_in))},
        compiler_params=_cparams(1),
        name="mixer_proj",
    )(x, mods, g_norm, w_t, bgt, cos_t, sin_t, *alias_in)


def _attn_kernel(*refs, lam_init, cached, nb, tq, seq):
    q_ref, k_ref, vt_ref = refs[:3]
    pos = 3
    if cached:
        ck_ref, cv_ref = refs[pos:pos + 2]
        pos += 2
    lam_ref, g_ref = refs[pos:pos + 2]
    o_ref = refs[-1]
    lp = lam_ref[...]
    lam = (jnp.exp(jnp.sum(lp[0:1] * lp[1:2], axis=-1, keepdims=True))
           - jnp.exp(jnp.sum(lp[2:3] * lp[3:4], axis=-1, keepdims=True)) + lam_init)
    g_sub = g_ref[...]
    lane = lax.broadcasted_iota(jnp.int32, (tq, DV_ATTN), 1)
    heads = range(N_ATTN)
    results = []
    for bi in range(nb):
        ks = slice(bi * seq, (bi + 1) * seq)
        keys = [[k_ref[h, ks] for h in heads]]
        vals_t = [[vt_ref[h * DV_ATTN:(h + 1) * DV_ATTN, ks] for h in heads]]
        if cached:
            keys.insert(0, [ck_ref[h].astype(BF16) for h in heads])
            vals_t.insert(0, [cv_ref[h].T.astype(BF16) for h in heads])
        qh = [q_ref[h, bi * tq:(bi + 1) * tq].astype(F32) * (ATTN_SCALE * LOG2E) for h in heads]
        exps, dens = [], []
        for first_map in (True, False):
            qm = [jnp.where((lane < DH_ATTN) == first_map, q, 0.0).astype(BF16) for q in qh]
            s = [jnp.concatenate([_bdot_nt(kg[h], qm[h]) for h in heads], axis=1) for kg in keys]
            m = functools.reduce(jnp.maximum, [jnp.max(si, axis=0, keepdims=True) for si in s])
            e = [jnp.exp2(si - m) for si in s]
            exps.append(e)
            dens.append(functools.reduce(jnp.add, [jnp.sum(ei, axis=0, keepdims=True) for ei in e]))
        ratio = lam * dens[0] / dens[1]
        inv = 1.0 / dens[0]
        probs = [(e1 - e2 * ratio).astype(BF16) for e1, e2 in zip(*exps)]
        for h in heads:
            cols = slice(h * tq, (h + 1) * tq)
            o_t = None
            for pj, vg in zip(probs, vals_t):
                part = jnp.dot(vg[h], pj[:, cols], preferred_element_type=F32)
                o_t = part if o_t is None else o_t + part
            o_t = o_t * inv[:, cols]
            y_t = o_t * lax.rsqrt(jnp.mean(o_t * o_t, axis=0, keepdims=True) + EPS)
            results.append((bi, h, (y_t.T * g_sub * (1.0 - lam_init)).astype(o_ref.dtype)))
    for bi, h, y in results:
        o_ref[bi * tq:(bi + 1) * tq, h * DV_ATTN:(h + 1) * DV_ATTN] = y


def _attn_four_kernel(*refs, n_attn, attn_kw, four_kw, once_per_seq):
    a_ref, f_ref = refs[-2:]
    _attn_kernel(*refs[:n_attn], a_ref, **attn_kw)
    four = functools.partial(_four_kernel, *refs[n_attn:n_attn + 4], f_ref, **four_kw)
    if once_per_seq:
        pl.when(pl.program_id(1) == 0)(four)
    else:
        four()


def _attn_four_call(q, k, v_t, cache, lam_p, g_sub, zf, outs_prev, cast_w, st, *, layer, lam_init, latent):
    n = st.n_all
    if latent:
        batch, seq, nb, tq = st.dec_batch, st.dec_seq, 1, st.dec_seq // 2
        row0 = st.n_ctx
    else:
        batch, seq, nb, tq = st.batch, st.seq, 4, st.seq
        row0 = 0
    nq = seq // tq
    if nq > 1 and nb > 1:
        raise ValueError("query blocks of several sequences are not contiguous rows")
    q0, s0 = row0 // (nb * tq), row0 // (nb * seq)
    in_specs = [pl.BlockSpec((N_ATTN, nb * tq, DV_ATTN), lambda b, i: (0, q0 + b * nq + i, 0)),
                pl.BlockSpec((N_ATTN, nb * seq, DV_ATTN), lambda b, i: (0, s0 + b, 0)),
                pl.BlockSpec((W_ATTN, nb * seq), lambda b, i: (0, s0 + b))]
    args = [q, k, v_t]
    if cache is not None:
        past = cache[0].shape[3]
        c_spec = pl.BlockSpec((None, None, N_ATTN, past, DV_ATTN), lambda b, i: (b, layer, 0, 0, 0))
        in_specs += [c_spec, c_spec]
        args += list(cache)
    in_specs += [pl.BlockSpec((None, 4, DH_ATTN), lambda b, i: (layer, 0, 0)),
                 pl.BlockSpec((None, 1, DV_ATTN), lambda b, i: (layer, 0, 0))]
    args += [lam_p, g_sub]
    n_attn = len(args)
    w_d, ct, s_t = _dft_tables(seq)
    seq_rows = pl.BlockSpec((nb * seq, W_FOUR), lambda b, i: (s0 + b, 0))
    in_specs += [seq_rows,
                 pl.BlockSpec((DG_FOUR, 2 * DG_FOUR), lambda b, i: (0, 0)),
                 pl.BlockSpec((seq, seq), lambda b, i: (0, 0)),
                 pl.BlockSpec((seq, seq), lambda b, i: (0, 0))]
    args += [zf, w_d, ct, s_t]
    aliases = {}
    if outs_prev is not None:
        aliases = {len(args): 0, len(args) + 1: 1}
        in_specs += [pl.BlockSpec(memory_space=pl.ANY)] * 2
        args += list(outs_prev)
    casts = _CastJobs(cast_w, (batch // nb) * nq, lambda b, i: b * nq + i)
    body = functools.partial(
        _attn_four_kernel, n_attn=n_attn, once_per_seq=nq > 1,
        attn_kw=dict(lam_init=lam_init, cached=cache is not None, nb=nb, tq=tq, seq=seq),
        four_kw=dict(nb=nb, seq=seq))
    return pl.pallas_call(
        casts.wrap(body, len(args), 2),
        grid=(batch // nb, nq),
        in_specs=in_specs + casts.in_specs(),
        out_specs=[pl.BlockSpec((nb * tq, W_ATTN), lambda b, i: (q0 + b * nq + i, 0)), seq_rows]
        + casts.out_specs(),
        out_shape=[jax.ShapeDtypeStruct((n, W_ATTN), BF16), jax.ShapeDtypeStruct((n, W_FOUR), BF16)]
        + casts.out_shapes(),
        input_output_aliases=aliases,
        compiler_params=_cparams(2),
        name="attn_fourier",
    )(*args, *casts.args())


def _dft_tables(seq):
    def cs(n):
        j = np.arange(n)
        ang = 2.0 * np.pi * ((j[:, None] * j[None, :]) % n) / n
        return np.cos(ang) / math.sqrt(n), np.sin(ang) / math.sqrt(n)

    cd, sd = cs(DG_FOUR)
    ct, st = cs(seq)
    w_d = jnp.asarray(np.concatenate([cd, sd], axis=1), F32)
    return w_d.astype(BF16), jnp.asarray(ct, F32).astype(BF16), jnp.asarray(-st, F32).astype(BF16)


def _four_kernel(*refs, nb, seq):
    z_ref, wd_ref, ct_ref, st_ref = refs[:4]
    o_ref = refs[-1]
    y_cos, y_sin = [], []
    for gidx in range(N_FOUR):
        y = jnp.dot(z_ref[:, gidx * DG_FOUR:(gidx + 1) * DG_FOUR], wd_ref[...], preferred_element_type=F32)
        y_cos.append(y[:, :DG_FOUR].astype(BF16))
        y_sin.append(y[:, DG_FOUR:].astype(BF16))
    y_cos = jnp.concatenate(y_cos, axis=1)
    y_sin = jnp.concatenate(y_sin, axis=1)
    for s in range(nb):
        rows = slice(s * seq, (s + 1) * seq)
        o_ref[rows, :] = (jnp.dot(ct_ref[...], y_cos[rows], preferred_element_type=F32)
                          + jnp.dot(st_ref[...], y_sin[rows], preferred_element_type=F32)).astype(o_ref.dtype)


def _mlstm_kernel(*refs, seq, nb, seeded, emit_state, n_alias):
    mq_ref, mk_ref, vt_ref, mo_ref, gt_ref, g_ref = refs[:6]
    pos = 6
    if seeded:
        c0_ref, n0_ref, m0_ref = refs[pos:pos + 3]
        pos += 3
    pos += n_alias
    o_ref = refs[pos]
    if emit_state:
        c1_ref, n1_ref, m1_ref = refs[pos + 1:pos + 4]

    L = min(MLSTM_CHUNK, seq)
    nc = seq // L
    s_idx = lax.broadcasted_iota(jnp.int32, (L, L), 0)
    t_idx = lax.broadcasted_iota(jnp.int32, (L, L), 1)
    before = (s_idx <= t_idx, s_idx >= t_idx)
    tri = (jnp.where(before[1], 1.0, 0.0).astype(BF16),
           jnp.where(before[0], 1.0, 0.0).astype(BF16))
    chains = [(bi, d, h) for bi in range(nb) for d in range(2) for h in range(N_MLSTM)]

    state = {}
    for (bi, d, h) in chains:
        if seeded:
            state[bi, d, h] = (c0_ref[bi, d, h], n0_ref[bi, d, h:h + 1, :], m0_ref[bi, d:d + 1, h:h + 1])
        else:
            state[bi, d, h] = (None, None, jnp.zeros((1, 1), F32))

    terms = {}
    for c in range(nc):
        for bi in range(nb):
            for d in range(2):
                r0 = bi * seq + (c if d == 0 else nc - 1 - c) * L
                g_t = gt_ref[:, r0:r0 + L]
                hi = g_t.astype(BF16)
                rem = g_t - hi.astype(F32)
                mid = rem.astype(BF16)
                lo = (rem - mid.astype(F32)).astype(BF16)
                pieces = _bdot_nt(jnp.concatenate([hi, mid, lo], axis=0), tri[d])
                cum_t = pieces[0:N_GATE] + pieces[N_GATE:2 * N_GATE] + pieces[2 * N_GATE:]
                c_t = (g_t - pltpu.roll(cum_t, N_GATE - N_MLSTM, 0)) * LOG2E
                col = jnp.concatenate([c_t, jnp.zeros((LANES - N_GATE, L), F32)], axis=0).T
                terms[c, bi, d] = (r0, col, g_t, cum_t)

    h_t = {}
    for c in range(nc):
        new_state = {}
        need_update = emit_state or c < nc - 1
        group = [(d, h) for d in range(2) for h in range(N_MLSTM)]
        for bi in range(nb):
            has_state = state[bi, 0, 0][0] is not None
            qs, ks, vts, cbs, s0s, i_rows, b_rows, m_prevs = [], [], [], [], [], [], [], []
            for d, h in group:
                r0, col, g_t, cum_t = terms[c, bi, d]
                ci = 2 * d * N_MLSTM + h
                cf = ci + N_MLSTM
                hs = slice(h * DH_MLSTM, (h + 1) * DH_MLSTM)
                qs.append(mq_ref[r0:r0 + L, hs])
                ks.append(mk_ref[r0:r0 + L, hs])
                vts.append(vt_ref[hs, r0:r0 + L])
                cbs.append(jnp.where(before[d], col[:, ci:ci + 1], -jnp.inf))
                s0s.append(_bdot_nt(ks[-1], qs[-1]))
                i_rows.append(g_t[ci:ci + 1, :])
                b_rows.append(cum_t[cf:cf + 1, :])
                m_prevs.append(jnp.broadcast_to(state[bi, d, h][2], (1, L)))
            cb = jnp.concatenate(cbs, axis=1)
            i_row, b_row = jnp.concatenate(i_rows, axis=1), jnp.concatenate(b_rows, axis=1)
            m_prev = jnp.concatenate(m_prevs, axis=1)
            m2_prev = m_prev * LOG2E
            m2_row = jnp.maximum(jnp.max(cb, axis=0, keepdims=True), m2_prev)
            s_t = jnp.concatenate(s0s, axis=1) * jnp.exp2(cb - m2_row)
            den = jnp.sum(s_t, axis=0, keepdims=True)
            s_bf = s_t.astype(BF16)
            m_t = b_row + m2_row * LN2
            floor = jnp.exp(-m_t)
            if has_state:
                sp = jnp.exp2(m2_prev - m2_row)
            if need_update:
                tot, new = [], []
                for j, (d, h) in enumerate(group):
                    last = j * L + (L - 1 if d == 0 else 0)
                    tot.append(jnp.broadcast_to(b_row[:, last:last + 1], (1, L)))
                    new.append(jnp.broadcast_to(m_t[:, last:last + 1], (1, L)))
                b_tot, m_new = jnp.concatenate(tot, axis=1), jnp.concatenate(new, axis=1)
                wl = jnp.exp(b_tot + (i_row - b_row) - m_new)
                if has_state:
                    decay = jnp.exp(b_tot + m_prev - m_new)
            for j, (d, h) in enumerate(group):
                cols = slice(j * L, (j + 1) * L)
                c_prev, n_prev, _ = state[bi, d, h]
                num_t = jnp.dot(vts[j].astype(BF16), s_bf[:, cols], preferred_element_type=F32)
                den_j = den[:, cols]
                if has_state:
                    cn = jnp.concatenate([c_prev, jnp.broadcast_to(n_prev, (SUBLANES, DH_MLSTM))], axis=0)
                    cq = _bdot_nt(cn, qs[j])
                    num_t = num_t + sp[:, cols] * cq[:DH_MLSTM]
                    den_j = den_j + sp[:, cols] * cq[DH_MLSTM:DH_MLSTM + 1]
                h_t[bi, d, h, c] = num_t / jnp.maximum(jnp.abs(den_j), floor[:, cols])
                if need_update:
                    wl_j = wl[:, cols]
                    vw = jnp.concatenate([vts[j] * wl_j, jnp.broadcast_to(wl_j, (SUBLANES, L))], axis=0)
                    upd = jnp.dot(vw.astype(BF16), ks[j], preferred_element_type=F32)
                    c_new, n_new = upd[:DH_MLSTM], upd[DH_MLSTM:DH_MLSTM + 1]
                    if has_state:
                        decay_j = decay[:, j * L:j * L + 1]
                        c_new = decay_j * c_prev + c_new
                        n_new = decay_j * n_prev + n_new
                    new_state[bi, d, h] = (c_new, n_new, m_new[:, j * L:j * L + 1])
        state = new_state

    g_m = g_ref[...]
    for bi in range(nb):
        for h in range(N_MLSTM):
            hs = slice(h * DH_MLSTM, (h + 1) * DH_MLSTM)
            fwd = [h_t[bi, 0, h, c] for c in range(nc)]
            bwd = [h_t[bi, 1, h, nc - 1 - c] for c in range(nc)]
            hsum = (fwd[0] if nc == 1 else jnp.concatenate(fwd, axis=1)) \
                + (bwd[0] if nc == 1 else jnp.concatenate(bwd, axis=1))
            y = hsum * lax.rsqrt(jnp.mean(hsum * hsum, axis=0, keepdims=True) + EPS)
            rows = slice(bi * seq, (bi + 1) * seq)
            o_ref[rows, hs] = (y.T * g_m * _sigmoid(mo_ref[rows, hs])).astype(o_ref.dtype)

    if emit_state:
        for (bi, d, h) in chains:
            c_fin, n_fin, m_fin = state[bi, d, h]
            c1_ref[bi, d, h] = c_fin
            n1_ref[bi, d, h:h + 1, :] = n_fin
            m1_ref[bi, d:d + 1, h:h + 1] = m_fin


def _mlstm_call(mq, mk, vt, mo, gates_t, g_m, state, alias_prev, cast_w, st, *, layer, latent):
    if latent:
        batch, seq, nb, row0 = st.dec_batch, st.dec_seq, 1, st.n_ctx
    else:
        batch, seq, nb, row0 = st.batch, st.seq, 4, 0
    emit_state = not latent
    rows = nb * seq
    b0 = row0 // rows
    tok_spec = pl.BlockSpec((rows, W_MLSTM), lambda b: (b0 + b, 0))
    in_specs = [tok_spec, tok_spec,
                pl.BlockSpec((W_MLSTM, rows), lambda b: (0, b0 + b)),
                tok_spec,
                pl.BlockSpec((N_GATE, rows), lambda b: (0, b0 + b)),
                pl.BlockSpec((None, 1, DH_MLSTM), lambda b: (layer, 0, 0))]
    args = [mq, mk, vt, mo, gates_t, g_m]
    state_specs = [
        pl.BlockSpec((nb, None, 2, N_MLSTM, DH_MLSTM, DH_MLSTM), lambda b: (b, layer, 0, 0, 0, 0)),
        pl.BlockSpec((nb, None, 2, N_MLSTM, DH_MLSTM), lambda b: (b, layer, 0, 0, 0)),
        pl.BlockSpec((nb, None, 2, N_MLSTM), lambda b: (b, layer, 0, 0)),
    ]
    if state is not None:
        in_specs += state_specs
        args += list(state)
    alias_in = list(alias_prev) if alias_prev is not None else []
    n_in = len(args)
    in_specs += [pl.BlockSpec(memory_space=pl.ANY)] * len(alias_in)
    args += alias_in
    out_specs = [tok_spec]
    out_shape = [jax.ShapeDtypeStruct((st.n_all, W_MLSTM), BF16)]
    if emit_state:
        out_specs += state_specs
        out_shape += [
            jax.ShapeDtypeStruct((batch, DEPTH, 2, N_MLSTM, DH_MLSTM, DH_MLSTM), F32),
            jax.ShapeDtypeStruct((batch, DEPTH, 2, N_MLSTM, DH_MLSTM), F32),
            jax.ShapeDtypeStruct((batch, DEPTH, 2, N_MLSTM), F32),
        ]
    first_out = 1 if emit_state else 0
    casts = _CastJobs(cast_w, batch // nb, lambda b: b)
    body = functools.partial(_mlstm_kernel, seq=seq, nb=nb, seeded=state is not None, emit_state=emit_state,
                             n_alias=len(alias_in))
    return pl.pallas_call(
        casts.wrap(body, len(args), len(out_specs)),
        grid=(batch // nb,),
        in_specs=in_specs + casts.in_specs(),
        out_specs=out_specs + casts.out_specs(),
        out_shape=out_shape + casts.out_shapes(),
        input_output_aliases={n_in + j: first_out + j for j in range(len(alias_in))},
        compiler_params=_cparams(1),
        name="mlstm",
    )(*args, *casts.args())


def _ctx_mixer_kernel(*refs, n_attn, attn_kw, four_kw, mlstm_kw):
    a_ref, f_ref = refs[-6:-4]
    _attn_kernel(*refs[:n_attn], a_ref, **attn_kw)
    _four_kernel(*refs[n_attn:n_attn + 4], f_ref, **four_kw)
    _mlstm_kernel(*refs[n_attn + 4:n_attn + 10], *refs[-4:], **mlstm_kw)


def _ctx_mixer_call(q, k, v_t, lam_p, g_sub, zf, mq, mk, vt, mo, gates_t, g_m, states_prev, cast_w, st, *,
                    layer, lam_init):
    n, seq, nb = st.n_all, st.seq, 4
    rows = nb * seq
    heads = pl.BlockSpec((N_ATTN, rows, DV_ATTN), lambda b: (0, b, 0))
    tok = pl.BlockSpec((rows, W_ATTN), lambda b: (b, 0))
    w_d, ct, s_t = _dft_tables(seq)
    in_specs = [heads, heads,
                pl.BlockSpec((W_ATTN, rows), lambda b: (0, b)),
                pl.BlockSpec((None, 4, DH_ATTN), lambda b: (layer, 0, 0)),
                pl.BlockSpec((None, 1, DV_ATTN), lambda b: (layer, 0, 0)),
                tok,
                pl.BlockSpec((DG_FOUR, 2 * DG_FOUR), lambda b: (0, 0)),
                pl.BlockSpec((seq, seq), lambda b: (0, 0)),
                pl.BlockSpec((seq, seq), lambda b: (0, 0)),
                tok, tok,
                pl.BlockSpec((W_MLSTM, rows), lambda b: (0, b)),
                tok,
                pl.BlockSpec((N_GATE, rows), lambda b: (0, b)),
                pl.BlockSpec((None, 1, DH_MLSTM), lambda b: (layer, 0, 0))]
    args = [q, k, v_t, lam_p, g_sub, zf, w_d, ct, s_t, mq, mk, vt, mo, gates_t, g_m]
    n_attn = 5
    state_specs = [
        pl.BlockSpec((nb, None, 2, N_MLSTM, DH_MLSTM, DH_MLSTM), lambda b: (b, layer, 0, 0, 0, 0)),
        pl.BlockSpec((nb, None, 2, N_MLSTM, DH_MLSTM), lambda b: (b, layer, 0, 0, 0)),
        pl.BlockSpec((nb, None, 2, N_MLSTM), lambda b: (b, layer, 0, 0)),
    ]
    aliases = {}
    if states_prev is not None:
        aliases = {len(args) + j: 3 + j for j in range(3)}
        in_specs += [pl.BlockSpec(memory_space=pl.ANY)] * 3
        args += list(states_prev)
    casts = _CastJobs(cast_w, st.batch // nb, lambda b: b)
    body = functools.partial(
        _ctx_mixer_kernel, n_attn=n_attn,
        attn_kw=dict(lam_init=lam_init, cached=False, nb=nb, tq=seq, seq=seq),
        four_kw=dict(nb=nb, seq=seq),
        mlstm_kw=dict(seq=seq, nb=nb, seeded=False, emit_state=True, n_alias=0))
    return pl.pallas_call(
        casts.wrap(body, len(args), 6),
        grid=(st.batch // nb,),
        in_specs=in_specs + casts.in_specs(),
        out_specs=[tok, tok, tok] + state_specs + casts.out_specs(),
        out_shape=[jax.ShapeDtypeStruct((n, W_ATTN), BF16), jax.ShapeDtypeStruct((n, W_FOUR), BF16),
                   jax.ShapeDtypeStruct((n, W_MLSTM), BF16),
                   jax.ShapeDtypeStruct((st.batch, DEPTH, 2, N_MLSTM, DH_MLSTM, DH_MLSTM), F32),
                   jax.ShapeDtypeStruct((st.batch, DEPTH, 2, N_MLSTM, DH_MLSTM), F32),
                   jax.ShapeDtypeStruct((st.batch, DEPTH, 2, N_MLSTM), F32)] + casts.out_shapes(),
        input_output_aliases=aliases,
        compiler_params=_cparams(1),
        name="ctx_mixer",
    )(*args, *casts.args())


def _merge_kernel(x_ref, a_ref, f_ref, m_ref, mod_ref, g_ref, wg_ref, wa_ref, wf_ref, wm_ref, wo_ref, o_ref):
    x = x_ref[...]
    mod = mod_ref[...]
    u = _modulated_norm(x, g_ref[...], mod, 3).astype(BF16)
    merged = None
    for j, (br_ref, w_ref) in enumerate(((a_ref, wa_ref), (f_ref, wf_ref), (m_ref, wm_ref))):
        gate = _sigmoid(jnp.dot(u, wg_ref[:, j * D_MODEL:(j + 1) * D_MODEL], preferred_element_type=F32))
        term = gate * _bdot(br_ref[...], w_ref[...])
        merged = term if merged is None else merged + term
    out = _bdot(merged, wo_ref[...])
    o_ref[...] = x + mod[5:6] * out


def _merge_call(x, a, f, m, mods, g_norm, wg, wa, wf, wm, wo, st, *, layer):
    tm = TM_WIDE
    n = st.n_all
    br_spec = pl.BlockSpec((tm, W_ATTN), lambda i: (i, 0))
    return pl.pallas_call(
        _merge_kernel,
        grid=(n // tm,),
        in_specs=[
            pl.BlockSpec((tm, D_MODEL), lambda i: (i, 0)),
            br_spec, br_spec, br_spec,
            st.mod_spec(tm),
            pl.BlockSpec((None, None, 1, D_MODEL), lambda i: (layer, 1, 0, 0)),
            _layer_spec((D_MODEL, 3 * D_MODEL), None),
            _layer_spec((W_ATTN, D_MODEL), None), _layer_spec((W_FOUR, D_MODEL), None),
            _layer_spec((W_MLSTM, D_MODEL), None),
            _layer_spec((D_MODEL, D_MODEL), None),
        ],
        out_specs=pl.BlockSpec((tm, D_MODEL), lambda i: (i, 0)),
        out_shape=jax.ShapeDtypeStruct((n, D_MODEL), F32),
        compiler_params=_cparams(1),
        name="merge",
    )(x, a, f, m, mods, g_norm, wg, wa, wf, wm, wo)


def _rope_tables(n_tok):
    tok = np.arange(n_tok)
    inv = ROPE_BASE ** (-np.arange(ROPE_AXIS_PAIRS, dtype=np.float32) / ROPE_AXIS_PAIRS)
    ang = np.concatenate([(tok // GRID_W).astype(np.float32)[:, None] * inv,
                          (tok % GRID_W).astype(np.float32)[:, None] * inv], axis=-1).astype(np.float32)
    c, s = np.cos(ang), np.sin(ang)
    cos_t = np.concatenate([c, c, c, c], axis=-1)
    sin_t = np.concatenate([-s, s, -s, s], axis=-1)
    return jnp.asarray(cos_t, F32), jnp.asarray(sin_t, F32)


def kernel(x_prompt, x_sample, cache_k, cache_v, state_C, state_n, state_m, c, c_ctx, w_ada, b_ada, g_norm,
           w_ffn1_in, w_ffn1_out, w_ffn2_in, w_ffn2_out, w_in, b_mgate, attn_lambda, g_attn_sub, g_mlstm,
           w_branch_gate, w_br_attn, w_br_four, w_br_mlstm, w_out, g_final):
    batch, seq, _ = x_prompt.shape
    dec_batch, dec_seq, _ = x_sample.shape
    st = _Stream(batch, seq, dec_batch, dec_seq)
    cond = jnp.zeros((N_COND, D_MODEL), F32).at[0].set(c_ctx).at[1:1 + dec_batch].set(c)
    mods_flat, *ffn1_w = _ada_call(cond, w_ada, b_ada, [(w_ffn1_in, 0), (w_ffn1_out, 0)])
    mods_all = mods_flat.reshape(DEPTH, N_COND, N_MOD, D_MODEL)

    b_gate_t = b_mgate[:, :, None]
    g_norm4 = g_norm[:, :, None, :]
    g_fin = g_final[None, :]
    w_t = jnp.swapaxes(w_in, 1, 2).astype(BF16)
    g_sub3 = g_attn_sub[:, None, :]
    g_m3 = g_mlstm[:, None, :]
    cos_t, sin_t = _rope_tables(dec_seq)
    lat_state = (state_C, state_n, state_m)

    xs = (x_prompt.reshape(batch * seq, D_MODEL), x_sample.reshape(dec_batch * dec_seq, D_MODEL))
    kv, states = None, None
    for l in range(DEPTH):
        mods = mods_all[l]
        lam_init = 0.8 - 0.6 * math.exp(-0.3 * l)
        nxt = l + 1 < DEPTH
        (x,) = _ffn_call(xs, mods, g_norm4, g_fin, *ffn1_w, st, layer=l, sub=0, final=False)
        q, k, ck, cv, zf, mq, mk, mo, vat, vt, gates_t = _proj_call(
            x, mods, g_norm4, w_t, b_gate_t, cos_t, sin_t, kv, st, layer=l)
        kv = (ck, cv)
        a, f, m, c_fin, n_fin, m_fin, ffn2_in, *nxt_out = _ctx_mixer_call(
            q, k, vat, attn_lambda, g_sub3, zf, mq, mk, vt, mo, gates_t, g_m3, states,
            [(w_ffn2_in, l)] + ([(w_ffn1_out, l + 1)] if nxt else []), st, layer=l, lam_init=lam_init)
        states = [c_fin, n_fin, m_fin]
        a, f, ffn2_out, w_bg, *nxt_in = _attn_four_call(
            q, k, vat, (cache_k, cache_v), attn_lambda, g_sub3, zf, (a, f),
            [(w_ffn2_out, l), (w_branch_gate, l)] + ([(w_ffn1_in, l + 1)] if nxt else []), st,
            layer=l, lam_init=lam_init, latent=True)
        m, w_ba, w_bf, w_bm, w_o = _mlstm_call(
            mq, mk, vt, mo, gates_t, g_m3, lat_state, (m,),
            [(w_br_attn, l), (w_br_four, l), (w_br_mlstm, l), (w_out, l)], st, layer=l, latent=True)
        x = _merge_call(x, a, f, m, mods, g_norm4, w_bg, w_ba, w_bf, w_bm, w_o, st, layer=l)
        xs = tuple(_ffn_call((x,), mods, g_norm4, g_fin, ffn2_in, ffn2_out, st, layer=l, sub=2,
                             final=(l == DEPTH - 1)))
        if nxt:
            ffn1_w = (nxt_in[0], nxt_out[0])
    y_prompt, y_sample = xs
    return (y_prompt.reshape(batch, seq, D_MODEL), y_sample.reshape(dec_batch, dec_seq, D_MODEL),
            *kv, *states)
```

```python
import functools
import math

import numpy as np
import jax
import jax.numpy as jnp
from jax import lax
from jax.experimental import pallas as pl
from jax.experimental.pallas import tpu as pltpu

D_MODEL = 1024
DEPTH = 2
GRID_W = 64
N_ATTN = 4
DH_ATTN = 64
DV_ATTN = 2 * DH_ATTN
W_ATTN = N_ATTN * DV_ATTN
N_FOUR = 4
DG_FOUR = 128
W_FOUR = N_FOUR * DG_FOUR
N_MLSTM = 4
DH_MLSTM = 128
W_MLSTM = N_MLSTM * DH_MLSTM
N_GATE = 4 * N_MLSTM
LANES = 128
SUBLANES = 16
MV_LO = 3 * W_ATTN + W_FOUR + 2 * W_MLSTM
MV_HI = MV_LO + W_MLSTM
P_MAIN = MV_HI + W_MLSTM
D_FF = 2816
N_MOD = 9
N_COND = 8
ROPE_BASE = 10000.0
ROPE_AXIS_PAIRS = DH_ATTN // 4
ATTN_SCALE = DH_ATTN ** -0.5
LOG2E = 1.4426950408889634
LN2 = 0.6931471805599453
MLSTM_K_SCALE = DH_MLSTM ** -0.5
EPS = 1e-6
MLSTM_CHUNK = 256
VMEM_LIMIT = 56 * 1024 * 1024
TM_WIDE = 1024
TM_PROJ = 1024

F32 = jnp.float32
BF16 = jnp.bfloat16


def _cparams(n_grid):
    return pltpu.CompilerParams(dimension_semantics=("arbitrary",) * n_grid,
                                vmem_limit_bytes=VMEM_LIMIT)


def _bdot(a, b):
    return jnp.dot(a.astype(BF16), b.astype(BF16), preferred_element_type=F32)


def _bdot_nt(a, b):
    return lax.dot_general(a.astype(BF16), b.astype(BF16), (((1,), (1,)), ((), ())),
                           preferred_element_type=F32)


def _sigmoid(x):
    return 1.0 / (1.0 + jnp.exp(-x))


def _log_sigmoid(x):
    return jnp.minimum(x, 0.0) - jnp.log1p(jnp.exp(-jnp.abs(x)))


def _rms(x, g):
    return x * lax.rsqrt(jnp.mean(x * x, axis=-1, keepdims=True) + EPS) * g


def _modulated_norm(x, g, mod, base):
    gain = g * (1.0 + mod[base + 1:base + 2])
    return x * lax.rsqrt(jnp.mean(x * x, axis=-1, keepdims=True) + EPS) * gain + mod[base:base + 1]


def _layer_spec(shape, layer):
    if layer is None:
        return pl.BlockSpec(shape, lambda *_: (0,) * len(shape), pipeline_mode=pl.Buffered(1))
    return pl.BlockSpec((None,) + shape, lambda *_: (layer,) + (0,) * len(shape),
                        pipeline_mode=pl.Buffered(1))


class _CastJobs:
    def __init__(self, jobs, steps, step_index):
        self.jobs, self.steps, self.step_index = list(jobs), steps, step_index

    def __len__(self):
        return len(self.jobs)

    def _rows(self, w):
        rows = w.shape[1] // self.steps
        if rows * self.steps != w.shape[1] or rows % SUBLANES:
            raise ValueError("weight rows do not split into bf16 row tiles over the grid")
        return rows

    def in_specs(self):
        return [pl.BlockSpec((None, self._rows(w), w.shape[2]),
                             lambda *g, layer=layer: (layer, self.step_index(*g), 0))
                for w, layer in self.jobs]

    def out_specs(self):
        return [pl.BlockSpec((self._rows(w), w.shape[2]), lambda *g: (self.step_index(*g), 0))
                for w, _ in self.jobs]

    def out_shapes(self):
        return [jax.ShapeDtypeStruct(w.shape[1:], BF16) for w, _ in self.jobs]

    def args(self):
        return [w for w, _ in self.jobs]

    def wrap(self, body, n_in, n_out):
        n = len(self.jobs)
        if n == 0:
            return body

        def kernel(*refs):
            ins, cast_in = refs[:n_in], refs[n_in:n_in + n]
            outs = refs[n_in + n:n_in + n + n_out]
            cast_out = refs[n_in + n + n_out:n_in + 2 * n + n_out]
            for src, dst in zip(cast_in, cast_out):
                dst[...] = src[...].astype(dst.dtype)
            body(*ins, *outs, *refs[n_in + 2 * n + n_out:])

        return kernel


class _Stream:
    def __init__(self, batch, seq, dec_batch, dec_seq):
        self.batch, self.seq, self.dec_batch, self.dec_seq = batch, seq, dec_batch, dec_seq
        self.n_ctx = batch * seq
        self.n_all = self.n_ctx + dec_batch * dec_seq

    def ctx_tiles(self, tm):
        if self.n_ctx % tm or self.dec_seq % tm:
            raise ValueError("row tile must divide the context rows and one latent sequence")
        return self.n_ctx // tm

    def mod_spec(self, tm):
        n_pt, tps = self.ctx_tiles(tm), self.dec_seq // tm
        return pl.BlockSpec((None, N_MOD, D_MODEL),
                            lambda i: (jnp.where(i < n_pt, 0, 1 + (i - n_pt) // tps), 0, 0))


def _ada_kernel(c_ref, w_ref, b_ref, o_ref):
    c = c_ref[...]
    s = c * _sigmoid(c)
    o_ref[...] = _bdot(s, w_ref[...]) + b_ref[...]


def _ada_call(cond, w_ada, b_ada, cast_w):
    tn = 2304
    n_out = N_MOD * D_MODEL
    nj = n_out // tn
    casts = _CastJobs(cast_w, DEPTH * nj, lambda l, j: l * nj + j)
    return pl.pallas_call(
        casts.wrap(_ada_kernel, 3, 1),
        grid=(DEPTH, nj),
        in_specs=[
            pl.BlockSpec((N_COND, D_MODEL), lambda l, j: (0, 0)),
            pl.BlockSpec((None, D_MODEL, tn), lambda l, j: (l, 0, j)),
            pl.BlockSpec((None, 1, tn), lambda l, j: (l, 0, j)),
        ] + casts.in_specs(),
        out_specs=[pl.BlockSpec((None, N_COND, tn), lambda l, j: (l, 0, j))] + casts.out_specs(),
        out_shape=[jax.ShapeDtypeStruct((DEPTH, N_COND, n_out), F32)] + casts.out_shapes(),
        compiler_params=_cparams(2),
        name="adaln",
    )(cond, w_ada, b_ada.reshape(DEPTH, 1, n_out), *casts.args())


FF_CHUNKS = (768, 768, 768, 512)


def _ffn_kernel(*refs, base, n_x, final, n_pt):
    x_refs = refs[:n_x]
    mod_ref, g_ref, gf_ref, win_ref, wout_ref = refs[n_x:n_x + 5]
    o_refs = refs[n_x + 5:]
    i = pl.program_id(0)
    if n_x == 2:
        x = jnp.where(i < n_pt, x_refs[0][...], x_refs[1][...])
    else:
        x = x_refs[0][...]
    mod = mod_ref[...]
    u = _modulated_norm(x, g_ref[...], mod, base).astype(BF16)
    y = None
    lo = 0
    for width in FF_CHUNKS:
        a = jnp.dot(u, win_ref[:, lo:lo + width], preferred_element_type=F32)
        g = jnp.dot(u, win_ref[:, D_FF + lo:D_FF + lo + width], preferred_element_type=F32)
        hh = (a * _sigmoid(a) * g).astype(BF16)
        part = jnp.dot(hh, wout_ref[lo:lo + width, :], preferred_element_type=F32)
        y = part if y is None else y + part
        lo += width
    xn = x + 0.5 * mod[base + 2:base + 3] * y
    if not final:
        o_refs[0][...] = xn
    else:
        xn = _rms(xn, gf_ref[...])

        @pl.when(i < n_pt)
        def _():
            o_refs[0][...] = xn

        @pl.when(i >= n_pt)
        def _():
            o_refs[1][...] = xn


def _ffn_call(xs, mods, g_norm, g_final, w_in, w_out, st, *, layer, sub, final):
    tm = TM_WIDE
    n_pt = st.ctx_tiles(tm)
    n_tiles = st.n_all // tm
    ctx_spec = pl.BlockSpec((tm, D_MODEL), lambda i: (jnp.minimum(i, n_pt - 1), 0))
    lat_spec = pl.BlockSpec((tm, D_MODEL), lambda i: (jnp.maximum(i - n_pt, 0), 0))
    all_spec = pl.BlockSpec((tm, D_MODEL), lambda i: (i, 0))
    x_specs = [ctx_spec, lat_spec] if len(xs) == 2 else [all_spec]
    if final:
        out_specs = [ctx_spec, lat_spec]
        out_shape = [jax.ShapeDtypeStruct((st.n_ctx, D_MODEL), F32),
                     jax.ShapeDtypeStruct((st.n_all - st.n_ctx, D_MODEL), F32)]
    else:
        out_specs = [all_spec]
        out_shape = [jax.ShapeDtypeStruct((st.n_all, D_MODEL), F32)]
    return pl.pallas_call(
        functools.partial(_ffn_kernel, base=3 * sub, n_x=len(xs), final=final, n_pt=n_pt),
        grid=(n_tiles,),
        in_specs=x_specs + [
            st.mod_spec(tm),
            pl.BlockSpec((None, None, 1, D_MODEL), lambda i: (layer, sub, 0, 0)),
            pl.BlockSpec((1, D_MODEL), lambda i: (0, 0)),
            _layer_spec((D_MODEL, 2 * D_FF), None),
            _layer_spec((D_FF, D_MODEL), None),
        ],
        out_specs=out_specs,
        out_shape=out_shape,
        compiler_params=_cparams(1),
        name="ffn",
    )(*xs, mods, g_norm, g_final, w_in, w_out)


def _swap32(x):
    lane = lax.broadcasted_iota(jnp.int32, x.shape, 1)
    return jnp.where((lane & (DH_ATTN // 2)) == 0,
                     pltpu.roll(x, LANES - DH_ATTN // 2, 1), pltpu.roll(x, DH_ATTN // 2, 1))


def _proj_kernel(*refs, n_alias, n_pt, seqs_per_tile, seq):
    (x_ref, mod_ref, g_ref, wt_ref, bgt_ref, cos_ref, sin_ref) = refs[:7]
    (q_ref, k_ref, ck_ref, cv_ref, zf_ref, mq_ref, mk_ref, mo_ref, vat_ref, vt_ref, gt_ref) = refs[7 + n_alias:]
    i = pl.program_id(0)
    is_latent = i >= n_pt
    x = x_ref[...]
    u = _modulated_norm(x, g_ref[...], mod_ref[...], 3).astype(BF16)
    zb = _bdot_nt(u, wt_ref[3 * W_ATTN:MV_LO, :])
    zf_ref[...] = zb[:, :W_FOUR].astype(zf_ref.dtype)
    off = W_FOUR
    mq_ref[...] = zb[:, off:off + W_MLSTM].astype(mq_ref.dtype)
    mk_ref[...] = (zb[:, off + W_MLSTM:off + 2 * W_MLSTM] * MLSTM_K_SCALE).astype(mk_ref.dtype)
    mo_ref[...] = _bdot_nt(u, wt_ref[MV_HI:P_MAIN, :])
    vat_ref[...] = _bdot_nt(wt_ref[2 * W_ATTN:3 * W_ATTN, :], u).astype(vat_ref.dtype)
    vt_ref[...] = _bdot_nt(wt_ref[MV_LO:MV_HI, :], u)
    gpt = _bdot_nt(wt_ref[P_MAIN:, :], u) + bgt_ref[...]
    sub = lax.broadcasted_iota(jnp.int32, gpt.shape, 0)
    gt_ref[...] = jnp.where((sub & N_MLSTM) != 0, _log_sigmoid(gpt), gpt)

    za = _bdot_nt(u, wt_ref[:3 * W_ATTN, :])
    cos_t = cos_ref[...]
    sin_t = sin_ref[...]
    plain_k = []
    for h in range(N_ATTN):
        zq = za[:, h * DV_ATTN:(h + 1) * DV_ATTN]
        zk = za[:, W_ATTN + h * DV_ATTN:W_ATTN + (h + 1) * DV_ATTN]
        plain_k.append(zk)
        q_ref[h] = jnp.where(is_latent, zq * cos_t + _swap32(zq) * sin_t, zq).astype(q_ref.dtype)
        k_ref[h] = jnp.where(is_latent, zk * cos_t + _swap32(zk) * sin_t, zk).astype(k_ref.dtype)

    @pl.when(i < n_pt)
    def _():
        for h in range(N_ATTN):
            zv = za[:, 2 * W_ATTN + h * DV_ATTN:2 * W_ATTN + (h + 1) * DV_ATTN]
            for s in range(seqs_per_tile):
                rs = slice(s * seq, (s + 1) * seq)
                ck_ref[s, h] = plain_k[h][rs]
                cv_ref[s, h] = zv[rs]


def _proj_call(x, mods, g_norm, w_t, bgt, cos_t, sin_t, kv_prev, st, *, layer):
    tm = TM_PROJ
    n_pt = st.ctx_tiles(tm)
    if tm % st.seq:
        raise ValueError("a context row tile must hold whole sequences")
    seqs_per_tile, tps = tm // st.seq, st.dec_seq // tm
    n = st.n_all
    head_shape = jax.ShapeDtypeStruct((N_ATTN, n, DV_ATTN), BF16)
    head_spec = pl.BlockSpec((N_ATTN, tm, DV_ATTN), lambda i: (0, i, 0))
    kv_shape = jax.ShapeDtypeStruct((st.batch, DEPTH, N_ATTN, st.seq, DV_ATTN), F32)
    kv_spec = pl.BlockSpec((seqs_per_tile, None, N_ATTN, st.seq, DV_ATTN),
                           lambda i: (jnp.minimum(i, n_pt - 1), layer, 0, 0, 0))
    tok_shape = jax.ShapeDtypeStruct((n, W_ATTN), F32)
    tok_bf16 = jax.ShapeDtypeStruct((n, W_ATTN), BF16)
    tok_spec = pl.BlockSpec((tm, W_ATTN), lambda i: (i, 0))
    rope_spec = pl.BlockSpec((tm, LANES), lambda i: (jnp.maximum(i - n_pt, 0) % tps, 0))
    alias_in = list(kv_prev) if kv_prev is not None else []
    n_in = 7
    return pl.pallas_call(
        functools.partial(_proj_kernel, n_alias=len(alias_in), n_pt=n_pt, seqs_per_tile=seqs_per_tile,
                          seq=st.seq),
        grid=(n // tm,),
        in_specs=[
            pl.BlockSpec((tm, D_MODEL), lambda i: (i, 0)),
            st.mod_spec(tm),
            pl.BlockSpec((None, None, 1, D_MODEL), lambda i: (layer, 1, 0, 0)),
            _layer_spec((w_t.shape[1], D_MODEL), layer),
            _layer_spec((N_GATE, 1), layer),
            rope_spec, rope_spec,
        ] + [pl.BlockSpec(memory_space=pl.ANY)] * len(alias_in),
        out_specs=[head_spec, head_spec, kv_spec, kv_spec, tok_spec, tok_spec, tok_spec, tok_spec,
                   pl.BlockSpec((W_ATTN, tm), lambda i: (0, i)),
                   pl.BlockSpec((W_MLSTM, tm), lambda i: (0, i)),
                   pl.BlockSpec((N_GATE, tm), lambda i: (0, i))],
        out_shape=[head_shape, head_shape, kv_shape, kv_shape, tok_bf16, tok_bf16, tok_bf16, tok_shape,
                   jax.ShapeDtypeStruct((W_ATTN, n), BF16),
                   jax.ShapeDtypeStruct((W_MLSTM, n), F32),
                   jax.ShapeDtypeStruct((N_GATE, n), F32)],
        input_output_aliases={n_in + j: 2 + j for j in range(len(alias_in))},
        compiler_params=_cparams(1),
        name="mixer_proj",
    )(x, mods, g_norm, w_t, bgt, cos_t, sin_t, *alias_in)


def _attn_kernel(*refs, lam_init, cached, nb, tq, seq):
    q_ref, k_ref, vt_ref = refs[:3]
    pos = 3
    if cached:
        ck_ref, cv_ref = refs[pos:pos + 2]
        pos += 2
    lam_ref, g_ref = refs[pos:pos + 2]
    o_ref = refs[-1]
    lp = lam_ref[...]
    lam = (jnp.exp(jnp.sum(lp[0:1] * lp[1:2], axis=-1, keepdims=True))
           - jnp.exp(jnp.sum(lp[2:3] * lp[3:4], axis=-1, keepdims=True)) + lam_init)
    g_sub = g_ref[...]
    lane = lax.broadcasted_iota(jnp.int32, (tq, DV_ATTN), 1)
    heads = range(N_ATTN)
    results = []
    for bi in range(nb):
        ks = slice(bi * seq, (bi + 1) * seq)
        keys = [[k_ref[h, ks] for h in heads]]
        vals_t = [[vt_ref[h * DV_ATTN:(h + 1) * DV_ATTN, ks] for h in heads]]
        if cached:
            keys.insert(0, [ck_ref[h].astype(BF16) for h in heads])
            vals_t.insert(0, [cv_ref[h].T.astype(BF16) for h in heads])
        qh = [q_ref[h, bi * tq:(bi + 1) * tq].astype(F32) * (ATTN_SCALE * LOG2E) for h in heads]
        exps, dens = [], []
        for first_map in (True, False):
            qm = [jnp.where((lane < DH_ATTN) == first_map, q, 0.0).astype(BF16) for q in qh]
            s = [jnp.concatenate([_bdot_nt(kg[h], qm[h]) for h in heads], axis=1) for kg in keys]
            m = functools.reduce(jnp.maximum, [jnp.max(si, axis=0, keepdims=True) for si in s])
            e = [jnp.exp2(si - m) for si in s]
            exps.append(e)
            dens.append(functools.reduce(jnp.add, [jnp.sum(ei, axis=0, keepdims=True) for ei in e]))
        ratio = lam * dens[0] / dens[1]
        inv = 1.0 / dens[0]
        probs = [(e1 - e2 * ratio).astype(BF16) for e1, e2 in zip(*exps)]
        for h in heads:
            cols = slice(h * tq, (h + 1) * tq)
            o_t = None
            for pj, vg in zip(probs, vals_t):
                part = jnp.dot(vg[h], pj[:, cols], preferred_element_type=F32)
                o_t = part if o_t is None else o_t + part
            o_t = o_t * inv[:, cols]
            y_t = o_t * lax.rsqrt(jnp.mean(o_t * o_t, axis=0, keepdims=True) + EPS)
            results.append((bi, h, (y_t.T * g_sub * (1.0 - lam_init)).astype(o_ref.dtype)))
    for bi, h, y in results:
        o_ref[bi * tq:(bi + 1) * tq, h * DV_ATTN:(h + 1) * DV_ATTN] = y


def _attn_four_kernel(*refs, n_attn, attn_kw, four_kw, once_per_seq):
    a_ref, f_ref = refs[-2:]
    _attn_kernel(*refs[:n_attn], a_ref, **attn_kw)
    four = functools.partial(_four_kernel, *refs[n_attn:n_attn + 4], f_ref, **four_kw)
    if once_per_seq:
        pl.when(pl.program_id(1) == 0)(four)
    else:
        four()


def _attn_four_call(q, k, v_t, cache, lam_p, g_sub, zf, outs_prev, cast_w, st, *, layer, lam_init, latent):
    n = st.n_all
    if latent:
        batch, seq, nb, tq = st.dec_batch, st.dec_seq, 1, st.dec_seq // 2
        row0 = st.n_ctx
    else:
        batch, seq, nb, tq = st.batch, st.seq, 4, st.seq
        row0 = 0
    nq = seq // tq
    if nq > 1 and nb > 1:
        raise ValueError("query blocks of several sequences are not contiguous rows")
    q0, s0 = row0 // (nb * tq), row0 // (nb * seq)
    in_specs = [pl.BlockSpec((N_ATTN, nb * tq, DV_ATTN), lambda b, i: (0, q0 + b * nq + i, 0)),
                pl.BlockSpec((N_ATTN, nb * seq, DV_ATTN), lambda b, i: (0, s0 + b, 0)),
                pl.BlockSpec((W_ATTN, nb * seq), lambda b, i: (0, s0 + b))]
    args = [q, k, v_t]
    if cache is not None:
        past = cache[0].shape[3]
        c_spec = pl.BlockSpec((None, None, N_ATTN, past, DV_ATTN), lambda b, i: (b, layer, 0, 0, 0))
        in_specs += [c_spec, c_spec]
        args += list(cache)
    in_specs += [pl.BlockSpec((None, 4, DH_ATTN), lambda b, i: (layer, 0, 0)),
                 pl.BlockSpec((None, 1, DV_ATTN), lambda b, i: (layer, 0, 0))]
    args += [lam_p, g_sub]
    n_attn = len(args)
    w_d, ct, s_t = _dft_tables(seq)
    seq_rows = pl.BlockSpec((nb * seq, W_FOUR), lambda b, i: (s0 + b, 0))
    in_specs += [seq_rows,
                 pl.BlockSpec((DG_FOUR, 2 * DG_FOUR), lambda b, i: (0, 0)),
                 pl.BlockSpec((seq, seq), lambda b, i: (0, 0)),
                 pl.BlockSpec((seq, seq), lambda b, i: (0, 0))]
    args += [zf, w_d, ct, s_t]
    aliases = {}
    if outs_prev is not None:
        aliases = {len(args): 0, len(args) + 1: 1}
        in_specs += [pl.BlockSpec(memory_space=pl.ANY)] * 2
        args += list(outs_prev)
    casts = _CastJobs(cast_w, (batch // nb) * nq, lambda b, i: b * nq + i)
    body = functools.partial(
        _attn_four_kernel, n_attn=n_attn, once_per_seq=nq > 1,
        attn_kw=dict(lam_init=lam_init, cached=cache is not None, nb=nb, tq=tq, seq=seq),
        four_kw=dict(nb=nb, seq=seq))
    return pl.pallas_call(
        casts.wrap(body, len(args), 2),
        grid=(batch // nb, nq),
        in_specs=in_specs + casts.in_specs(),
        out_specs=[pl.BlockSpec((nb * tq, W_ATTN), lambda b, i: (q0 + b * nq + i, 0)), seq_rows]
        + casts.out_specs(),
        out_shape=[jax.ShapeDtypeStruct((n, W_ATTN), BF16), jax.ShapeDtypeStruct((n, W_FOUR), BF16)]
        + casts.out_shapes(),
        input_output_aliases=aliases,
        compiler_params=_cparams(2),
        name="attn_fourier",
    )(*args, *casts.args())


def _dft_tables(seq):
    def cs(n):
        j = np.arange(n)
        ang = 2.0 * np.pi * ((j[:, None] * j[None, :]) % n) / n
        return np.cos(ang) / math.sqrt(n), np.sin(ang) / math.sqrt(n)

    cd, sd = cs(DG_FOUR)
    ct, st = cs(seq)
    w_d = jnp.asarray(np.concatenate([cd, sd], axis=1), F32)
    return w_d.astype(BF16), jnp.asarray(ct, F32).astype(BF16), jnp.asarray(-st, F32).astype(BF16)


def _four_kernel(*refs, nb, seq):
    z_ref, wd_ref, ct_ref, st_ref = refs[:4]
    o_ref = refs[-1]
    y_cos, y_sin = [], []
    for gidx in range(N_FOUR):
        y = jnp.dot(z_ref[:, gidx * DG_FOUR:(gidx + 1) * DG_FOUR], wd_ref[...], preferred_element_type=F32)
        y_cos.append(y[:, :DG_FOUR].astype(BF16))
        y_sin.append(y[:, DG_FOUR:].astype(BF16))
    y_cos = jnp.concatenate(y_cos, axis=1)
    y_sin = jnp.concatenate(y_sin, axis=1)
    for s in range(nb):
        rows = slice(s * seq, (s + 1) * seq)
        o_ref[rows, :] = (jnp.dot(ct_ref[...], y_cos[rows], preferred_element_type=F32)
                          + jnp.dot(st_ref[...], y_sin[rows], preferred_element_type=F32)).astype(o_ref.dtype)


def _mlstm_kernel(*refs, seq, nb, seeded, emit_state, n_alias):
    mq_ref, mk_ref, vt_ref, mo_ref, gt_ref, g_ref = refs[:6]
    pos = 6
    if seeded:
        c0_ref, n0_ref, m0_ref = refs[pos:pos + 3]
        pos += 3
    pos += n_alias
    o_ref = refs[pos]
    if emit_state:
        c1_ref, n1_ref, m1_ref = refs[pos + 1:pos + 4]

    L = min(MLSTM_CHUNK, seq)
    nc = seq // L
    s_idx = lax.broadcasted_iota(jnp.int32, (L, L), 0)
    t_idx = lax.broadcasted_iota(jnp.int32, (L, L), 1)
    before = (s_idx <= t_idx, s_idx >= t_idx)
    tri = (jnp.where(before[1], 1.0, 0.0).astype(BF16),
           jnp.where(before[0], 1.0, 0.0).astype(BF16))
    chains = [(bi, d, h) for bi in range(nb) for d in range(2) for h in range(N_MLSTM)]

    state = {}
    for (bi, d, h) in chains:
        if seeded:
            state[bi, d, h] = (c0_ref[bi, d, h], n0_ref[bi, d, h:h + 1, :], m0_ref[bi, d:d + 1, h:h + 1])
        else:
            state[bi, d, h] = (None, None, jnp.zeros((1, 1), F32))

    terms = {}
    for c in range(nc):
        for bi in range(nb):
            for d in range(2):
                r0 = bi * seq + (c if d == 0 else nc - 1 - c) * L
                g_t = gt_ref[:, r0:r0 + L]
                hi = g_t.astype(BF16)
                rem = g_t - hi.astype(F32)
                mid = rem.astype(BF16)
                lo = (rem - mid.astype(F32)).astype(BF16)
                pieces = _bdot_nt(jnp.concatenate([hi, mid, lo], axis=0), tri[d])
                cum_t = pieces[0:N_GATE] + pieces[N_GATE:2 * N_GATE] + pieces[2 * N_GATE:]
                c_t = (g_t - pltpu.roll(cum_t, N_GATE - N_MLSTM, 0)) * LOG2E
                col = jnp.concatenate([c_t, jnp.zeros((LANES - N_GATE, L), F32)], axis=0).T
                terms[c, bi, d] = (r0, col, g_t, cum_t)

    h_t = {}
    for c in range(nc):
        new_state = {}
        need_update = emit_state or c < nc - 1
        group = [(d, h) for d in range(2) for h in range(N_MLSTM)]
        for bi in range(nb):
            has_state = state[bi, 0, 0][0] is not None
            qs, ks, vts, cbs, s0s, i_rows, b_rows, m_prevs = [], [], [], [], [], [], [], []
            for d, h in group:
                r0, col, g_t, cum_t = terms[c, bi, d]
                ci = 2 * d * N_MLSTM + h
                cf = ci + N_MLSTM
                hs = slice(h * DH_MLSTM, (h + 1) * DH_MLSTM)
                qs.append(mq_ref[r0:r0 + L, hs])
                ks.append(mk_ref[r0:r0 + L, hs])
                vts.append(vt_ref[hs, r0:r0 + L])
                cbs.append(jnp.where(before[d], col[:, ci:ci + 1], -jnp.inf))
                s0s.append(_bdot_nt(ks[-1], qs[-1]))
                i_rows.append(g_t[ci:ci + 1, :])
                b_rows.append(cum_t[cf:cf + 1, :])
                m_prevs.append(jnp.broadcast_to(state[bi, d, h][2], (1, L)))
            cb = jnp.concatenate(cbs, axis=1)
            i_row, b_row = jnp.concatenate(i_rows, axis=1), jnp.concatenate(b_rows, axis=1)
            m_prev = jnp.concatenate(m_prevs, axis=1)
            m2_prev = m_prev * LOG2E
            m2_row = jnp.maximum(jnp.max(cb, axis=0, keepdims=True), m2_prev)
            s_t = jnp.concatenate(s0s, axis=1) * jnp.exp2(cb - m2_row)
            den = jnp.sum(s_t, axis=0, keepdims=True)
            s_bf = s_t.astype(BF16)
            m_t = b_row + m2_row * LN2
            floor = jnp.exp(-m_t)
            if has_state:
                sp = jnp.exp2(m2_prev - m2_row)
            if need_update:
                tot, new = [], []
                for j, (d, h) in enumerate(group):
                    last = j * L + (L - 1 if d == 0 else 0)
                    tot.append(jnp.broadcast_to(b_row[:, last:last + 1], (1, L)))
                    new.append(jnp.broadcast_to(m_t[:, last:last + 1], (1, L)))
                b_tot, m_new = jnp.concatenate(tot, axis=1), jnp.concatenate(new, axis=1)
                wl = jnp.exp(b_tot + (i_row - b_row) - m_new)
                if has_state:
                    decay = jnp.exp(b_tot + m_prev - m_new)
            for j, (d, h) in enumerate(group):
                cols = slice(j * L, (j + 1) * L)
                c_prev, n_prev, _ = state[bi, d, h]
                num_t = jnp.dot(vts[j].astype(BF16), s_bf[:, cols], preferred_element_type=F32)
                den_j = den[:, cols]
                if has_state:
                    cn = jnp.concatenate([c_prev, jnp.broadcast_to(n_prev, (SUBLANES, DH_MLSTM))], axis=0)
                    cq = _bdot_nt(cn, qs[j])
                    num_t = num_t + sp[:, cols] * cq[:DH_MLSTM]
                    den_j = den_j + sp[:, cols] * cq[DH_MLSTM:DH_MLSTM + 1]
                h_t[bi, d, h, c] = num_t / jnp.maximum(jnp.abs(den_j), floor[:, cols])
                if need_update:
                    wl_j = wl[:, cols]
                    vw = jnp.concatenate([vts[j] * wl_j, jnp.broadcast_to(wl_j, (SUBLANES, L))], axis=0)
                    upd = jnp.dot(vw.astype(BF16), ks[j], preferred_element_type=F32)
                    c_new, n_new = upd[:DH_MLSTM], upd[DH_MLSTM:DH_MLSTM + 1]
                    if has_state:
                        decay_j = decay[:, j * L:j * L + 1]
                        c_new = decay_j * c_prev + c_new
                        n_new = decay_j * n_prev + n_new
                    new_state[bi, d, h] = (c_new, n_new, m_new[:, j * L:j * L + 1])
        state = new_state

    g_m = g_ref[...]
    for bi in range(nb):
        for h in range(N_MLSTM):
            hs = slice(h * DH_MLSTM, (h + 1) * DH_MLSTM)
            fwd = [h_t[bi, 0, h, c] for c in range(nc)]
            bwd = [h_t[bi, 1, h, nc - 1 - c] for c in range(nc)]
            hsum = (fwd[0] if nc == 1 else jnp.concatenate(fwd, axis=1)) \
                + (bwd[0] if nc == 1 else jnp.concatenate(bwd, axis=1))
            y = hsum * lax.rsqrt(jnp.mean(hsum * hsum, axis=0, keepdims=True) + EPS)
            rows = slice(bi * seq, (bi + 1) * seq)
            o_ref[rows, hs] = (y.T * g_m * _sigmoid(mo_ref[rows, hs])).astype(o_ref.dtype)

    if emit_state:
        for (bi, d, h) in chains:
            c_fin, n_fin, m_fin = state[bi, d, h]
            c1_ref[bi, d, h] = c_fin
            n1_ref[bi, d, h:h + 1, :] = n_fin
            m1_ref[bi, d:d + 1, h:h + 1] = m_fin


def _mlstm_call(mq, mk, vt, mo, gates_t, g_m, state, alias_prev, cast_w, st, *, layer, latent):
    if latent:
        batch, seq, nb, row0 = st.dec_batch, st.dec_seq, 1, st.n_ctx
    else:
        batch, seq, nb, row0 = st.batch, st.seq, 4, 0
    emit_state = not latent
    rows = nb * seq
    b0 = row0 // rows
    tok_spec = pl.BlockSpec((rows, W_MLSTM), lambda b: (b0 + b, 0))
    in_specs = [tok_spec, tok_spec,
                pl.BlockSpec((W_MLSTM, rows), lambda b: (0, b0 + b)),
                tok_spec,
                pl.BlockSpec((N_GATE, rows), lambda b: (0, b0 + b)),
                pl.BlockSpec((None, 1, DH_MLSTM), lambda b: (layer, 0, 0))]
    args = [mq, mk, vt, mo, gates_t, g_m]
    state_specs = [
        pl.BlockSpec((nb, None, 2, N_MLSTM, DH_MLSTM, DH_MLSTM), lambda b: (b, layer, 0, 0, 0, 0)),
        pl.BlockSpec((nb, None, 2, N_MLSTM, DH_MLSTM), lambda b: (b, layer, 0, 0, 0)),
        pl.BlockSpec((nb, None, 2, N_MLSTM), lambda b: (b, layer, 0, 0)),
    ]
    if state is not None:
        in_specs += state_specs
        args += list(state)
    alias_in = list(alias_prev) if alias_prev is not None else []
    n_in = len(args)
    in_specs += [pl.BlockSpec(memory_space=pl.ANY)] * len(alias_in)
    args += alias_in
    out_specs = [tok_spec]
    out_shape = [jax.ShapeDtypeStruct((st.n_all, W_MLSTM), BF16)]
    if emit_state:
        out_specs += state_specs
        out_shape += [
            jax.ShapeDtypeStruct((batch, DEPTH, 2, N_MLSTM, DH_MLSTM, DH_MLSTM), F32),
            jax.ShapeDtypeStruct((batch, DEPTH, 2, N_MLSTM, DH_MLSTM), F32),
            jax.ShapeDtypeStruct((batch, DEPTH, 2, N_MLSTM), F32),
        ]
    first_out = 1 if emit_state else 0
    casts = _CastJobs(cast_w, batch // nb, lambda b: b)
    body = functools.partial(_mlstm_kernel, seq=seq, nb=nb, seeded=state is not None, emit_state=emit_state,
                             n_alias=len(alias_in))
    return pl.pallas_call(
        casts.wrap(body, len(args), len(out_specs)),
        grid=(batch // nb,),
        in_specs=in_specs + casts.in_specs(),
        out_specs=out_specs + casts.out_specs(),
        out_shape=out_shape + casts.out_shapes(),
        input_output_aliases={n_in + j: first_out + j for j in range(len(alias_in))},
        compiler_params=_cparams(1),
        name="mlstm",
    )(*args, *casts.args())


def _ctx_mixer_kernel(*refs, n_attn, attn_kw, four_kw, mlstm_kw):
    a_ref, f_ref = refs[-6:-4]
    _attn_kernel(*refs[:n_attn], a_ref, **attn_kw)
    _four_kernel(*refs[n_attn:n_attn + 4], f_ref, **four_kw)
    _mlstm_kernel(*refs[n_attn + 4:n_attn + 10], *refs[-4:], **mlstm_kw)


def _ctx_mixer_call(q, k, v_t, lam_p, g_sub, zf, mq, mk, vt, mo, gates_t, g_m, states_prev, cast_w, st, *,
                    layer, lam_init):
    n, seq, nb = st.n_all, st.seq, 4
    rows = nb * seq
    heads = pl.BlockSpec((N_ATTN, rows, DV_ATTN), lambda b: (0, b, 0))
    tok = pl.BlockSpec((rows, W_ATTN), lambda b: (b, 0))
    w_d, ct, s_t = _dft_tables(seq)
    in_specs = [heads, heads,
                pl.BlockSpec((W_ATTN, rows), lambda b: (0, b)),
                pl.BlockSpec((None, 4, DH_ATTN), lambda b: (layer, 0, 0)),
                pl.BlockSpec((None, 1, DV_ATTN), lambda b: (layer, 0, 0)),
                tok,
                pl.BlockSpec((DG_FOUR, 2 * DG_FOUR), lambda b: (0, 0)),
                pl.BlockSpec((seq, seq), lambda b: (0, 0)),
                pl.BlockSpec((seq, seq), lambda b: (0, 0)),
                tok, tok,
                pl.BlockSpec((W_MLSTM, rows), lambda b: (0, b)),
                tok,
                pl.BlockSpec((N_GATE, rows), lambda b: (0, b)),
                pl.BlockSpec((None, 1, DH_MLSTM), lambda b: (layer, 0, 0))]
    args = [q, k, v_t, lam_p, g_sub, zf, w_d, ct, s_t, mq, mk, vt, mo, gates_t, g_m]
    n_attn = 5
    state_specs = [
        pl.BlockSpec((nb, None, 2, N_MLSTM, DH_MLSTM, DH_MLSTM), lambda b: (b, layer, 0, 0, 0, 0)),
        pl.BlockSpec((nb, None, 2, N_MLSTM, DH_MLSTM), lambda b: (b, layer, 0, 0, 0)),
        pl.BlockSpec((nb, None, 2, N_MLSTM), lambda b: (b, layer, 0, 0)),
    ]
    aliases = {}
    if states_prev is not None:
        aliases = {len(args) + j: 3 + j for j in range(3)}
        in_specs += [pl.BlockSpec(memory_space=pl.ANY)] * 3
        args += list(states_prev)
    casts = _CastJobs(cast_w, st.batch // nb, lambda b: b)
    body = functools.partial(
        _ctx_mixer_kernel, n_attn=n_attn,
        attn_kw=dict(lam_init=lam_init, cached=False, nb=nb, tq=seq, seq=seq),
        four_kw=dict(nb=nb, seq=seq),
        mlstm_kw=dict(seq=seq, nb=nb, seeded=False, emit_state=True, n_alias=0))
    return pl.pallas_call(
        casts.wrap(body, len(args), 6),
        grid=(st.batch // nb,),
        in_specs=in_specs + casts.in_specs(),
        out_specs=[tok, tok, tok] + state_specs + casts.out_specs(),
        out_shape=[jax.ShapeDtypeStruct((n, W_ATTN), BF16), jax.ShapeDtypeStruct((n, W_FOUR), BF16),
                   jax.ShapeDtypeStruct((n, W_MLSTM), BF16),
                   jax.ShapeDtypeStruct((st.batch, DEPTH, 2, N_MLSTM, DH_MLSTM, DH_MLSTM), F32),
                   jax.ShapeDtypeStruct((st.batch, DEPTH, 2, N_MLSTM, DH_MLSTM), F32),
                   jax.ShapeDtypeStruct((st.batch, DEPTH, 2, N_MLSTM), F32)] + casts.out_shapes(),
        input_output_aliases=aliases,
        compiler_params=_cparams(1),
        name="ctx_mixer",
    )(*args, *casts.args())


def _merge_kernel(x_ref, a_ref, f_ref, m_ref, mod_ref, g_ref, wg_ref, wa_ref, wf_ref, wm_ref, wo_ref, o_ref):
    x = x_ref[...]
    mod = mod_ref[...]
    u = _modulated_norm(x, g_ref[...], mod, 3).astype(BF16)
    merged = None
    for j, (br_ref, w_ref) in enumerate(((a_ref, wa_ref), (f_ref, wf_ref), (m_ref, wm_ref))):
        gate = _sigmoid(jnp.dot(u, wg_ref[:, j * D_MODEL:(j + 1) * D_MODEL], preferred_element_type=F32))
        term = gate * _bdot(br_ref[...], w_ref[...])
        merged = term if merged is None else merged + term
    out = _bdot(merged, wo_ref[...])
    o_ref[...] = x + mod[5:6] * out


def _merge_call(x, a, f, m, mods, g_norm, wg, wa, wf, wm, wo, st, *, layer):
    tm = TM_WIDE
    n = st.n_all
    br_spec = pl.BlockSpec((tm, W_ATTN), lambda i: (i, 0))
    return pl.pallas_call(
        _merge_kernel,
        grid=(n // tm,),
        in_specs=[
            pl.BlockSpec((tm, D_MODEL), lambda i: (i, 0)),
            br_spec, br_spec, br_spec,
            st.mod_spec(tm),
            pl.BlockSpec((None, None, 1, D_MODEL), lambda i: (layer, 1, 0, 0)),
            _layer_spec((D_MODEL, 3 * D_MODEL), None),
            _layer_spec((W_ATTN, D_MODEL), None), _layer_spec((W_FOUR, D_MODEL), None),
            _layer_spec((W_MLSTM, D_MODEL), None),
            _layer_spec((D_MODEL, D_MODEL), None),
        ],
        out_specs=pl.BlockSpec((tm, D_MODEL), lambda i: (i, 0)),
        out_shape=jax.ShapeDtypeStruct((n, D_MODEL), F32),
        compiler_params=_cparams(1),
        name="merge",
    )(x, a, f, m, mods, g_norm, wg, wa, wf, wm, wo)


def _rope_tables(n_tok):
    tok = np.arange(n_tok)
    inv = ROPE_BASE ** (-np.arange(ROPE_AXIS_PAIRS, dtype=np.float32) / ROPE_AXIS_PAIRS)
    ang = np.concatenate([(tok // GRID_W).astype(np.float32)[:, None] * inv,
                          (tok % GRID_W).astype(np.float32)[:, None] * inv], axis=-1).astype(np.float32)
    c, s = np.cos(ang), np.sin(ang)
    cos_t = np.concatenate([c, c, c, c], axis=-1)
    sin_t = np.concatenate([-s, s, -s, s], axis=-1)
    return jnp.asarray(cos_t, F32), jnp.asarray(sin_t, F32)


def kernel(x_prompt, x_sample, cache_k, cache_v, state_C, state_n, state_m, c, c_ctx, w_ada, b_ada, g_norm,
           w_ffn1_in, w_ffn1_out, w_ffn2_in, w_ffn2_out, w_in, b_mgate, attn_lambda, g_attn_sub, g_mlstm,
           w_branch_gate, w_br_attn, w_br_four, w_br_mlstm, w_out, g_final):
    batch, seq, _ = x_prompt.shape
    dec_batch, dec_seq, _ = x_sample.shape
    st = _Stream(batch, seq, dec_batch, dec_seq)
    cond = jnp.zeros((N_COND, D_MODEL), F32).at[0].set(c_ctx).at[1:1 + dec_batch].set(c)
    mods_flat, *ffn1_w = _ada_call(cond, w_ada, b_ada, [(w_ffn1_in, 0), (w_ffn1_out, 0)])
    mods_all = mods_flat.reshape(DEPTH, N_COND, N_MOD, D_MODEL)

    b_gate_t = b_mgate[:, :, None]
    g_norm4 = g_norm[:, :, None, :]
    g_fin = g_final[None, :]
    w_t = jnp.swapaxes(w_in, 1, 2).astype(BF16)
    g_sub3 = g_attn_sub[:, None, :]
    g_m3 = g_mlstm[:, None, :]
    cos_t, sin_t = _rope_tables(dec_seq)
    lat_state = (state_C, state_n, state_m)

    xs = (x_prompt.reshape(batch * seq, D_MODEL), x_sample.reshape(dec_batch * dec_seq, D_MODEL))
    kv, states = None, None
    for l in range(DEPTH):
        mods = mods_all[l]
        lam_init = 0.8 - 0.6 * math.exp(-0.3 * l)
        nxt = l + 1 < DEPTH
        (x,) = _ffn_call(xs, mods, g_norm4, g_fin, *ffn1_w, st, layer=l, sub=0, final=False)
        q, k, ck, cv, zf, mq, mk, mo, vat, vt, gates_t = _proj_call(
            x, mods, g_norm4, w_t, b_gate_t, cos_t, sin_t, kv, st, layer=l)
        kv = (ck, cv)
        a, f, m, c_fin, n_fin, m_fin, ffn2_in, *nxt_out = _ctx_mixer_call(
            q, k, vat, attn_lambda, g_sub3, zf, mq, mk, vt, mo, gates_t, g_m3, states,
            [(w_ffn2_in, l)] + ([(w_ffn1_out, l + 1)] if nxt else []), st, layer=l, lam_init=lam_init)
        states = [c_fin, n_fin, m_fin]
        a, f, ffn2_out, w_bg, *nxt_in = _attn_four_call(
            q, k, vat, (cache_k, cache_v), attn_lambda, g_sub3, zf, (a, f),
            [(w_ffn2_out, l), (w_branch_gate, l)] + ([(w_ffn1_in, l + 1)] if nxt else []), st,
            layer=l, lam_init=lam_init, latent=True)
        m, w_ba, w_bf, w_bm, w_o = _mlstm_call(
            mq, mk, vt, mo, gates_t, g_m3, lat_state, (m,),
            [(w_br_attn, l), (w_br_four, l), (w_br_mlstm, l), (w_out, l)], st, layer=l, latent=True)
        x = _merge_call(x, a, f, m, mods, g_norm4, w_bg, w_ba, w_bf, w_bm, w_o, st, layer=l)
        xs = tuple(_ffn_call((x,), mods, g_norm4, g_fin, ffn2_in, ffn2_out, st, layer=l, sub=2,
                             final=(l == DEPTH - 1)))
        if nxt:
            ffn1_w = (nxt_in[0], nxt_out[0])
    y_prompt, y_sample = xs
    return (y_prompt.reshape(batch, seq, D_MODEL), y_sample.reshape(dec_batch, dec_seq, D_MODEL),
            *kv, *states)
```

```python
import functools
import math

import numpy as np
import jax
import jax.numpy as jnp
from jax import lax
from jax.experimental import pallas as pl
from jax.experimental.pallas import tpu as pltpu

D_MODEL = 1024
DEPTH = 2
GRID_W = 64
N_ATTN = 4
DH_ATTN = 64
DV_ATTN = 2 * DH_ATTN
W_ATTN = N_ATTN * DV_ATTN
N_FOUR = 4
DG_FOUR = 128
W_FOUR = N_FOUR * DG_FOUR
N_MLSTM = 4
DH_MLSTM = 128
W_MLSTM = N_MLSTM * DH_MLSTM
N_GATE = 4 * N_MLSTM
LANES = 128
SUBLANES = 16
MV_LO = 3 * W_ATTN + W_FOUR + 2 * W_MLSTM
MV_HI = MV_LO + W_MLSTM
P_MAIN = MV_HI + W_MLSTM
D_FF = 2816
N_MOD = 9
N_COND = 8
ROPE_BASE = 10000.0
ROPE_AXIS_PAIRS = DH_ATTN // 4
ATTN_SCALE = DH_ATTN ** -0.5
LOG2E = 1.4426950408889634
LN2 = 0.6931471805599453
MLSTM_K_SCALE = DH_MLSTM ** -0.5
EPS = 1e-6
MLSTM_CHUNK = 256
VMEM_LIMIT = 56 * 1024 * 1024
TM_WIDE = 1024
TM_PROJ = 1024

F32 = jnp.float32
BF16 = jnp.bfloat16


def _cparams(n_grid):
    return pltpu.CompilerParams(dimension_semantics=("arbitrary",) * n_grid,
                                vmem_limit_bytes=VMEM_LIMIT)


def _bdot(a, b):
    return jnp.dot(a.astype(BF16), b.astype(BF16), preferred_element_type=F32)


def _bdot_nt(a, b):
    return lax.dot_general(a.astype(BF16), b.astype(BF16), (((1,), (1,)), ((), ())),
                           preferred_element_type=F32)


def _sigmoid(x):
    return 1.0 / (1.0 + jnp.exp(-x))


def _log_sigmoid(x):
    return jnp.minimum(x, 0.0) - jnp.log1p(jnp.exp(-jnp.abs(x)))


def _rms(x, g):
    return x * lax.rsqrt(jnp.mean(x * x, axis=-1, keepdims=True) + EPS) * g


def _modulated_norm(x, g, mod, base):
    gain = g * (1.0 + mod[base + 1:base + 2])
    return x * lax.rsqrt(jnp.mean(x * x, axis=-1, keepdims=True) + EPS) * gain + mod[base:base + 1]


def _layer_spec(shape, layer):
    if layer is None:
        return pl.BlockSpec(shape, lambda *_: (0,) * len(shape), pipeline_mode=pl.Buffered(1))
    return pl.BlockSpec((None,) + shape, lambda *_: (layer,) + (0,) * len(shape),
                        pipeline_mode=pl.Buffered(1))


class _CastJobs:
    def __init__(self, jobs, steps, step_index):
        self.jobs, self.steps, self.step_index = list(jobs), steps, step_index

    def __len__(self):
        return len(self.jobs)

    def _rows(self, w):
        rows = w.shape[1] // self.steps
        if rows * self.steps != w.shape[1] or rows % SUBLANES:
            raise ValueError("weight rows do not split into bf16 row tiles over the grid")
        return rows

    def in_specs(self):
        return [pl.BlockSpec((None, self._rows(w), w.shape[2]),
                             lambda *g, layer=layer: (layer, self.step_index(*g), 0))
                for w, layer in self.jobs]

    def out_specs(self):
        return [pl.BlockSpec((self._rows(w), w.shape[2]), lambda *g: (self.step_index(*g), 0))
                for w, _ in self.jobs]

    def out_shapes(self):
        return [jax.ShapeDtypeStruct(w.shape[1:], BF16) for w, _ in self.jobs]

    def args(self):
        return [w for w, _ in self.jobs]

    def wrap(self, body, n_in, n_out):
        n = len(self.jobs)
        if n == 0:
            return body

        def kernel(*refs):
            ins, cast_in = refs[:n_in], refs[n_in:n_in + n]
            outs = refs[n_in + n:n_in + n + n_out]
            cast_out = refs[n_in + n + n_out:n_in + 2 * n + n_out]
            for src, dst in zip(cast_in, cast_out):
                dst[...] = src[...].astype(dst.dtype)
            body(*ins, *outs, *refs[n_in + 2 * n + n_out:])

        return kernel


class _Stream:
    def __init__(self, batch, seq, dec_batch, dec_seq):
        self.batch, self.seq, self.dec_batch, self.dec_seq = batch, seq, dec_batch, dec_seq
        self.n_ctx = batch * seq
        self.n_all = self.n_ctx + dec_batch * dec_seq

    def ctx_tiles(self, tm):
        if self.n_ctx % tm or self.dec_seq % tm:
            raise ValueError("row tile must divide the context rows and one latent sequence")
        return self.n_ctx // tm

    def mod_spec(self, tm):
        n_pt, tps = self.ctx_tiles(tm), self.dec_seq // tm
        return pl.BlockSpec((None, N_MOD, D_MODEL),
                            lambda i: (jnp.where(i < n_pt, 0, 1 + (i - n_pt) // tps), 0, 0))


def _ada_kernel(c_ref, w_ref, b_ref, o_ref):
    c = c_ref[...]
    s = c * _sigmoid(c)
    o_ref[...] = _bdot(s, w_ref[...]) + b_ref[...]


def _ada_call(cond, w_ada, b_ada, cast_w):
    tn = 2304
    n_out = N_MOD * D_MODEL
    nj = n_out // tn
    casts = _CastJobs(cast_w, DEPTH * nj, lambda l, j: l * nj + j)
    return pl.pallas_call(
        casts.wrap(_ada_kernel, 3, 1),
        grid=(DEPTH, nj),
        in_specs=[
            pl.BlockSpec((N_COND, D_MODEL), lambda l, j: (0, 0)),
            pl.BlockSpec((None, D_MODEL, tn), lambda l, j: (l, 0, j)),
            pl.BlockSpec((None, 1, tn), lambda l, j: (l, 0, j)),
        ] + casts.in_specs(),
        out_specs=[pl.BlockSpec((None, N_COND, tn), lambda l, j: (l, 0, j))] + casts.out_specs(),
        out_shape=[jax.ShapeDtypeStruct((DEPTH, N_COND, n_out), F32)] + casts.out_shapes(),
        compiler_params=_cparams(2),
        name="adaln",
    )(cond, w_ada, b_ada.reshape(DEPTH, 1, n_out), *casts.args())


FF_CHUNKS = (768, 768, 768, 512)


def _ffn_kernel(*refs, base, n_x, final, n_pt):
    x_refs = refs[:n_x]
    mod_ref, g_ref, gf_ref, win_ref, wout_ref = refs[n_x:n_x + 5]
    o_refs = refs[n_x + 5:]
    i = pl.program_id(0)
    if n_x == 2:
        x = jnp.where(i < n_pt, x_refs[0][...], x_refs[1][...])
    else:
        x = x_refs[0][...]
    mod = mod_ref[...]
    u = _modulated_norm(x, g_ref[...], mod, base).astype(BF16)
    y = None
    lo = 0
    for width in FF_CHUNKS:
        a = jnp.dot(u, win_ref[:, lo:lo + width], preferred_element_type=F32)
        g = jnp.dot(u, win_ref[:, D_FF + lo:D_FF + lo + width], preferred_element_type=F32)
        hh = (a * _sigmoid(a) * g).astype(BF16)
        part = jnp.dot(hh, wout_ref[lo:lo + width, :], preferred_element_type=F32)
        y = part if y is None else y + part
        lo += width
    xn = x + 0.5 * mod[base + 2:base + 3] * y
    if not final:
        o_refs[0][...] = xn
    else:
        xn = _rms(xn, gf_ref[...])

        @pl.when(i < n_pt)
        def _():
            o_refs[0][...] = xn

        @pl.when(i >= n_pt)
        def _():
            o_refs[1][...] = xn


def _ffn_call(xs, mods, g_norm, g_final, w_in, w_out, st, *, layer, sub, final):
    tm = TM_WIDE
    n_pt = st.ctx_tiles(tm)
    n_tiles = st.n_all // tm
    ctx_spec = pl.BlockSpec((tm, D_MODEL), lambda i: (jnp.minimum(i, n_pt - 1), 0))
    lat_spec = pl.BlockSpec((tm, D_MODEL), lambda i: (jnp.maximum(i - n_pt, 0), 0))
    all_spec = pl.BlockSpec((tm, D_MODEL), lambda i: (i, 0))
    x_specs = [ctx_spec, lat_spec] if len(xs) == 2 else [all_spec]
    if final:
        out_specs = [ctx_spec, lat_spec]
        out_shape = [jax.ShapeDtypeStruct((st.n_ctx, D_MODEL), F32),
                     jax.ShapeDtypeStruct((st.n_all - st.n_ctx, D_MODEL), F32)]
    else:
        out_specs = [all_spec]
        out_shape = [jax.ShapeDtypeStruct((st.n_all, D_MODEL), F32)]
    return pl.pallas_call(
        functools.partial(_ffn_kernel, base=3 * sub, n_x=len(xs), final=final, n_pt=n_pt),
        grid=(n_tiles,),
        in_specs=x_specs + [
            st.mod_spec(tm),
            pl.BlockSpec((None, None, 1, D_MODEL), lambda i: (layer, sub, 0, 0)),
            pl.BlockSpec((1, D_MODEL), lambda i: (0, 0)),
            _layer_spec((D_MODEL, 2 * D_FF), None),
            _layer_spec((D_FF, D_MODEL), None),
        ],
        out_specs=out_specs,
        out_shape=out_shape,
        compiler_params=_cparams(1),
        name="ffn",
    )(*xs, mods, g_norm, g_final, w_in, w_out)


def _swap32(x):
    lane = lax.broadcasted_iota(jnp.int32, x.shape, 1)
    return jnp.where((lane & (DH_ATTN // 2)) == 0,
                     pltpu.roll(x, LANES - DH_ATTN // 2, 1), pltpu.roll(x, DH_ATTN // 2, 1))


def _proj_kernel(*refs, n_alias, n_pt, seqs_per_tile, seq):
    (x_ref, mod_ref, g_ref, wt_ref, bgt_ref, cos_ref, sin_ref) = refs[:7]
    (q_ref, k_ref, ck_ref, cv_ref, zf_ref, mq_ref, mk_ref, mo_ref, vat_ref, vt_ref, gt_ref) = refs[7 + n_alias:]
    i = pl.program_id(0)
    is_latent = i >= n_pt
    x = x_ref[...]
    u = _modulated_norm(x, g_ref[...], mod_ref[...], 3).astype(BF16)
    zb = _bdot_nt(u, wt_ref[3 * W_ATTN:MV_LO, :])
    zf_ref[...] = zb[:, :W_FOUR].astype(zf_ref.dtype)
    off = W_FOUR
    mq_ref[...] = zb[:, off:off + W_MLSTM].astype(mq_ref.dtype)
    mk_ref[...] = (zb[:, off + W_MLSTM:off + 2 * W_MLSTM] * MLSTM_K_SCALE).astype(mk_ref.dtype)
    mo_ref[...] = _bdot_nt(u, wt_ref[MV_HI:P_MAIN, :])
    vat_ref[...] = _bdot_nt(wt_ref[2 * W_ATTN:3 * W_ATTN, :], u).astype(vat_ref.dtype)
    vt_ref[...] = _bdot_nt(wt_ref[MV_LO:MV_HI, :], u)
    gpt = _bdot_nt(wt_ref[P_MAIN:, :], u) + bgt_ref[...]
    sub = lax.broadcasted_iota(jnp.int32, gpt.shape, 0)
    gt_ref[...] = jnp.where((sub & N_MLSTM) != 0, _log_sigmoid(gpt), gpt)

    za = _bdot_nt(u, wt_ref[:3 * W_ATTN, :])
    cos_t = cos_ref[...]
    sin_t = sin_ref[...]
    plain_k = []
    for h in range(N_ATTN):
        zq = za[:, h * DV_ATTN:(h + 1) * DV_ATTN]
        zk = za[:, W_ATTN + h * DV_ATTN:W_ATTN + (h + 1) * DV_ATTN]
        plain_k.append(zk)
        q_ref[h] = jnp.where(is_latent, zq * cos_t + _swap32(zq) * sin_t, zq).astype(q_ref.dtype)
        k_ref[h] = jnp.where(is_latent, zk * cos_t + _swap32(zk) * sin_t, zk).astype(k_ref.dtype)

    @pl.when(i < n_pt)
    def _():
        for h in range(N_ATTN):
            zv = za[:, 2 * W_ATTN + h * DV_ATTN:2 * W_ATTN + (h + 1) * DV_ATTN]
            for s in range(seqs_per_tile):
                rs = slice(s * seq, (s + 1) * seq)
                ck_ref[s, h] = plain_k[h][rs]
                cv_ref[s, h] = zv[rs]


def _proj_call(x, mods, g_norm, w_t, bgt, cos_t, sin_t, kv_prev, st, *, layer):
    tm = TM_PROJ
    n_pt = st.ctx_tiles(tm)
    if tm % st.seq:
        raise ValueError("a context row tile must hold whole sequences")
    seqs_per_tile, tps = tm // st.seq, st.dec_seq // tm
    n = st.n_all
    head_shape = jax.ShapeDtypeStruct((N_ATTN, n, DV_ATTN), BF16)
    head_spec = pl.BlockSpec((N_ATTN, tm, DV_ATTN), lambda i: (0, i, 0))
    kv_shape = jax.ShapeDtypeStruct((st.batch, DEPTH, N_ATTN, st.seq, DV_ATTN), F32)
    kv_spec = pl.BlockSpec((seqs_per_tile, None, N_ATTN, st.seq, DV_ATTN),
                           lambda i: (jnp.minimum(i, n_pt - 1), layer, 0, 0, 0))
    tok_shape = jax.ShapeDtypeStruct((n, W_ATTN), F32)
    tok_bf16 = jax.ShapeDtypeStruct((n, W_ATTN), BF16)
    tok_spec = pl.BlockSpec((tm, W_ATTN), lambda i: (i, 0))
    rope_spec = pl.BlockSpec((tm, LANES), lambda i: (jnp.maximum(i - n_pt, 0) % tps, 0))
    alias_in = list(kv_prev) if kv_prev is not None else []
    n_in = 7
    return pl.pallas_call(
        functools.partial(_proj_kernel, n_alias=len(alias_in), n_pt=n_pt, seqs_per_tile=seqs_per_tile,
                          seq=st.seq),
        grid=(n // tm,),
        in_specs=[
            pl.BlockSpec((tm, D_MODEL), lambda i: (i, 0)),
            st.mod_spec(tm),
            pl.BlockSpec((None, None, 1, D_MODEL), lambda i: (layer, 1, 0, 0)),
            _layer_spec((w_t.shape[1], D_MODEL), layer),
            _layer_spec((N_GATE, 1), layer),
            rope_spec, rope_spec,
        ] + [pl.BlockSpec(memory_space=pl.ANY)] * len(alias_in),
        out_specs=[head_spec, head_spec, kv_spec, kv_spec, tok_spec, tok_spec, tok_spec, tok_spec,
                   pl.BlockSpec((W_ATTN, tm), lambda i: (0, i)),
                   pl.BlockSpec((W_MLSTM, tm), lambda i: (0, i)),
                   pl.BlockSpec((N_GATE, tm), lambda i: (0, i))],
        out_shape=[head_shape, head_shape, kv_shape, kv_shape, tok_bf16, tok_bf16, tok_bf16, tok_shape,
                   jax.ShapeDtypeStruct((W_ATTN, n), BF16),
                   jax.ShapeDtypeStruct((W_MLSTM, n), F32),
                   jax.ShapeDtypeStruct((N_GATE, n), F32)],
        input_output_aliases={n_in + j: 2 + j for j in range(len(alias_in))},
        compiler_params=_cparams(1),
        name="mixer_proj",
    )(x, mods, g_norm, w_t, bgt, cos_t, sin_t, *alias_in)


def _attn_kernel(*refs, lam_init, cached, nb, tq, seq):
    q_ref, k_ref, vt_ref = refs[:3]
    pos = 3
    if cached:
        ck_ref, cv_ref = refs[pos:pos + 2]
        pos += 2
    lam_ref, g_ref = refs[pos:pos + 2]
    o_ref = refs[-1]
    lp = lam_ref[...]
    lam = (jnp.exp(jnp.sum(lp[0:1] * lp[1:2], axis=-1, keepdims=True))
           - jnp.exp(jnp.sum(lp[2:3] * lp[3:4], axis=-1, keepdims=True)) + lam_init)
    g_sub = g_ref[...]
    lane = lax.broadcasted_iota(jnp.int32, (tq, DV_ATTN), 1)
    heads = range(N_ATTN)
    results = []
    for bi in range(nb):
        ks = slice(bi * seq, (bi + 1) * seq)
        keys = [[k_ref[h, ks] for h in heads]]
        vals_t = [[vt_ref[h * DV_ATTN:(h + 1) * DV_ATTN, ks] for h in heads]]
        if cached:
            keys.insert(0, [ck_ref[h].astype(BF16) for h in heads])
            vals_t.insert(0, [cv_ref[h].T.astype(BF16) for h in heads])
        qh = [q_ref[h, bi * tq:(bi + 1) * tq].astype(F32) * (ATTN_SCALE * LOG2E) for h in heads]
        exps, dens = [], []
        for first_map in (True, False):
            qm = [jnp.where((lane < DH_ATTN) == first_map, q, 0.0).astype(BF16) for q in qh]
            s = [jnp.concatenate([_bdot_nt(kg[h], qm[h]) for h in heads], axis=1) for kg in keys]
            m = functools.reduce(jnp.maximum, [jnp.max(si, axis=0, keepdims=True) for si in s])
            e = [jnp.exp2(si - m) for si in s]
            exps.append(e)
            dens.append(functools.reduce(jnp.add, [jnp.sum(ei, axis=0, keepdims=True) for ei in e]))
        ratio = lam * dens[0] / dens[1]
        inv = 1.0 / dens[0]
        probs = [(e1 - e2 * ratio).astype(BF16) for e1, e2 in zip(*exps)]
        for h in heads:
            cols = slice(h * tq, (h + 1) * tq)
            o_t = None
            for pj, vg in zip(probs, vals_t):
                part = jnp.dot(vg[h], pj[:, cols], preferred_element_type=F32)
                o_t = part if o_t is None else o_t + part
            o_t = o_t * inv[:, cols]
            y_t = o_t * lax.rsqrt(jnp.mean(o_t * o_t, axis=0, keepdims=True) + EPS)
            results.append((bi, h, (y_t.T * g_sub * (1.0 - lam_init)).astype(o_ref.dtype)))
    for bi, h, y in results:
        o_ref[bi * tq:(bi + 1) * tq, h * DV_ATTN:(h + 1) * DV_ATTN] = y


def _attn_four_kernel(*refs, n_attn, attn_kw, four_kw, once_per_seq):
    a_ref, f_ref = refs[-2:]
    _attn_kernel(*refs[:n_attn], a_ref, **attn_kw)
    four = functools.partial(_four_kernel, *refs[n_attn:n_attn + 4], f_ref, **four_kw)
    if once_per_seq:
        pl.when(pl.program_id(1) == 0)(four)
    else:
        four()


def _attn_four_call(q, k, v_t, cache, lam_p, g_sub, zf, outs_prev, cast_w, st, *, layer, lam_init):
    n = st.n_all
    batch, seq, nb, tq = st.dec_batch, st.dec_seq, 1, st.dec_seq // 2
    nq = seq // tq
    q0, s0 = st.n_ctx // (nb * tq), st.n_ctx // (nb * seq)
    in_specs = [pl.BlockSpec((N_ATTN, nb * tq, DV_ATTN), lambda b, i: (0, q0 + b * nq + i, 0)),
                pl.BlockSpec((N_ATTN, nb * seq, DV_ATTN), lambda b, i: (0, s0 + b, 0)),
                pl.BlockSpec((W_ATTN, nb * seq), lambda b, i: (0, s0 + b))]
    args = [q, k, v_t]
    if cache is not None:
        past = cache[0].shape[3]
        c_spec = pl.BlockSpec((None, None, N_ATTN, past, DV_ATTN), lambda b, i: (b, layer, 0, 0, 0))
        in_specs += [c_spec, c_spec]
        args += list(cache)
    in_specs += [pl.BlockSpec((None, 4, DH_ATTN), lambda b, i: (layer, 0, 0)),
                 pl.BlockSpec((None, 1, DV_ATTN), lambda b, i: (layer, 0, 0))]
    args += [lam_p, g_sub]
    n_attn = len(args)
    w_d, ct, s_t = _dft_tables(seq)
    seq_rows = pl.BlockSpec((nb * seq, W_FOUR), lambda b, i: (s0 + b, 0))
    in_specs += [seq_rows,
                 pl.BlockSpec((DG_FOUR, 2 * DG_FOUR), lambda b, i: (0, 0)),
                 pl.BlockSpec((seq, seq), lambda b, i: (0, 0)),
                 pl.BlockSpec((seq, seq), lambda b, i: (0, 0))]
    args += [zf, w_d, ct, s_t]
    aliases = {}
    if outs_prev is not None:
        aliases = {len(args): 0, len(args) + 1: 1}
        in_specs += [pl.BlockSpec(memory_space=pl.ANY)] * 2
        args += list(outs_prev)
    casts = _CastJobs(cast_w, (batch // nb) * nq, lambda b, i: b * nq + i)
    body = functools.partial(
        _attn_four_kernel, n_attn=n_attn, once_per_seq=nq > 1,
        attn_kw=dict(lam_init=lam_init, cached=cache is not None, nb=nb, tq=tq, seq=seq),
        four_kw=dict(nb=nb, seq=seq))
    return pl.pallas_call(
        casts.wrap(body, len(args), 2),
        grid=(batch // nb, nq),
        in_specs=in_specs + casts.in_specs(),
        out_specs=[pl.BlockSpec((nb * tq, W_ATTN), lambda b, i: (q0 + b * nq + i, 0)), seq_rows]
        + casts.out_specs(),
        out_shape=[jax.ShapeDtypeStruct((n, W_ATTN), BF16), jax.ShapeDtypeStruct((n, W_FOUR), BF16)]
        + casts.out_shapes(),
        input_output_aliases=aliases,
        compiler_params=_cparams(2),
        name="attn_fourier",
    )(*args, *casts.args())


def _dft_tables(seq):
    def cs(n):
        j = np.arange(n)
        ang = 2.0 * np.pi * ((j[:, None] * j[None, :]) % n) / n
        return np.cos(ang) / math.sqrt(n), np.sin(ang) / math.sqrt(n)

    cd, sd = cs(DG_FOUR)
    ct, st = cs(seq)
    w_d = jnp.asarray(np.concatenate([cd, sd], axis=1), F32)
    return w_d.astype(BF16), jnp.asarray(ct, F32).astype(BF16), jnp.asarray(-st, F32).astype(BF16)


def _four_kernel(*refs, nb, seq):
    z_ref, wd_ref, ct_ref, st_ref = refs[:4]
    o_ref = refs[-1]
    y_cos, y_sin = [], []
    for gidx in range(N_FOUR):
        y = jnp.dot(z_ref[:, gidx * DG_FOUR:(gidx + 1) * DG_FOUR], wd_ref[...], preferred_element_type=F32)
        y_cos.append(y[:, :DG_FOUR].astype(BF16))
        y_sin.append(y[:, DG_FOUR:].astype(BF16))
    y_cos = jnp.concatenate(y_cos, axis=1)
    y_sin = jnp.concatenate(y_sin, axis=1)
    for s in range(nb):
        rows = slice(s * seq, (s + 1) * seq)
        o_ref[rows, :] = (jnp.dot(ct_ref[...], y_cos[rows], preferred_element_type=F32)
                          + jnp.dot(st_ref[...], y_sin[rows], preferred_element_type=F32)).astype(o_ref.dtype)


def _mlstm_kernel(*refs, seq, nb, seeded, emit_state, n_alias):
    mq_ref, mk_ref, vt_ref, mo_ref, gt_ref, g_ref = refs[:6]
    pos = 6
    if seeded:
        c0_ref, n0_ref, m0_ref = refs[pos:pos + 3]
        pos += 3
    pos += n_alias
    o_ref = refs[pos]
    if emit_state:
        c1_ref, n1_ref, m1_ref = refs[pos + 1:pos + 4]

    L = min(MLSTM_CHUNK, seq)
    nc = seq // L
    s_idx = lax.broadcasted_iota(jnp.int32, (L, L), 0)
    t_idx = lax.broadcasted_iota(jnp.int32, (L, L), 1)
    before = (s_idx <= t_idx, s_idx >= t_idx)
    tri = (jnp.where(before[1], 1.0, 0.0).astype(BF16),
           jnp.where(before[0], 1.0, 0.0).astype(BF16))
    chains = [(bi, d, h) for bi in range(nb) for d in range(2) for h in range(N_MLSTM)]

    state = {}
    for (bi, d, h) in chains:
        if seeded:
            state[bi, d, h] = (c0_ref[bi, d, h], n0_ref[bi, d, h:h + 1, :], m0_ref[bi, d:d + 1, h:h + 1])
        else:
            state[bi, d, h] = (None, None, jnp.zeros((1, 1), F32))

    terms = {}
    for c in range(nc):
        for bi in range(nb):
            for d in range(2):
                r0 = bi * seq + (c if d == 0 else nc - 1 - c) * L
                g_t = gt_ref[:, r0:r0 + L]
                hi = g_t.astype(BF16)
                rem = g_t - hi.astype(F32)
                mid = rem.astype(BF16)
                lo = (rem - mid.astype(F32)).astype(BF16)
                pieces = _bdot_nt(jnp.concatenate([hi, mid, lo], axis=0), tri[d])
                cum_t = pieces[0:N_GATE] + pieces[N_GATE:2 * N_GATE] + pieces[2 * N_GATE:]
                c_t = (g_t - pltpu.roll(cum_t, N_GATE - N_MLSTM, 0)) * LOG2E
                col = jnp.concatenate([c_t, jnp.zeros((LANES - N_GATE, L), F32)], axis=0).T
                terms[c, bi, d] = (r0, col, g_t, cum_t)

    h_t = {}
    for c in range(nc):
        new_state = {}
        need_update = emit_state or c < nc - 1
        group = [(d, h) for d in range(2) for h in range(N_MLSTM)]
        for bi in range(nb):
            has_state = state[bi, 0, 0][0] is not None
            qs, ks, vts, cbs, s0s, i_rows, b_rows, m_prevs = [], [], [], [], [], [], [], []
            for d, h in group:
                r0, col, g_t, cum_t = terms[c, bi, d]
                ci = 2 * d * N_MLSTM + h
                cf = ci + N_MLSTM
                hs = slice(h * DH_MLSTM, (h + 1) * DH_MLSTM)
                qs.append(mq_ref[r0:r0 + L, hs])
                ks.append(mk_ref[r0:r0 + L, hs])
                vts.append(vt_ref[hs, r0:r0 + L])
                cbs.append(jnp.where(before[d], col[:, ci:ci + 1], -jnp.inf))
                s0s.append(_bdot_nt(ks[-1], qs[-1]))
                i_rows.append(g_t[ci:ci + 1, :])
                b_rows.append(cum_t[cf:cf + 1, :])
                m_prevs.append(jnp.broadcast_to(state[bi, d, h][2], (1, L)))
            cb = jnp.concatenate(cbs, axis=1)
            i_row, b_row = jnp.concatenate(i_rows, axis=1), jnp.concatenate(b_rows, axis=1)
            m_prev = jnp.concatenate(m_prevs, axis=1)
            m2_prev = m_prev * LOG2E
            m2_row = jnp.maximum(jnp.max(cb, axis=0, keepdims=True), m2_prev)
            s_t = jnp.concatenate(s0s, axis=1) * jnp.exp2(cb - m2_row)
            den = jnp.sum(s_t, axis=0, keepdims=True)
            s_bf = s_t.astype(BF16)
            m_t = b_row + m2_row * LN2
            floor = jnp.exp(-m_t)
            if has_state:
                sp = jnp.exp2(m2_prev - m2_row)
            if need_update:
                tot, new = [], []
                for j, (d, h) in enumerate(group):
                    last = j * L + (L - 1 if d == 0 else 0)
                    tot.append(jnp.broadcast_to(b_row[:, last:last + 1], (1, L)))
                    new.append(jnp.broadcast_to(m_t[:, last:last + 1], (1, L)))
                b_tot, m_new = jnp.concatenate(tot, axis=1), jnp.concatenate(new, axis=1)
                wl = jnp.exp(b_tot + (i_row - b_row) - m_new)
                if has_state:
                    decay = jnp.exp(b_tot + m_prev - m_new)
            for j, (d, h) in enumerate(group):
                cols = slice(j * L, (j + 1) * L)
                c_prev, n_prev, _ = state[bi, d, h]
                num_t = jnp.dot(vts[j].astype(BF16), s_bf[:, cols], preferred_element_type=F32)
                den_j = den[:, cols]
                if has_state:
                    cn = jnp.concatenate([c_prev, jnp.broadcast_to(n_prev, (SUBLANES, DH_MLSTM))], axis=0)
                    cq = _bdot_nt(cn, qs[j])
                    num_t = num_t + sp[:, cols] * cq[:DH_MLSTM]
                    den_j = den_j + sp[:, cols] * cq[DH_MLSTM:DH_MLSTM + 1]
                h_t[bi, d, h, c] = num_t / jnp.maximum(jnp.abs(den_j), floor[:, cols])
                if need_update:
                    wl_j = wl[:, cols]
                    vw = jnp.concatenate([vts[j] * wl_j, jnp.broadcast_to(wl_j, (SUBLANES, L))], axis=0)
                    upd = jnp.dot(vw.astype(BF16), ks[j], preferred_element_type=F32)
                    c_new, n_new = upd[:DH_MLSTM], upd[DH_MLSTM:DH_MLSTM + 1]
                    if has_state:
                        decay_j = decay[:, j * L:j * L + 1]
                        c_new = decay_j * c_prev + c_new
                        n_new = decay_j * n_prev + n_new
                    new_state[bi, d, h] = (c_new, n_new, m_new[:, j * L:j * L + 1])
        state = new_state

    g_m = g_ref[...]
    for bi in range(nb):
        for h in range(N_MLSTM):
            hs = slice(h * DH_MLSTM, (h + 1) * DH_MLSTM)
            fwd = [h_t[bi, 0, h, c] for c in range(nc)]
            bwd = [h_t[bi, 1, h, nc - 1 - c] for c in range(nc)]
            hsum = (fwd[0] if nc == 1 else jnp.concatenate(fwd, axis=1)) \
                + (bwd[0] if nc == 1 else jnp.concatenate(bwd, axis=1))
            y = hsum * lax.rsqrt(jnp.mean(hsum * hsum, axis=0, keepdims=True) + EPS)
            rows = slice(bi * seq, (bi + 1) * seq)
            o_ref[rows, hs] = (y.T * g_m * _sigmoid(mo_ref[rows, hs])).astype(o_ref.dtype)

    if emit_state:
        for (bi, d, h) in chains:
            c_fin, n_fin, m_fin = state[bi, d, h]
            c1_ref[bi, d, h] = c_fin
            n1_ref[bi, d, h:h + 1, :] = n_fin
            m1_ref[bi, d:d + 1, h:h + 1] = m_fin


def _mlstm_call(mq, mk, vt, mo, gates_t, g_m, state, out_prev, cast_w, st, *, layer):
    batch, seq = st.dec_batch, st.dec_seq
    b0 = st.n_ctx // seq
    tok_spec = pl.BlockSpec((seq, W_MLSTM), lambda b: (b0 + b, 0))
    in_specs = [tok_spec, tok_spec,
                pl.BlockSpec((W_MLSTM, seq), lambda b: (0, b0 + b)),
                tok_spec,
                pl.BlockSpec((N_GATE, seq), lambda b: (0, b0 + b)),
                pl.BlockSpec((None, 1, DH_MLSTM), lambda b: (layer, 0, 0)),
                pl.BlockSpec((1, None, 2, N_MLSTM, DH_MLSTM, DH_MLSTM), lambda b: (b, layer, 0, 0, 0, 0)),
                pl.BlockSpec((1, None, 2, N_MLSTM, DH_MLSTM), lambda b: (b, layer, 0, 0, 0)),
                pl.BlockSpec((1, None, 2, N_MLSTM), lambda b: (b, layer, 0, 0)),
                pl.BlockSpec(memory_space=pl.ANY)]
    args = [mq, mk, vt, mo, gates_t, g_m, *state, out_prev]
    casts = _CastJobs(cast_w, batch, lambda b: b)
    body = functools.partial(_mlstm_kernel, seq=seq, nb=1, seeded=True, emit_state=False, n_alias=1)
    return pl.pallas_call(
        casts.wrap(body, len(args), 1),
        grid=(batch,),
        in_specs=in_specs + casts.in_specs(),
        out_specs=[tok_spec] + casts.out_specs(),
        out_shape=[jax.ShapeDtypeStruct((st.n_all, W_MLSTM), BF16)] + casts.out_shapes(),
        input_output_aliases={len(args) - 1: 0},
        compiler_params=_cparams(1),
        name="mlstm",
    )(*args, *casts.args())


def _ctx_mixer_kernel(*refs, n_attn, attn_kw, four_kw, mlstm_kw):
    a_ref, f_ref = refs[-6:-4]
    _attn_kernel(*refs[:n_attn], a_ref, **attn_kw)
    _four_kernel(*refs[n_attn:n_attn + 4], f_ref, **four_kw)
    _mlstm_kernel(*refs[n_attn + 4:n_attn + 10], *refs[-4:], **mlstm_kw)


def _ctx_mixer_call(q, k, v_t, lam_p, g_sub, zf, mq, mk, vt, mo, gates_t, g_m, states_prev, cast_w, st, *,
                    layer, lam_init):
    n, seq, nb = st.n_all, st.seq, 4
    rows = nb * seq
    heads = pl.BlockSpec((N_ATTN, rows, DV_ATTN), lambda b: (0, b, 0))
    tok = pl.BlockSpec((rows, W_ATTN), lambda b: (b, 0))
    w_d, ct, s_t = _dft_tables(seq)
    in_specs = [heads, heads,
                pl.BlockSpec((W_ATTN, rows), lambda b: (0, b)),
                pl.BlockSpec((None, 4, DH_ATTN), lambda b: (layer, 0, 0)),
                pl.BlockSpec((None, 1, DV_ATTN), lambda b: (layer, 0, 0)),
                tok,
                pl.BlockSpec((DG_FOUR, 2 * DG_FOUR), lambda b: (0, 0)),
                pl.BlockSpec((seq, seq), lambda b: (0, 0)),
                pl.BlockSpec((seq, seq), lambda b: (0, 0)),
                tok, tok,
                pl.BlockSpec((W_MLSTM, rows), lambda b: (0, b)),
                tok,
                pl.BlockSpec((N_GATE, rows), lambda b: (0, b)),
                pl.BlockSpec((None, 1, DH_MLSTM), lambda b: (layer, 0, 0))]
    args = [q, k, v_t, lam_p, g_sub, zf, w_d, ct, s_t, mq, mk, vt, mo, gates_t, g_m]
    n_attn = 5
    state_specs = [
        pl.BlockSpec((nb, None, 2, N_MLSTM, DH_MLSTM, DH_MLSTM), lambda b: (b, layer, 0, 0, 0, 0)),
        pl.BlockSpec((nb, None, 2, N_MLSTM, DH_MLSTM), lambda b: (b, layer, 0, 0, 0)),
        pl.BlockSpec((nb, None, 2, N_MLSTM), lambda b: (b, layer, 0, 0)),
    ]
    aliases = {}
    if states_prev is not None:
        aliases = {len(args) + j: 3 + j for j in range(3)}
        in_specs += [pl.BlockSpec(memory_space=pl.ANY)] * 3
        args += list(states_prev)
    casts = _CastJobs(cast_w, st.batch // nb, lambda b: b)
    body = functools.partial(
        _ctx_mixer_kernel, n_attn=n_attn,
        attn_kw=dict(lam_init=lam_init, cached=False, nb=nb, tq=seq, seq=seq),
        four_kw=dict(nb=nb, seq=seq),
        mlstm_kw=dict(seq=seq, nb=nb, seeded=False, emit_state=True, n_alias=0))
    return pl.pallas_call(
        casts.wrap(body, len(args), 6),
        grid=(st.batch // nb,),
        in_specs=in_specs + casts.in_specs(),
        out_specs=[tok, tok, tok] + state_specs + casts.out_specs(),
        out_shape=[jax.ShapeDtypeStruct((n, W_ATTN), BF16), jax.ShapeDtypeStruct((n, W_FOUR), BF16),
                   jax.ShapeDtypeStruct((n, W_MLSTM), BF16),
                   jax.ShapeDtypeStruct((st.batch, DEPTH, 2, N_MLSTM, DH_MLSTM, DH_MLSTM), F32),
                   jax.ShapeDtypeStruct((st.batch, DEPTH, 2, N_MLSTM, DH_MLSTM), F32),
                   jax.ShapeDtypeStruct((st.batch, DEPTH, 2, N_MLSTM), F32)] + casts.out_shapes(),
        input_output_aliases=aliases,
        compiler_params=_cparams(1),
        name="ctx_mixer",
    )(*args, *casts.args())


def _merge_kernel(x_ref, a_ref, f_ref, m_ref, mod_ref, g_ref, wg_ref, wa_ref, wf_ref, wm_ref, wo_ref, o_ref):
    x = x_ref[...]
    mod = mod_ref[...]
    u = _modulated_norm(x, g_ref[...], mod, 3).astype(BF16)
    merged = None
    for j, (br_ref, w_ref) in enumerate(((a_ref, wa_ref), (f_ref, wf_ref), (m_ref, wm_ref))):
        gate = _sigmoid(jnp.dot(u, wg_ref[:, j * D_MODEL:(j + 1) * D_MODEL], preferred_element_type=F32))
        term = gate * _bdot(br_ref[...], w_ref[...])
        merged = term if merged is None else merged + term
    out = _bdot(merged, wo_ref[...])
    o_ref[...] = x + mod[5:6] * out


def _merge_call(x, a, f, m, mods, g_norm, wg, wa, wf, wm, wo, st, *, layer):
    tm = TM_WIDE
    n = st.n_all
    br_spec = pl.BlockSpec((tm, W_ATTN), lambda i: (i, 0))
    return pl.pallas_call(
        _merge_kernel,
        grid=(n // tm,),
        in_specs=[
            pl.BlockSpec((tm, D_MODEL), lambda i: (i, 0)),
            br_spec, br_spec, br_spec,
            st.mod_spec(tm),
            pl.BlockSpec((None, None, 1, D_MODEL), lambda i: (layer, 1, 0, 0)),
            _layer_spec((D_MODEL, 3 * D_MODEL), None),
            _layer_spec((W_ATTN, D_MODEL), None), _layer_spec((W_FOUR, D_MODEL), None),
            _layer_spec((W_MLSTM, D_MODEL), None),
            _layer_spec((D_MODEL, D_MODEL), None),
        ],
        out_specs=pl.BlockSpec((tm, D_MODEL), lambda i: (i, 0)),
        out_shape=jax.ShapeDtypeStruct((n, D_MODEL), F32),
        compiler_params=_cparams(1),
        name="merge",
    )(x, a, f, m, mods, g_norm, wg, wa, wf, wm, wo)


def _rope_tables(n_tok):
    tok = np.arange(n_tok)
    inv = ROPE_BASE ** (-np.arange(ROPE_AXIS_PAIRS, dtype=np.float32) / ROPE_AXIS_PAIRS)
    ang = np.concatenate([(tok // GRID_W).astype(np.float32)[:, None] * inv,
                          (tok % GRID_W).astype(np.float32)[:, None] * inv], axis=-1).astype(np.float32)
    c, s = np.cos(ang), np.sin(ang)
    cos_t = np.concatenate([c, c, c, c], axis=-1)
    sin_t = np.concatenate([-s, s, -s, s], axis=-1)
    return jnp.asarray(cos_t, F32), jnp.asarray(sin_t, F32)


def kernel(x_prompt, x_sample, cache_k, cache_v, state_C, state_n, state_m, c, c_ctx, w_ada, b_ada, g_norm,
           w_ffn1_in, w_ffn1_out, w_ffn2_in, w_ffn2_out, w_in, b_mgate, attn_lambda, g_attn_sub, g_mlstm,
           w_branch_gate, w_br_attn, w_br_four, w_br_mlstm, w_out, g_final):
    batch, seq, _ = x_prompt.shape
    dec_batch, dec_seq, _ = x_sample.shape
    st = _Stream(batch, seq, dec_batch, dec_seq)
    cond = jnp.zeros((N_COND, D_MODEL), F32).at[0].set(c_ctx).at[1:1 + dec_batch].set(c)
    mods_flat, *ffn1_w = _ada_call(cond, w_ada, b_ada, [(w_ffn1_in, 0), (w_ffn1_out, 0)])
    mods_all = mods_flat.reshape(DEPTH, N_COND, N_MOD, D_MODEL)

    b_gate_t = b_mgate[:, :, None]
    g_norm4 = g_norm[:, :, None, :]
    g_fin = g_final[None, :]
    w_t = jnp.swapaxes(w_in, 1, 2).astype(BF16)
    g_sub3 = g_attn_sub[:, None, :]
    g_m3 = g_mlstm[:, None, :]
    cos_t, sin_t = _rope_tables(dec_seq)
    lat_state = (state_C, state_n, state_m)

    xs = (x_prompt.reshape(batch * seq, D_MODEL), x_sample.reshape(dec_batch * dec_seq, D_MODEL))
    kv, states = None, None
    for l in range(DEPTH):
        mods = mods_all[l]
        lam_init = 0.8 - 0.6 * math.exp(-0.3 * l)
        nxt = l + 1 < DEPTH
        (x,) = _ffn_call(xs, mods, g_norm4, g_fin, *ffn1_w, st, layer=l, sub=0, final=False)
        q, k, ck, cv, zf, mq, mk, mo, vat, vt, gates_t = _proj_call(
            x, mods, g_norm4, w_t, b_gate_t, cos_t, sin_t, kv, st, layer=l)
        kv = (ck, cv)
        a, f, m, c_fin, n_fin, m_fin, ffn2_in, *nxt_out = _ctx_mixer_call(
            q, k, vat, attn_lambda, g_sub3, zf, mq, mk, vt, mo, gates_t, g_m3, states,
            [(w_ffn2_in, l)] + ([(w_ffn1_out, l + 1)] if nxt else []), st, layer=l, lam_init=lam_init)
        states = [c_fin, n_fin, m_fin]
        a, f, ffn2_out, w_bg, *nxt_in = _attn_four_call(
            q, k, vat, (cache_k, cache_v), attn_lambda, g_sub3, zf, (a, f),
            [(w_ffn2_out, l), (w_branch_gate, l)] + ([(w_ffn1_in, l + 1)] if nxt else []), st,
            layer=l, lam_init=lam_init)
        m, w_ba, w_bf, w_bm, w_o = _mlstm_call(
            mq, mk, vt, mo, gates_t, g_m3, lat_state, m,
            [(w_br_attn, l), (w_br_four, l), (w_br_mlstm, l), (w_out, l)], st, layer=l)
        x = _merge_call(x, a, f, m, mods, g_norm4, w_bg, w_ba, w_bf, w_bm, w_o, st, layer=l)
        xs = tuple(_ffn_call((x,), mods, g_norm4, g_fin, ffn2_in, ffn2_out, st, layer=l, sub=2,
                             final=(l == DEPTH - 1)))
        if nxt:
            ffn1_w = (nxt_in[0], nxt_out[0])
    y_prompt, y_sample = xs
    return (y_prompt.reshape(batch, seq, D_MODEL), y_sample.reshape(dec_batch, dec_seq, D_MODEL),
            *kv, *states)
```

```python
import functools
import math

import numpy as np
import jax
import jax.numpy as jnp
from jax import lax
from jax.experimental import pallas as pl
from jax.experimental.pallas import tpu as pltpu

D_MODEL = 1024
DEPTH = 2
GRID_W = 64
N_ATTN = 4
DH_ATTN = 64
DV_ATTN = 2 * DH_ATTN
W_ATTN = N_ATTN * DV_ATTN
N_FOUR = 4
DG_FOUR = 128
W_FOUR = N_FOUR * DG_FOUR
N_MLSTM = 4
DH_MLSTM = 128
W_MLSTM = N_MLSTM * DH_MLSTM
N_GATE = 4 * N_MLSTM
LANES = 128
SUBLANES = 16
MV_LO = 3 * W_ATTN + W_FOUR + 2 * W_MLSTM
MV_HI = MV_LO + W_MLSTM
P_MAIN = MV_HI + W_MLSTM
D_FF = 2816
N_MOD = 9
N_COND = 8
ROPE_BASE = 10000.0
ROPE_AXIS_PAIRS = DH_ATTN // 4
ATTN_SCALE = DH_ATTN ** -0.5
LOG2E = 1.4426950408889634
LN2 = 0.6931471805599453
MLSTM_K_SCALE = DH_MLSTM ** -0.5
EPS = 1e-6
MLSTM_CHUNK = 256
VMEM_LIMIT = 56 * 1024 * 1024
VMEM_LIMIT_FFN = 60 * 1024 * 1024
TM_WIDE = 1024
TM_PROJ = 1024

F32 = jnp.float32
BF16 = jnp.bfloat16


def _cparams(n_grid, vmem_limit=VMEM_LIMIT):
    return pltpu.CompilerParams(dimension_semantics=("arbitrary",) * n_grid,
                                vmem_limit_bytes=vmem_limit)


def _bdot(a, b):
    return jnp.dot(a.astype(BF16), b.astype(BF16), preferred_element_type=F32)


def _bdot_nt(a, b):
    return lax.dot_general(a.astype(BF16), b.astype(BF16), (((1,), (1,)), ((), ())),
                           preferred_element_type=F32)


def _sigmoid(x):
    return 1.0 / (1.0 + jnp.exp(-x))


def _log_sigmoid(x):
    return jnp.minimum(x, 0.0) - jnp.log1p(jnp.exp(-jnp.abs(x)))


def _rms(x, g):
    return x * lax.rsqrt(jnp.mean(x * x, axis=-1, keepdims=True) + EPS) * g


def _modulated_norm(x, g, mod, base):
    gain = g * (1.0 + mod[base + 1:base + 2])
    return x * lax.rsqrt(jnp.mean(x * x, axis=-1, keepdims=True) + EPS) * gain + mod[base:base + 1]


def _layer_spec(shape, layer):
    if layer is None:
        return pl.BlockSpec(shape, lambda *_: (0,) * len(shape), pipeline_mode=pl.Buffered(1))
    return pl.BlockSpec((None,) + shape, lambda *_: (layer,) + (0,) * len(shape),
                        pipeline_mode=pl.Buffered(1))


class _CastJobs:
    def __init__(self, jobs, steps, step_index):
        self.jobs, self.steps, self.step_index = list(jobs), steps, step_index

    def __len__(self):
        return len(self.jobs)

    def _rows(self, w):
        rows = w.shape[1] // self.steps
        if rows * self.steps != w.shape[1] or rows % SUBLANES:
            raise ValueError("weight rows do not split into bf16 row tiles over the grid")
        return rows

    def in_specs(self):
        return [pl.BlockSpec((None, self._rows(w), w.shape[2]),
                             lambda *g, layer=layer: (layer, self.step_index(*g), 0))
                for w, layer in self.jobs]

    def out_specs(self):
        return [pl.BlockSpec((self._rows(w), w.shape[2]), lambda *g: (self.step_index(*g), 0))
                for w, _ in self.jobs]

    def out_shapes(self):
        return [jax.ShapeDtypeStruct(w.shape[1:], BF16) for w, _ in self.jobs]

    def args(self):
        return [w for w, _ in self.jobs]

    def wrap(self, body, n_in, n_out):
        n = len(self.jobs)
        if n == 0:
            return body

        def kernel(*refs):
            ins, cast_in = refs[:n_in], refs[n_in:n_in + n]
            outs = refs[n_in + n:n_in + n + n_out]
            cast_out = refs[n_in + n + n_out:n_in + 2 * n + n_out]
            for src, dst in zip(cast_in, cast_out):
                dst[...] = src[...].astype(dst.dtype)
            body(*ins, *outs, *refs[n_in + 2 * n + n_out:])

        return kernel


class _Stream:
    def __init__(self, batch, seq, dec_batch, dec_seq):
        self.batch, self.seq, self.dec_batch, self.dec_seq = batch, seq, dec_batch, dec_seq
        self.n_ctx = batch * seq
        self.n_all = self.n_ctx + dec_batch * dec_seq

    def ctx_tiles(self, tm):
        if self.n_ctx % tm or self.dec_seq % tm:
            raise ValueError("row tile must divide the context rows and one latent sequence")
        return self.n_ctx // tm

    def mod_spec(self, tm):
        n_pt, tps = self.ctx_tiles(tm), self.dec_seq // tm
        return pl.BlockSpec((None, N_MOD, D_MODEL),
                            lambda i: (jnp.where(i < n_pt, 0, 1 + (i - n_pt) // tps), 0, 0))


def _ada_kernel(c_ref, w_ref, b_ref, o_ref):
    c = c_ref[...]
    s = c * _sigmoid(c)
    o_ref[...] = _bdot(s, w_ref[...]) + b_ref[...]


def _ada_call(cond, w_ada, b_ada, cast_w):
    tn = 2304
    n_out = N_MOD * D_MODEL
    nj = n_out // tn
    casts = _CastJobs(cast_w, DEPTH * nj, lambda l, j: l * nj + j)
    return pl.pallas_call(
        casts.wrap(_ada_kernel, 3, 1),
        grid=(DEPTH, nj),
        in_specs=[
            pl.BlockSpec((N_COND, D_MODEL), lambda l, j: (0, 0)),
            pl.BlockSpec((None, D_MODEL, tn), lambda l, j: (l, 0, j)),
            pl.BlockSpec((None, 1, tn), lambda l, j: (l, 0, j)),
        ] + casts.in_specs(),
        out_specs=[pl.BlockSpec((None, N_COND, tn), lambda l, j: (l, 0, j))] + casts.out_specs(),
        out_shape=[jax.ShapeDtypeStruct((DEPTH, N_COND, n_out), F32)] + casts.out_shapes(),
        compiler_params=_cparams(2),
        name="adaln",
    )(cond, w_ada, b_ada.reshape(DEPTH, 1, n_out), *casts.args())


FF_CHUNKS = (1024, 1024, 768)


def _ffn_weight_copies(win_hbm, wout_hbm, win_ref, wout_ref, sems):
    groups, lo = [], 0
    for c, width in enumerate(FF_CHUNKS):
        groups.append((
            pltpu.make_async_copy(win_hbm.at[:, lo:lo + width], win_ref.at[:, lo:lo + width], sems.at[3 * c]),
            pltpu.make_async_copy(win_hbm.at[:, D_FF + lo:D_FF + lo + width],
                                  win_ref.at[:, D_FF + lo:D_FF + lo + width], sems.at[3 * c + 1]),
            pltpu.make_async_copy(wout_hbm.at[lo:lo + width, :], wout_ref.at[lo:lo + width, :],
                                  sems.at[3 * c + 2])))
        lo += width
    return groups


def _ffn_kernel(*refs, base, n_x, final, n_pt, n_out):
    x_refs = refs[:n_x]
    mod_ref, g_ref, gf_ref, win_hbm, wout_hbm = refs[n_x:n_x + 5]
    o_refs = refs[n_x + 5:n_x + 5 + n_out]
    win_ref, wout_ref, sems = refs[n_x + 5 + n_out:]
    copies = _ffn_weight_copies(win_hbm, wout_hbm, win_ref, wout_ref, sems)
    for group in copies:
        for cp in group:
            cp.start()
    i = pl.program_id(0)
    if n_x == 2:
        x = jnp.where(i < n_pt, x_refs[0][...], x_refs[1][...])
    else:
        x = x_refs[0][...]
    mod = mod_ref[...]
    u = _modulated_norm(x, g_ref[...], mod, base).astype(BF16)
    y = None
    lo = 0
    for c, width in enumerate(FF_CHUNKS):
        for cp in copies[c]:
            cp.wait()
        a = jnp.dot(u, win_ref[:, lo:lo + width], preferred_element_type=F32)
        g = jnp.dot(u, win_ref[:, D_FF + lo:D_FF + lo + width], preferred_element_type=F32)
        hh = (a * _sigmoid(a) * g).astype(BF16)
        part = jnp.dot(hh, wout_ref[lo:lo + width, :], preferred_element_type=F32)
        y = part if y is None else y + part
        lo += width
    xn = x + 0.5 * mod[base + 2:base + 3] * y
    if not final:
        o_refs[0][...] = xn
    else:
        xn = _rms(xn, gf_ref[...])

        @pl.when(i < n_pt)
        def _():
            o_refs[0][...] = xn

        @pl.when(i >= n_pt)
        def _():
            o_refs[1][...] = xn


def _ffn_call(xs, mods, g_norm, g_final, w_in, w_out, st, *, layer, sub, final):
    tm = TM_WIDE
    n_pt = st.ctx_tiles(tm)
    n_tiles = st.n_all // tm
    ctx_spec = pl.BlockSpec((tm, D_MODEL), lambda i: (jnp.minimum(i, n_pt - 1), 0))
    lat_spec = pl.BlockSpec((tm, D_MODEL), lambda i: (jnp.maximum(i - n_pt, 0), 0))
    all_spec = pl.BlockSpec((tm, D_MODEL), lambda i: (i, 0))
    x_specs = [ctx_spec, lat_spec] if len(xs) == 2 else [all_spec]
    if final:
        out_specs = [ctx_spec, lat_spec]
        out_shape = [jax.ShapeDtypeStruct((st.n_ctx, D_MODEL), F32),
                     jax.ShapeDtypeStruct((st.n_all - st.n_ctx, D_MODEL), F32)]
    else:
        out_specs = [all_spec]
        out_shape = [jax.ShapeDtypeStruct((st.n_all, D_MODEL), F32)]
    return pl.pallas_call(
        functools.partial(_ffn_kernel, base=3 * sub, n_x=len(xs), final=final, n_pt=n_pt,
                          n_out=len(out_specs)),
        grid=(n_tiles,),
        in_specs=x_specs + [
            st.mod_spec(tm),
            pl.BlockSpec((None, None, 1, D_MODEL), lambda i: (layer, sub, 0, 0)),
            pl.BlockSpec((1, D_MODEL), lambda i: (0, 0)),
            pl.BlockSpec(memory_space=pl.ANY),
            pl.BlockSpec(memory_space=pl.ANY),
        ],
        out_specs=out_specs,
        out_shape=out_shape,
        scratch_shapes=[pltpu.VMEM((D_MODEL, 2 * D_FF), BF16), pltpu.VMEM((D_FF, D_MODEL), BF16),
                        pltpu.SemaphoreType.DMA((3 * len(FF_CHUNKS),))],
        compiler_params=_cparams(1, VMEM_LIMIT_FFN),
        name="ffn",
    )(*xs, mods, g_norm, g_final, w_in, w_out)


def _swap32(x):
    lane = lax.broadcasted_iota(jnp.int32, x.shape, 1)
    return jnp.where((lane & (DH_ATTN // 2)) == 0,
                     pltpu.roll(x, LANES - DH_ATTN // 2, 1), pltpu.roll(x, DH_ATTN // 2, 1))


def _proj_kernel(*refs, n_alias, n_pt, seqs_per_tile, seq):
    (x_ref, mod_ref, g_ref, wt_ref, bgt_ref, cos_ref, sin_ref) = refs[:7]
    (q_ref, k_ref, ck_ref, cv_ref, zf_ref, mq_ref, mk_ref, mo_ref, vat_ref, vt_ref, gt_ref) = refs[7 + n_alias:]
    i = pl.program_id(0)
    is_latent = i >= n_pt
    x = x_ref[...]
    u = _modulated_norm(x, g_ref[...], mod_ref[...], 3).astype(BF16)
    zb = _bdot_nt(u, wt_ref[3 * W_ATTN:MV_LO, :])
    zf_ref[...] = zb[:, :W_FOUR].astype(zf_ref.dtype)
    off = W_FOUR
    mq_ref[...] = zb[:, off:off + W_MLSTM].astype(mq_ref.dtype)
    mk_ref[...] = (zb[:, off + W_MLSTM:off + 2 * W_MLSTM] * MLSTM_K_SCALE).astype(mk_ref.dtype)
    mo_ref[...] = _bdot_nt(u, wt_ref[MV_HI:P_MAIN, :])
    vat_ref[...] = _bdot_nt(wt_ref[2 * W_ATTN:3 * W_ATTN, :], u).astype(vat_ref.dtype)
    vt_ref[...] = _bdot_nt(wt_ref[MV_LO:MV_HI, :], u)
    gpt = _bdot_nt(wt_ref[P_MAIN:, :], u) + bgt_ref[...]
    sub = lax.broadcasted_iota(jnp.int32, gpt.shape, 0)
    gt_ref[...] = jnp.where((sub & N_MLSTM) != 0, _log_sigmoid(gpt), gpt)

    za = _bdot_nt(u, wt_ref[:3 * W_ATTN, :])
    cos_t = cos_ref[...]
    sin_t = sin_ref[...]
    plain_k = []
    for h in range(N_ATTN):
        zq = za[:, h * DV_ATTN:(h + 1) * DV_ATTN]
        zk = za[:, W_ATTN + h * DV_ATTN:W_ATTN + (h + 1) * DV_ATTN]
        plain_k.append(zk)
        q_ref[h] = jnp.where(is_latent, zq * cos_t + _swap32(zq) * sin_t, zq).astype(q_ref.dtype)
        k_ref[h] = jnp.where(is_latent, zk * cos_t + _swap32(zk) * sin_t, zk).astype(k_ref.dtype)

    @pl.when(i < n_pt)
    def _():
        for h in range(N_ATTN):
            zv = za[:, 2 * W_ATTN + h * DV_ATTN:2 * W_ATTN + (h + 1) * DV_ATTN]
            for s in range(seqs_per_tile):
                rs = slice(s * seq, (s + 1) * seq)
                ck_ref[s, h] = plain_k[h][rs]
                cv_ref[s, h] = zv[rs]


def _proj_call(x, mods, g_norm, w_t, bgt, cos_t, sin_t, kv_prev, st, *, layer):
    tm = TM_PROJ
    n_pt = st.ctx_tiles(tm)
    if tm % st.seq:
        raise ValueError("a context row tile must hold whole sequences")
    seqs_per_tile, tps = tm // st.seq, st.dec_seq // tm
    n = st.n_all
    head_shape = jax.ShapeDtypeStruct((N_ATTN, n, DV_ATTN), BF16)
    head_spec = pl.BlockSpec((N_ATTN, tm, DV_ATTN), lambda i: (0, i, 0))
    kv_shape = jax.ShapeDtypeStruct((st.batch, DEPTH, N_ATTN, st.seq, DV_ATTN), F32)
    kv_spec = pl.BlockSpec((seqs_per_tile, None, N_ATTN, st.seq, DV_ATTN),
                           lambda i: (jnp.minimum(i, n_pt - 1), layer, 0, 0, 0))
    tok_shape = jax.ShapeDtypeStruct((n, W_ATTN), F32)
    tok_bf16 = jax.ShapeDtypeStruct((n, W_ATTN), BF16)
    tok_spec = pl.BlockSpec((tm, W_ATTN), lambda i: (i, 0))
    rope_spec = pl.BlockSpec((tm, LANES), lambda i: (jnp.maximum(i - n_pt, 0) % tps, 0))
    alias_in = list(kv_prev) if kv_prev is not None else []
    n_in = 7
    return pl.pallas_call(
        functools.partial(_proj_kernel, n_alias=len(alias_in), n_pt=n_pt, seqs_per_tile=seqs_per_tile,
                          seq=st.seq),
        grid=(n // tm,),
        in_specs=[
            pl.BlockSpec((tm, D_MODEL), lambda i: (i, 0)),
            st.mod_spec(tm),
            pl.BlockSpec((None, None, 1, D_MODEL), lambda i: (layer, 1, 0, 0)),
            _layer_spec((w_t.shape[1], D_MODEL), layer),
            _layer_spec((N_GATE, 1), layer),
            rope_spec, rope_spec,
        ] + [pl.BlockSpec(memory_space=pl.ANY)] * len(alias_in),
        out_specs=[head_spec, head_spec, kv_spec, kv_spec, tok_spec, tok_spec, tok_spec, tok_spec,
                   pl.BlockSpec((W_ATTN, tm), lambda i: (0, i)),
                   pl.BlockSpec((W_MLSTM, tm), lambda i: (0, i)),
                   pl.BlockSpec((N_GATE, tm), lambda i: (0, i))],
        out_shape=[head_shape, head_shape, kv_shape, kv_shape, tok_bf16, tok_bf16, tok_bf16, tok_shape,
                   jax.ShapeDtypeStruct((W_ATTN, n), BF16),
                   jax.ShapeDtypeStruct((W_MLSTM, n), F32),
                   jax.ShapeDtypeStruct((N_GATE, n), F32)],
        input_output_aliases={n_in + j: 2 + j for j in range(len(alias_in))},
        compiler_params=_cparams(1),
        name="mixer_proj",
    )(x, mods, g_norm, w_t, bgt, cos_t, sin_t, *alias_in)


def _attn_kernel(*refs, lam_init, cached, nb, tq, seq):
    q_ref, k_ref, vt_ref = refs[:3]
    pos = 3
    if cached:
        ck_ref, cv_ref = refs[pos:pos + 2]
        pos += 2
    lam_ref, g_ref = refs[pos:pos + 2]
    o_ref = refs[-1]
    lp = lam_ref[...]
    lam = (jnp.exp(jnp.sum(lp[0:1] * lp[1:2], axis=-1, keepdims=True))
           - jnp.exp(jnp.sum(lp[2:3] * lp[3:4], axis=-1, keepdims=True)) + lam_init)
    g_sub = g_ref[...]
    lane = lax.broadcasted_iota(jnp.int32, (tq, DV_ATTN), 1)
    heads = range(N_ATTN)
    results = []
    for bi in range(nb):
        ks = slice(bi * seq, (bi + 1) * seq)
        keys = [[k_ref[h, ks] for h in heads]]
        vals_t = [[vt_ref[h * DV_ATTN:(h + 1) * DV_ATTN, ks] for h in heads]]
        if cached:
            keys.insert(0, [ck_ref[h].astype(BF16) for h in heads])
            vals_t.insert(0, [cv_ref[h].T.astype(BF16) for h in heads])
        qh = [q_ref[h, bi * tq:(bi + 1) * tq].astype(F32) * (ATTN_SCALE * LOG2E) for h in heads]
        exps, dens = [], []
        for first_map in (True, False):
            qm = [jnp.where((lane < DH_ATTN) == first_map, q, 0.0).astype(BF16) for q in qh]
            s = [jnp.concatenate([_bdot_nt(kg[h], qm[h]) for h in heads], axis=1) for kg in keys]
            m = functools.reduce(jnp.maximum, [jnp.max(si, axis=0, keepdims=True) for si in s])
            e = [jnp.exp2(si - m) for si in s]
            exps.append(e)
            dens.append(functools.reduce(jnp.add, [jnp.sum(ei, axis=0, keepdims=True) for ei in e]))
        ratio = lam * dens[0] / dens[1]
        inv = 1.0 / dens[0]
        probs = [(e1 - e2 * ratio).astype(BF16) for e1, e2 in zip(*exps)]
        for h in heads:
            cols = slice(h * tq, (h + 1) * tq)
            o_t = None
            for pj, vg in zip(probs, vals_t):
                part = jnp.dot(vg[h], pj[:, cols], preferred_element_type=F32)
                o_t = part if o_t is None else o_t + part
            o_t = o_t * inv[:, cols]
            y_t = o_t * lax.rsqrt(jnp.mean(o_t * o_t, axis=0, keepdims=True) + EPS)
            results.append((bi, h, (y_t.T * g_sub * (1.0 - lam_init)).astype(o_ref.dtype)))
    for bi, h, y in results:
        o_ref[bi * tq:(bi + 1) * tq, h * DV_ATTN:(h + 1) * DV_ATTN] = y


def _attn_four_kernel(*refs, n_attn, attn_kw, four_kw, once_per_seq):
    a_ref, f_ref = refs[-2:]
    _attn_kernel(*refs[:n_attn], a_ref, **attn_kw)
    four = functools.partial(_four_kernel, *refs[n_attn:n_attn + 4], f_ref, **four_kw)
    if once_per_seq:
        pl.when(pl.program_id(1) == 0)(four)
    else:
        four()


def _attn_four_call(q, k, v_t, cache, lam_p, g_sub, zf, outs_prev, cast_w, st, *, layer, lam_init, latent):
    n = st.n_all
    if latent:
        batch, seq, nb, tq = st.dec_batch, st.dec_seq, 1, st.dec_seq // 2
        row0 = st.n_ctx
    else:
        batch, seq, nb, tq = st.batch, st.seq, 4, st.seq
        row0 = 0
    nq = seq // tq
    if nq > 1 and nb > 1:
        raise ValueError("query blocks of several sequences are not contiguous rows")
    q0, s0 = row0 // (nb * tq), row0 // (nb * seq)
    in_specs = [pl.BlockSpec((N_ATTN, nb * tq, DV_ATTN), lambda b, i: (0, q0 + b * nq + i, 0)),
                pl.BlockSpec((N_ATTN, nb * seq, DV_ATTN), lambda b, i: (0, s0 + b, 0)),
                pl.BlockSpec((W_ATTN, nb * seq), lambda b, i: (0, s0 + b))]
    args = [q, k, v_t]
    if cache is not None:
        past = cache[0].shape[3]
        c_spec = pl.BlockSpec((None, None, N_ATTN, past, DV_ATTN), lambda b, i: (b, layer, 0, 0, 0))
        in_specs += [c_spec, c_spec]
        args += list(cache)
    in_specs += [pl.BlockSpec((None, 4, DH_ATTN), lambda b, i: (layer, 0, 0)),
                 pl.BlockSpec((None, 1, DV_ATTN), lambda b, i: (layer, 0, 0))]
    args += [lam_p, g_sub]
    n_attn = len(args)
    w_d, ct, s_t = _dft_tables(seq)
    seq_rows = pl.BlockSpec((nb * seq, W_FOUR), lambda b, i: (s0 + b, 0))
    in_specs += [seq_rows,
                 pl.BlockSpec((DG_FOUR, 2 * DG_FOUR), lambda b, i: (0, 0)),
                 pl.BlockSpec((seq, seq), lambda b, i: (0, 0)),
                 pl.BlockSpec((seq, seq), lambda b, i: (0, 0))]
    args += [zf, w_d, ct, s_t]
    aliases = {}
    if outs_prev is not None:
        aliases = {len(args): 0, len(args) + 1: 1}
        in_specs += [pl.BlockSpec(memory_space=pl.ANY)] * 2
        args += list(outs_prev)
    casts = _CastJobs(cast_w, (batch // nb) * nq, lambda b, i: b * nq + i)
    body = functools.partial(
        _attn_four_kernel, n_attn=n_attn, once_per_seq=nq > 1,
        attn_kw=dict(lam_init=lam_init, cached=cache is not None, nb=nb, tq=tq, seq=seq),
        four_kw=dict(nb=nb, seq=seq))
    return pl.pallas_call(
        casts.wrap(body, len(args), 2),
        grid=(batch // nb, nq),
        in_specs=in_specs + casts.in_specs(),
        out_specs=[pl.BlockSpec((nb * tq, W_ATTN), lambda b, i: (q0 + b * nq + i, 0)), seq_rows]
        + casts.out_specs(),
        out_shape=[jax.ShapeDtypeStruct((n, W_ATTN), BF16), jax.ShapeDtypeStruct((n, W_FOUR), BF16)]
        + casts.out_shapes(),
        input_output_aliases=aliases,
        compiler_params=_cparams(2),
        name="attn_fourier",
    )(*args, *casts.args())


def _dft_tables(seq):
    def cs(n):
        j = np.arange(n)
        ang = 2.0 * np.pi * ((j[:, None] * j[None, :]) % n) / n
        return np.cos(ang) / math.sqrt(n), np.sin(ang) / math.sqrt(n)

    cd, sd = cs(DG_FOUR)
    ct, st = cs(seq)
    w_d = jnp.asarray(np.concatenate([cd, sd], axis=1), F32)
    return w_d.astype(BF16), jnp.asarray(ct, F32).astype(BF16), jnp.asarray(-st, F32).astype(BF16)


def _four_kernel(*refs, nb, seq):
    z_ref, wd_ref, ct_ref, st_ref = refs[:4]
    o_ref = refs[-1]
    y_cos, y_sin = [], []
    for gidx in range(N_FOUR):
        y = jnp.dot(z_ref[:, gidx * DG_FOUR:(gidx + 1) * DG_FOUR], wd_ref[...], preferred_element_type=F32)
        y_cos.append(y[:, :DG_FOUR].astype(BF16))
        y_sin.append(y[:, DG_FOUR:].astype(BF16))
    y_cos = jnp.concatenate(y_cos, axis=1)
    y_sin = jnp.concatenate(y_sin, axis=1)
    for s in range(nb):
        rows = slice(s * seq, (s + 1) * seq)
        o_ref[rows, :] = (jnp.dot(ct_ref[...], y_cos[rows], preferred_element_type=F32)
                          + jnp.dot(st_ref[...], y_sin[rows], preferred_element_type=F32)).astype(o_ref.dtype)


def _mlstm_kernel(*refs, seq, nb, seeded, emit_state, n_alias):
    mq_ref, mk_ref, vt_ref, mo_ref, gt_ref, g_ref = refs[:6]
    pos = 6
    if seeded:
        c0_ref, n0_ref, m0_ref = refs[pos:pos + 3]
        pos += 3
    pos += n_alias
    o_ref = refs[pos]
    if emit_state:
        c1_ref, n1_ref, m1_ref = refs[pos + 1:pos + 4]

    L = min(MLSTM_CHUNK, seq)
    nc = seq // L
    s_idx = lax.broadcasted_iota(jnp.int32, (L, L), 0)
    t_idx = lax.broadcasted_iota(jnp.int32, (L, L), 1)
    before = (s_idx <= t_idx, s_idx >= t_idx)
    tri = (jnp.where(before[1], 1.0, 0.0).astype(BF16),
           jnp.where(before[0], 1.0, 0.0).astype(BF16))
    chains = [(bi, d, h) for bi in range(nb) for d in range(2) for h in range(N_MLSTM)]

    state = {}
    for (bi, d, h) in chains:
        if seeded:
            state[bi, d, h] = (c0_ref[bi, d, h], n0_ref[bi, d, h:h + 1, :], m0_ref[bi, d:d + 1, h:h + 1])
        else:
            state[bi, d, h] = (None, None, jnp.zeros((1, 1), F32))

    terms = {}
    for c in range(nc):
        for bi in range(nb):
            for d in range(2):
                r0 = bi * seq + (c if d == 0 else nc - 1 - c) * L
                g_t = gt_ref[:, r0:r0 + L]
                hi = g_t.astype(BF16)
                rem = g_t - hi.astype(F32)
                mid = rem.astype(BF16)
                lo = (rem - mid.astype(F32)).astype(BF16)
                pieces = _bdot_nt(jnp.concatenate([hi, mid, lo], axis=0), tri[d])
                cum_t = pieces[0:N_GATE] + pieces[N_GATE:2 * N_GATE] + pieces[2 * N_GATE:]
                c_t = (g_t - pltpu.roll(cum_t, N_GATE - N_MLSTM, 0)) * LOG2E
                col = jnp.concatenate([c_t, jnp.zeros((LANES - N_GATE, L), F32)], axis=0).T
                terms[c, bi, d] = (r0, col, g_t, cum_t)

    h_t = {}
    for c in range(nc):
        new_state = {}
        need_update = emit_state or c < nc - 1
        group = [(d, h) for d in range(2) for h in range(N_MLSTM)]
        for bi in range(nb):
            has_state = state[bi, 0, 0][0] is not None
            qs, ks, vts, cbs, s0s, i_rows, b_rows, m_prevs = [], [], [], [], [], [], [], []
            for d, h in group:
                r0, col, g_t, cum_t = terms[c, bi, d]
                ci = 2 * d * N_MLSTM + h
                cf = ci + N_MLSTM
                hs = slice(h * DH_MLSTM, (h + 1) * DH_MLSTM)
                qs.append(mq_ref[r0:r0 + L, hs])
                ks.append(mk_ref[r0:r0 + L, hs])
                vts.append(vt_ref[hs, r0:r0 + L])
                cbs.append(jnp.where(before[d], col[:, ci:ci + 1], -jnp.inf))
                s0s.append(_bdot_nt(ks[-1], qs[-1]))
                i_rows.append(g_t[ci:ci + 1, :])
                b_rows.append(cum_t[cf:cf + 1, :])
                m_prevs.append(jnp.broadcast_to(state[bi, d, h][2], (1, L)))
            cb = jnp.concatenate(cbs, axis=1)
            i_row, b_row = jnp.concatenate(i_rows, axis=1), jnp.concatenate(b_rows, axis=1)
            m_prev = jnp.concatenate(m_prevs, axis=1)
            m2_prev = m_prev * LOG2E
            m2_row = jnp.maximum(jnp.max(cb, axis=0, keepdims=True), m2_prev)
            s_t = jnp.concatenate(s0s, axis=1) * jnp.exp2(cb - m2_row)
            den = jnp.sum(s_t, axis=0, keepdims=True)
            s_bf = s_t.astype(BF16)
            m_t = b_row + m2_row * LN2
            floor = jnp.exp(-m_t)
            if has_state:
                sp = jnp.exp2(m2_prev - m2_row)
            if need_update:
                tot, new = [], []
                for j, (d, h) in enumerate(group):
                    last = j * L + (L - 1 if d == 0 else 0)
                    tot.append(jnp.broadcast_to(b_row[:, last:last + 1], (1, L)))
                    new.append(jnp.broadcast_to(m_t[:, last:last + 1], (1, L)))
                b_tot, m_new = jnp.concatenate(tot, axis=1), jnp.concatenate(new, axis=1)
                wl = jnp.exp(b_tot + (i_row - b_row) - m_new)
                if has_state:
                    decay = jnp.exp(b_tot + m_prev - m_new)
            for j, (d, h) in enumerate(group):
                cols = slice(j * L, (j + 1) * L)
                c_prev, n_prev, _ = state[bi, d, h]
                num_t = jnp.dot(vts[j].astype(BF16), s_bf[:, cols], preferred_element_type=F32)
                den_j = den[:, cols]
                if has_state:
                    cn = jnp.concatenate([c_prev, jnp.broadcast_to(n_prev, (SUBLANES, DH_MLSTM))], axis=0)
                    cq = _bdot_nt(cn, qs[j])
                    num_t = num_t + sp[:, cols] * cq[:DH_MLSTM]
                    den_j = den_j + sp[:, cols] * cq[DH_MLSTM:DH_MLSTM + 1]
                h_t[bi, d, h, c] = num_t / jnp.maximum(jnp.abs(den_j), floor[:, cols])
                if need_update:
                    wl_j = wl[:, cols]
                    vw = jnp.concatenate([vts[j] * wl_j, jnp.broadcast_to(wl_j, (SUBLANES, L))], axis=0)
                    upd = jnp.dot(vw.astype(BF16), ks[j], preferred_element_type=F32)
                    c_new, n_new = upd[:DH_MLSTM], upd[DH_MLSTM:DH_MLSTM + 1]
                    if has_state:
                        decay_j = decay[:, j * L:j * L + 1]
                        c_new = decay_j * c_prev + c_new
                        n_new = decay_j * n_prev + n_new
                    new_state[bi, d, h] = (c_new, n_new, m_new[:, j * L:j * L + 1])
        state = new_state

    g_m = g_ref[...]
    for bi in range(nb):
        for h in range(N_MLSTM):
            hs = slice(h * DH_MLSTM, (h + 1) * DH_MLSTM)
            fwd = [h_t[bi, 0, h, c] for c in range(nc)]
            bwd = [h_t[bi, 1, h, nc - 1 - c] for c in range(nc)]
            hsum = (fwd[0] if nc == 1 else jnp.concatenate(fwd, axis=1)) \
                + (bwd[0] if nc == 1 else jnp.concatenate(bwd, axis=1))
            y = hsum * lax.rsqrt(jnp.mean(hsum * hsum, axis=0, keepdims=True) + EPS)
            rows = slice(bi * seq, (bi + 1) * seq)
            o_ref[rows, hs] = (y.T * g_m * _sigmoid(mo_ref[rows, hs])).astype(o_ref.dtype)

    if emit_state:
        for (bi, d, h) in chains:
            c_fin, n_fin, m_fin = state[bi, d, h]
            c1_ref[bi, d, h] = c_fin
            n1_ref[bi, d, h:h + 1, :] = n_fin
            m1_ref[bi, d:d + 1, h:h + 1] = m_fin


def _mlstm_call(mq, mk, vt, mo, gates_t, g_m, state, alias_prev, cast_w, st, *, layer, latent):
    if latent:
        batch, seq, nb, row0 = st.dec_batch, st.dec_seq, 1, st.n_ctx
    else:
        batch, seq, nb, row0 = st.batch, st.seq, 4, 0
    emit_state = not latent
    rows = nb * seq
    b0 = row0 // rows
    tok_spec = pl.BlockSpec((rows, W_MLSTM), lambda b: (b0 + b, 0))
    in_specs = [tok_spec, tok_spec,
                pl.BlockSpec((W_MLSTM, rows), lambda b: (0, b0 + b)),
                tok_spec,
                pl.BlockSpec((N_GATE, rows), lambda b: (0, b0 + b)),
                pl.BlockSpec((None, 1, DH_MLSTM), lambda b: (layer, 0, 0))]
    args = [mq, mk, vt, mo, gates_t, g_m]
    state_specs = [
        pl.BlockSpec((nb, None, 2, N_MLSTM, DH_MLSTM, DH_MLSTM), lambda b: (b, layer, 0, 0, 0, 0)),
        pl.BlockSpec((nb, None, 2, N_MLSTM, DH_MLSTM), lambda b: (b, layer, 0, 0, 0)),
        pl.BlockSpec((nb, None, 2, N_MLSTM), lambda b: (b, layer, 0, 0)),
    ]
    if state is not None:
        in_specs += state_specs
        args += list(state)
    alias_in = list(alias_prev) if alias_prev is not None else []
    n_in = len(args)
    in_specs += [pl.BlockSpec(memory_space=pl.ANY)] * len(alias_in)
    args += alias_in
    out_specs = [tok_spec]
    out_shape = [jax.ShapeDtypeStruct((st.n_all, W_MLSTM), BF16)]
    if emit_state:
        out_specs += state_specs
        out_shape += [
            jax.ShapeDtypeStruct((batch, DEPTH, 2, N_MLSTM, DH_MLSTM, DH_MLSTM), F32),
            jax.ShapeDtypeStruct((batch, DEPTH, 2, N_MLSTM, DH_MLSTM), F32),
            jax.ShapeDtypeStruct((batch, DEPTH, 2, N_MLSTM), F32),
        ]
    first_out = 1 if emit_state else 0
    casts = _CastJobs(cast_w, batch // nb, lambda b: b)
    body = functools.partial(_mlstm_kernel, seq=seq, nb=nb, seeded=state is not None, emit_state=emit_state,
                             n_alias=len(alias_in))
    return pl.pallas_call(
        casts.wrap(body, len(args), len(out_specs)),
        grid=(batch // nb,),
        in_specs=in_specs + casts.in_specs(),
        out_specs=out_specs + casts.out_specs(),
        out_shape=out_shape + casts.out_shapes(),
        input_output_aliases={n_in + j: first_out + j for j in range(len(alias_in))},
        compiler_params=_cparams(1),
        name="mlstm",
    )(*args, *casts.args())


def _ctx_mixer_kernel(*refs, n_attn, attn_kw, four_kw, mlstm_kw):
    a_ref, f_ref = refs[-6:-4]
    _attn_kernel(*refs[:n_attn], a_ref, **attn_kw)
    _four_kernel(*refs[n_attn:n_attn + 4], f_ref, **four_kw)
    _mlstm_kernel(*refs[n_attn + 4:n_attn + 10], *refs[-4:], **mlstm_kw)


def _ctx_mixer_call(q, k, v_t, lam_p, g_sub, zf, mq, mk, vt, mo, gates_t, g_m, states_prev, cast_w, st, *,
                    layer, lam_init):
    n, seq, nb = st.n_all, st.seq, 4
    rows = nb * seq
    heads = pl.BlockSpec((N_ATTN, rows, DV_ATTN), lambda b: (0, b, 0))
    tok = pl.BlockSpec((rows, W_ATTN), lambda b: (b, 0))
    w_d, ct, s_t = _dft_tables(seq)
    in_specs = [heads, heads,
                pl.BlockSpec((W_ATTN, rows), lambda b: (0, b)),
                pl.BlockSpec((None, 4, DH_ATTN), lambda b: (layer, 0, 0)),
                pl.BlockSpec((None, 1, DV_ATTN), lambda b: (layer, 0, 0)),
                tok,
                pl.BlockSpec((DG_FOUR, 2 * DG_FOUR), lambda b: (0, 0)),
                pl.BlockSpec((seq, seq), lambda b: (0, 0)),
                pl.BlockSpec((seq, seq), lambda b: (0, 0)),
                tok, tok,
                pl.BlockSpec((W_MLSTM, rows), lambda b: (0, b)),
                tok,
                pl.BlockSpec((N_GATE, rows), lambda b: (0, b)),
                pl.BlockSpec((None, 1, DH_MLSTM), lambda b: (layer, 0, 0))]
    args = [q, k, v_t, lam_p, g_sub, zf, w_d, ct, s_t, mq, mk, vt, mo, gates_t, g_m]
    n_attn = 5
    state_specs = [
        pl.BlockSpec((nb, None, 2, N_MLSTM, DH_MLSTM, DH_MLSTM), lambda b: (b, layer, 0, 0, 0, 0)),
        pl.BlockSpec((nb, None, 2, N_MLSTM, DH_MLSTM), lambda b: (b, layer, 0, 0, 0)),
        pl.BlockSpec((nb, None, 2, N_MLSTM), lambda b: (b, layer, 0, 0)),
    ]
    aliases = {}
    if states_prev is not None:
        aliases = {len(args) + j: 3 + j for j in range(3)}
        in_specs += [pl.BlockSpec(memory_space=pl.ANY)] * 3
        args += list(states_prev)
    casts = _CastJobs(cast_w, st.batch // nb, lambda b: b)
    body = functools.partial(
        _ctx_mixer_kernel, n_attn=n_attn,
        attn_kw=dict(lam_init=lam_init, cached=False, nb=nb, tq=seq, seq=seq),
        four_kw=dict(nb=nb, seq=seq),
        mlstm_kw=dict(seq=seq, nb=nb, seeded=False, emit_state=True, n_alias=0))
    return pl.pallas_call(
        casts.wrap(body, len(args), 6),
        grid=(st.batch // nb,),
        in_specs=in_specs + casts.in_specs(),
        out_specs=[tok, tok, tok] + state_specs + casts.out_specs(),
        out_shape=[jax.ShapeDtypeStruct((n, W_ATTN), BF16), jax.ShapeDtypeStruct((n, W_FOUR), BF16),
                   jax.ShapeDtypeStruct((n, W_MLSTM), BF16),
                   jax.ShapeDtypeStruct((st.batch, DEPTH, 2, N_MLSTM, DH_MLSTM, DH_MLSTM), F32),
                   jax.ShapeDtypeStruct((st.batch, DEPTH, 2, N_MLSTM, DH_MLSTM), F32),
                   jax.ShapeDtypeStruct((st.batch, DEPTH, 2, N_MLSTM), F32)] + casts.out_shapes(),
        input_output_aliases=aliases,
        compiler_params=_cparams(1),
        name="ctx_mixer",
    )(*args, *casts.args())


def _merge_kernel(x_ref, a_ref, f_ref, m_ref, mod_ref, g_ref, wg_ref, wa_ref, wf_ref, wm_ref, wo_ref, o_ref):
    x = x_ref[...]
    mod = mod_ref[...]
    u = _modulated_norm(x, g_ref[...], mod, 3).astype(BF16)
    merged = None
    for j, (br_ref, w_ref) in enumerate(((a_ref, wa_ref), (f_ref, wf_ref), (m_ref, wm_ref))):
        gate = _sigmoid(jnp.dot(u, wg_ref[:, j * D_MODEL:(j + 1) * D_MODEL], preferred_element_type=F32))
        term = gate * _bdot(br_ref[...], w_ref[...])
        merged = term if merged is None else merged + term
    out = _bdot(merged, wo_ref[...])
    o_ref[...] = x + mod[5:6] * out


def _merge_call(x, a, f, m, mods, g_norm, wg, wa, wf, wm, wo, st, *, layer):
    tm = TM_WIDE
    n = st.n_all
    br_spec = pl.BlockSpec((tm, W_ATTN), lambda i: (i, 0))
    return pl.pallas_call(
        _merge_kernel,
        grid=(n // tm,),
        in_specs=[
            pl.BlockSpec((tm, D_MODEL), lambda i: (i, 0)),
            br_spec, br_spec, br_spec,
            st.mod_spec(tm),
            pl.BlockSpec((None, None, 1, D_MODEL), lambda i: (layer, 1, 0, 0)),
            _layer_spec((D_MODEL, 3 * D_MODEL), None),
            _layer_spec((W_ATTN, D_MODEL), None), _layer_spec((W_FOUR, D_MODEL), None),
            _layer_spec((W_MLSTM, D_MODEL), None),
            _layer_spec((D_MODEL, D_MODEL), None),
        ],
        out_specs=pl.BlockSpec((tm, D_MODEL), lambda i: (i, 0)),
        out_shape=jax.ShapeDtypeStruct((n, D_MODEL), F32),
        compiler_params=_cparams(1),
        name="merge",
    )(x, a, f, m, mods, g_norm, wg, wa, wf, wm, wo)


def _rope_tables(n_tok):
    tok = np.arange(n_tok)
    inv = ROPE_BASE ** (-np.arange(ROPE_AXIS_PAIRS, dtype=np.float32) / ROPE_AXIS_PAIRS)
    ang = np.concatenate([(tok // GRID_W).astype(np.float32)[:, None] * inv,
                          (tok % GRID_W).astype(np.float32)[:, None] * inv], axis=-1).astype(np.float32)
    c, s = np.cos(ang), np.sin(ang)
    cos_t = np.concatenate([c, c, c, c], axis=-1)
    sin_t = np.concatenate([-s, s, -s, s], axis=-1)
    return jnp.asarray(cos_t, F32), jnp.asarray(sin_t, F32)


def kernel(x_prompt, x_sample, cache_k, cache_v, state_C, state_n, state_m, c, c_ctx, w_ada, b_ada, g_norm,
           w_ffn1_in, w_ffn1_out, w_ffn2_in, w_ffn2_out, w_in, b_mgate, attn_lambda, g_attn_sub, g_mlstm,
           w_branch_gate, w_br_attn, w_br_four, w_br_mlstm, w_out, g_final):
    batch, seq, _ = x_prompt.shape
    dec_batch, dec_seq, _ = x_sample.shape
    st = _Stream(batch, seq, dec_batch, dec_seq)
    cond = jnp.zeros((N_COND, D_MODEL), F32).at[0].set(c_ctx).at[1:1 + dec_batch].set(c)
    mods_flat, *ffn1_w = _ada_call(cond, w_ada, b_ada, [(w_ffn1_in, 0), (w_ffn1_out, 0)])
    mods_all = mods_flat.reshape(DEPTH, N_COND, N_MOD, D_MODEL)

    b_gate_t = b_mgate[:, :, None]
    g_norm4 = g_norm[:, :, None, :]
    g_fin = g_final[None, :]
    w_t = jnp.swapaxes(w_in, 1, 2).astype(BF16)
    g_sub3 = g_attn_sub[:, None, :]
    g_m3 = g_mlstm[:, None, :]
    cos_t, sin_t = _rope_tables(dec_seq)
    lat_state = (state_C, state_n, state_m)

    xs = (x_prompt.reshape(batch * seq, D_MODEL), x_sample.reshape(dec_batch * dec_seq, D_MODEL))
    kv, states = None, None
    for l in range(DEPTH):
        mods = mods_all[l]
        lam_init = 0.8 - 0.6 * math.exp(-0.3 * l)
        nxt = l + 1 < DEPTH
        (x,) = _ffn_call(xs, mods, g_norm4, g_fin, *ffn1_w, st, layer=l, sub=0, final=False)
        q, k, ck, cv, zf, mq, mk, mo, vat, vt, gates_t = _proj_call(
            x, mods, g_norm4, w_t, b_gate_t, cos_t, sin_t, kv, st, layer=l)
        kv = (ck, cv)
        a, f, m, c_fin, n_fin, m_fin, ffn2_in, *nxt_out = _ctx_mixer_call(
            q, k, vat, attn_lambda, g_sub3, zf, mq, mk, vt, mo, gates_t, g_m3, states,
            [(w_ffn2_in, l)] + ([(w_ffn1_out, l + 1)] if nxt else []), st, layer=l, lam_init=lam_init)
        states = [c_fin, n_fin, m_fin]
        a, f, ffn2_out, w_bg, *nxt_in = _attn_four_call(
            q, k, vat, (cache_k, cache_v), attn_lambda, g_sub3, zf, (a, f),
            [(w_ffn2_out, l), (w_branch_gate, l)] + ([(w_ffn1_in, l + 1)] if nxt else []), st,
            layer=l, lam_init=lam_init, latent=True)
        m, w_ba, w_bf, w_bm, w_o = _mlstm_call(
            mq, mk, vt, mo, gates_t, g_m3, lat_state, (m,),
            [(w_br_attn, l), (w_br_four, l), (w_br_mlstm, l), (w_out, l)], st, layer=l, latent=True)
        x = _merge_call(x, a, f, m, mods, g_norm4, w_bg, w_ba, w_bf, w_bm, w_o, st, layer=l)
        xs = tuple(_ffn_call((x,), mods, g_norm4, g_fin, ffn2_in, ffn2_out, st, layer=l, sub=2,
                             final=(l == DEPTH - 1)))
        if nxt:
            ffn1_w = (nxt_in[0], nxt_out[0])
    y_prompt, y_sample = xs
    return (y_prompt.reshape(batch, seq, D_MODEL), y_sample.reshape(dec_batch, dec_seq, D_MODEL),
            *kv, *states)
```
